```python
import jax, jax.numpy as jnp
from jax import lax
import numpy as np

D_MODEL = 1024
BATCH = 8
SEQ = 2048
DEPTH = 4
DEC_BATCH = 128
DEC_SEQ = 1
PAST_LEN = 8192
PAGE_SIZE = 128

N_A_LAYERS = DEPTH // 2
N_B_LAYERS = DEPTH - N_A_LAYERS
HG_HEAD_K = 128
HG_HEADS = D_MODEL // HG_HEAD_K
HG_HEAD_V = D_MODEL // HG_HEADS
HG_KEY_DIM = HG_HEADS * HG_HEAD_K
HG_VAL_DIM = HG_HEADS * HG_HEAD_V
HG_CHUNK = 64
ATT_HEAD_DIM = 64
ATT_Q_HEADS = D_MODEL // ATT_HEAD_DIM
ATT_KV_HEADS = max(1, ATT_Q_HEADS // 8)
ATT_GROUP = ATT_Q_HEADS // ATT_KV_HEADS
ATT_Q_DIM = ATT_Q_HEADS * ATT_HEAD_DIM
WINDOW = 128
ATT_BLOCK = WINDOW
ROPE_THETA = 500000.0
ROT_DIM = ATT_HEAD_DIM // 4
NORM_EPS = 1e-6
MASK_VALUE = -1e30

kernel_name = 'yoco_hgrn2_swa_sink_decode_step'

F32 = jnp.float32


def _rmsnorm(x, g):
    xf = x.astype(F32)
    r = lax.rsqrt(jnp.mean(xf * xf, axis=-1, keepdims=True) + NORM_EPS)
    return (xf * r * g.astype(F32)).astype(x.dtype)


def _rope(x, pos):
    half = ROT_DIM // 2
    inv_freq = 1.0 / (ROPE_THETA ** (jnp.arange(half, dtype=F32) * 2.0 / ROT_DIM))
    ang = pos.astype(F32)[:, None] * inv_freq[None, :]
    cos = jnp.cos(ang)[None, :, None, :]
    sin = jnp.sin(ang)[None, :, None, :]
    xf = x.astype(F32)
    x1, x2, rest = xf[..., :half], xf[..., half:ROT_DIM], xf[..., ROT_DIM:]
    out = jnp.concatenate([x1 * cos - x2 * sin, x2 * cos + x1 * sin, rest], axis=-1)
    return out.astype(x.dtype)


def _chunked_gated_recurrence(q, k, v, log_f, s0):
    B, T, H, _ = q.shape
    dv = v.shape[-1]
    C = min(HG_CHUNK, T)
    pad = (-T) % C
    n = (T + pad) // C

    def blocks(a):
        a = jnp.pad(a, ((0, 0), (0, pad), (0, 0), (0, 0)))
        return a.reshape(B, n, C, H, a.shape[-1]).transpose(1, 0, 3, 2, 4)

    causal = jnp.tril(jnp.ones((C, C), dtype=bool))[:, :, None]

    def step(S, inp):
        qc, kc, vc, lc = inp
        b = jnp.cumsum(lc, axis=2)
        o = jnp.einsum('bhtd,bhde->bhte', qc * jnp.exp(b), S)
        diff = b[:, :, :, None, :] - b[:, :, None, :, :]
        decay = jnp.where(causal, jnp.exp(jnp.where(causal, diff, 0.0)), 0.0)
        a = jnp.einsum('bhtd,bhsd,bhtsd->bhts', qc, kc, decay)
        o = o + jnp.einsum('bhts,bhse->bhte', a, vc)
        b_end = b[:, :, -1:, :]
        S = (jnp.exp(b_end[:, :, 0, :])[..., None] * S
             + jnp.einsum('bhsd,bhse->bhde', kc * jnp.exp(b_end - b), vc))
        return S, o

    s_fin, o = lax.scan(step, s0, (blocks(q), blocks(k), blocks(v), blocks(log_f)))
    o = o.transpose(1, 0, 3, 2, 4).reshape(B, n * C, H, dv)[:, :T]
    return o, s_fin


def _hgrn2_layer(h, s0, norm_g, w_in, lb, gnorm, w_out):
    B, T, _ = h.shape
    u = _rmsnorm(h, norm_g) @ w_in
    q, f, i, g = jnp.split(u, [HG_KEY_DIM, 2 * HG_KEY_DIM, 2 * HG_KEY_DIM + HG_VAL_DIM], axis=-1)
    q = jax.nn.silu(q.astype(F32)).reshape(B, T, HG_HEADS, HG_HEAD_K)
    f = f.astype(F32).reshape(B, T, HG_HEADS, HG_HEAD_K)
    lb = lb.astype(F32).reshape(HG_HEADS, HG_HEAD_K)
    log_f = jax.nn.log_sigmoid(f) + jnp.log1p(lb * jnp.exp(-f))
    k = (1.0 - lb) * jax.nn.sigmoid(-f)
    v = i.astype(F32).reshape(B, T, HG_HEADS, HG_HEAD_V)
    o, s_new = _chunked_gated_recurrence(q, k, v, log_f, s0.astype(F32))
    o = _rmsnorm(o, gnorm).reshape(B, T, HG_VAL_DIM)
    y = (o * jax.nn.silu(g.astype(F32))).astype(h.dtype) @ w_out
    return h + y, s_new.astype(s0.dtype)


def _shared_kv(h, kv_norm, w_kv, pos):
    B, T, _ = h.shape
    u = _rmsnorm(h, kv_norm) @ w_kv
    k, v = jnp.split(u, 2, axis=-1)
    k = _rope(k.reshape(B, T, ATT_KV_HEADS, ATT_HEAD_DIM), pos)
    v = v.reshape(B, T, ATT_KV_HEADS, ATT_HEAD_DIM)
    return k, v


def _sink_attention(q, k, v, mask, sinks):
    scale = ATT_HEAD_DIM ** -0.5
    s = jnp.einsum('bnqkgd,bnskd->bnkgqs', q.astype(F32), k.astype(F32)) * scale
    valid = mask[None, :, None, None]
    s = jnp.where(valid, s, MASK_VALUE)
    sink = sinks.astype(F32)[None, None, :, :, None, None]
    m = jnp.maximum(jnp.max(s, axis=-1, keepdims=True), sink)
    p = jnp.where(valid, jnp.exp(s - m), 0.0)
    denom = jnp.sum(p, axis=-1, keepdims=True) + jnp.exp(sink - m)
    return jnp.einsum('bnkgqs,bnskd->bnqkgd', p / denom, v.astype(F32))


def _swa_layer(h, k, v, pos, past_k, past_v, norm_g, w_in, sinks, w_out):
    B, T, _ = h.shape
    u = _rmsnorm(h, norm_g) @ w_in
    q, gate = jnp.split(u, [ATT_Q_DIM], axis=-1)
    q = _rope(q.reshape(B, T, ATT_Q_HEADS, ATT_HEAD_DIM), pos)
    q = q.reshape(B, T, ATT_KV_HEADS, ATT_GROUP, ATT_HEAD_DIM)
    sinks_g = sinks.reshape(ATT_KV_HEADS, ATT_GROUP)
    if past_k is None:
        nb = T // ATT_BLOCK
        qb = q.reshape(B, nb, ATT_BLOCK, ATT_KV_HEADS, ATT_GROUP, ATT_HEAD_DIM)

        def band(a):
            ab = a.reshape(B, nb, ATT_BLOCK, ATT_KV_HEADS, ATT_HEAD_DIM)
            prev = jnp.concatenate([jnp.zeros_like(ab[:, :1]), ab[:, :-1]], axis=1)
            return jnp.concatenate([prev, ab], axis=2)

        tq = jnp.arange(ATT_BLOCK)[:, None]
        j = jnp.arange(2 * ATT_BLOCK)[None, :]
        rel = tq + ATT_BLOCK - j
        mask = ((rel >= 0) & (rel <= WINDOW))[None]
        mask = mask & ((jnp.arange(nb)[:, None, None] > 0) | (j[None] >= ATT_BLOCK))
        o = _sink_attention(qb, band(k), band(v), mask, sinks_g)
    else:
        w = past_k.shape[1]
        kk = jnp.concatenate([past_k.astype(k.dtype), k], axis=1)[:, None]
        vv = jnp.concatenate([past_v.astype(v.dtype), v], axis=1)[:, None]
        kpos = jnp.concatenate([pos[0] - w + jnp.arange(w, dtype=pos.dtype), pos])
        rel = pos[:, None] - kpos[None, :]
        mask = ((rel >= 0) & (rel <= WINDOW))[None]
        o = _sink_attention(q[:, None], kk, vv, mask, sinks_g)
    o = o.reshape(B, T, ATT_Q_DIM)
    y = (o * jax.nn.silu(gate.astype(F32))).astype(h.dtype) @ w_out
    return h + y


def _trunk(x, pos, s0, past_k, past_v, a_norm, a_w_in, a_lb_logits, a_gnorm, a_w_out,
           kv_norm, w_kv, b_norm, b_w_in, b_sinks, b_w_out, final_norm):
    p = jax.nn.softmax(a_lb_logits.astype(F32), axis=0)
    lbs = jnp.cumsum(p, axis=0) - p[0:1]
    h = x
    new_states = []
    k = v = None
    for layer in range(DEPTH):
        if layer < N_A_LAYERS:
            h, s = _hgrn2_layer(h, s0[layer], a_norm[layer], a_w_in[layer], lbs[layer],
                                a_gnorm[layer], a_w_out[layer])
            new_states.append(s)
        else:
            if layer == N_A_LAYERS:
                k, v = _shared_kv(h, kv_norm, w_kv, pos)
            j = layer - N_A_LAYERS
            h = _swa_layer(h, k, v, pos, past_k, past_v, b_norm[j], b_w_in[j], b_sinks[j], b_w_out[j])
    return _rmsnorm(h, final_norm), jnp.stack(new_states), k, v


def setup_inputs(seed: int = 0) -> dict:
    key = jax.random.key(seed)
    ks = jax.random.split(key, 20)
    D = D_MODEL
    w_rows = min(WINDOW, PAST_LEN)
    a_in = 2 * HG_KEY_DIM + 2 * HG_VAL_DIM
    b_in = 2 * ATT_Q_DIM
    kv_out = 2 * ATT_KV_HEADS * ATT_HEAD_DIM
    nrm = jax.random.normal
    return {
        'x_prompt': nrm(ks[0], (BATCH, SEQ, D), F32),
        'x_sample': nrm(ks[1], (DEC_BATCH, DEC_SEQ, D), F32),
        'state_hgrn': 0.5 * nrm(ks[2], (N_A_LAYERS, DEC_BATCH, HG_HEADS, HG_HEAD_K, HG_HEAD_V), F32),
        'cache_k_win': nrm(ks[3], (DEC_BATCH, w_rows, ATT_KV_HEADS, ATT_HEAD_DIM), F32),
        'cache_v_win': nrm(ks[4], (DEC_BATCH, w_rows, ATT_KV_HEADS, ATT_HEAD_DIM), F32),
        'a_norm': 1.0 + 0.05 * nrm(ks[5], (N_A_LAYERS, D), F32),
        'a_w_in': nrm(ks[6], (N_A_LAYERS, D, a_in), F32) * D ** -0.5,
        'a_lb_logits': nrm(ks[7], (N_A_LAYERS, HG_KEY_DIM), F32),
        'a_gnorm': 1.0 + 0.05 * nrm(ks[8], (N_A_LAYERS, HG_HEAD_V), F32),
        'a_w_out': nrm(ks[9], (N_A_LAYERS, HG_VAL_DIM, D), F32) * HG_VAL_DIM ** -0.5,
        'kv_norm': 1.0 + 0.05 * nrm(ks[10], (D,), F32),
        'w_kv': nrm(ks[11], (D, kv_out), F32) * D ** -0.5,
        'b_norm': 1.0 + 0.05 * nrm(ks[12], (N_B_LAYERS, D), F32),
        'b_w_in': nrm(ks[13], (N_B_LAYERS, D, b_in), F32) * D ** -0.5,
        'b_sinks': nrm(ks[14], (N_B_LAYERS, ATT_Q_HEADS), F32),
        'b_w_out': nrm(ks[15], (N_B_LAYERS, ATT_Q_DIM, D), F32) * ATT_Q_DIM ** -0.5,
        'final_norm': 1.0 + 0.05 * nrm(ks[16], (D,), F32),
    }


def reference(x_prompt, x_sample, state_hgrn, cache_k_win, cache_v_win, a_norm, a_w_in,
              a_lb_logits, a_gnorm, a_w_out, kv_norm, w_kv, b_norm, b_w_in, b_sinks,
              b_w_out, final_norm):
    bp, tp, _ = x_prompt.shape
    ts = x_sample.shape[1]
    s0_prompt = jnp.zeros((N_A_LAYERS, bp, HG_HEADS, HG_HEAD_K, HG_HEAD_V), x_prompt.dtype)
    pos_p = jnp.arange(tp, dtype=jnp.int32)
    pos_s = PAST_LEN + jnp.arange(ts, dtype=jnp.int32)
    y_prompt, st_p, k_p, v_p = _trunk(x_prompt, pos_p, s0_prompt, None, None,
                                      a_norm, a_w_in, a_lb_logits, a_gnorm, a_w_out,
                                      kv_norm, w_kv, b_norm, b_w_in, b_sinks, b_w_out, final_norm)
    y_sample, st_s, k_s, v_s = _trunk(x_sample, pos_s, state_hgrn, cache_k_win, cache_v_win,
                                      a_norm, a_w_in, a_lb_logits, a_gnorm, a_w_out,
                                      kv_norm, w_kv, b_norm, b_w_in, b_sinks, b_w_out, final_norm)
    w_keep = min(WINDOW, tp)
    return (y_prompt, y_sample, st_p, st_s, k_p[:, -w_keep:], v_p[:, -w_keep:], k_s, v_s)
```

```python
import functools

import numpy as np
import jax
import jax.numpy as jnp
from jax import lax
from jax.experimental import pallas as pl
from jax.experimental.pallas import tpu as pltpu

F32 = jnp.float32
BF16 = jnp.bfloat16

D_MODEL = 1024
HG_HEADS = 8
HG_DK = 128
HG_DV = 128
HG_KEY = HG_HEADS * HG_DK
HG_VAL = HG_HEADS * HG_DV
HG_CHUNK = 64
HG_HALF = HG_CHUNK // 2
HG_SUB = 16
ATT_HD = 64
ATT_QH = 16
ATT_KVH = 2
ATT_GROUP = ATT_QH // ATT_KVH
ATT_QD = ATT_QH * ATT_HD
ATT_KVD = ATT_KVH * ATT_HD
WINDOW = 128
ROPE_THETA = 500000.0
ROT_DIM = ATT_HD // 4
NORM_EPS = 1e-6
MASK_VALUE = -1e30
NEG_BIG = -1e30
PAST_LEN = 8192

SUBLANES = 8
LANES = 128
PROMPT_TILE = 256
DEC_BLOCK = 8
VMEM_LIMIT = 48 * 1024 * 1024


def _cparams(*sem):
    return pltpu.CompilerParams(dimension_semantics=sem, vmem_limit_bytes=VMEM_LIMIT)


def _mm(a, b):
    return jnp.dot(a.astype(BF16), b.astype(BF16), preferred_element_type=F32)


def _mm_nt(a, b):
    return lax.dot_general(a.astype(BF16), b.astype(BF16), (((1,), (1,)), ((), ())),
                           preferred_element_type=F32)


def _mm_tn(a, b):
    return lax.dot_general(a.astype(BF16), b.astype(BF16), (((0,), (0,)), ((), ())),
                           preferred_element_type=F32)


def _rms_scale(x, g):
    ms = jnp.mean(x * x, axis=-1, keepdims=True)
    return x * lax.rsqrt(ms + NORM_EPS) * g


def _silu(x):
    return x * (1.0 / (1.0 + jnp.exp(-x)))


def _lower_bound(lbl, layer):
    n = lbl.shape[0]
    rows = [lbl[i:i + 1, :] for i in range(n)]
    m = functools.reduce(jnp.maximum, rows)
    es = [jnp.exp(r - m) for r in rows]
    inv = 1.0 / functools.reduce(jnp.add, es)
    ps = [e * inv for e in es]
    return functools.reduce(jnp.add, ps[:layer + 1]) - ps[0]


def _forget_gates(f, lb):
    e = jnp.exp(-f)
    r = 1.0 / (1.0 + e)
    return lb + (1.0 - lb) * r, (1.0 - lb) * (e * r)


def _chunk_cumsum(x):
    rows = x.shape[0]
    pos = lax.broadcasted_iota(jnp.int32, x.shape, 0) % HG_CHUNK
    s = 1
    while s < HG_CHUNK:
        x = x + jnp.where(pos >= s, pltpu.roll(x, s, axis=0), 0.0)
        s *= 2
    del rows
    return x


def _diag_blocks(q, k, v, b):
    sub_iota = lax.broadcasted_iota(jnp.int32, (SUBLANES, LANES), 0)
    outs = []
    for blk in range(HG_CHUNK // HG_SUB):
        base = blk * HG_SUB
        groups = HG_SUB // SUBLANES
        qs = [q[base + g * SUBLANES: base + (g + 1) * SUBLANES] for g in range(groups)]
        bs = [b[base + g * SUBLANES: base + (g + 1) * SUBLANES] for g in range(groups)]
        accs = [jnp.zeros((SUBLANES, HG_DV), F32) for _ in range(groups)]
        for s in range(HG_SUB):
            row = base + s
            k_s = k[row:row + 1]
            b_s = b[row:row + 1]
            v_s = v[row:row + 1]
            for g in range(s // SUBLANES, groups):
                d = bs[g] - b_s
                if g == s // SUBLANES:
                    d = jnp.where(sub_iota >= (s % SUBLANES), d, NEG_BIG)
                p = qs[g] * (k_s * jnp.exp(d))
                a = jnp.sum(p, axis=-1, keepdims=True)
                accs[g] = accs[g] + a * v_s
        outs.extend(accs)
    return jnp.concatenate(outs, axis=0)


def _hgrn_chunk(q, k, v, b, st):
    C, H = HG_CHUNK, HG_HALF
    o_inter = _mm_nt(q * jnp.exp(b), st)
    r0 = b[H - 1:H]
    a0 = _mm_nt(q[H:] * jnp.exp(b[H:] - r0), k[:H] * jnp.exp(r0 - b[:H]))
    o_hi = _mm(a0, v[:H])
    Q = HG_SUB
    o_q = []
    for base in (0, H):
        r1 = b[base + Q - 1: base + Q]
        a1 = _mm_nt(q[base + Q: base + 2 * Q] * jnp.exp(b[base + Q: base + 2 * Q] - r1),
                    k[base: base + Q] * jnp.exp(r1 - b[base: base + Q]))
        o_q.append(_mm(a1, v[base: base + Q]))
    zero = jnp.zeros((Q, HG_DV), F32)
    o_off = jnp.concatenate([zero, o_q[0], o_hi[:Q], o_hi[Q:] + o_q[1]], axis=0)
    o = o_inter + o_off + _diag_blocks(q, k, v, b)
    b_end = b[C - 1:C]
    st_new = st * jnp.exp(b_end) + _mm_tn(v, k * jnp.exp(b_end - b))
    return o, st_new


def _hgrn_prompt_kernel(layer, h_ref, ng_ref, win_ref, lbl_ref, gn_ref, wout_ref,
                        out_ref, sfin_ref, q_s, k_s, v_s, b_s, o_s, st_s):
    t = pl.program_id(1)
    nt = pl.num_programs(1)
    tile = h_ref.shape[1]

    @pl.when(t == 0)
    def _():
        st_s[...] = jnp.zeros_like(st_s)

    x = h_ref[0]
    u = _mm(_rms_scale(x, ng_ref[...]), win_ref[...])
    lb = _lower_bound(lbl_ref[...], layer)
    for h in range(HG_HEADS):
        sl = slice(h * HG_DK, (h + 1) * HG_DK)
        q_s[h] = _silu(u[:, sl])
        fg, kin = _forget_gates(u[:, HG_KEY + h * HG_DK: HG_KEY + (h + 1) * HG_DK], lb[:, sl])
        k_s[h] = kin
        b_s[h] = _chunk_cumsum(jnp.log(fg))
        v_s[h] = u[:, 2 * HG_KEY + h * HG_DV: 2 * HG_KEY + (h + 1) * HG_DV]

    def body(i, carry):
        c = i // HG_HEADS
        h = i % HG_HEADS
        rows = pl.ds(pl.multiple_of(c * HG_CHUNK, HG_CHUNK), HG_CHUNK)
        o, st_new = _hgrn_chunk(q_s[h, rows, :], k_s[h, rows, :], v_s[h, rows, :],
                                b_s[h, rows, :], st_s[h])
        o_s[h, rows, :] = o
        st_s[h] = st_new
        return carry

    lax.fori_loop(0, (tile // HG_CHUNK) * HG_HEADS, body, 0)

    gn = gn_ref[...]
    gated = []
    for h in range(HG_HEADS):
        g = u[:, 2 * HG_KEY + HG_VAL + h * HG_DV: 2 * HG_KEY + HG_VAL + (h + 1) * HG_DV]
        gated.append(_rms_scale(o_s[h], gn) * _silu(g))
    y = _mm(jnp.concatenate(gated, axis=1), wout_ref[...])
    out_ref[0] = x + y

    @pl.when(t == nt - 1)
    def _():
        for h in range(HG_HEADS):
            sfin_ref[0, h] = st_s[h].T


def _hgrn_prompt_layer(h, norm_g, w_in, lb_logits, layer, gnorm, w_out):
    B, T, D = h.shape
    tile = min(PROMPT_TILE, T)
    n_a = lb_logits.shape[0]
    const = lambda b, t: (0, 0)
    head_scratch = pltpu.VMEM((HG_HEADS, tile, HG_DK), F32)
    return pl.pallas_call(
        functools.partial(_hgrn_prompt_kernel, layer),
        grid=(B, T // tile),
        in_specs=[
            pl.BlockSpec((1, tile, D), lambda b, t: (b, t, 0)),
            pl.BlockSpec((1, D), const),
            pl.BlockSpec(w_in.shape, const),
            pl.BlockSpec((n_a, HG_KEY), const),
            pl.BlockSpec((1, HG_DV), const),
            pl.BlockSpec(w_out.shape, const),
        ],
        out_specs=[
            pl.BlockSpec((1, tile, D), lambda b, t: (b, t, 0)),
            pl.BlockSpec((1, HG_HEADS, HG_DK, HG_DV), lambda b, t: (b, 0, 0, 0)),
        ],
        out_shape=[
            jax.ShapeDtypeStruct((B, T, D), F32),
            jax.ShapeDtypeStruct((B, HG_HEADS, HG_DK, HG_DV), F32),
        ],
        scratch_shapes=[head_scratch] * 5 + [pltpu.VMEM((HG_HEADS, HG_DV, HG_DK), F32)],
        compiler_params=_cparams("arbitrary", "arbitrary"),
        name=f"hgrn_prompt_l{layer}",
    )(h, norm_g.reshape(1, D), w_in, lb_logits, gnorm.reshape(1, HG_DV), w_out)


def _rope_tables(pos):
    half = ROT_DIM // 2
    pos = np.asarray(pos, np.float64)
    inv_freq = 1.0 / (ROPE_THETA ** (np.arange(half, dtype=np.float64) * 2.0 / ROT_DIM))
    ang = (pos[:, None].astype(np.float32) * inv_freq[None, :].astype(np.float32)).astype(np.float64)
    cos, sin = np.cos(ang), np.sin(ang)
    n = pos.shape[0]
    c = np.ones((n, ATT_HD)); s_up = np.zeros((n, ATT_HD)); s_dn = np.zeros((n, ATT_HD))
    c[:, :half] = cos; c[:, half:ROT_DIM] = cos
    s_dn[:, :half] = -sin
    s_up[:, half:ROT_DIM] = sin
    rep = LANES // ATT_HD
    tab = [np.tile(a, (1, rep)).astype(np.float32) for a in (c, s_up, s_dn)]
    return tuple(jnp.asarray(a) for a in tab)


def _rope_cols(x, cos, s_up, s_dn):
    half = ROT_DIM // 2
    cols = []
    for c in range(x.shape[1] // LANES):
        xc = x[:, c * LANES:(c + 1) * LANES]
        cols.append(xc * cos + pltpu.roll(xc, half, axis=1) * s_up
                    + pltpu.roll(xc, LANES - half, axis=1) * s_dn)
    return cols[0] if len(cols) == 1 else jnp.concatenate(cols, axis=1)


def _kv_kernel(h_ref, ng_ref, w_ref, cos_ref, sup_ref, sdn_ref, k_ref, v_ref):
    x = h_ref[0]
    u = _mm(_rms_scale(x, ng_ref[...]), w_ref[...])
    k_ref[0] = _rope_cols(u[:, :ATT_KVD], cos_ref[...], sup_ref[...], sdn_ref[...])
    v_ref[0] = u[:, ATT_KVD:]


def _shared_kv(h, kv_norm, w_kv, tables):
    B, T, D = h.shape
    tile = min(PROMPT_TILE, T)
    const = lambda b, t: (0, 0)
    tab = pl.BlockSpec((tile, LANES), lambda b, t: (t, 0))
    kv_spec = pl.BlockSpec((1, tile, ATT_KVD), lambda b, t: (b, t, 0))
    return pl.pallas_call(
        _kv_kernel,
        grid=(B, T // tile),
        in_specs=[pl.BlockSpec((1, tile, D), lambda b, t: (b, t, 0)),
                  pl.BlockSpec((1, D), const), pl.BlockSpec(w_kv.shape, const), tab, tab, tab],
        out_specs=[kv_spec, kv_spec],
        out_shape=[jax.ShapeDtypeStruct((B, T, ATT_KVD), F32)] * 2,
        compiler_params=_cparams("arbitrary", "arbitrary"),
        name="shared_kv",
    )(h, kv_norm.reshape(1, D), w_kv, *tables)


def _swa_block(q_cols, k_lo, k_hi, v_lo, v_hi, valid, sinks_ref):
    outs = []
    pairs_per_kv = ATT_GROUP // 2
    for c, qc in enumerate(q_cols):
        kvh = c // pairs_per_kv
        acc = None
        for half, (kk, vv) in enumerate(((k_lo[kvh], v_lo[kvh]), (k_hi[kvh], v_hi[kvh]))):
            head = 2 * c + half
            s = jnp.where(valid, _mm_nt(qc, kk), MASK_VALUE)
            sink = sinks_ref[0, head]
            m = jnp.maximum(jnp.max(s, axis=-1, keepdims=True), sink)
            p = jnp.where(valid, jnp.exp(s - m), 0.0)
            denom = jnp.sum(p, axis=-1, keepdims=True) + jnp.exp(sink - m)
            o = _mm(p * (1.0 / denom), vv)
            acc = o if acc is None else acc + o
        outs.append(acc)
    return outs


def _split_kv_heads(a):
    lane = lax.broadcasted_iota(jnp.int32, a.shape, 1)
    low = lane < ATT_HD
    sw = pltpu.roll(a, ATT_HD, axis=1)
    lo = [jnp.where(low, a, 0.0), jnp.where(low, sw, 0.0)]
    hi = [jnp.where(low, 0.0, sw), jnp.where(low, 0.0, a)]
    return lo, hi


def _swa_prompt_kernel(final, h_ref, ng_ref, win_ref, sinks_ref, wout_ref, fin_ref,
                       kp_ref, kc_ref, vp_ref, vc_ref, cos_ref, sup_ref, sdn_ref, out_ref):
    t = pl.program_id(1)
    tile = h_ref.shape[1]
    nblk = tile // WINDOW
    x = h_ref[0]
    u = _mm(_rms_scale(x, ng_ref[...]), win_ref[...])
    scale = ATT_HD ** -0.5
    cos, sup, sdn = cos_ref[...], sup_ref[...], sdn_ref[...]
    k_all = jnp.concatenate([kp_ref[0], kc_ref[0]], axis=0)
    v_all = jnp.concatenate([vp_ref[0], vc_ref[0]], axis=0)
    tq = lax.broadcasted_iota(jnp.int32, (WINDOW, 2 * WINDOW), 0)
    j = lax.broadcasted_iota(jnp.int32, (WINDOW, 2 * WINDOW), 1)
    rel = tq + WINDOW - j
    band = (rel >= 0) & (rel <= WINDOW)
    o_blocks = []
    for blk in range(nblk):
        rows = slice(blk * WINDOW, (blk + 1) * WINDOW)
        keys = slice(blk * WINDOW, (blk + 2) * WINDOW)
        k_lo, k_hi = _split_kv_heads(k_all[keys])
        v_lo, v_hi = _split_kv_heads(v_all[keys])
        if blk == 0:
            valid = band & (j >= WINDOW * (t == 0).astype(jnp.int32))
        else:
            valid = band
        q_cols = []
        for c in range(ATT_QD // LANES):
            qc = u[rows, c * LANES:(c + 1) * LANES]
            q_cols.append(_rope_cols(qc, cos[rows], sup[rows], sdn[rows]) * scale)
        o_blocks.append(jnp.concatenate(
            _swa_block(q_cols, k_lo, k_hi, v_lo, v_hi, valid, sinks_ref), axis=1))
    o = jnp.concatenate(o_blocks, axis=0) if nblk > 1 else o_blocks[0]
    y = x + _mm(o * _silu(u[:, ATT_QD:]), wout_ref[...])
    if final:
        y = _rms_scale(y, fin_ref[...])
    out_ref[0] = y


def _swa_prompt_layer(h, k, v, tables, norm_g, w_in, sinks, w_out, final_norm, final):
    B, T, D = h.shape
    tile = min(PROMPT_TILE, T)
    per = tile // WINDOW
    const = lambda b, t: (0, 0)
    tab = pl.BlockSpec((tile, LANES), lambda b, t: (t, 0))
    prev = pl.BlockSpec((1, WINDOW, ATT_KVD), lambda b, t: (b, jnp.maximum(t * per - 1, 0), 0))
    cur = pl.BlockSpec((1, tile, ATT_KVD), lambda b, t: (b, t, 0))
    return pl.pallas_call(
        functools.partial(_swa_prompt_kernel, final),
        grid=(B, T // tile),
        in_specs=[
            pl.BlockSpec((1, tile, D), lambda b, t: (b, t, 0)),
            pl.BlockSpec((1, D), const),
            pl.BlockSpec(w_in.shape, const),
            pl.BlockSpec(memory_space=pltpu.SMEM),
            pl.BlockSpec(w_out.shape, const),
            pl.BlockSpec((1, D), const),
            prev, cur, prev, cur, tab, tab, tab,
        ],
        out_specs=pl.BlockSpec((1, tile, D), lambda b, t: (b, t, 0)),
        out_shape=jax.ShapeDtypeStruct((B, T, D), F32),
        compiler_params=_cparams("arbitrary", "arbitrary"),
        name="swa_prompt_final" if final else "swa_prompt",
    )(h, norm_g.reshape(1, D), w_in, sinks.reshape(1, ATT_QH), w_out, final_norm.reshape(1, D),
      k, k, v, v, *tables)


def _hgrn_decode_proj_kernel(layer, h_ref, ng_ref, win_ref, lbl_ref, q_ref, f_ref, k_ref, v_ref, g_ref):
    u = _mm(_rms_scale(h_ref[...], ng_ref[...]), win_ref[...])
    lb = _lower_bound(lbl_ref[...], layer)
    q_ref[...] = _silu(u[:, :HG_KEY])
    fg, kin = _forget_gates(u[:, HG_KEY:2 * HG_KEY], lb)
    f_ref[...] = fg
    k_ref[...] = kin
    v_ref[...] = u[:, 2 * HG_KEY:2 * HG_KEY + HG_VAL]
    g_ref[...] = _silu(u[:, 2 * HG_KEY + HG_VAL:])


def _col(row):
    n = row.shape[1]
    return jnp.broadcast_to(row, (n, n)).T


def _hgrn_decode_state_kernel(s_ref, q_ref, f_ref, k_ref, v_ref, snew_ref, o_ref):
    for j in range(s_ref.shape[0]):
        for h in range(HG_HEADS):
            sl = slice(h * HG_DK, (h + 1) * HG_DK)
            f_c = _col(f_ref[j:j + 1, sl])
            k_c = _col(k_ref[j:j + 1, sl])
            q_c = _col(q_ref[j:j + 1, sl])
            v_r = v_ref[j:j + 1, h * HG_DV:(h + 1) * HG_DV]
            s_new = f_c * s_ref[j, h] + k_c * v_r
            snew_ref[j, h] = s_new
            o_ref[j:j + 1, h * HG_DV:(h + 1) * HG_DV] = jnp.sum(q_c * s_new, axis=0, keepdims=True)


def _hgrn_decode_out_kernel(h_ref, o_ref, g_ref, gn_ref, wout_ref, out_ref):
    gn = gn_ref[...]
    gated = []
    for h in range(HG_HEADS):
        sl = slice(h * HG_DV, (h + 1) * HG_DV)
        gated.append(_rms_scale(o_ref[:, sl], gn) * g_ref[:, sl])
    out_ref[...] = h_ref[...] + _mm(jnp.concatenate(gated, axis=1), wout_ref[...])


def _whole(shape):
    return pl.BlockSpec(shape, lambda *_: (0,) * len(shape))


def _hgrn_decode_layer(h, s0, norm_g, w_in, lb_logits, layer, gnorm, w_out):
    B, D = h.shape
    n_a = lb_logits.shape[0]
    vec = jax.ShapeDtypeStruct((B, HG_KEY), F32)
    q, f, k, v, g = pl.pallas_call(
        functools.partial(_hgrn_decode_proj_kernel, layer),
        grid=(1,),
        in_specs=[_whole((B, D)), _whole((1, D)), _whole(w_in.shape), _whole((n_a, HG_KEY))],
        out_specs=[_whole((B, HG_KEY))] * 5,
        out_shape=[vec] * 5,
        compiler_params=_cparams("arbitrary"),
        name=f"hgrn_decode_proj_l{layer}",
    )(h, norm_g.reshape(1, D), w_in, lb_logits)

    bb = DEC_BLOCK
    row_blk = pl.BlockSpec((bb, HG_KEY), lambda i: (i, 0))
    st_blk = pl.BlockSpec((bb, HG_HEADS, HG_DK, HG_DV), lambda i: (i, 0, 0, 0))
    s_new, o = pl.pallas_call(
        _hgrn_decode_state_kernel,
        grid=(B // bb,),
        in_specs=[st_blk, row_blk, row_blk, row_blk, row_blk],
        out_specs=[st_blk, row_blk],
        out_shape=[jax.ShapeDtypeStruct(s0.shape, F32), vec],
        compiler_params=_cparams("arbitrary"),
        name=f"hgrn_decode_state_l{layer}",
    )(s0, q, f, k, v)

    h_new = pl.pallas_call(
        _hgrn_decode_out_kernel,
        grid=(1,),
        in_specs=[_whole((B, D)), _whole((B, HG_VAL)), _whole((B, HG_VAL)), _whole((1, HG_DV)),
                  _whole(w_out.shape)],
        out_specs=_whole((B, D)),
        out_shape=jax.ShapeDtypeStruct((B, D), F32),
        compiler_params=_cparams("arbitrary"),
        name=f"hgrn_decode_out_l{layer}",
    )(h, o, g, gnorm.reshape(1, HG_DV), w_out)
    return h_new, s_new


def _swa_decode_proj_kernel(h_ref, ng_ref, win_ref, cos_ref, sup_ref, sdn_ref, q_ref, g_ref):
    u = _mm(_rms_scale(h_ref[...], ng_ref[...]), win_ref[...])
    q = _rope_cols(u[:, :ATT_QD], cos_ref[...], sup_ref[...], sdn_ref[...])
    q_ref[...] = q * (ATT_HD ** -0.5)
    g_ref[...] = _silu(u[:, ATT_QD:])


def _swa_decode_attn_kernel(q_ref, kc_ref, vc_ref, kn_ref, vn_ref, sinks_ref, o_ref):
    for j in range(q_ref.shape[0]):
        kc, vc = kc_ref[j], vc_ref[j]
        kn, vn = kn_ref[j], vn_ref[j]
        for kvh in range(ATT_KVH):
            qg = q_ref[j, kvh * ATT_GROUP:(kvh + 1) * ATT_GROUP, :]
            zeros = jnp.zeros_like(qg)
            qp = jnp.concatenate([qg, zeros] if kvh == 0 else [zeros, qg], axis=1)
            s_c = _mm_nt(qp, kc)
            s_n = jnp.sum(qp.astype(BF16).astype(F32) * kn.astype(BF16).astype(F32),
                          axis=-1, keepdims=True)
            sink = sinks_ref[kvh * ATT_GROUP:(kvh + 1) * ATT_GROUP, :]
            m = jnp.maximum(jnp.maximum(jnp.max(s_c, axis=-1, keepdims=True), s_n), sink)
            p_c = jnp.exp(s_c - m)
            p_n = jnp.exp(s_n - m)
            denom = jnp.sum(p_c, axis=-1, keepdims=True) + p_n + jnp.exp(sink - m)
            inv = 1.0 / denom
            o = _mm(p_c * inv, vc) + (p_n * inv).astype(BF16).astype(F32) * vn.astype(BF16).astype(F32)
            o_ref[j, kvh * ATT_GROUP:(kvh + 1) * ATT_GROUP, :] = (
                o[:, :ATT_HD] if kvh == 0 else o[:, ATT_HD:])


def _swa_decode_out_kernel(final, h_ref, o_ref, g_ref, wout_ref, fin_ref, out_ref):
    y = h_ref[...] + _mm(o_ref[...] * g_ref[...], wout_ref[...])
    if final:
        y = _rms_scale(y, fin_ref[...])
    out_ref[...] = y


def _swa_decode_layer(h, k_new, v_new, cache_k, cache_v, tables, norm_g, w_in, sinks, w_out,
                      final_norm, final):
    B, D = h.shape
    W = cache_k.shape[1]
    qd = jax.ShapeDtypeStruct((B, ATT_QD), F32)
    q, g = pl.pallas_call(
        _swa_decode_proj_kernel,
        grid=(1,),
        in_specs=[_whole((B, D)), _whole((1, D)), _whole(w_in.shape)] + [_whole((1, LANES))] * 3,
        out_specs=[_whole((B, ATT_QD))] * 2,
        out_shape=[qd, qd],
        compiler_params=_cparams("arbitrary"),
        name="swa_decode_proj",
    )(h, norm_g.reshape(1, D), w_in, *tables)

    bb = DEC_BLOCK
    o = pl.pallas_call(
        _swa_decode_attn_kernel,
        grid=(B // bb,),
        in_specs=[pl.BlockSpec((bb, ATT_QH, ATT_HD), lambda i: (i, 0, 0)),
                  pl.BlockSpec((bb, W, ATT_KVD), lambda i: (i, 0, 0)),
                  pl.BlockSpec((bb, W, ATT_KVD), lambda i: (i, 0, 0)),
                  pl.BlockSpec((bb, 1, ATT_KVD), lambda i: (i, 0, 0)),
                  pl.BlockSpec((bb, 1, ATT_KVD), lambda i: (i, 0, 0)),
                  pl.BlockSpec((ATT_QH, 1), lambda i: (0, 0))],
        out_specs=pl.BlockSpec((bb, ATT_QH, ATT_HD), lambda i: (i, 0, 0)),
        out_shape=jax.ShapeDtypeStruct((B, ATT_QH, ATT_HD), F32),
        compiler_params=_cparams("arbitrary"),
        name="swa_decode_attn",
    )(q.reshape(B, ATT_QH, ATT_HD), cache_k.reshape(B, W, ATT_KVD), cache_v.reshape(B, W, ATT_KVD),
      k_new.reshape(B, 1, ATT_KVD), v_new.reshape(B, 1, ATT_KVD), sinks.reshape(ATT_QH, 1))

    return pl.pallas_call(
        functools.partial(_swa_decode_out_kernel, final),
        grid=(1,),
        in_specs=[_whole((B, D)), _whole((B, ATT_QD)), _whole((B, ATT_QD)), _whole(w_out.shape),
                  _whole((1, D))],
        out_specs=_whole((B, D)),
        out_shape=jax.ShapeDtypeStruct((B, D), F32),
        compiler_params=_cparams("arbitrary"),
        name="swa_decode_out_final" if final else "swa_decode_out",
    )(h, o.reshape(B, ATT_QD), g, w_out, final_norm.reshape(1, D))


def kernel(x_prompt, x_sample, state_hgrn, cache_k_win, cache_v_win, a_norm, a_w_in, a_lb_logits,
           a_gnorm, a_w_out, kv_norm, w_kv, b_norm, b_w_in, b_sinks, b_w_out, final_norm):
    B, T, D = x_prompt.shape
    BD, TD, _ = x_sample.shape
    n_a = a_w_in.shape[0]
    n_b = b_w_in.shape[0]
    a_w_in_b, a_w_out_b = a_w_in.astype(BF16), a_w_out.astype(BF16)
    b_w_in_b, b_w_out_b = b_w_in.astype(BF16), b_w_out.astype(BF16)
    w_kv_b = w_kv.astype(BF16)

    tab_p = _rope_tables(np.arange(T))
    h = x_prompt
    st_p = []
    for l in range(n_a):
        h, s = _hgrn_prompt_layer(h, a_norm[l], a_w_in_b[l], a_lb_logits, l, a_gnorm[l], a_w_out_b[l])
        st_p.append(s)
    k_p, v_p = _shared_kv(h, kv_norm, w_kv_b, tab_p)
    for l in range(n_b):
        h = _swa_prompt_layer(h, k_p, v_p, tab_p, b_norm[l], b_w_in_b[l], b_sinks[l], b_w_out_b[l],
                              final_norm, l == n_b - 1)
    y_prompt = h
    w_keep = min(WINDOW, T)
    k_win = k_p[:, T - w_keep:].reshape(B, w_keep, ATT_KVH, ATT_HD)
    v_win = v_p[:, T - w_keep:].reshape(B, w_keep, ATT_KVH, ATT_HD)

    tab_s = _rope_tables(PAST_LEN + np.arange(TD))
    hs = x_sample.reshape(BD * TD, D)
    st_s = []
    for l in range(n_a):
        hs, s = _hgrn_decode_layer(hs, state_hgrn[l], a_norm[l], a_w_in_b[l], a_lb_logits, l,
                                   a_gnorm[l], a_w_out_b[l])
        st_s.append(s)
    k_s, v_s = _shared_kv(hs.reshape(1, BD, D), kv_norm, w_kv_b,
                          tuple(jnp.broadcast_to(t, (BD, LANES)) for t in tab_s))
    k_s, v_s = k_s.reshape(BD, ATT_KVD), v_s.reshape(BD, ATT_KVD)
    for l in range(n_b):
        hs = _swa_decode_layer(hs, k_s, v_s, cache_k_win, cache_v_win, tab_s, b_norm[l], b_w_in_b[l],
                               b_sinks[l], b_w_out_b[l], final_norm, l == n_b - 1)
    y_sample = hs.reshape(BD, TD, D)

    return (y_prompt, y_sample, jnp.stack(st_p), jnp.stack(st_s), k_win, v_win,
            k_s.reshape(BD, TD, ATT_KVH, ATT_HD), v_s.reshape(BD, TD, ATT_KVH, ATT_HD))
```

```python
import functools

import numpy as np
import jax
import jax.numpy as jnp
from jax import lax
from jax.experimental import pallas as pl
from jax.experimental.pallas import tpu as pltpu

F32 = jnp.float32
BF16 = jnp.bfloat16

D_MODEL = 1024
HG_HEADS = 8
HG_DK = 128
HG_DV = 128
HG_KEY = HG_HEADS * HG_DK
HG_VAL = HG_HEADS * HG_DV
HG_CHUNK = 64
HG_HALF = HG_CHUNK // 2
HG_SUB = 16
HG_MAX_HALF_DECAY = 60.0
ATT_HD = 64
ATT_QH = 16
ATT_KVH = 2
ATT_GROUP = ATT_QH // ATT_KVH
ATT_QD = ATT_QH * ATT_HD
ATT_KVD = ATT_KVH * ATT_HD
WINDOW = 128
ROPE_THETA = 500000.0
ROT_DIM = ATT_HD // 4
NORM_EPS = 1e-6
MASK_VALUE = -1e30
NEG_BIG = -1e30
PAST_LEN = 8192

SUBLANES = 8
LANES = 128
PROMPT_TILE = 256
DEC_BLOCK = 8
VMEM_LIMIT = 48 * 1024 * 1024


def _cparams(*sem):
    return pltpu.CompilerParams(dimension_semantics=sem, vmem_limit_bytes=VMEM_LIMIT)


def _mm(a, b):
    return jnp.dot(a.astype(BF16), b.astype(BF16), preferred_element_type=F32)


def _mm_nt(a, b):
    return lax.dot_general(a.astype(BF16), b.astype(BF16), (((1,), (1,)), ((), ())),
                           preferred_element_type=F32)


def _mm_tn(a, b):
    return lax.dot_general(a.astype(BF16), b.astype(BF16), (((0,), (0,)), ((), ())),
                           preferred_element_type=F32)


def _rms_scale(x, g):
    ms = jnp.mean(x * x, axis=-1, keepdims=True)
    return x * lax.rsqrt(ms + NORM_EPS) * g


def _silu(x):
    return x * (1.0 / (1.0 + jnp.exp(-x)))


def _lower_bound(lbl, layer):
    n = lbl.shape[0]
    rows = [lbl[i:i + 1, :] for i in range(n)]
    m = functools.reduce(jnp.maximum, rows)
    es = [jnp.exp(r - m) for r in rows]
    inv = 1.0 / functools.reduce(jnp.add, es)
    ps = [e * inv for e in es]
    return functools.reduce(jnp.add, ps[:layer + 1]) - ps[0]


def _forget_gates(f, lb):
    e = jnp.exp(-f)
    r = 1.0 / (1.0 + e)
    return lb + (1.0 - lb) * r, (1.0 - lb) * (e * r)


def _chunk_cumsum(x):
    rows = x.shape[0]
    pos = lax.broadcasted_iota(jnp.int32, x.shape, 0) % HG_CHUNK
    s = 1
    while s < HG_CHUNK:
        x = x + jnp.where(pos >= s, pltpu.roll(x, s, axis=0), 0.0)
        s *= 2
    del rows
    return x


def _diag_blocks(q, k, v, b):
    sub_iota = lax.broadcasted_iota(jnp.int32, (SUBLANES, LANES), 0)
    outs = []
    for blk in range(HG_CHUNK // HG_SUB):
        base = blk * HG_SUB
        groups = HG_SUB // SUBLANES
        qs = [q[base + g * SUBLANES: base + (g + 1) * SUBLANES] for g in range(groups)]
        bs = [b[base + g * SUBLANES: base + (g + 1) * SUBLANES] for g in range(groups)]
        accs = [jnp.zeros((SUBLANES, HG_DV), F32) for _ in range(groups)]
        for s in range(HG_SUB):
            row = base + s
            k_s = k[row:row + 1]
            b_s = b[row:row + 1]
            v_s = v[row:row + 1]
            for g in range(s // SUBLANES, groups):
                d = bs[g] - b_s
                if g == s // SUBLANES:
                    d = jnp.where(sub_iota >= (s % SUBLANES), d, NEG_BIG)
                p = qs[g] * (k_s * jnp.exp(d))
                a = jnp.sum(p, axis=-1, keepdims=True)
                accs[g] = accs[g] + a * v_s
        outs.extend(accs)
    return jnp.concatenate(outs, axis=0)


def _hgrn_chunk(q, k, v, b, st):
    C, H = HG_CHUNK, HG_HALF
    o_inter = _mm_nt(q * jnp.exp(b), st)
    r0 = b[H - 1:H]
    a0 = _mm_nt(q[H:] * jnp.exp(b[H:] - r0), k[:H] * jnp.exp(r0 - b[:H]))
    o_hi = _mm(a0, v[:H])
    Q = HG_SUB
    o_q = []
    for base in (0, H):
        r1 = b[base + Q - 1: base + Q]
        a1 = _mm_nt(q[base + Q: base + 2 * Q] * jnp.exp(b[base + Q: base + 2 * Q] - r1),
                    k[base: base + Q] * jnp.exp(r1 - b[base: base + Q]))
        o_q.append(_mm(a1, v[base: base + Q]))
    zero = jnp.zeros((Q, HG_DV), F32)
    o_off = jnp.concatenate([zero, o_q[0], o_hi[:Q], o_hi[Q:] + o_q[1]], axis=0)
    o = o_inter + o_off + _diag_blocks(q, k, v, b)
    b_end = b[C - 1:C]
    st_new = st * jnp.exp(b_end) + _mm_tn(v, k * jnp.exp(b_end - b))
    return o, st_new


def _hgrn_chunk_bounded(q, k, v, b, st):
    C, H = HG_CHUNK, HG_HALF
    b0, b1 = b[:H], b[H:]
    r = b0[H - 1:H]
    eq0, eq1 = jnp.exp(b0), jnp.exp(b1 - r)
    e_mid, e_hi = eq0[H - 1:H], eq1[H - 1:H]
    qt0, qt1 = q[:H] * eq0, q[H:] * eq1
    kt0, kt1 = k[:H] * jnp.exp(-b0), k[H:] * jnp.exp(r - b1)
    kt0m = kt0 * e_mid
    t_i = lax.broadcasted_iota(jnp.int32, (H, C), 0)
    s_i = lax.broadcasted_iota(jnp.int32, (H, C), 1)
    a0 = jnp.where(s_i <= t_i, _mm_nt(qt0, jnp.concatenate([kt0, kt1], axis=0)), 0.0)
    a1 = jnp.where(s_i <= t_i + H, _mm_nt(qt1, jnp.concatenate([kt0m, kt1], axis=0)), 0.0)
    o = (_mm_nt(jnp.concatenate([qt0, qt1 * e_mid], axis=0), st)
         + _mm(jnp.concatenate([a0, a1], axis=0), v))
    kd = jnp.concatenate([kt0m * e_hi, kt1 * e_hi], axis=0)
    st_new = st * (e_mid * e_hi) + _mm_tn(v, kd)
    return o, st_new


def _max_half_decay(b_s, tile):
    n = tile // HG_CHUNK
    worst = None
    for h in range(HG_HEADS):
        mid = b_s[h, pl.ds(HG_HALF - 1, n, stride=HG_CHUNK), :]
        end = b_s[h, pl.ds(HG_CHUNK - 1, n, stride=HG_CHUNK), :]
        w = jnp.maximum(-mid, mid - end)
        worst = w if worst is None else jnp.maximum(worst, w)
    return jnp.max(worst)


def _hgrn_prompt_kernel(layer, h_ref, ng_ref, win_ref, lbl_ref, gn_ref, wout_ref,
                        out_ref, sfin_ref, q_s, k_s, v_s, b_s, o_s, st_s):
    t = pl.program_id(1)
    nt = pl.num_programs(1)
    tile = h_ref.shape[1]

    @pl.when(t == 0)
    def _():
        st_s[...] = jnp.zeros_like(st_s)

    x = h_ref[0]
    u = _mm(_rms_scale(x, ng_ref[...]), win_ref[...])
    lb = _lower_bound(lbl_ref[...], layer)
    for h in range(HG_HEADS):
        sl = slice(h * HG_DK, (h + 1) * HG_DK)
        q_s[h] = _silu(u[:, sl])
        fg, kin = _forget_gates(u[:, HG_KEY + h * HG_DK: HG_KEY + (h + 1) * HG_DK], lb[:, sl])
        k_s[h] = kin
        b_s[h] = _chunk_cumsum(jnp.log(fg))
        v_s[h] = u[:, 2 * HG_KEY + h * HG_DV: 2 * HG_KEY + (h + 1) * HG_DV]

    def chunk_head(chunk_fn, c, h):
        rows = pl.ds(pl.multiple_of(c * HG_CHUNK, HG_CHUNK), HG_CHUNK)
        o, st_new = chunk_fn(q_s[h, rows, :], k_s[h, rows, :], v_s[h, rows, :],
                             b_s[h, rows, :], st_s[h])
        o_s[h, rows, :] = o
        st_s[h] = st_new

    def bounded_body(c, carry):
        for h in range(HG_HEADS):
            chunk_head(_hgrn_chunk_bounded, c, h)
        return carry

    def general_body(i, carry):
        chunk_head(_hgrn_chunk, i // HG_HEADS, i % HG_HEADS)
        return carry

    bounded = _max_half_decay(b_s, tile) < HG_MAX_HALF_DECAY

    @pl.when(bounded)
    def _():
        lax.fori_loop(0, tile // HG_CHUNK, bounded_body, 0)

    @pl.when(jnp.logical_not(bounded))
    def _():
        lax.fori_loop(0, (tile // HG_CHUNK) * HG_HEADS, general_body, 0)

    gn = gn_ref[...]
    gated = []
    for h in range(HG_HEADS):
        g = u[:, 2 * HG_KEY + HG_VAL + h * HG_DV: 2 * HG_KEY + HG_VAL + (h + 1) * HG_DV]
        gated.append(_rms_scale(o_s[h], gn) * _silu(g))
    y = _mm(jnp.concatenate(gated, axis=1), wout_ref[...])
    out_ref[0] = x + y

    @pl.when(t == nt - 1)
    def _():
        for h in range(HG_HEADS):
            sfin_ref[0, h] = st_s[h].T


def _hgrn_prompt_layer(h, norm_g, w_in, lb_logits, layer, gnorm, w_out):
    B, T, D = h.shape
    tile = min(PROMPT_TILE, T)
    n_a = lb_logits.shape[0]
    const = lambda b, t: (0, 0)
    head_scratch = pltpu.VMEM((HG_HEADS, tile, HG_DK), F32)
    return pl.pallas_call(
        functools.partial(_hgrn_prompt_kernel, layer),
        grid=(B, T // tile),
        in_specs=[
            pl.BlockSpec((1, tile, D), lambda b, t: (b, t, 0)),
            pl.BlockSpec((1, D), const),
            pl.BlockSpec(w_in.shape, const),
            pl.BlockSpec((n_a, HG_KEY), const),
            pl.BlockSpec((1, HG_DV), const),
            pl.BlockSpec(w_out.shape, const),
        ],
        out_specs=[
            pl.BlockSpec((1, tile, D), lambda b, t: (b, t, 0)),
            pl.BlockSpec((1, HG_HEADS, HG_DK, HG_DV), lambda b, t: (b, 0, 0, 0)),
        ],
        out_shape=[
            jax.ShapeDtypeStruct((B, T, D), F32),
            jax.ShapeDtypeStruct((B, HG_HEADS, HG_DK, HG_DV), F32),
        ],
        scratch_shapes=[head_scratch] * 5 + [pltpu.VMEM((HG_HEADS, HG_DV, HG_DK), F32)],
        compiler_params=_cparams("arbitrary", "arbitrary"),
        name=f"hgrn_prompt_l{layer}",
    )(h, norm_g.reshape(1, D), w_in, lb_logits, gnorm.reshape(1, HG_DV), w_out)


def _rope_tables(pos):
    half = ROT_DIM // 2
    pos = np.asarray(pos, np.float64)
    inv_freq = 1.0 / (ROPE_THETA ** (np.arange(half, dtype=np.float64) * 2.0 / ROT_DIM))
    ang = (pos[:, None].astype(np.float32) * inv_freq[None, :].astype(np.float32)).astype(np.float64)
    cos, sin = np.cos(ang), np.sin(ang)
    n = pos.shape[0]
    c = np.ones((n, ATT_HD)); s_up = np.zeros((n, ATT_HD)); s_dn = np.zeros((n, ATT_HD))
    c[:, :half] = cos; c[:, half:ROT_DIM] = cos
    s_dn[:, :half] = -sin
    s_up[:, half:ROT_DIM] = sin
    rep = LANES // ATT_HD
    tab = [np.tile(a, (1, rep)).astype(np.float32) for a in (c, s_up, s_dn)]
    return tuple(jnp.asarray(a) for a in tab)


def _rope_cols(x, cos, s_up, s_dn):
    half = ROT_DIM // 2
    cols = []
    for c in range(x.shape[1] // LANES):
        xc = x[:, c * LANES:(c + 1) * LANES]
        cols.append(xc * cos + pltpu.roll(xc, half, axis=1) * s_up
                    + pltpu.roll(xc, LANES - half, axis=1) * s_dn)
    return cols[0] if len(cols) == 1 else jnp.concatenate(cols, axis=1)


def _kv_kernel(h_ref, ng_ref, w_ref, cos_ref, sup_ref, sdn_ref, k_ref, v_ref):
    x = h_ref[0]
    u = _mm(_rms_scale(x, ng_ref[...]), w_ref[...])
    k_ref[0] = _rope_cols(u[:, :ATT_KVD], cos_ref[...], sup_ref[...], sdn_ref[...])
    v_ref[0] = u[:, ATT_KVD:]


def _shared_kv(h, kv_norm, w_kv, tables):
    B, T, D = h.shape
    tile = min(PROMPT_TILE, T)
    const = lambda b, t: (0, 0)
    tab = pl.BlockSpec((tile, LANES), lambda b, t: (t, 0))
    kv_spec = pl.BlockSpec((1, tile, ATT_KVD), lambda b, t: (b, t, 0))
    return pl.pallas_call(
        _kv_kernel,
        grid=(B, T // tile),
        in_specs=[pl.BlockSpec((1, tile, D), lambda b, t: (b, t, 0)),
                  pl.BlockSpec((1, D), const), pl.BlockSpec(w_kv.shape, const), tab, tab, tab],
        out_specs=[kv_spec, kv_spec],
        out_shape=[jax.ShapeDtypeStruct((B, T, ATT_KVD), F32)] * 2,
        compiler_params=_cparams("arbitrary", "arbitrary"),
        name="shared_kv",
    )(h, kv_norm.reshape(1, D), w_kv, *tables)


def _swa_block(q_cols, k_lo, k_hi, v_lo, v_hi, valid, sinks_ref):
    outs = []
    pairs_per_kv = ATT_GROUP // 2
    for c, qc in enumerate(q_cols):
        kvh = c // pairs_per_kv
        acc = None
        for half, (kk, vv) in enumerate(((k_lo[kvh], v_lo[kvh]), (k_hi[kvh], v_hi[kvh]))):
            head = 2 * c + half
            s = jnp.where(valid, _mm_nt(qc, kk), MASK_VALUE)
            sink = sinks_ref[0, head]
            m = jnp.maximum(jnp.max(s, axis=-1, keepdims=True), sink)
            p = jnp.where(valid, jnp.exp(s - m), 0.0)
            denom = jnp.sum(p, axis=-1, keepdims=True) + jnp.exp(sink - m)
            o = _mm(p * (1.0 / denom), vv)
            acc = o if acc is None else acc + o
        outs.append(acc)
    return outs


def _split_kv_heads(a):
    lane = lax.broadcasted_iota(jnp.int32, a.shape, 1)
    low = lane < ATT_HD
    sw = pltpu.roll(a, ATT_HD, axis=1)
    lo = [jnp.where(low, a, 0.0), jnp.where(low, sw, 0.0)]
    hi = [jnp.where(low, 0.0, sw), jnp.where(low, 0.0, a)]
    return lo, hi


def _swa_prompt_kernel(final, h_ref, ng_ref, win_ref, sinks_ref, wout_ref, fin_ref,
                       kp_ref, kc_ref, vp_ref, vc_ref, cos_ref, sup_ref, sdn_ref, out_ref):
    t = pl.program_id(1)
    tile = h_ref.shape[1]
    nblk = tile // WINDOW
    x = h_ref[0]
    u = _mm(_rms_scale(x, ng_ref[...]), win_ref[...])
    scale = ATT_HD ** -0.5
    cos, sup, sdn = cos_ref[...], sup_ref[...], sdn_ref[...]
    k_all = jnp.concatenate([kp_ref[0], kc_ref[0]], axis=0)
    v_all = jnp.concatenate([vp_ref[0], vc_ref[0]], axis=0)
    tq = lax.broadcasted_iota(jnp.int32, (WINDOW, 2 * WINDOW), 0)
    j = lax.broadcasted_iota(jnp.int32, (WINDOW, 2 * WINDOW), 1)
    rel = tq + WINDOW - j
    band = (rel >= 0) & (rel <= WINDOW)
    o_blocks = []
    for blk in range(nblk):
        rows = slice(blk * WINDOW, (blk + 1) * WINDOW)
        keys = slice(blk * WINDOW, (blk + 2) * WINDOW)
        k_lo, k_hi = _split_kv_heads(k_all[keys])
        v_lo, v_hi = _split_kv_heads(v_all[keys])
        if blk == 0:
            valid = band & (j >= WINDOW * (t == 0).astype(jnp.int32))
        else:
            valid = band
        q_cols = []
        for c in range(ATT_QD // LANES):
            qc = u[rows, c * LANES:(c + 1) * LANES]
            q_cols.append(_rope_cols(qc, cos[rows], sup[rows], sdn[rows]) * scale)
        o_blocks.append(jnp.concatenate(
            _swa_block(q_cols, k_lo, k_hi, v_lo, v_hi, valid, sinks_ref), axis=1))
    o = jnp.concatenate(o_blocks, axis=0) if nblk > 1 else o_blocks[0]
    y = x + _mm(o * _silu(u[:, ATT_QD:]), wout_ref[...])
    if final:
        y = _rms_scale(y, fin_ref[...])
    out_ref[0] = y


def _swa_prompt_layer(h, k, v, tables, norm_g, w_in, sinks, w_out, final_norm, final):
    B, T, D = h.shape
    tile = min(PROMPT_TILE, T)
    per = tile // WINDOW
    const = lambda b, t: (0, 0)
    tab = pl.BlockSpec((tile, LANES), lambda b, t: (t, 0))
    prev = pl.BlockSpec((1, WINDOW, ATT_KVD), lambda b, t: (b, jnp.maximum(t * per - 1, 0), 0))
    cur = pl.BlockSpec((1, tile, ATT_KVD), lambda b, t: (b, t, 0))
    return pl.pallas_call(
        functools.partial(_swa_prompt_kernel, final),
        grid=(B, T // tile),
        in_specs=[
            pl.BlockSpec((1, tile, D), lambda b, t: (b, t, 0)),
            pl.BlockSpec((1, D), const),
            pl.BlockSpec(w_in.shape, const),
            pl.BlockSpec(memory_space=pltpu.SMEM),
            pl.BlockSpec(w_out.shape, const),
            pl.BlockSpec((1, D), const),
            prev, cur, prev, cur, tab, tab, tab,
        ],
        out_specs=pl.BlockSpec((1, tile, D), lambda b, t: (b, t, 0)),
        out_shape=jax.ShapeDtypeStruct((B, T, D), F32),
        compiler_params=_cparams("arbitrary", "arbitrary"),
        name="swa_prompt_final" if final else "swa_prompt",
    )(h, norm_g.reshape(1, D), w_in, sinks.reshape(1, ATT_QH), w_out, final_norm.reshape(1, D),
      k, k, v, v, *tables)


def _hgrn_decode_proj_kernel(layer, h_ref, ng_ref, win_ref, lbl_ref, q_ref, f_ref, k_ref, v_ref, g_ref):
    u = _mm(_rms_scale(h_ref[...], ng_ref[...]), win_ref[...])
    lb = _lower_bound(lbl_ref[...], layer)
    q_ref[...] = _silu(u[:, :HG_KEY])
    fg, kin = _forget_gates(u[:, HG_KEY:2 * HG_KEY], lb)
    f_ref[...] = fg
    k_ref[...] = kin
    v_ref[...] = u[:, 2 * HG_KEY:2 * HG_KEY + HG_VAL]
    g_ref[...] = _silu(u[:, 2 * HG_KEY + HG_VAL:])


def _col(row):
    n = row.shape[1]
    return jnp.broadcast_to(row, (n, n)).T


def _hgrn_decode_state_kernel(s_ref, q_ref, f_ref, k_ref, v_ref, *rest):
    snew_ref, o_ref = rest[-2:]
    for j in range(s_ref.shape[1]):
        for h in range(HG_HEADS):
            sl = slice(h * HG_DK, (h + 1) * HG_DK)
            f_c = _col(f_ref[j:j + 1, sl])
            k_c = _col(k_ref[j:j + 1, sl])
            q_c = _col(q_ref[j:j + 1, sl])
            v_r = v_ref[j:j + 1, h * HG_DV:(h + 1) * HG_DV]
            s_new = f_c * s_ref[0, j, h] + k_c * v_r
            snew_ref[0, j, h] = s_new
            o_ref[j:j + 1, h * HG_DV:(h + 1) * HG_DV] = jnp.sum(q_c * s_new, axis=0, keepdims=True)


def _hgrn_decode_out_kernel(h_ref, o_ref, g_ref, gn_ref, wout_ref, out_ref):
    gn = gn_ref[...]
    gated = []
    for h in range(HG_HEADS):
        sl = slice(h * HG_DV, (h + 1) * HG_DV)
        gated.append(_rms_scale(o_ref[:, sl], gn) * g_ref[:, sl])
    out_ref[...] = h_ref[...] + _mm(jnp.concatenate(gated, axis=1), wout_ref[...])


def _whole(shape):
    return pl.BlockSpec(shape, lambda *_: (0,) * len(shape))


def _hgrn_decode_layer(h, states, new_states, norm_g, w_in, lb_logits, layer, gnorm, w_out):
    B, D = h.shape
    n_a = lb_logits.shape[0]
    vec = jax.ShapeDtypeStruct((B, HG_KEY), F32)
    q, f, k, v, g = pl.pallas_call(
        functools.partial(_hgrn_decode_proj_kernel, layer),
        grid=(1,),
        in_specs=[_whole((B, D)), _whole((1, D)), _whole(w_in.shape), _whole((n_a, HG_KEY))],
        out_specs=[_whole((B, HG_KEY))] * 5,
        out_shape=[vec] * 5,
        compiler_params=_cparams("arbitrary"),
        name=f"hgrn_decode_proj_l{layer}",
    )(h, norm_g.reshape(1, D), w_in, lb_logits)

    bb = DEC_BLOCK
    row_blk = pl.BlockSpec((bb, HG_KEY), lambda i: (i, 0))
    st_blk = pl.BlockSpec((1, bb, HG_HEADS, HG_DK, HG_DV), lambda i: (layer, i, 0, 0, 0))
    carried = [] if new_states is None else [new_states]
    s_new, o = pl.pallas_call(
        _hgrn_decode_state_kernel,
        grid=(B // bb,),
        in_specs=[st_blk, row_blk, row_blk, row_blk, row_blk] + [pl.BlockSpec(memory_space=pl.ANY)] * len(carried),
        out_specs=[st_blk, row_blk],
        out_shape=[jax.ShapeDtypeStruct(states.shape, F32), vec],
        input_output_aliases={5: 0} if carried else {},
        compiler_params=_cparams("arbitrary"),
        name=f"hgrn_decode_state_l{layer}",
    )(states, q, f, k, v, *carried)

    h_new = pl.pallas_call(
        _hgrn_decode_out_kernel,
        grid=(1,),
        in_specs=[_whole((B, D)), _whole((B, HG_VAL)), _whole((B, HG_VAL)), _whole((1, HG_DV)),
                  _whole(w_out.shape)],
        out_specs=_whole((B, D)),
        out_shape=jax.ShapeDtypeStruct((B, D), F32),
        compiler_params=_cparams("arbitrary"),
        name=f"hgrn_decode_out_l{layer}",
    )(h, o, g, gnorm.reshape(1, HG_DV), w_out)
    return h_new, s_new


def _swa_decode_proj_kernel(h_ref, ng_ref, win_ref, cos_ref, sup_ref, sdn_ref, q_ref, g_ref):
    u = _mm(_rms_scale(h_ref[...], ng_ref[...]), win_ref[...])
    q = _rope_cols(u[:, :ATT_QD], cos_ref[...], sup_ref[...], sdn_ref[...])
    q_ref[...] = q * (ATT_HD ** -0.5)
    g_ref[...] = _silu(u[:, ATT_QD:])


def _swa_decode_attn_kernel(q_ref, kc_ref, vc_ref, kn_ref, vn_ref, sinks_ref, o_ref):
    for j in range(q_ref.shape[0]):
        kc, vc = kc_ref[j], vc_ref[j]
        kn, vn = kn_ref[j], vn_ref[j]
        for kvh in range(ATT_KVH):
            qg = q_ref[j, kvh * ATT_GROUP:(kvh + 1) * ATT_GROUP, :]
            zeros = jnp.zeros_like(qg)
            qp = jnp.concatenate([qg, zeros] if kvh == 0 else [zeros, qg], axis=1)
            s_c = _mm_nt(qp, kc)
            s_n = jnp.sum(qp.astype(BF16).astype(F32) * kn.astype(BF16).astype(F32),
                          axis=-1, keepdims=True)
            sink = sinks_ref[kvh * ATT_GROUP:(kvh + 1) * ATT_GROUP, :]
            m = jnp.maximum(jnp.maximum(jnp.max(s_c, axis=-1, keepdims=True), s_n), sink)
            p_c = jnp.exp(s_c - m)
            p_n = jnp.exp(s_n - m)
            denom = jnp.sum(p_c, axis=-1, keepdims=True) + p_n + jnp.exp(sink - m)
            inv = 1.0 / denom
            o = _mm(p_c * inv, vc) + (p_n * inv).astype(BF16).astype(F32) * vn.astype(BF16).astype(F32)
            o_ref[j, kvh * ATT_GROUP:(kvh + 1) * ATT_GROUP, :] = (
                o[:, :ATT_HD] if kvh == 0 else o[:, ATT_HD:])


def _swa_decode_out_kernel(final, h_ref, o_ref, g_ref, wout_ref, fin_ref, out_ref):
    y = h_ref[...] + _mm(o_ref[...] * g_ref[...], wout_ref[...])
    if final:
        y = _rms_scale(y, fin_ref[...])
    out_ref[...] = y


def _swa_decode_layer(h, k_new, v_new, cache_k, cache_v, tables, norm_g, w_in, sinks, w_out,
                      final_norm, final):
    B, D = h.shape
    W = cache_k.shape[1]
    qd = jax.ShapeDtypeStruct((B, ATT_QD), F32)
    q, g = pl.pallas_call(
        _swa_decode_proj_kernel,
        grid=(1,),
        in_specs=[_whole((B, D)), _whole((1, D)), _whole(w_in.shape)] + [_whole((1, LANES))] * 3,
        out_specs=[_whole((B, ATT_QD))] * 2,
        out_shape=[qd, qd],
        compiler_params=_cparams("arbitrary"),
        name="swa_decode_proj",
    )(h, norm_g.reshape(1, D), w_in, *tables)

    bb = DEC_BLOCK
    o = pl.pallas_call(
        _swa_decode_attn_kernel,
        grid=(B // bb,),
        in_specs=[pl.BlockSpec((bb, ATT_QH, ATT_HD), lambda i: (i, 0, 0)),
                  pl.BlockSpec((bb, W, ATT_KVD), lambda i: (i, 0, 0)),
                  pl.BlockSpec((bb, W, ATT_KVD), lambda i: (i, 0, 0)),
                  pl.BlockSpec((bb, 1, ATT_KVD), lambda i: (i, 0, 0)),
                  pl.BlockSpec((bb, 1, ATT_KVD), lambda i: (i, 0, 0)),
                  pl.BlockSpec((ATT_QH, 1), lambda i: (0, 0))],
        out_specs=pl.BlockSpec((bb, ATT_QH, ATT_HD), lambda i: (i, 0, 0)),
        out_shape=jax.ShapeDtypeStruct((B, ATT_QH, ATT_HD), F32),
        compiler_params=_cparams("arbitrary"),
        name="swa_decode_attn",
    )(q.reshape(B, ATT_QH, ATT_HD), cache_k.reshape(B, W, ATT_KVD), cache_v.reshape(B, W, ATT_KVD),
      k_new.reshape(B, 1, ATT_KVD), v_new.reshape(B, 1, ATT_KVD), sinks.reshape(ATT_QH, 1))

    return pl.pallas_call(
        functools.partial(_swa_decode_out_kernel, final),
        grid=(1,),
        in_specs=[_whole((B, D)), _whole((B, ATT_QD)), _whole((B, ATT_QD)), _whole(w_out.shape),
                  _whole((1, D))],
        out_specs=_whole((B, D)),
        out_shape=jax.ShapeDtypeStruct((B, D), F32),
        compiler_params=_cparams("arbitrary"),
        name="swa_decode_out_final" if final else "swa_decode_out",
    )(h, o.reshape(B, ATT_QD), g, w_out, final_norm.reshape(1, D))


def kernel(x_prompt, x_sample, state_hgrn, cache_k_win, cache_v_win, a_norm, a_w_in, a_lb_logits,
           a_gnorm, a_w_out, kv_norm, w_kv, b_norm, b_w_in, b_sinks, b_w_out, final_norm):
    B, T, D = x_prompt.shape
    BD, TD, _ = x_sample.shape
    n_a = a_w_in.shape[0]
    n_b = b_w_in.shape[0]
    a_w_in_b, a_w_out_b = a_w_in.astype(BF16), a_w_out.astype(BF16)
    b_w_in_b, b_w_out_b = b_w_in.astype(BF16), b_w_out.astype(BF16)
    w_kv_b = w_kv.astype(BF16)

    tab_p = _rope_tables(np.arange(T))
    h = x_prompt
    st_p = []
    for l in range(n_a):
        h, s = _hgrn_prompt_layer(h, a_norm[l], a_w_in_b[l], a_lb_logits, l, a_gnorm[l], a_w_out_b[l])
        st_p.append(s)
    k_p, v_p = _shared_kv(h, kv_norm, w_kv_b, tab_p)
    for l in range(n_b):
        h = _swa_prompt_layer(h, k_p, v_p, tab_p, b_norm[l], b_w_in_b[l], b_sinks[l], b_w_out_b[l],
                              final_norm, l == n_b - 1)
    y_prompt = h
    w_keep = min(WINDOW, T)
    k_win = k_p[:, T - w_keep:].reshape(B, w_keep, ATT_KVH, ATT_HD)
    v_win = v_p[:, T - w_keep:].reshape(B, w_keep, ATT_KVH, ATT_HD)

    tab_s = _rope_tables(PAST_LEN + np.arange(TD))
    hs = x_sample.reshape(BD * TD, D)
    st_s = None
    for l in range(n_a):
        hs, st_s = _hgrn_decode_layer(hs, state_hgrn, st_s, a_norm[l], a_w_in_b[l], a_lb_logits, l,
                                      a_gnorm[l], a_w_out_b[l])
    k_s, v_s = _shared_kv(hs.reshape(1, BD, D), kv_norm, w_kv_b,
                          tuple(jnp.broadcast_to(t, (BD, LANES)) for t in tab_s))
    k_s, v_s = k_s.reshape(BD, ATT_KVD), v_s.reshape(BD, ATT_KVD)
    for l in range(n_b):
        hs = _swa_decode_layer(hs, k_s, v_s, cache_k_win, cache_v_win, tab_s, b_norm[l], b_w_in_b[l],
                               b_sinks[l], b_w_out_b[l], final_norm, l == n_b - 1)
    y_sample = hs.reshape(BD, TD, D)

    return (y_prompt, y_sample, jnp.stack(st_p), st_s, k_win, v_win,
            k_s.reshape(BD, TD, ATT_KVH, ATT_HD), v_s.reshape(BD, TD, ATT_KVH, ATT_HD))
```

```python
import functools

import numpy as np
import jax
import jax.numpy as jnp
from jax import lax
from jax.experimental import pallas as pl
from jax.experimental.pallas import tpu as pltpu

F32 = jnp.float32
BF16 = jnp.bfloat16

D_MODEL = 1024
HG_HEADS = 8
HG_DK = 128
HG_DV = 128
HG_KEY = HG_HEADS * HG_DK
HG_VAL = HG_HEADS * HG_DV
HG_CHUNK = 64
HG_HALF = HG_CHUNK // 2
HG_SUB = 16
HG_MAX_HALF_DECAY = 60.0
ATT_HD = 64
ATT_QH = 16
ATT_KVH = 2
ATT_GROUP = ATT_QH // ATT_KVH
ATT_QD = ATT_QH * ATT_HD
ATT_KVD = ATT_KVH * ATT_HD
WINDOW = 128
ROPE_THETA = 500000.0
ROT_DIM = ATT_HD // 4
NORM_EPS = 1e-6
MASK_VALUE = -1e30
NEG_BIG = -1e30
PAST_LEN = 8192

SUBLANES = 8
LANES = 128
PROMPT_TILE = 256
DEC_BLOCK = 8
VMEM_LIMIT = 48 * 1024 * 1024


def _cparams(*sem):
    return pltpu.CompilerParams(dimension_semantics=sem, vmem_limit_bytes=VMEM_LIMIT)


def _mm(a, b):
    return jnp.dot(a.astype(BF16), b.astype(BF16), preferred_element_type=F32)


def _mm_nt(a, b):
    return lax.dot_general(a.astype(BF16), b.astype(BF16), (((1,), (1,)), ((), ())),
                           preferred_element_type=F32)


def _mm_tn(a, b):
    return lax.dot_general(a.astype(BF16), b.astype(BF16), (((0,), (0,)), ((), ())),
                           preferred_element_type=F32)


def _rms_scale(x, g):
    ms = jnp.mean(x * x, axis=-1, keepdims=True)
    return x * lax.rsqrt(ms + NORM_EPS) * g


def _silu(x):
    return x * (1.0 / (1.0 + jnp.exp(-x)))


def _lower_bound(lbl, layer):
    n = lbl.shape[0]
    rows = [lbl[i:i + 1, :] for i in range(n)]
    m = functools.reduce(jnp.maximum, rows)
    es = [jnp.exp(r - m) for r in rows]
    inv = 1.0 / functools.reduce(jnp.add, es)
    ps = [e * inv for e in es]
    return functools.reduce(jnp.add, ps[:layer + 1]) - ps[0]


def _forget_gates(f, lb):
    e = jnp.exp(-f)
    r = 1.0 / (1.0 + e)
    return lb + (1.0 - lb) * r, (1.0 - lb) * (e * r)


def _chunk_cumsum(x):
    rows = x.shape[0]
    pos = lax.broadcasted_iota(jnp.int32, x.shape, 0) % HG_CHUNK
    s = 1
    while s < HG_CHUNK:
        x = x + jnp.where(pos >= s, pltpu.roll(x, s, axis=0), 0.0)
        s *= 2
    del rows
    return x


def _diag_blocks(q, k, v, b):
    sub_iota = lax.broadcasted_iota(jnp.int32, (SUBLANES, LANES), 0)
    outs = []
    for blk in range(HG_CHUNK // HG_SUB):
        base = blk * HG_SUB
        groups = HG_SUB // SUBLANES
        qs = [q[base + g * SUBLANES: base + (g + 1) * SUBLANES] for g in range(groups)]
        bs = [b[base + g * SUBLANES: base + (g + 1) * SUBLANES] for g in range(groups)]
        accs = [jnp.zeros((SUBLANES, HG_DV), F32) for _ in range(groups)]
        for s in range(HG_SUB):
            row = base + s
            k_s = k[row:row + 1]
            b_s = b[row:row + 1]
            v_s = v[row:row + 1]
            for g in range(s // SUBLANES, groups):
                d = bs[g] - b_s
                if g == s // SUBLANES:
                    d = jnp.where(sub_iota >= (s % SUBLANES), d, NEG_BIG)
                p = qs[g] * (k_s * jnp.exp(d))
                a = jnp.sum(p, axis=-1, keepdims=True)
                accs[g] = accs[g] + a * v_s
        outs.extend(accs)
    return jnp.concatenate(outs, axis=0)


def _hgrn_chunk(q, k, v, b, st):
    C, H = HG_CHUNK, HG_HALF
    o_inter = _mm_nt(q * jnp.exp(b), st)
    r0 = b[H - 1:H]
    a0 = _mm_nt(q[H:] * jnp.exp(b[H:] - r0), k[:H] * jnp.exp(r0 - b[:H]))
    o_hi = _mm(a0, v[:H])
    Q = HG_SUB
    o_q = []
    for base in (0, H):
        r1 = b[base + Q - 1: base + Q]
        a1 = _mm_nt(q[base + Q: base + 2 * Q] * jnp.exp(b[base + Q: base + 2 * Q] - r1),
                    k[base: base + Q] * jnp.exp(r1 - b[base: base + Q]))
        o_q.append(_mm(a1, v[base: base + Q]))
    zero = jnp.zeros((Q, HG_DV), F32)
    o_off = jnp.concatenate([zero, o_q[0], o_hi[:Q], o_hi[Q:] + o_q[1]], axis=0)
    o = o_inter + o_off + _diag_blocks(q, k, v, b)
    b_end = b[C - 1:C]
    st_new = st * jnp.exp(b_end) + _mm_tn(v, k * jnp.exp(b_end - b))
    return o, st_new


def _hgrn_chunks_bounded(heads):
    C, H = HG_CHUNK, HG_HALF
    t_i = lax.broadcasted_iota(jnp.int32, (C, C), 0)
    s_i = lax.broadcasted_iota(jnp.int32, (C, C), 1)
    causal = s_i <= t_i
    first = []
    for q, k, v, b, st in heads:
        b0, b1 = b[:H], b[H:]
        r = b0[H - 1:H]
        eq0, eq1 = jnp.exp(b0), jnp.exp(b1 - r)
        e_mid, e_hi = eq0[H - 1:H], eq1[H - 1:H]
        qt0, qt1 = q[:H] * eq0, q[H:] * eq1
        kt0, kt1 = k[:H] * jnp.exp(-b0), k[H:] * jnp.exp(r - b1)
        kt0m = kt0 * e_mid
        s0 = _mm_nt(qt0, jnp.concatenate([kt0, kt1], axis=0))
        s1 = _mm_nt(qt1, jnp.concatenate([kt0m, kt1], axis=0))
        o_inter = _mm_nt(jnp.concatenate([qt0, qt1 * e_mid], axis=0), st)
        kd = jnp.concatenate([kt0m * e_hi, kt1 * e_hi], axis=0)
        st_new = st * (e_mid * e_hi) + _mm_tn(v, kd)
        first.append((s0, s1, o_inter, st_new))
    out = []
    for (q, k, v, b, st), (s0, s1, o_inter, st_new) in zip(heads, first):
        a = jnp.where(causal, jnp.concatenate([s0, s1], axis=0), 0.0)
        out.append((o_inter + _mm(a, v), st_new))
    return out


def _max_half_decay(b_s, tile):
    n = tile // HG_CHUNK
    worst = None
    for h in range(HG_HEADS):
        mid = b_s[h, pl.ds(HG_HALF - 1, n, stride=HG_CHUNK), :]
        end = b_s[h, pl.ds(HG_CHUNK - 1, n, stride=HG_CHUNK), :]
        w = jnp.maximum(-mid, mid - end)
        worst = w if worst is None else jnp.maximum(worst, w)
    return jnp.max(worst)


def _hgrn_prompt_kernel(layer, h_ref, ng_ref, win_ref, lbl_ref, gn_ref, wout_ref,
                        out_ref, sfin_ref, q_s, k_s, v_s, b_s, o_s, st_s):
    t = pl.program_id(1)
    nt = pl.num_programs(1)
    tile = h_ref.shape[1]

    @pl.when(t == 0)
    def _():
        st_s[...] = jnp.zeros_like(st_s)

    x = h_ref[0]
    u = _mm(_rms_scale(x, ng_ref[...]), win_ref[...])
    lb = _lower_bound(lbl_ref[...], layer)
    for h in range(HG_HEADS):
        sl = slice(h * HG_DK, (h + 1) * HG_DK)
        q_s[h] = _silu(u[:, sl])
        fg, kin = _forget_gates(u[:, HG_KEY + h * HG_DK: HG_KEY + (h + 1) * HG_DK], lb[:, sl])
        k_s[h] = kin
        b_s[h] = _chunk_cumsum(jnp.log(fg))
        v_s[h] = u[:, 2 * HG_KEY + h * HG_DV: 2 * HG_KEY + (h + 1) * HG_DV]

    def chunk_rows(c):
        return pl.ds(pl.multiple_of(c * HG_CHUNK, HG_CHUNK), HG_CHUNK)

    def load(c, h):
        rows = chunk_rows(c)
        return q_s[h, rows, :], k_s[h, rows, :], v_s[h, rows, :], b_s[h, rows, :], st_s[h]

    def bounded_body(c, carry):
        res = _hgrn_chunks_bounded([load(c, h) for h in range(HG_HEADS)])
        for h, (o, st_new) in enumerate(res):
            o_s[h, chunk_rows(c), :] = o
            st_s[h] = st_new
        return carry

    def general_body(i, carry):
        c, h = i // HG_HEADS, i % HG_HEADS
        o, st_new = _hgrn_chunk(*load(c, h))
        o_s[h, chunk_rows(c), :] = o
        st_s[h] = st_new
        return carry

    bounded = _max_half_decay(b_s, tile) < HG_MAX_HALF_DECAY

    @pl.when(bounded)
    def _():
        lax.fori_loop(0, tile // HG_CHUNK, bounded_body, 0)

    @pl.when(jnp.logical_not(bounded))
    def _():
        lax.fori_loop(0, (tile // HG_CHUNK) * HG_HEADS, general_body, 0)

    gn = gn_ref[...]
    gated = []
    for h in range(HG_HEADS):
        g = u[:, 2 * HG_KEY + HG_VAL + h * HG_DV: 2 * HG_KEY + HG_VAL + (h + 1) * HG_DV]
        gated.append(_rms_scale(o_s[h], gn) * _silu(g))
    y = _mm(jnp.concatenate(gated, axis=1), wout_ref[...])
    out_ref[0] = x + y

    @pl.when(t == nt - 1)
    def _():
        for h in range(HG_HEADS):
            sfin_ref[0, h] = st_s[h].T


def _hgrn_prompt_layer(h, norm_g, w_in, lb_logits, layer, gnorm, w_out):
    B, T, D = h.shape
    tile = min(PROMPT_TILE, T)
    n_a = lb_logits.shape[0]
    const = lambda b, t: (0, 0)
    head_scratch = pltpu.VMEM((HG_HEADS, tile, HG_DK), F32)
    return pl.pallas_call(
        functools.partial(_hgrn_prompt_kernel, layer),
        grid=(B, T // tile),
        in_specs=[
            pl.BlockSpec((1, tile, D), lambda b, t: (b, t, 0)),
            pl.BlockSpec((1, D), const),
            pl.BlockSpec(w_in.shape, const),
            pl.BlockSpec((n_a, HG_KEY), const),
            pl.BlockSpec((1, HG_DV), const),
            pl.BlockSpec(w_out.shape, const),
        ],
        out_specs=[
            pl.BlockSpec((1, tile, D), lambda b, t: (b, t, 0)),
            pl.BlockSpec((1, HG_HEADS, HG_DK, HG_DV), lambda b, t: (b, 0, 0, 0)),
        ],
        out_shape=[
            jax.ShapeDtypeStruct((B, T, D), F32),
            jax.ShapeDtypeStruct((B, HG_HEADS, HG_DK, HG_DV), F32),
        ],
        scratch_shapes=[head_scratch] * 5 + [pltpu.VMEM((HG_HEADS, HG_DV, HG_DK), F32)],
        compiler_params=_cparams("arbitrary", "arbitrary"),
        name=f"hgrn_prompt_l{layer}",
    )(h, norm_g.reshape(1, D), w_in, lb_logits, gnorm.reshape(1, HG_DV), w_out)


def _rope_tables(pos):
    half = ROT_DIM // 2
    pos = np.asarray(pos, np.float64)
    inv_freq = 1.0 / (ROPE_THETA ** (np.arange(half, dtype=np.float64) * 2.0 / ROT_DIM))
    ang = (pos[:, None].astype(np.float32) * inv_freq[None, :].astype(np.float32)).astype(np.float64)
    cos, sin = np.cos(ang), np.sin(ang)
    n = pos.shape[0]
    c = np.ones((n, ATT_HD)); s_up = np.zeros((n, ATT_HD)); s_dn = np.zeros((n, ATT_HD))
    c[:, :half] = cos; c[:, half:ROT_DIM] = cos
    s_dn[:, :half] = -sin
    s_up[:, half:ROT_DIM] = sin
    rep = LANES // ATT_HD
    tab = [np.tile(a, (1, rep)).astype(np.float32) for a in (c, s_up, s_dn)]
    return tuple(jnp.asarray(a) for a in tab)


def _rope_cols(x, cos, s_up, s_dn):
    half = ROT_DIM // 2
    cols = []
    for c in range(x.shape[1] // LANES):
        xc = x[:, c * LANES:(c + 1) * LANES]
        cols.append(xc * cos + pltpu.roll(xc, half, axis=1) * s_up
                    + pltpu.roll(xc, LANES - half, axis=1) * s_dn)
    return cols[0] if len(cols) == 1 else jnp.concatenate(cols, axis=1)


def _kv_kernel(h_ref, ng_ref, w_ref, cos_ref, sup_ref, sdn_ref, k_ref, v_ref):
    x = h_ref[0]
    u = _mm(_rms_scale(x, ng_ref[...]), w_ref[...])
    k_ref[0] = _rope_cols(u[:, :ATT_KVD], cos_ref[...], sup_ref[...], sdn_ref[...])
    v_ref[0] = u[:, ATT_KVD:]


def _shared_kv(h, kv_norm, w_kv, tables):
    B, T, D = h.shape
    tile = min(PROMPT_TILE, T)
    const = lambda b, t: (0, 0)
    tab = pl.BlockSpec((tile, LANES), lambda b, t: (t, 0))
    kv_spec = pl.BlockSpec((1, tile, ATT_KVD), lambda b, t: (b, t, 0))
    return pl.pallas_call(
        _kv_kernel,
        grid=(B, T // tile),
        in_specs=[pl.BlockSpec((1, tile, D), lambda b, t: (b, t, 0)),
                  pl.BlockSpec((1, D), const), pl.BlockSpec(w_kv.shape, const), tab, tab, tab],
        out_specs=[kv_spec, kv_spec],
        out_shape=[jax.ShapeDtypeStruct((B, T, ATT_KVD), F32)] * 2,
        compiler_params=_cparams("arbitrary", "arbitrary"),
        name="shared_kv",
    )(h, kv_norm.reshape(1, D), w_kv, *tables)


PAIRS_PER_KV = ATT_GROUP // 2


def _swa_attention(blocks, sinks_ref):
    low = lax.broadcasted_iota(jnp.int32, (WINDOW, LANES), 1) < ATT_HD
    low4 = jnp.concatenate([low] * PAIRS_PER_KV, axis=0)
    groups = [(blk, kvh) for blk in range(len(blocks)) for kvh in range(ATT_KVH)]

    def score_products(g):
        blk, kvh = groups[g]
        q_cols, k_lo, k_hi, _, _, bias = blocks[blk]
        bias4 = jnp.concatenate([bias] * PAIRS_PER_KV, axis=0)
        qs = jnp.concatenate(q_cols[kvh * PAIRS_PER_KV:(kvh + 1) * PAIRS_PER_KV], axis=0)
        return [_mm_nt(qs, kk) + bias4 for kk in (k_lo[kvh], k_hi[kvh])]

    def softmax_terms(g, scores):
        kvh = groups[g][1]
        cols = range(kvh * PAIRS_PER_KV, (kvh + 1) * PAIRS_PER_KV)
        ps, excess = [], []
        for par, s in enumerate(scores):
            row_max = jnp.max(s, axis=-1, keepdims=True)
            m = jnp.concatenate(
                [jnp.maximum(row_max[i * WINDOW:(i + 1) * WINDOW], sinks_ref[0, 2 * c + par])
                 for i, c in enumerate(cols)], axis=0)
            ps.append(jnp.exp(s - m))
            excess.append(jnp.concatenate(
                [sinks_ref[0, 2 * c + par] - m[i * WINDOW:(i + 1) * WINDOW]
                 for i, c in enumerate(cols)], axis=0))
        return ps, jnp.exp(jnp.where(low4, excess[0], excess[1]))

    def value_products(g, ps, sink_term):
        blk, kvh = groups[g]
        v_lo, v_hi = blocks[blk][3], blocks[blk][4]
        r_even, r_odd = _mm(ps[0], v_lo[kvh]), _mm(ps[1], v_hi[kvh])
        num = jnp.where(low4, r_even, r_odd)
        den = pltpu.roll(jnp.where(low4, r_odd, r_even), ATT_HD, axis=1) + sink_term
        return num * (1.0 / den)

    scores = [score_products(g) for g in range(len(groups))]
    terms = [softmax_terms(g, s) for g, s in enumerate(scores)]
    outs = [value_products(g, ps, sink_term) for g, (ps, sink_term) in enumerate(terms)]
    res = []
    for b in range(len(blocks)):
        cols = []
        for kvh in range(ATT_KVH):
            o = outs[b * ATT_KVH + kvh]
            cols.extend(o[i * WINDOW:(i + 1) * WINDOW] for i in range(PAIRS_PER_KV))
        res.append(jnp.concatenate(cols, axis=1))
    return res


def _split_kv_heads(a, fill):
    lane = lax.broadcasted_iota(jnp.int32, a.shape, 1)
    low = lane < ATT_HD
    sw = pltpu.roll(a, ATT_HD, axis=1)
    lo = [jnp.where(low, a, fill), jnp.where(low, sw, fill)]
    hi = [jnp.where(low, fill, sw), jnp.where(low, fill, a)]
    return lo, hi


def _swa_prompt_kernel(final, h_ref, ng_ref, win_ref, sinks_ref, wout_ref, fin_ref,
                       kp_ref, kc_ref, vp_ref, vc_ref, cos_ref, sup_ref, sdn_ref, out_ref):
    t = pl.program_id(1)
    tile = h_ref.shape[1]
    nblk = tile // WINDOW
    x = h_ref[0]
    u = _mm(_rms_scale(x, ng_ref[...]), win_ref[...])
    scale = ATT_HD ** -0.5
    cos, sup, sdn = cos_ref[...], sup_ref[...], sdn_ref[...]
    k_all = jnp.concatenate([kp_ref[0], kc_ref[0]], axis=0)
    v_all = jnp.concatenate([vp_ref[0], vc_ref[0]], axis=0)
    tq = lax.broadcasted_iota(jnp.int32, (WINDOW, 2 * WINDOW), 0)
    j = lax.broadcasted_iota(jnp.int32, (WINDOW, 2 * WINDOW), 1)
    rel = tq + WINDOW - j
    band = (rel >= 0) & (rel <= WINDOW)
    blocks = []
    for blk in range(nblk):
        rows = slice(blk * WINDOW, (blk + 1) * WINDOW)
        keys = slice(blk * WINDOW, (blk + 2) * WINDOW)
        k_lo, k_hi = _split_kv_heads(k_all[keys], 0.0)
        v_lo, v_hi = _split_kv_heads(v_all[keys], 1.0)
        if blk == 0:
            valid = band & (j >= WINDOW * (t == 0).astype(jnp.int32))
        else:
            valid = band
        q_cols = []
        for c in range(ATT_QD // LANES):
            qc = u[rows, c * LANES:(c + 1) * LANES]
            q_cols.append(_rope_cols(qc, cos[rows], sup[rows], sdn[rows]) * scale)
        blocks.append((q_cols, k_lo, k_hi, v_lo, v_hi, jnp.where(valid, 0.0, MASK_VALUE)))
    o_blocks = _swa_attention(blocks, sinks_ref)
    o = jnp.concatenate(o_blocks, axis=0) if nblk > 1 else o_blocks[0]
    y = x + _mm(o * _silu(u[:, ATT_QD:]), wout_ref[...])
    if final:
        y = _rms_scale(y, fin_ref[...])
    out_ref[0] = y


def _swa_prompt_layer(h, k, v, tables, norm_g, w_in, sinks, w_out, final_norm, final):
    B, T, D = h.shape
    tile = min(PROMPT_TILE, T)
    per = tile // WINDOW
    const = lambda b, t: (0, 0)
    tab = pl.BlockSpec((tile, LANES), lambda b, t: (t, 0))
    prev = pl.BlockSpec((1, WINDOW, ATT_KVD), lambda b, t: (b, jnp.maximum(t * per - 1, 0), 0))
    cur = pl.BlockSpec((1, tile, ATT_KVD), lambda b, t: (b, t, 0))
    return pl.pallas_call(
        functools.partial(_swa_prompt_kernel, final),
        grid=(B, T // tile),
        in_specs=[
            pl.BlockSpec((1, tile, D), lambda b, t: (b, t, 0)),
            pl.BlockSpec((1, D), const),
            pl.BlockSpec(w_in.shape, const),
            pl.BlockSpec(memory_space=pltpu.SMEM),
            pl.BlockSpec(w_out.shape, const),
            pl.BlockSpec((1, D), const),
            prev, cur, prev, cur, tab, tab, tab,
        ],
        out_specs=pl.BlockSpec((1, tile, D), lambda b, t: (b, t, 0)),
        out_shape=jax.ShapeDtypeStruct((B, T, D), F32),
        compiler_params=_cparams("arbitrary", "arbitrary"),
        name="swa_prompt_final" if final else "swa_prompt",
    )(h, norm_g.reshape(1, D), w_in, sinks.reshape(1, ATT_QH), w_out, final_norm.reshape(1, D),
      k, k, v, v, *tables)


def _hgrn_decode_proj_kernel(layer, h_ref, ng_ref, win_ref, lbl_ref, q_ref, f_ref, k_ref, v_ref, g_ref):
    u = _mm(_rms_scale(h_ref[...], ng_ref[...]), win_ref[...])
    lb = _lower_bound(lbl_ref[...], layer)
    q_ref[...] = _silu(u[:, :HG_KEY])
    fg, kin = _forget_gates(u[:, HG_KEY:2 * HG_KEY], lb)
    f_ref[...] = fg
    k_ref[...] = kin
    v_ref[...] = u[:, 2 * HG_KEY:2 * HG_KEY + HG_VAL]
    g_ref[...] = _silu(u[:, 2 * HG_KEY + HG_VAL:])


def _col(row):
    n = row.shape[1]
    return jnp.broadcast_to(row, (n, n)).T


def _hgrn_decode_state_kernel(s_ref, q_ref, f_ref, k_ref, v_ref, *rest):
    snew_ref, o_ref = rest[-2:]
    for j in range(s_ref.shape[1]):
        for h in range(HG_HEADS):
            sl = slice(h * HG_DK, (h + 1) * HG_DK)
            f_c = _col(f_ref[j:j + 1, sl])
            k_c = _col(k_ref[j:j + 1, sl])
            q_c = _col(q_ref[j:j + 1, sl])
            v_r = v_ref[j:j + 1, h * HG_DV:(h + 1) * HG_DV]
            s_new = f_c * s_ref[0, j, h] + k_c * v_r
            snew_ref[0, j, h] = s_new
            o_ref[j:j + 1, h * HG_DV:(h + 1) * HG_DV] = jnp.sum(q_c * s_new, axis=0, keepdims=True)


def _hgrn_decode_out_kernel(h_ref, o_ref, g_ref, gn_ref, wout_ref, out_ref):
    gn = gn_ref[...]
    gated = []
    for h in range(HG_HEADS):
        sl = slice(h * HG_DV, (h + 1) * HG_DV)
        gated.append(_rms_scale(o_ref[:, sl], gn) * g_ref[:, sl])
    out_ref[...] = h_ref[...] + _mm(jnp.concatenate(gated, axis=1), wout_ref[...])


def _whole(shape):
    return pl.BlockSpec(shape, lambda *_: (0,) * len(shape))


def _hgrn_decode_layer(h, states, new_states, norm_g, w_in, lb_logits, layer, gnorm, w_out):
    B, D = h.shape
    n_a = lb_logits.shape[0]
    vec = jax.ShapeDtypeStruct((B, HG_KEY), F32)
    q, f, k, v, g = pl.pallas_call(
        functools.partial(_hgrn_decode_proj_kernel, layer),
        grid=(1,),
        in_specs=[_whole((B, D)), _whole((1, D)), _whole(w_in.shape), _whole((n_a, HG_KEY))],
        out_specs=[_whole((B, HG_KEY))] * 5,
        out_shape=[vec] * 5,
        compiler_params=_cparams("arbitrary"),
        name=f"hgrn_decode_proj_l{layer}",
    )(h, norm_g.reshape(1, D), w_in, lb_logits)

    bb = DEC_BLOCK
    row_blk = pl.BlockSpec((bb, HG_KEY), lambda i: (i, 0))
    st_blk = pl.BlockSpec((1, bb, HG_HEADS, HG_DK, HG_DV), lambda i: (layer, i, 0, 0, 0))
    carried = [] if new_states is None else [new_states]
    s_new, o = pl.pallas_call(
        _hgrn_decode_state_kernel,
        grid=(B // bb,),
        in_specs=[st_blk, row_blk, row_blk, row_blk, row_blk] + [pl.BlockSpec(memory_space=pl.ANY)] * len(carried),
        out_specs=[st_blk, row_blk],
        out_shape=[jax.ShapeDtypeStruct(states.shape, F32), vec],
        input_output_aliases={5: 0} if carried else {},
        compiler_params=_cparams("arbitrary"),
        name=f"hgrn_decode_state_l{layer}",
    )(states, q, f, k, v, *carried)

    h_new = pl.pallas_call(
        _hgrn_decode_out_kernel,
        grid=(1,),
        in_specs=[_whole((B, D)), _whole((B, HG_VAL)), _whole((B, HG_VAL)), _whole((1, HG_DV)),
                  _whole(w_out.shape)],
        out_specs=_whole((B, D)),
        out_shape=jax.ShapeDtypeStruct((B, D), F32),
        compiler_params=_cparams("arbitrary"),
        name=f"hgrn_decode_out_l{layer}",
    )(h, o, g, gnorm.reshape(1, HG_DV), w_out)
    return h_new, s_new


def _swa_decode_proj_kernel(h_ref, ng_ref, win_ref, cos_ref, sup_ref, sdn_ref, q_ref, g_ref):
    u = _mm(_rms_scale(h_ref[...], ng_ref[...]), win_ref[...])
    q = _rope_cols(u[:, :ATT_QD], cos_ref[...], sup_ref[...], sdn_ref[...])
    q_ref[...] = q * (ATT_HD ** -0.5)
    g_ref[...] = _silu(u[:, ATT_QD:])


def _swa_decode_attn_kernel(q_ref, kc_ref, vc_ref, kn_ref, vn_ref, sinks_ref, o_ref):
    for j in range(q_ref.shape[0]):
        kc, vc = kc_ref[j], vc_ref[j]
        kn, vn = kn_ref[j], vn_ref[j]
        for kvh in range(ATT_KVH):
            qg = q_ref[j, kvh * ATT_GROUP:(kvh + 1) * ATT_GROUP, :]
            zeros = jnp.zeros_like(qg)
            qp = jnp.concatenate([qg, zeros] if kvh == 0 else [zeros, qg], axis=1)
            s_c = _mm_nt(qp, kc)
            s_n = jnp.sum(qp.astype(BF16).astype(F32) * kn.astype(BF16).astype(F32),
                          axis=-1, keepdims=True)
            sink = sinks_ref[kvh * ATT_GROUP:(kvh + 1) * ATT_GROUP, :]
            m = jnp.maximum(jnp.maximum(jnp.max(s_c, axis=-1, keepdims=True), s_n), sink)
            p_c = jnp.exp(s_c - m)
            p_n = jnp.exp(s_n - m)
            denom = jnp.sum(p_c, axis=-1, keepdims=True) + p_n + jnp.exp(sink - m)
            inv = 1.0 / denom
            o = _mm(p_c * inv, vc) + (p_n * inv).astype(BF16).astype(F32) * vn.astype(BF16).astype(F32)
            o_ref[j, kvh * ATT_GROUP:(kvh + 1) * ATT_GROUP, :] = (
                o[:, :ATT_HD] if kvh == 0 else o[:, ATT_HD:])


def _swa_decode_out_kernel(final, h_ref, o_ref, g_ref, wout_ref, fin_ref, out_ref):
    y = h_ref[...] + _mm(o_ref[...] * g_ref[...], wout_ref[...])
    if final:
        y = _rms_scale(y, fin_ref[...])
    out_ref[...] = y


def _swa_decode_layer(h, k_new, v_new, cache_k, cache_v, tables, norm_g, w_in, sinks, w_out,
                      final_norm, final):
    B, D = h.shape
    W = cache_k.shape[1]
    qd = jax.ShapeDtypeStruct((B, ATT_QD), F32)
    q, g = pl.pallas_call(
        _swa_decode_proj_kernel,
        grid=(1,),
        in_specs=[_whole((B, D)), _whole((1, D)), _whole(w_in.shape)] + [_whole((1, LANES))] * 3,
        out_specs=[_whole((B, ATT_QD))] * 2,
        out_shape=[qd, qd],
        compiler_params=_cparams("arbitrary"),
        name="swa_decode_proj",
    )(h, norm_g.reshape(1, D), w_in, *tables)

    bb = DEC_BLOCK
    o = pl.pallas_call(
        _swa_decode_attn_kernel,
        grid=(B // bb,),
        in_specs=[pl.BlockSpec((bb, ATT_QH, ATT_HD), lambda i: (i, 0, 0)),
                  pl.BlockSpec((bb, W, ATT_KVD), lambda i: (i, 0, 0)),
                  pl.BlockSpec((bb, W, ATT_KVD), lambda i: (i, 0, 0)),
                  pl.BlockSpec((bb, 1, ATT_KVD), lambda i: (i, 0, 0)),
                  pl.BlockSpec((bb, 1, ATT_KVD), lambda i: (i, 0, 0)),
                  pl.BlockSpec((ATT_QH, 1), lambda i: (0, 0))],
        out_specs=pl.BlockSpec((bb, ATT_QH, ATT_HD), lambda i: (i, 0, 0)),
        out_shape=jax.ShapeDtypeStruct((B, ATT_QH, ATT_HD), F32),
        compiler_params=_cparams("arbitrary"),
        name="swa_decode_attn",
    )(q.reshape(B, ATT_QH, ATT_HD), cache_k.reshape(B, W, ATT_KVD), cache_v.reshape(B, W, ATT_KVD),
      k_new.reshape(B, 1, ATT_KVD), v_new.reshape(B, 1, ATT_KVD), sinks.reshape(ATT_QH, 1))

    return pl.pallas_call(
        functools.partial(_swa_decode_out_kernel, final),
        grid=(1,),
        in_specs=[_whole((B, D)), _whole((B, ATT_QD)), _whole((B, ATT_QD)), _whole(w_out.shape),
                  _whole((1, D))],
        out_specs=_whole((B, D)),
        out_shape=jax.ShapeDtypeStruct((B, D), F32),
        compiler_params=_cparams("arbitrary"),
        name="swa_decode_out_final" if final else "swa_decode_out",
    )(h, o.reshape(B, ATT_QD), g, w_out, final_norm.reshape(1, D))


def kernel(x_prompt, x_sample, state_hgrn, cache_k_win, cache_v_win, a_norm, a_w_in, a_lb_logits,
           a_gnorm, a_w_out, kv_norm, w_kv, b_norm, b_w_in, b_sinks, b_w_out, final_norm):
    B, T, D = x_prompt.shape
    BD, TD, _ = x_sample.shape
    n_a = a_w_in.shape[0]
    n_b = b_w_in.shape[0]
    a_w_in_b, a_w_out_b = a_w_in.astype(BF16), a_w_out.astype(BF16)
    b_w_in_b, b_w_out_b = b_w_in.astype(BF16), b_w_out.astype(BF16)
    w_kv_b = w_kv.astype(BF16)

    tab_p = _rope_tables(np.arange(T))
    h = x_prompt
    st_p = []
    for l in range(n_a):
        h, s = _hgrn_prompt_layer(h, a_norm[l], a_w_in_b[l], a_lb_logits, l, a_gnorm[l], a_w_out_b[l])
        st_p.append(s)
    k_p, v_p = _shared_kv(h, kv_norm, w_kv_b, tab_p)
    for l in range(n_b):
        h = _swa_prompt_layer(h, k_p, v_p, tab_p, b_norm[l], b_w_in_b[l], b_sinks[l], b_w_out_b[l],
                              final_norm, l == n_b - 1)
    y_prompt = h
    w_keep = min(WINDOW, T)
    k_win = k_p[:, T - w_keep:].reshape(B, w_keep, ATT_KVH, ATT_HD)
    v_win = v_p[:, T - w_keep:].reshape(B, w_keep, ATT_KVH, ATT_HD)

    tab_s = _rope_tables(PAST_LEN + np.arange(TD))
    hs = x_sample.reshape(BD * TD, D)
    st_s = None
    for l in range(n_a):
        hs, st_s = _hgrn_decode_layer(hs, state_hgrn, st_s, a_norm[l], a_w_in_b[l], a_lb_logits, l,
                                      a_gnorm[l], a_w_out_b[l])
    k_s, v_s = _shared_kv(hs.reshape(1, BD, D), kv_norm, w_kv_b,
                          tuple(jnp.broadcast_to(t, (BD, LANES)) for t in tab_s))
    k_s, v_s = k_s.reshape(BD, ATT_KVD), v_s.reshape(BD, ATT_KVD)
    for l in range(n_b):
        hs = _swa_decode_layer(hs, k_s, v_s, cache_k_win, cache_v_win, tab_s, b_norm[l], b_w_in_b[l],
                               b_sinks[l], b_w_out_b[l], final_norm, l == n_b - 1)
    y_sample = hs.reshape(BD, TD, D)

    return (y_prompt, y_sample, jnp.stack(st_p), st_s, k_win, v_win,
            k_s.reshape(BD, TD, ATT_KVH, ATT_HD), v_s.reshape(BD, TD, ATT_KVH, ATT_HD))
```

```python
import functools

import numpy as np
import jax
import jax.numpy as jnp
from jax import lax
from jax.experimental import pallas as pl
from jax.experimental.pallas import tpu as pltpu

F32 = jnp.float32
BF16 = jnp.bfloat16

D_MODEL = 1024
HG_HEADS = 8
HG_DK = 128
HG_DV = 128
HG_KEY = HG_HEADS * HG_DK
HG_VAL = HG_HEADS * HG_DV
HG_CHUNK = 64
HG_HALF = HG_CHUNK // 2
HG_SUB = 16
HG_MAX_HALF_DECAY = 60.0
ATT_HD = 64
ATT_QH = 16
ATT_KVH = 2
ATT_GROUP = ATT_QH // ATT_KVH
ATT_QD = ATT_QH * ATT_HD
ATT_KVD = ATT_KVH * ATT_HD
WINDOW = 128
ROPE_THETA = 500000.0
ROT_DIM = ATT_HD // 4
NORM_EPS = 1e-6
MASK_VALUE = -1e30
NEG_BIG = -1e30
PAST_LEN = 8192

SUBLANES = 8
LANES = 128
PROMPT_TILE = 256
DEC_BLOCK = 8
VMEM_LIMIT = 48 * 1024 * 1024


def _cparams(*sem):
    return pltpu.CompilerParams(dimension_semantics=sem, vmem_limit_bytes=VMEM_LIMIT)


def _mm(a, b):
    return jnp.dot(a.astype(BF16), b.astype(BF16), preferred_element_type=F32)


def _mm_nt(a, b):
    return lax.dot_general(a.astype(BF16), b.astype(BF16), (((1,), (1,)), ((), ())),
                           preferred_element_type=F32)


def _mm_tn(a, b):
    return lax.dot_general(a.astype(BF16), b.astype(BF16), (((0,), (0,)), ((), ())),
                           preferred_element_type=F32)


def _rms_scale(x, g):
    ms = jnp.mean(x * x, axis=-1, keepdims=True)
    return x * lax.rsqrt(ms + NORM_EPS) * g


def _silu(x):
    return x * (1.0 / (1.0 + jnp.exp(-x)))


def _lower_bound(lbl, layer):
    n = lbl.shape[0]
    rows = [lbl[i:i + 1, :] for i in range(n)]
    m = functools.reduce(jnp.maximum, rows)
    es = [jnp.exp(r - m) for r in rows]
    inv = 1.0 / functools.reduce(jnp.add, es)
    ps = [e * inv for e in es]
    return functools.reduce(jnp.add, ps[:layer + 1]) - ps[0]


def _forget_gates(f, lb):
    e = jnp.exp(-f)
    r = 1.0 / (1.0 + e)
    return lb + (1.0 - lb) * r, (1.0 - lb) * (e * r)


def _chunk_cumsum(x):
    rows = x.shape[0]
    pos = lax.broadcasted_iota(jnp.int32, x.shape, 0) % HG_CHUNK
    s = 1
    while s < HG_CHUNK:
        x = x + jnp.where(pos >= s, pltpu.roll(x, s, axis=0), 0.0)
        s *= 2
    del rows
    return x


def _diag_blocks(q, k, v, b):
    sub_iota = lax.broadcasted_iota(jnp.int32, (SUBLANES, LANES), 0)
    outs = []
    for blk in range(HG_CHUNK // HG_SUB):
        base = blk * HG_SUB
        groups = HG_SUB // SUBLANES
        qs = [q[base + g * SUBLANES: base + (g + 1) * SUBLANES] for g in range(groups)]
        bs = [b[base + g * SUBLANES: base + (g + 1) * SUBLANES] for g in range(groups)]
        accs = [jnp.zeros((SUBLANES, HG_DV), F32) for _ in range(groups)]
        for s in range(HG_SUB):
            row = base + s
            k_s = k[row:row + 1]
            b_s = b[row:row + 1]
            v_s = v[row:row + 1]
            for g in range(s // SUBLANES, groups):
                d = bs[g] - b_s
                if g == s // SUBLANES:
                    d = jnp.where(sub_iota >= (s % SUBLANES), d, NEG_BIG)
                p = qs[g] * (k_s * jnp.exp(d))
                a = jnp.sum(p, axis=-1, keepdims=True)
                accs[g] = accs[g] + a * v_s
        outs.extend(accs)
    return jnp.concatenate(outs, axis=0)


def _hgrn_chunk(q, k, v, b, st):
    C, H = HG_CHUNK, HG_HALF
    o_inter = _mm_nt(q * jnp.exp(b), st)
    r0 = b[H - 1:H]
    a0 = _mm_nt(q[H:] * jnp.exp(b[H:] - r0), k[:H] * jnp.exp(r0 - b[:H]))
    o_hi = _mm(a0, v[:H])
    Q = HG_SUB
    o_q = []
    for base in (0, H):
        r1 = b[base + Q - 1: base + Q]
        a1 = _mm_nt(q[base + Q: base + 2 * Q] * jnp.exp(b[base + Q: base + 2 * Q] - r1),
                    k[base: base + Q] * jnp.exp(r1 - b[base: base + Q]))
        o_q.append(_mm(a1, v[base: base + Q]))
    zero = jnp.zeros((Q, HG_DV), F32)
    o_off = jnp.concatenate([zero, o_q[0], o_hi[:Q], o_hi[Q:] + o_q[1]], axis=0)
    o = o_inter + o_off + _diag_blocks(q, k, v, b)
    b_end = b[C - 1:C]
    st_new = st * jnp.exp(b_end) + _mm_tn(v, k * jnp.exp(b_end - b))
    return o, st_new


def _hgrn_chunks_bounded(heads):
    C, H = HG_CHUNK, HG_HALF
    t_i = lax.broadcasted_iota(jnp.int32, (C, C), 0)
    s_i = lax.broadcasted_iota(jnp.int32, (C, C), 1)
    causal = s_i <= t_i
    first = []
    for q, k, v, b, st in heads:
        b0, b1 = b[:H], b[H:]
        r = b0[H - 1:H]
        eq0, eq1 = jnp.exp(b0), jnp.exp(b1 - r)
        e_mid, e_hi = eq0[H - 1:H], eq1[H - 1:H]
        qt0, qt1 = q[:H] * eq0, q[H:] * eq1
        kt0, kt1 = k[:H] * jnp.exp(-b0), k[H:] * jnp.exp(r - b1)
        kt0m = kt0 * e_mid
        s0 = _mm_nt(qt0, jnp.concatenate([kt0, kt1], axis=0))
        s1 = _mm_nt(qt1, jnp.concatenate([kt0m, kt1], axis=0))
        o_inter = _mm_nt(jnp.concatenate([qt0, qt1 * e_mid], axis=0), st)
        kd = jnp.concatenate([kt0m * e_hi, kt1 * e_hi], axis=0)
        st_new = st * (e_mid * e_hi) + _mm_tn(v, kd)
        first.append((s0, s1, o_inter, st_new))
    out = []
    for (q, k, v, b, st), (s0, s1, o_inter, st_new) in zip(heads, first):
        a = jnp.where(causal, jnp.concatenate([s0, s1], axis=0), 0.0)
        out.append((o_inter + _mm(a, v), st_new))
    return out


def _max_half_decay(b_s, tile):
    n = tile // HG_CHUNK
    worst = None
    for h in range(HG_HEADS):
        mid = b_s[h, pl.ds(HG_HALF - 1, n, stride=HG_CHUNK), :]
        end = b_s[h, pl.ds(HG_CHUNK - 1, n, stride=HG_CHUNK), :]
        w = jnp.maximum(-mid, mid - end)
        worst = w if worst is None else jnp.maximum(worst, w)
    return jnp.max(worst)


def _hgrn_prompt_kernel(layer, h_ref, ng_ref, win_ref, lbl_ref, gn_ref, wout_ref,
                        out_ref, sfin_ref, q_s, k_s, v_s, b_s, o_s, st_s):
    t = pl.program_id(1)
    nt = pl.num_programs(1)
    tile = h_ref.shape[1]

    @pl.when(t == 0)
    def _():
        st_s[...] = jnp.zeros_like(st_s)

    x = h_ref[0]
    u = _mm(_rms_scale(x, ng_ref[...]), win_ref[...])
    lb = _lower_bound(lbl_ref[...], layer)
    for h in range(HG_HEADS):
        sl = slice(h * HG_DK, (h + 1) * HG_DK)
        q_s[h] = _silu(u[:, sl])
        fg, kin = _forget_gates(u[:, HG_KEY + h * HG_DK: HG_KEY + (h + 1) * HG_DK], lb[:, sl])
        k_s[h] = kin
        b_s[h] = _chunk_cumsum(jnp.log(fg))
        v_s[h] = u[:, 2 * HG_KEY + h * HG_DV: 2 * HG_KEY + (h + 1) * HG_DV]

    def chunk_rows(c):
        return pl.ds(pl.multiple_of(c * HG_CHUNK, HG_CHUNK), HG_CHUNK)

    def load(c, h):
        rows = chunk_rows(c)
        return q_s[h, rows, :], k_s[h, rows, :], v_s[h, rows, :], b_s[h, rows, :], st_s[h]

    def bounded_body(c, carry):
        res = _hgrn_chunks_bounded([load(c, h) for h in range(HG_HEADS)])
        for h, (o, st_new) in enumerate(res):
            o_s[h, chunk_rows(c), :] = o
            st_s[h] = st_new
        return carry

    def general_body(i, carry):
        c, h = i // HG_HEADS, i % HG_HEADS
        o, st_new = _hgrn_chunk(*load(c, h))
        o_s[h, chunk_rows(c), :] = o
        st_s[h] = st_new
        return carry

    bounded = _max_half_decay(b_s, tile) < HG_MAX_HALF_DECAY

    @pl.when(bounded)
    def _():
        lax.fori_loop(0, tile // HG_CHUNK, bounded_body, 0)

    @pl.when(jnp.logical_not(bounded))
    def _():
        lax.fori_loop(0, (tile // HG_CHUNK) * HG_HEADS, general_body, 0)

    gn = gn_ref[...]
    gated = []
    for h in range(HG_HEADS):
        g = u[:, 2 * HG_KEY + HG_VAL + h * HG_DV: 2 * HG_KEY + HG_VAL + (h + 1) * HG_DV]
        gated.append(_rms_scale(o_s[h], gn) * _silu(g))
    y = _mm(jnp.concatenate(gated, axis=1), wout_ref[...])
    out_ref[0] = x + y

    @pl.when(t == nt - 1)
    def _():
        for h in range(HG_HEADS):
            sfin_ref[0, h] = st_s[h].T


def _hgrn_prompt_layer(h, norm_g, w_in, lb_logits, layer, gnorm, w_out):
    B, T, D = h.shape
    tile = min(PROMPT_TILE, T)
    n_a = lb_logits.shape[0]
    const = lambda b, t: (0, 0)
    head_scratch = pltpu.VMEM((HG_HEADS, tile, HG_DK), F32)
    return pl.pallas_call(
        functools.partial(_hgrn_prompt_kernel, layer),
        grid=(B, T // tile),
        in_specs=[
            pl.BlockSpec((1, tile, D), lambda b, t: (b, t, 0)),
            pl.BlockSpec((1, D), const),
            pl.BlockSpec(w_in.shape, const),
            pl.BlockSpec((n_a, HG_KEY), const),
            pl.BlockSpec((1, HG_DV), const),
            pl.BlockSpec(w_out.shape, const),
        ],
        out_specs=[
            pl.BlockSpec((1, tile, D), lambda b, t: (b, t, 0)),
            pl.BlockSpec((1, HG_HEADS, HG_DK, HG_DV), lambda b, t: (b, 0, 0, 0)),
        ],
        out_shape=[
            jax.ShapeDtypeStruct((B, T, D), F32),
            jax.ShapeDtypeStruct((B, HG_HEADS, HG_DK, HG_DV), F32),
        ],
        scratch_shapes=[head_scratch] * 5 + [pltpu.VMEM((HG_HEADS, HG_DV, HG_DK), F32)],
        compiler_params=_cparams("arbitrary", "arbitrary"),
        name=f"hgrn_prompt_l{layer}",
    )(h, norm_g.reshape(1, D), w_in, lb_logits, gnorm.reshape(1, HG_DV), w_out)


def _rope_tables(pos):
    half = ROT_DIM // 2
    pos = np.asarray(pos, np.float64)
    inv_freq = 1.0 / (ROPE_THETA ** (np.arange(half, dtype=np.float64) * 2.0 / ROT_DIM))
    ang = (pos[:, None].astype(np.float32) * inv_freq[None, :].astype(np.float32)).astype(np.float64)
    cos, sin = np.cos(ang), np.sin(ang)
    n = pos.shape[0]
    c = np.ones((n, ATT_HD)); s_up = np.zeros((n, ATT_HD)); s_dn = np.zeros((n, ATT_HD))
    c[:, :half] = cos; c[:, half:ROT_DIM] = cos
    s_dn[:, :half] = -sin
    s_up[:, half:ROT_DIM] = sin
    rep = LANES // ATT_HD
    tab = [np.tile(a, (1, rep)).astype(np.float32) for a in (c, s_up, s_dn)]
    return tuple(jnp.asarray(a) for a in tab)


def _rope_cols(x, cos, s_up, s_dn):
    half = ROT_DIM // 2
    cols = []
    for c in range(x.shape[1] // LANES):
        xc = x[:, c * LANES:(c + 1) * LANES]
        cols.append(xc * cos + pltpu.roll(xc, half, axis=1) * s_up
                    + pltpu.roll(xc, LANES - half, axis=1) * s_dn)
    return cols[0] if len(cols) == 1 else jnp.concatenate(cols, axis=1)


def _kv_kernel(h_ref, ng_ref, w_ref, cos_ref, sup_ref, sdn_ref, k_ref, v_ref):
    x = h_ref[0]
    u = _mm(_rms_scale(x, ng_ref[...]), w_ref[...])
    k_ref[0] = _rope_cols(u[:, :ATT_KVD], cos_ref[...], sup_ref[...], sdn_ref[...])
    v_ref[0] = u[:, ATT_KVD:]


def _shared_kv(h, kv_norm, w_kv, tables):
    B, T, D = h.shape
    tile = min(PROMPT_TILE, T)
    const = lambda b, t: (0, 0)
    tab = pl.BlockSpec((tile, LANES), lambda b, t: (t, 0))
    kv_spec = pl.BlockSpec((1, tile, ATT_KVD), lambda b, t: (b, t, 0))
    return pl.pallas_call(
        _kv_kernel,
        grid=(B, T // tile),
        in_specs=[pl.BlockSpec((1, tile, D), lambda b, t: (b, t, 0)),
                  pl.BlockSpec((1, D), const), pl.BlockSpec(w_kv.shape, const), tab, tab, tab],
        out_specs=[kv_spec, kv_spec],
        out_shape=[jax.ShapeDtypeStruct((B, T, ATT_KVD), F32)] * 2,
        compiler_params=_cparams("arbitrary", "arbitrary"),
        name="shared_kv",
    )(h, kv_norm.reshape(1, D), w_kv, *tables)


PAIRS_PER_KV = ATT_GROUP // 2


def _swa_attention(blocks, sinks_ref, side_product):
    low = lax.broadcasted_iota(jnp.int32, (WINDOW, LANES), 1) < ATT_HD
    low4 = jnp.concatenate([low] * PAIRS_PER_KV, axis=0)
    low_keys = lax.broadcasted_iota(jnp.int32, blocks[0][3][0].shape, 1) < ATT_HD
    groups =[(blk, kvh) for blk in range(len(blocks)) for kvh in range(ATT_KVH)]

    def score_products(g):
        blk, kvh = groups[g]
        q_cols, k_lo, k_hi, _, _, bias = blocks[blk]
        bias4 = jnp.concatenate([bias] * PAIRS_PER_KV, axis=0)
        qs = jnp.concatenate(q_cols[kvh * PAIRS_PER_KV:(kvh + 1) * PAIRS_PER_KV], axis=0)
        return [_mm_nt(qs, kk) + bias4 for kk in (k_lo[kvh], k_hi[kvh])]

    def softmax_terms(g, scores):
        kvh = groups[g][1]
        cols = range(kvh * PAIRS_PER_KV, (kvh + 1) * PAIRS_PER_KV)
        ps, excess = [], []
        for par, s in enumerate(scores):
            row_max = jnp.max(s, axis=-1, keepdims=True)
            m = jnp.concatenate(
                [jnp.maximum(row_max[i * WINDOW:(i + 1) * WINDOW], sinks_ref[0, 2 * c + par])
                 for i, c in enumerate(cols)], axis=0)
            ps.append(jnp.exp(s - m))
            excess.append(jnp.concatenate(
                [sinks_ref[0, 2 * c + par] - m[i * WINDOW:(i + 1) * WINDOW]
                 for i, c in enumerate(cols)], axis=0))
        return ps, jnp.exp(jnp.where(low4, excess[0], excess[1]))

    def value_products(g, ps, sink_term):
        blk, kvh = groups[g]
        v_lo, v_hi = blocks[blk][3], blocks[blk][4]
        ones = jnp.ones_like(v_lo[kvh])
        acc = (_mm(ps[0], jnp.concatenate([v_lo[kvh], jnp.where(low_keys, ones, 0.0)], axis=1))
               + _mm(ps[1], jnp.concatenate([v_hi[kvh], jnp.where(low_keys, 0.0, ones)], axis=1)))
        return acc[:, :LANES] * (1.0 / (acc[:, LANES:] + sink_term))

    scores = [score_products(g) for g in range(len(groups))]
    side = side_product()
    terms = [softmax_terms(g, s) for g, s in enumerate(scores)]
    outs = [value_products(g, ps, sink_term) for g, (ps, sink_term) in enumerate(terms)]
    res = []
    for b in range(len(blocks)):
        cols = []
        for kvh in range(ATT_KVH):
            o = outs[b * ATT_KVH + kvh]
            cols.extend(o[i * WINDOW:(i + 1) * WINDOW] for i in range(PAIRS_PER_KV))
        res.append(jnp.concatenate(cols, axis=1))
    return res, side


def _split_kv_heads(a, fill):
    lane = lax.broadcasted_iota(jnp.int32, a.shape, 1)
    low = lane < ATT_HD
    sw = pltpu.roll(a, ATT_HD, axis=1)
    lo = [jnp.where(low, a, fill), jnp.where(low, sw, fill)]
    hi = [jnp.where(low, fill, sw), jnp.where(low, fill, a)]
    return lo, hi


def _swa_prompt_kernel(final, h_ref, ng_ref, win_ref, sinks_ref, wout_ref, fin_ref,
                       kp_ref, kc_ref, vp_ref, vc_ref, cos_ref, sup_ref, sdn_ref, out_ref):
    t = pl.program_id(1)
    tile = h_ref.shape[1]
    nblk = tile // WINDOW
    x = h_ref[0]
    xn = _rms_scale(x, ng_ref[...]).astype(BF16)
    u = _mm(xn, win_ref[:, :ATT_QD])
    scale = ATT_HD ** -0.5
    cos, sup, sdn = cos_ref[...], sup_ref[...], sdn_ref[...]
    k_all = jnp.concatenate([kp_ref[0], kc_ref[0]], axis=0)
    v_all = jnp.concatenate([vp_ref[0], vc_ref[0]], axis=0)
    tq = lax.broadcasted_iota(jnp.int32, (WINDOW, 2 * WINDOW), 0)
    j = lax.broadcasted_iota(jnp.int32, (WINDOW, 2 * WINDOW), 1)
    rel = tq + WINDOW - j
    band = (rel >= 0) & (rel <= WINDOW)
    blocks = []
    for blk in range(nblk):
        rows = slice(blk * WINDOW, (blk + 1) * WINDOW)
        keys = slice(blk * WINDOW, (blk + 2) * WINDOW)
        k_lo, k_hi = _split_kv_heads(k_all[keys], 0.0)
        v_lo, v_hi = _split_kv_heads(v_all[keys], 0.0)
        if blk == 0:
            valid = band & (j >= WINDOW * (t == 0).astype(jnp.int32))
        else:
            valid = band
        q_cols = []
        for c in range(ATT_QD // LANES):
            qc = u[rows, c * LANES:(c + 1) * LANES]
            q_cols.append(_rope_cols(qc, cos[rows], sup[rows], sdn[rows]) * scale)
        blocks.append((q_cols, k_lo, k_hi, v_lo, v_hi, jnp.where(valid, 0.0, MASK_VALUE)))
    o_blocks, gate = _swa_attention(blocks, sinks_ref, lambda: _mm(xn, win_ref[:, ATT_QD:]))
    o = jnp.concatenate(o_blocks, axis=0) if nblk > 1 else o_blocks[0]
    y = x + _mm(o * _silu(gate), wout_ref[...])
    if final:
        y = _rms_scale(y, fin_ref[...])
    out_ref[0] = y


def _swa_prompt_layer(h, k, v, tables, norm_g, w_in, sinks, w_out, final_norm, final):
    B, T, D = h.shape
    tile = min(PROMPT_TILE, T)
    per = tile // WINDOW
    const = lambda b, t: (0, 0)
    tab = pl.BlockSpec((tile, LANES), lambda b, t: (t, 0))
    prev = pl.BlockSpec((1, WINDOW, ATT_KVD), lambda b, t: (b, jnp.maximum(t * per - 1, 0), 0))
    cur = pl.BlockSpec((1, tile, ATT_KVD), lambda b, t: (b, t, 0))
    return pl.pallas_call(
        functools.partial(_swa_prompt_kernel, final),
        grid=(B, T // tile),
        in_specs=[
            pl.BlockSpec((1, tile, D), lambda b, t: (b, t, 0)),
            pl.BlockSpec((1, D), const),
            pl.BlockSpec(w_in.shape, const),
            pl.BlockSpec(memory_space=pltpu.SMEM),
            pl.BlockSpec(w_out.shape, const),
            pl.BlockSpec((1, D), const),
            prev, cur, prev, cur, tab, tab, tab,
        ],
        out_specs=pl.BlockSpec((1, tile, D), lambda b, t: (b, t, 0)),
        out_shape=jax.ShapeDtypeStruct((B, T, D), F32),
        compiler_params=_cparams("arbitrary", "arbitrary"),
        name="swa_prompt_final" if final else "swa_prompt",
    )(h, norm_g.reshape(1, D), w_in, sinks.reshape(1, ATT_QH), w_out, final_norm.reshape(1, D),
      k, k, v, v, *tables)


def _col(row):
    n = row.shape[1]
    return jnp.broadcast_to(row, (n, n)).T


def _hgrn_decode_kernel(x_ref, s_ref, ng_ref, win_ref, lbl_ref, gn_ref, wout_ref,
                        snew_ref, hout_ref, h_s, q_s, f_s, v_s, g_s, o_s, oblk_s):
    l, i = pl.program_id(0), pl.program_id(1)
    n_layers, n_blocks = pl.num_programs(0), pl.num_programs(1)
    bb = s_ref.shape[1]

    @pl.when(i == 0)
    def _():
        @pl.when(l == 0)
        def _():
            h_s[...] = x_ref[...]

        u = _mm(_rms_scale(h_s[...], ng_ref[0]), win_ref[0])
        lbl = lbl_ref[...]
        lb = jnp.zeros_like(lbl[:1])
        for layer in range(lbl.shape[0]):
            lb = jnp.where(l == layer, _lower_bound(lbl, layer), lb)
        q_s[...] = _silu(u[:, :HG_KEY])
        f_s[...] = _forget_gates(u[:, HG_KEY:2 * HG_KEY], lb)[0]
        v_s[...] = u[:, 2 * HG_KEY:2 * HG_KEY + HG_VAL]
        g_s[...] = _silu(u[:, 2 * HG_KEY + HG_VAL:])

    rows = pl.ds(pl.multiple_of(i * bb, bb), bb)
    qb, fb, vb = q_s[rows, :], f_s[rows, :], v_s[rows, :]
    for j in range(bb):
        for h in range(HG_HEADS):
            sl = slice(h * HG_DK, (h + 1) * HG_DK)
            v_r = vb[j:j + 1, h * HG_DV:(h + 1) * HG_DV]
            s_new = _col(fb[j:j + 1, sl]) * (s_ref[0, j, h] - v_r) + v_r
            snew_ref[0, j, h] = s_new
            oblk_s[j:j + 1, h * HG_DV:(h + 1) * HG_DV] = jnp.sum(
                _col(qb[j:j + 1, sl]) * s_new, axis=0, keepdims=True)
    o_s[rows, :] = oblk_s[...]

    @pl.when(i == n_blocks - 1)
    def _():
        gn = gn_ref[0]
        gated = []
        for h in range(HG_HEADS):
            sl = slice(h * HG_DV, (h + 1) * HG_DV)
            gated.append(_rms_scale(o_s[:, sl], gn) * g_s[:, sl])
        h_new = h_s[...] + _mm(jnp.concatenate(gated, axis=1), wout_ref[0])
        h_s[...] = h_new

        @pl.when(l == n_layers - 1)
        def _():
            hout_ref[...] = h_new


def _whole(shape):
    return pl.BlockSpec(shape, lambda *_: (0,) * len(shape))


def _hgrn_decode(h, states, norms, w_in, lb_logits, gnorms, w_out):
    B, D = h.shape
    n_a = states.shape[0]
    bb = DEC_BLOCK
    per_layer = lambda *tail: pl.BlockSpec((1,) + tail, lambda l, i: (l,) + (0,) * len(tail))
    st_blk = pl.BlockSpec((1, bb, HG_HEADS, HG_DK, HG_DV), lambda l, i: (l, i, 0, 0, 0))
    wide = pltpu.VMEM((B, HG_KEY), F32)
    s_new, h_new = pl.pallas_call(
        _hgrn_decode_kernel,
        grid=(n_a, B // bb),
        in_specs=[_whole((B, D)), st_blk, per_layer(1, D), per_layer(*w_in.shape[1:]),
                  _whole(lb_logits.shape), per_layer(1, HG_DV), per_layer(*w_out.shape[1:])],
        out_specs=[st_blk, _whole((B, D))],
        out_shape=[jax.ShapeDtypeStruct(states.shape, F32), jax.ShapeDtypeStruct((B, D), F32)],
        scratch_shapes=[pltpu.VMEM((B, D), F32)] + [wide] * 5 + [pltpu.VMEM((bb, HG_VAL), F32)],
        compiler_params=_cparams("arbitrary", "arbitrary"),
        name="hgrn_decode",
    )(h, states, norms.reshape(n_a, 1, D), w_in, lb_logits, gnorms.reshape(n_a, 1, HG_DV), w_out)
    return h_new, s_new


def _bf16_round(x):
    return x.astype(BF16).astype(F32)


def _swa_decode_kernel(final, h_ref, ng_ref, win_ref, cos_ref, sup_ref, sdn_ref, kc_ref, vc_ref,
                       kn_ref, vn_ref, sinks_ref, wout_ref, fin_ref, out_ref, qp_s, g_s, os_s):
    i = pl.program_id(0)
    n_blocks = pl.num_programs(0)
    bb, W = kc_ref.shape[0], kc_ref.shape[1]
    B = h_ref.shape[0]
    lane = lax.broadcasted_iota(jnp.int32, (B, LANES), 1)

    @pl.when(i == 0)
    def _():
        u = _mm(_rms_scale(h_ref[...], ng_ref[...]), win_ref[...])
        q = _rope_cols(u[:, :ATT_QD], cos_ref[...], sup_ref[...], sdn_ref[...]) * (ATT_HD ** -0.5)
        g_s[...] = _silu(u[:, ATT_QD:])
        for head in range(ATT_QH):
            col = q[:, (head // 2) * LANES:(head // 2 + 1) * LANES]
            kvh = head // ATT_GROUP
            if head % 2 != kvh:
                col = pltpu.roll(col, ATT_HD, axis=1)
            qp_s[:, head, :] = jnp.where((lane >= ATT_HD) if kvh == 1 else (lane < ATT_HD), col, 0.0)

    rows = pl.ds(pl.multiple_of(i * bb, bb), bb)
    kn, vn = kn_ref[rows, :], vn_ref[rows, :]
    qp = [qp_s[i * bb + j] for j in range(bb)]
    s_c = jnp.concatenate([_mm_nt(qp[j], kc_ref[j]) for j in range(bb)], axis=0)
    qp_all = jnp.concatenate(qp, axis=0)
    expand = lambda a: jnp.concatenate(
        [jnp.broadcast_to(a[j:j + 1], (ATT_QH, ATT_KVD)) for j in range(bb)], axis=0)
    s_n = jnp.sum(_bf16_round(qp_all) * _bf16_round(expand(kn)), axis=-1, keepdims=True)
    sink = jnp.concatenate([sinks_ref[...]] * bb, axis=0)
    m = jnp.maximum(jnp.maximum(jnp.max(s_c, axis=-1, keepdims=True), s_n), sink)
    p_c = jnp.exp(s_c - m)
    p_n = jnp.exp(s_n - m)
    inv = 1.0 / (jnp.sum(p_c, axis=-1, keepdims=True) + p_n + jnp.exp(sink - m))
    p_c = p_c * inv
    o = jnp.concatenate([_mm(p_c[j * ATT_QH:(j + 1) * ATT_QH], vc_ref[j]) for j in range(bb)], axis=0)
    o = o + _bf16_round(p_n * inv) * _bf16_round(expand(vn))
    for j in range(bb):
        os_s[i * bb + j] = o[j * ATT_QH:(j + 1) * ATT_QH]

    @pl.when(i == n_blocks - 1)
    def _():
        cols = []
        for c in range(ATT_QD // LANES):
            halves = []
            for par in range(2):
                head = 2 * c + par
                t = os_s[:, head, :]
                if par != head // ATT_GROUP:
                    t = pltpu.roll(t, ATT_HD, axis=1)
                halves.append(t)
            cols.append(jnp.where(lane < ATT_HD, halves[0], halves[1]))
        y = h_ref[...] + _mm(jnp.concatenate(cols, axis=1) * g_s[...], wout_ref[...])
        if final:
            y = _rms_scale(y, fin_ref[...])
        out_ref[...] = y


def _swa_decode_layer(h, k_new, v_new, cache_k, cache_v, tables, norm_g, w_in, sinks, w_out,
                      final_norm, final):
    B, D = h.shape
    W = cache_k.shape[1]
    bb = DEC_BLOCK
    cache_blk = pl.BlockSpec((bb, W, ATT_KVD), lambda i: (i, 0, 0))
    head_rows = pltpu.VMEM((B, ATT_QH, ATT_KVD), F32)
    return pl.pallas_call(
        functools.partial(_swa_decode_kernel, final),
        grid=(B // bb,),
        in_specs=[_whole((B, D)), _whole((1, D)), _whole(w_in.shape)] + [_whole((1, LANES))] * 3
                 + [cache_blk, cache_blk, _whole((B, ATT_KVD)), _whole((B, ATT_KVD)),
                    _whole((ATT_QH, 1)), _whole(w_out.shape), _whole((1, D))],
        out_specs=_whole((B, D)),
        out_shape=jax.ShapeDtypeStruct((B, D), F32),
        scratch_shapes=[head_rows, pltpu.VMEM((B, ATT_QD), F32), head_rows],
        compiler_params=_cparams("arbitrary"),
        name="swa_decode_final" if final else "swa_decode",
    )(h, norm_g.reshape(1, D), w_in, *tables, cache_k.reshape(B, W, ATT_KVD),
      cache_v.reshape(B, W, ATT_KVD), k_new, v_new, sinks.reshape(ATT_QH, 1), w_out,
      final_norm.reshape(1, D))


def kernel(x_prompt, x_sample, state_hgrn, cache_k_win, cache_v_win, a_norm, a_w_in, a_lb_logits,
           a_gnorm, a_w_out, kv_norm, w_kv, b_norm, b_w_in, b_sinks, b_w_out, final_norm):
    B, T, D = x_prompt.shape
    BD, TD, _ = x_sample.shape
    n_a = a_w_in.shape[0]
    n_b = b_w_in.shape[0]
    a_w_in_b, a_w_out_b = a_w_in.astype(BF16), a_w_out.astype(BF16)
    b_w_in_b, b_w_out_b = b_w_in.astype(BF16), b_w_out.astype(BF16)
    w_kv_b = w_kv.astype(BF16)

    tab_p = _rope_tables(np.arange(T))
    h = x_prompt
    st_p = []
    for l in range(n_a):
        h, s = _hgrn_prompt_layer(h, a_norm[l], a_w_in_b[l], a_lb_logits, l, a_gnorm[l], a_w_out_b[l])
        st_p.append(s)
    k_p, v_p = _shared_kv(h, kv_norm, w_kv_b, tab_p)
    for l in range(n_b):
        h = _swa_prompt_layer(h, k_p, v_p, tab_p, b_norm[l], b_w_in_b[l], b_sinks[l], b_w_out_b[l],
                              final_norm, l == n_b - 1)
    y_prompt = h
    w_keep = min(WINDOW, T)
    k_win = k_p[:, T - w_keep:].reshape(B, w_keep, ATT_KVH, ATT_HD)
    v_win = v_p[:, T - w_keep:].reshape(B, w_keep, ATT_KVH, ATT_HD)

    tab_s = _rope_tables(PAST_LEN + np.arange(TD))
    hs = x_sample.reshape(BD * TD, D)
    hs, st_s = _hgrn_decode(hs, state_hgrn, a_norm, a_w_in_b, a_lb_logits, a_gnorm, a_w_out_b)
    k_s, v_s = _shared_kv(hs.reshape(1, BD, D), kv_norm, w_kv_b,
                          tuple(jnp.broadcast_to(t, (BD, LANES)) for t in tab_s))
    k_s, v_s = k_s.reshape(BD, ATT_KVD), v_s.reshape(BD, ATT_KVD)
    for l in range(n_b):
        hs = _swa_decode_layer(hs, k_s, v_s, cache_k_win, cache_v_win, tab_s, b_norm[l], b_w_in_b[l],
                               b_sinks[l], b_w_out_b[l], final_norm, l == n_b - 1)
    y_sample = hs.reshape(BD, TD, D)

    return (y_prompt, y_sample, jnp.stack(st_p), st_s, k_win, v_win,
            k_s.reshape(BD, TD, ATT_KVH, ATT_HD), v_s.reshape(BD, TD, ATT_KVH, ATT_HD))
```

```python
import functools

import numpy as np
import jax
import jax.numpy as jnp
from jax import lax
from jax.experimental import pallas as pl
from jax.experimental.pallas import tpu as pltpu

F32 = jnp.float32
BF16 = jnp.bfloat16

D_MODEL = 1024
HG_HEADS = 8
HG_DK = 128
HG_DV = 128
HG_KEY = HG_HEADS * HG_DK
HG_VAL = HG_HEADS * HG_DV
HG_CHUNK = 64
HG_HALF = HG_CHUNK // 2
HG_SUB = 16
HG_MAX_HALF_DECAY = 60.0
ATT_HD = 64
ATT_QH = 16
ATT_KVH = 2
ATT_GROUP = ATT_QH // ATT_KVH
ATT_QD = ATT_QH * ATT_HD
ATT_KVD = ATT_KVH * ATT_HD
WINDOW = 128
ROPE_THETA = 500000.0
ROT_DIM = ATT_HD // 4
NORM_EPS = 1e-6
MASK_VALUE = -1e30
NEG_BIG = -1e30
PAST_LEN = 8192

SUBLANES = 8
LANES = 128
PROMPT_TILE = 256
SWA_TILE = 512
DEC_BLOCK = 8
VMEM_LIMIT = 48 * 1024 * 1024


def _cparams(*sem):
    return pltpu.CompilerParams(dimension_semantics=sem, vmem_limit_bytes=VMEM_LIMIT)


def _mm(a, b):
    return jnp.dot(a.astype(BF16), b.astype(BF16), preferred_element_type=F32)


def _mm_nt(a, b):
    return lax.dot_general(a.astype(BF16), b.astype(BF16), (((1,), (1,)), ((), ())),
                           preferred_element_type=F32)


def _mm_tn(a, b):
    return lax.dot_general(a.astype(BF16), b.astype(BF16), (((0,), (0,)), ((), ())),
                           preferred_element_type=F32)


def _rms_scale(x, g):
    ms = jnp.mean(x * x, axis=-1, keepdims=True)
    return x * lax.rsqrt(ms + NORM_EPS) * g


def _silu(x):
    return x * (1.0 / (1.0 + jnp.exp(-x)))


def _lower_bound(lbl, layer):
    n = lbl.shape[0]
    rows = [lbl[i:i + 1, :] for i in range(n)]
    m = functools.reduce(jnp.maximum, rows)
    es = [jnp.exp(r - m) for r in rows]
    inv = 1.0 / functools.reduce(jnp.add, es)
    ps = [e * inv for e in es]
    return functools.reduce(jnp.add, ps[:layer + 1]) - ps[0]


def _forget_gates(f, lb):
    e = jnp.exp(-f)
    r = 1.0 / (1.0 + e)
    return lb + (1.0 - lb) * r, (1.0 - lb) * (e * r)


def _chunk_cumsum(x):
    rows = x.shape[0]
    pos = lax.broadcasted_iota(jnp.int32, x.shape, 0) % HG_CHUNK
    s = 1
    while s < HG_CHUNK:
        x = x + jnp.where(pos >= s, pltpu.roll(x, s, axis=0), 0.0)
        s *= 2
    del rows
    return x


def _diag_blocks(q, k, v, b):
    sub_iota = lax.broadcasted_iota(jnp.int32, (SUBLANES, LANES), 0)
    outs = []
    for blk in range(HG_CHUNK // HG_SUB):
        base = blk * HG_SUB
        groups = HG_SUB // SUBLANES
        qs = [q[base + g * SUBLANES: base + (g + 1) * SUBLANES] for g in range(groups)]
        bs = [b[base + g * SUBLANES: base + (g + 1) * SUBLANES] for g in range(groups)]
        accs = [jnp.zeros((SUBLANES, HG_DV), F32) for _ in range(groups)]
        for s in range(HG_SUB):
            row = base + s
            k_s = k[row:row + 1]
            b_s = b[row:row + 1]
            v_s = v[row:row + 1]
            for g in range(s // SUBLANES, groups):
                d = bs[g] - b_s
                if g == s // SUBLANES:
                    d = jnp.where(sub_iota >= (s % SUBLANES), d, NEG_BIG)
                p = qs[g] * (k_s * jnp.exp(d))
                a = jnp.sum(p, axis=-1, keepdims=True)
                accs[g] = accs[g] + a * v_s
        outs.extend(accs)
    return jnp.concatenate(outs, axis=0)


def _hgrn_chunk(q, k, v, b, st):
    C, H = HG_CHUNK, HG_HALF
    o_inter = _mm_nt(q * jnp.exp(b), st)
    r0 = b[H - 1:H]
    a0 = _mm_nt(q[H:] * jnp.exp(b[H:] - r0), k[:H] * jnp.exp(r0 - b[:H]))
    o_hi = _mm(a0, v[:H])
    Q = HG_SUB
    o_q = []
    for base in (0, H):
        r1 = b[base + Q - 1: base + Q]
        a1 = _mm_nt(q[base + Q: base + 2 * Q] * jnp.exp(b[base + Q: base + 2 * Q] - r1),
                    k[base: base + Q] * jnp.exp(r1 - b[base: base + Q]))
        o_q.append(_mm(a1, v[base: base + Q]))
    zero = jnp.zeros((Q, HG_DV), F32)
    o_off = jnp.concatenate([zero, o_q[0], o_hi[:Q], o_hi[Q:] + o_q[1]], axis=0)
    o = o_inter + o_off + _diag_blocks(q, k, v, b)
    b_end = b[C - 1:C]
    st_new = st * jnp.exp(b_end) + _mm_tn(v, k * jnp.exp(b_end - b))
    return o, st_new


def _hgrn_bounded_first(q, k, v, b, st):
    C, H = HG_CHUNK, HG_HALF
    b0, b1 = b[:H], b[H:]
    r = b0[H - 1:H]
    eq0, eq1 = jnp.exp(b0), jnp.exp(b1 - r)
    e_mid, e_hi = eq0[H - 1:H], eq1[H - 1:H]
    qt0, qt1 = q[:H] * eq0, q[H:] * eq1
    kt0, kt1 = k[:H] * jnp.exp(-b0), k[H:] * jnp.exp(r - b1)
    kt0m = kt0 * e_mid
    s0 = _mm_nt(qt0, jnp.concatenate([kt0, kt1], axis=0))
    s1 = _mm_nt(qt1, jnp.concatenate([kt0m, kt1], axis=0))
    o_inter = _mm_nt(jnp.concatenate([qt0, qt1 * e_mid], axis=0), st)
    kd = jnp.concatenate([kt0m * e_hi, kt1 * e_hi], axis=0)
    st_new = st * (e_mid * e_hi) + _mm_tn(v, kd)
    return jnp.concatenate([s0, s1], axis=0), o_inter, st_new


def _hgrn_bounded_second(scores, o_inter, v):
    t_i = lax.broadcasted_iota(jnp.int32, scores.shape, 0)
    s_i = lax.broadcasted_iota(jnp.int32, scores.shape, 1)
    return o_inter + _mm(jnp.where(s_i <= t_i, scores, 0.0), v)


def _max_half_decay(b_s, tile):
    n = tile // HG_CHUNK
    worst = None
    for h in range(HG_HEADS):
        mid = b_s[h, pl.ds(HG_HALF - 1, n, stride=HG_CHUNK), :]
        end = b_s[h, pl.ds(HG_CHUNK - 1, n, stride=HG_CHUNK), :]
        w = jnp.maximum(-mid, mid - end)
        worst = w if worst is None else jnp.maximum(worst, w)
    return jnp.max(worst)


def _hgrn_prompt_kernel(layer, with_kv, *refs):
    h_ref, ng_ref, win_ref, lbl_ref, gn_ref, wout_ref = refs[:6]
    refs = refs[6:]
    if with_kv:
        kv_in, refs = refs[:5], refs[5:]
    out_ref, sfin_ref = refs[:2]
    refs = refs[2:]
    if with_kv:
        kv_out, refs = refs[:2], refs[2:]
    q_s, k_s, v_s, b_s, o_s, st_s = refs
    t = pl.program_id(1)
    nt = pl.num_programs(1)
    tile = h_ref.shape[1]

    @pl.when(t == 0)
    def _():
        st_s[...] = jnp.zeros_like(st_s)

    x = h_ref[0]
    u = _mm(_rms_scale(x, ng_ref[...]), win_ref[...])
    lb = _lower_bound(lbl_ref[...], layer)
    for h in range(HG_HEADS):
        sl = slice(h * HG_DK, (h + 1) * HG_DK)
        q_s[h] = _silu(u[:, sl])
        fg, kin = _forget_gates(u[:, HG_KEY + h * HG_DK: HG_KEY + (h + 1) * HG_DK], lb[:, sl])
        k_s[h] = kin
        b_s[h] = _chunk_cumsum(jnp.log(fg))
        v_s[h] = u[:, 2 * HG_KEY + h * HG_DV: 2 * HG_KEY + (h + 1) * HG_DV]

    def chunk_rows(c):
        return pl.ds(pl.multiple_of(c * HG_CHUNK, HG_CHUNK), HG_CHUNK)

    def load(c, h):
        rows = chunk_rows(c)
        return q_s[h, rows, :], k_s[h, rows, :], v_s[h, rows, :], b_s[h, rows, :], st_s[h]

    def bounded_chunks():
        pending = []
        for c in range(tile // HG_CHUNK):
            rows = slice(c * HG_CHUNK, (c + 1) * HG_CHUNK)
            started = []
            for h in range(HG_HEADS):
                scores, o_inter, st_new = _hgrn_bounded_first(
                    q_s[h, rows, :], k_s[h, rows, :], v_s[h, rows, :], b_s[h, rows, :], st_s[h])
                st_s[h] = st_new
                started.append((h, rows, scores, o_inter))
            for h, prev_rows, scores, o_inter in pending:
                o_s[h, prev_rows, :] = _hgrn_bounded_second(scores, o_inter, v_s[h, prev_rows, :])
            pending = started
        for h, prev_rows, scores, o_inter in pending:
            o_s[h, prev_rows, :] = _hgrn_bounded_second(scores, o_inter, v_s[h, prev_rows, :])

    def general_body(i, carry):
        c, h = i // HG_HEADS, i % HG_HEADS
        o, st_new = _hgrn_chunk(*load(c, h))
        o_s[h, chunk_rows(c), :] = o
        st_s[h] = st_new
        return carry

    bounded = _max_half_decay(b_s, tile) < HG_MAX_HALF_DECAY

    pl.when(bounded)(bounded_chunks)

    @pl.when(jnp.logical_not(bounded))
    def _():
        lax.fori_loop(0, (tile // HG_CHUNK) * HG_HEADS, general_body, 0)

    gn = gn_ref[...]
    gated = []
    for h in range(HG_HEADS):
        g = u[:, 2 * HG_KEY + HG_VAL + h * HG_DV: 2 * HG_KEY + HG_VAL + (h + 1) * HG_DV]
        gated.append(_rms_scale(o_s[h], gn) * _silu(g))
    y = x + _mm(jnp.concatenate(gated, axis=1), wout_ref[...])
    out_ref[0] = y
    if with_kv:
        _kv_project(y, *kv_in, *kv_out)

    @pl.when(t == nt - 1)
    def _():
        for h in range(HG_HEADS):
            sfin_ref[0, h] = st_s[h].T


def _hgrn_prompt_layer(h, norm_g, w_in, lb_logits, layer, gnorm, w_out, kv=None):
    B, T, D = h.shape
    tile = min(PROMPT_TILE, T)
    n_a = lb_logits.shape[0]
    const = lambda b, t: (0, 0)
    row_tile = lambda width: pl.BlockSpec((1, tile, width), lambda b, t: (b, t, 0))
    head_scratch = pltpu.VMEM((HG_HEADS, tile, HG_DK), F32)
    operands = [h, norm_g.reshape(1, D), w_in, lb_logits, gnorm.reshape(1, HG_DV), w_out]
    in_specs = [row_tile(D), pl.BlockSpec((1, D), const), pl.BlockSpec(w_in.shape, const),
                pl.BlockSpec((n_a, HG_KEY), const), pl.BlockSpec((1, HG_DV), const),
                pl.BlockSpec(w_out.shape, const)]
    out_specs = [row_tile(D), pl.BlockSpec((1, HG_HEADS, HG_DK, HG_DV), lambda b, t: (b, 0, 0, 0))]
    out_shape = [jax.ShapeDtypeStruct((B, T, D), F32),
                 jax.ShapeDtypeStruct((B, HG_HEADS, HG_DK, HG_DV), F32)]
    if kv is not None:
        kv_norm, w_kv, tables = kv
        operands += [kv_norm.reshape(1, D), w_kv, *tables]
        in_specs += [pl.BlockSpec((1, D), const), pl.BlockSpec(w_kv.shape, const)]
        in_specs += [pl.BlockSpec((tile, LANES), lambda b, t: (t, 0))] * 3
        out_specs += [row_tile(ATT_KVD)] * 2
        out_shape += [jax.ShapeDtypeStruct((B, T, ATT_KVD), F32)] * 2
    return pl.pallas_call(
        functools.partial(_hgrn_prompt_kernel, layer, kv is not None),
        grid=(B, T // tile),
        in_specs=in_specs,
        out_specs=out_specs,
        out_shape=out_shape,
        scratch_shapes=[head_scratch] * 5 + [pltpu.VMEM((HG_HEADS, HG_DV, HG_DK), F32)],
        compiler_params=_cparams("arbitrary", "arbitrary"),
        name=f"hgrn_prompt_l{layer}",
    )(*operands)


def _rope_tables(pos):
    half = ROT_DIM // 2
    pos = np.asarray(pos, np.float64)
    inv_freq = 1.0 / (ROPE_THETA ** (np.arange(half, dtype=np.float64) * 2.0 / ROT_DIM))
    ang = (pos[:, None].astype(np.float32) * inv_freq[None, :].astype(np.float32)).astype(np.float64)
    cos, sin = np.cos(ang), np.sin(ang)
    n = pos.shape[0]
    c = np.ones((n, ATT_HD)); s_up = np.zeros((n, ATT_HD)); s_dn = np.zeros((n, ATT_HD))
    c[:, :half] = cos; c[:, half:ROT_DIM] = cos
    s_dn[:, :half] = -sin
    s_up[:, half:ROT_DIM] = sin
    rep = LANES // ATT_HD
    tab = [np.tile(a, (1, rep)).astype(np.float32) for a in (c, s_up, s_dn)]
    return tuple(jnp.asarray(a) for a in tab)


def _rope_cols(x, cos, s_up, s_dn):
    half = ROT_DIM // 2
    cols = []
    for c in range(x.shape[1] // LANES):
        xc = x[:, c * LANES:(c + 1) * LANES]
        cols.append(xc * cos + pltpu.roll(xc, half, axis=1) * s_up
                    + pltpu.roll(xc, LANES - half, axis=1) * s_dn)
    return cols[0] if len(cols) == 1 else jnp.concatenate(cols, axis=1)


def _kv_kernel(h_ref, *refs):
    _kv_project(h_ref[0], *refs)


def _kv_project(x, ng_ref, w_ref, cos_ref, sup_ref, sdn_ref, k_ref, v_ref):
    u = _mm(_rms_scale(x, ng_ref[...]), w_ref[...])
    k_ref[0] = _rope_cols(u[:, :ATT_KVD], cos_ref[...], sup_ref[...], sdn_ref[...])
    v_ref[0] = u[:, ATT_KVD:]


def _shared_kv(h, kv_norm, w_kv, tables):
    B, T, D = h.shape
    tile = min(PROMPT_TILE, T)
    const = lambda b, t: (0, 0)
    tab = pl.BlockSpec((tile, LANES), lambda b, t: (t, 0))
    kv_spec = pl.BlockSpec((1, tile, ATT_KVD), lambda b, t: (b, t, 0))
    return pl.pallas_call(
        _kv_kernel,
        grid=(B, T // tile),
        in_specs=[pl.BlockSpec((1, tile, D), lambda b, t: (b, t, 0)),
                  pl.BlockSpec((1, D), const), pl.BlockSpec(w_kv.shape, const), tab, tab, tab],
        out_specs=[kv_spec, kv_spec],
        out_shape=[jax.ShapeDtypeStruct((B, T, ATT_KVD), F32)] * 2,
        compiler_params=_cparams("arbitrary", "arbitrary"),
        name="shared_kv",
    )(h, kv_norm.reshape(1, D), w_kv, *tables)


PAIRS_PER_KV = ATT_GROUP // 2


def _swa_attention(blocks, sinks_ref, side_product):
    low = lax.broadcasted_iota(jnp.int32, (WINDOW, LANES), 1) < ATT_HD
    low4 = jnp.concatenate([low] * PAIRS_PER_KV, axis=0)
    low_keys = lax.broadcasted_iota(jnp.int32, blocks[0][3][0].shape, 1) < ATT_HD
    groups =[(blk, kvh) for blk in range(len(blocks)) for kvh in range(ATT_KVH)]

    def score_products(g):
        blk, kvh = groups[g]
        q_cols, k_lo, k_hi, _, _, bias = blocks[blk]
        bias4 = jnp.concatenate([bias] * PAIRS_PER_KV, axis=0)
        qs = jnp.concatenate(q_cols[kvh * PAIRS_PER_KV:(kvh + 1) * PAIRS_PER_KV], axis=0)
        return [_mm_nt(qs, kk) + bias4 for kk in (k_lo[kvh], k_hi[kvh])]

    def softmax_terms(g, scores):
        kvh = groups[g][1]
        cols = range(kvh * PAIRS_PER_KV, (kvh + 1) * PAIRS_PER_KV)
        ps, excess = [], []
        for par, s in enumerate(scores):
            row_max = jnp.max(s, axis=-1, keepdims=True)
            m = jnp.concatenate(
                [jnp.maximum(row_max[i * WINDOW:(i + 1) * WINDOW], sinks_ref[0, 2 * c + par])
                 for i, c in enumerate(cols)], axis=0)
            ps.append(jnp.exp(s - m))
            excess.append(jnp.concatenate(
                [sinks_ref[0, 2 * c + par] - m[i * WINDOW:(i + 1) * WINDOW]
                 for i, c in enumerate(cols)], axis=0))
        return ps, jnp.exp(jnp.where(low4, excess[0], excess[1]))

    def value_products(g, ps, sink_term):
        blk, kvh = groups[g]
        v_lo, v_hi = blocks[blk][3], blocks[blk][4]
        ones = jnp.ones_like(v_lo[kvh])
        acc = (_mm(ps[0], jnp.concatenate([v_lo[kvh], jnp.where(low_keys, ones, 0.0)], axis=1))
               + _mm(ps[1], jnp.concatenate([v_hi[kvh], jnp.where(low_keys, 0.0, ones)], axis=1)))
        return acc[:, :LANES] * (1.0 / (acc[:, LANES:] + sink_term))

    scores = [score_products(g) for g in range(len(groups))]
    side = side_product()
    terms = [softmax_terms(g, s) for g, s in enumerate(scores)]
    outs = [value_products(g, ps, sink_term) for g, (ps, sink_term) in enumerate(terms)]
    res = []
    for b in range(len(blocks)):
        cols = []
        for kvh in range(ATT_KVH):
            o = outs[b * ATT_KVH + kvh]
            cols.extend(o[i * WINDOW:(i + 1) * WINDOW] for i in range(PAIRS_PER_KV))
        res.append(jnp.concatenate(cols, axis=1))
    return res, side


def _split_kv_heads(a, fill):
    lane = lax.broadcasted_iota(jnp.int32, a.shape, 1)
    low = lane < ATT_HD
    sw = pltpu.roll(a, ATT_HD, axis=1)
    lo = [jnp.where(low, a, fill), jnp.where(low, sw, fill)]
    hi = [jnp.where(low, fill, sw), jnp.where(low, fill, a)]
    return lo, hi


def _swa_prompt_kernel(final, h_ref, ng_ref, win_ref, sinks_ref, wout_ref, fin_ref,
                       kp_ref, kc_ref, vp_ref, vc_ref, cos_ref, sup_ref, sdn_ref, out_ref):
    t = pl.program_id(1)
    tile = h_ref.shape[1]
    nblk = tile // WINDOW
    x = h_ref[0]
    xn = _rms_scale(x, ng_ref[...]).astype(BF16)
    u = _mm(xn, win_ref[:, :ATT_QD])
    scale = ATT_HD ** -0.5
    cos, sup, sdn = cos_ref[...], sup_ref[...], sdn_ref[...]
    k_all = jnp.concatenate([kp_ref[0], kc_ref[0]], axis=0)
    v_all = jnp.concatenate([vp_ref[0], vc_ref[0]], axis=0)
    tq = lax.broadcasted_iota(jnp.int32, (WINDOW, 2 * WINDOW), 0)
    j = lax.broadcasted_iota(jnp.int32, (WINDOW, 2 * WINDOW), 1)
    rel = tq + WINDOW - j
    band = (rel >= 0) & (rel <= WINDOW)
    blocks = []
    for blk in range(nblk):
        rows = slice(blk * WINDOW, (blk + 1) * WINDOW)
        keys = slice(blk * WINDOW, (blk + 2) * WINDOW)
        k_lo, k_hi = _split_kv_heads(k_all[keys], 0.0)
        v_lo, v_hi = _split_kv_heads(v_all[keys], 0.0)
        if blk == 0:
            valid = band & (j >= WINDOW * (t == 0).astype(jnp.int32))
        else:
            valid = band
        q_cols = []
        for c in range(ATT_QD // LANES):
            qc = u[rows, c * LANES:(c + 1) * LANES]
            q_cols.append(_rope_cols(qc, cos[rows], sup[rows], sdn[rows]) * scale)
        blocks.append((q_cols, k_lo, k_hi, v_lo, v_hi, jnp.where(valid, 0.0, MASK_VALUE)))
    o_blocks, gate = _swa_attention(blocks, sinks_ref, lambda: _mm(xn, win_ref[:, ATT_QD:]))
    o = jnp.concatenate(o_blocks, axis=0) if nblk > 1 else o_blocks[0]
    y = x + _mm(o * _silu(gate), wout_ref[...])
    if final:
        y = _rms_scale(y, fin_ref[...])
    out_ref[0] = y


def _swa_prompt_layer(h, k, v, tables, norm_g, w_in, sinks, w_out, final_norm, final):
    B, T, D = h.shape
    tile = min(SWA_TILE, T)
    per = tile // WINDOW
    const = lambda b, t: (0, 0)
    tab = pl.BlockSpec((tile, LANES), lambda b, t: (t, 0))
    prev = pl.BlockSpec((1, WINDOW, ATT_KVD), lambda b, t: (b, jnp.maximum(t * per - 1, 0), 0))
    cur = pl.BlockSpec((1, tile, ATT_KVD), lambda b, t: (b, t, 0))
    return pl.pallas_call(
        functools.partial(_swa_prompt_kernel, final),
        grid=(B, T // tile),
        in_specs=[
            pl.BlockSpec((1, tile, D), lambda b, t: (b, t, 0)),
            pl.BlockSpec((1, D), const),
            pl.BlockSpec(w_in.shape, const),
            pl.BlockSpec(memory_space=pltpu.SMEM),
            pl.BlockSpec(w_out.shape, const),
            pl.BlockSpec((1, D), const),
            prev, cur, prev, cur, tab, tab, tab,
        ],
        out_specs=pl.BlockSpec((1, tile, D), lambda b, t: (b, t, 0)),
        out_shape=jax.ShapeDtypeStruct((B, T, D), F32),
        compiler_params=_cparams("arbitrary", "arbitrary"),
        name="swa_prompt_final" if final else "swa_prompt",
    )(h, norm_g.reshape(1, D), w_in, sinks.reshape(1, ATT_QH), w_out, final_norm.reshape(1, D),
      k, k, v, v, *tables)


def _col(row):
    n = row.shape[1]
    return jnp.broadcast_to(row, (n, n)).T


def _hgrn_decode_kernel(x_ref, s_ref, ng_ref, win_ref, lbl_ref, gn_ref, wout_ref,
                        snew_ref, hout_ref, h_s, q_s, f_s, v_s, g_s, o_s, oblk_s):
    l, i = pl.program_id(0), pl.program_id(1)
    n_layers, n_blocks = pl.num_programs(0), pl.num_programs(1)
    bb = s_ref.shape[1]

    @pl.when(i == 0)
    def _():
        @pl.when(l == 0)
        def _():
            h_s[...] = x_ref[...]

        u = _mm(_rms_scale(h_s[...], ng_ref[0]), win_ref[0])
        lbl = lbl_ref[...]
        lb = jnp.zeros_like(lbl[:1])
        for layer in range(lbl.shape[0]):
            lb = jnp.where(l == layer, _lower_bound(lbl, layer), lb)
        q_s[...] = _silu(u[:, :HG_KEY])
        f_s[...] = _forget_gates(u[:, HG_KEY:2 * HG_KEY], lb)[0]
        v_s[...] = u[:, 2 * HG_KEY:2 * HG_KEY + HG_VAL]
        g_s[...] = _silu(u[:, 2 * HG_KEY + HG_VAL:])

    rows = pl.ds(pl.multiple_of(i * bb, bb), bb)
    qb, fb, vb = q_s[rows, :], f_s[rows, :], v_s[rows, :]
    for j in range(bb):
        for h in range(HG_HEADS):
            sl = slice(h * HG_DK, (h + 1) * HG_DK)
            v_r = vb[j:j + 1, h * HG_DV:(h + 1) * HG_DV]
            s_new = _col(fb[j:j + 1, sl]) * (s_ref[0, j, h] - v_r) + v_r
            snew_ref[0, j, h] = s_new
            oblk_s[j:j + 1, h * HG_DV:(h + 1) * HG_DV] = _mm(qb[j:j + 1, sl], s_new)
    o_s[rows, :] = oblk_s[...]

    @pl.when(i == n_blocks - 1)
    def _():
        gn = gn_ref[0]
        gated = []
        for h in range(HG_HEADS):
            sl = slice(h * HG_DV, (h + 1) * HG_DV)
            gated.append(_rms_scale(o_s[:, sl], gn) * g_s[:, sl])
        h_new = h_s[...] + _mm(jnp.concatenate(gated, axis=1), wout_ref[0])
        h_s[...] = h_new

        @pl.when(l == n_layers - 1)
        def _():
            hout_ref[...] = h_new


def _whole(shape):
    return pl.BlockSpec(shape, lambda *_: (0,) * len(shape))


def _hgrn_decode(h, states, norms, w_in, lb_logits, gnorms, w_out):
    B, D = h.shape
    n_a = states.shape[0]
    bb = DEC_BLOCK
    per_layer = lambda *tail: pl.BlockSpec((1,) + tail, lambda l, i: (l,) + (0,) * len(tail))
    st_blk = pl.BlockSpec((1, bb, HG_HEADS, HG_DK, HG_DV), lambda l, i: (l, i, 0, 0, 0))
    wide = pltpu.VMEM((B, HG_KEY), F32)
    s_new, h_new = pl.pallas_call(
        _hgrn_decode_kernel,
        grid=(n_a, B // bb),
        in_specs=[_whole((B, D)), st_blk, per_layer(1, D), per_layer(*w_in.shape[1:]),
                  _whole(lb_logits.shape), per_layer(1, HG_DV), per_layer(*w_out.shape[1:])],
        out_specs=[st_blk, _whole((B, D))],
        out_shape=[jax.ShapeDtypeStruct(states.shape, F32), jax.ShapeDtypeStruct((B, D), F32)],
        scratch_shapes=[pltpu.VMEM((B, D), F32)] + [wide] * 5 + [pltpu.VMEM((bb, HG_VAL), F32)],
        compiler_params=_cparams("arbitrary", "arbitrary"),
        name="hgrn_decode",
    )(h, states, norms.reshape(n_a, 1, D), w_in, lb_logits, gnorms.reshape(n_a, 1, HG_DV), w_out)
    return h_new, s_new


def _bf16_round(x):
    return x.astype(BF16).astype(F32)


def _swa_decode_kernel(final, h_ref, ng_ref, win_ref, cos_ref, sup_ref, sdn_ref, kc_ref, vc_ref,
                       kn_ref, vn_ref, sinks_ref, wout_ref, fin_ref, out_ref, qp_s, g_s, os_s):
    i = pl.program_id(0)
    n_blocks = pl.num_programs(0)
    bb, W = kc_ref.shape[0], kc_ref.shape[1]
    B = h_ref.shape[0]
    lane = lax.broadcasted_iota(jnp.int32, (B, LANES), 1)

    @pl.when(i == 0)
    def _():
        u = _mm(_rms_scale(h_ref[...], ng_ref[...]), win_ref[...])
        q = _rope_cols(u[:, :ATT_QD], cos_ref[...], sup_ref[...], sdn_ref[...]) * (ATT_HD ** -0.5)
        g_s[...] = _silu(u[:, ATT_QD:])
        for head in range(ATT_QH):
            col = q[:, (head // 2) * LANES:(head // 2 + 1) * LANES]
            kvh = head // ATT_GROUP
            if head % 2 != kvh:
                col = pltpu.roll(col, ATT_HD, axis=1)
            qp_s[:, head, :] = jnp.where((lane >= ATT_HD) if kvh == 1 else (lane < ATT_HD), col, 0.0)

    rows = pl.ds(pl.multiple_of(i * bb, bb), bb)
    kn, vn = kn_ref[rows, :], vn_ref[rows, :]
    qp = [qp_s[i * bb + j] for j in range(bb)]
    s_c = jnp.concatenate([_mm_nt(qp[j], kc_ref[j]) for j in range(bb)], axis=0)
    qp_all = jnp.concatenate(qp, axis=0)
    expand = lambda a: jnp.concatenate(
        [jnp.broadcast_to(a[j:j + 1], (ATT_QH, ATT_KVD)) for j in range(bb)], axis=0)
    s_n = jnp.sum(_bf16_round(qp_all) * _bf16_round(expand(kn)), axis=-1, keepdims=True)
    sink = jnp.concatenate([sinks_ref[...]] * bb, axis=0)
    m = jnp.maximum(jnp.maximum(jnp.max(s_c, axis=-1, keepdims=True), s_n), sink)
    p_c = jnp.exp(s_c - m)
    p_n = jnp.exp(s_n - m)
    inv = 1.0 / (jnp.sum(p_c, axis=-1, keepdims=True) + p_n + jnp.exp(sink - m))
    p_c = p_c * inv
    o = jnp.concatenate([_mm(p_c[j * ATT_QH:(j + 1) * ATT_QH], vc_ref[j]) for j in range(bb)], axis=0)
    o = o + _bf16_round(p_n * inv) * _bf16_round(expand(vn))
    for j in range(bb):
        os_s[i * bb + j] = o[j * ATT_QH:(j + 1) * ATT_QH]

    @pl.when(i == n_blocks - 1)
    def _():
        cols = []
        for c in range(ATT_QD // LANES):
            halves = []
            for par in range(2):
                head = 2 * c + par
                t = os_s[:, head, :]
                if par != head // ATT_GROUP:
                    t = pltpu.roll(t, ATT_HD, axis=1)
                halves.append(t)
            cols.append(jnp.where(lane < ATT_HD, halves[0], halves[1]))
        y = h_ref[...] + _mm(jnp.concatenate(cols, axis=1) * g_s[...], wout_ref[...])
        if final:
            y = _rms_scale(y, fin_ref[...])
        out_ref[...] = y


def _swa_decode_layer(h, k_new, v_new, cache_k, cache_v, tables, norm_g, w_in, sinks, w_out,
                      final_norm, final):
    B, D = h.shape
    W = cache_k.shape[1]
    bb = DEC_BLOCK
    cache_blk = pl.BlockSpec((bb, W, ATT_KVD), lambda i: (i, 0, 0))
    head_rows = pltpu.VMEM((B, ATT_QH, ATT_KVD), F32)
    return pl.pallas_call(
        functools.partial(_swa_decode_kernel, final),
        grid=(B // bb,),
        in_specs=[_whole((B, D)), _whole((1, D)), _whole(w_in.shape)] + [_whole((1, LANES))] * 3
                 + [cache_blk, cache_blk, _whole((B, ATT_KVD)), _whole((B, ATT_KVD)),
                    _whole((ATT_QH, 1)), _whole(w_out.shape), _whole((1, D))],
        out_specs=_whole((B, D)),
        out_shape=jax.ShapeDtypeStruct((B, D), F32),
        scratch_shapes=[head_rows, pltpu.VMEM((B, ATT_QD), F32), head_rows],
        compiler_params=_cparams("arbitrary"),
        name="swa_decode_final" if final else "swa_decode",
    )(h, norm_g.reshape(1, D), w_in, *tables, cache_k.reshape(B, W, ATT_KVD),
      cache_v.reshape(B, W, ATT_KVD), k_new, v_new, sinks.reshape(ATT_QH, 1), w_out,
      final_norm.reshape(1, D))


def kernel(x_prompt, x_sample, state_hgrn, cache_k_win, cache_v_win, a_norm, a_w_in, a_lb_logits,
           a_gnorm, a_w_out, kv_norm, w_kv, b_norm, b_w_in, b_sinks, b_w_out, final_norm):
    B, T, D = x_prompt.shape
    BD, TD, _ = x_sample.shape
    n_a = a_w_in.shape[0]
    n_b = b_w_in.shape[0]
    a_w_in_b, a_w_out_b = a_w_in.astype(BF16), a_w_out.astype(BF16)
    b_w_in_b, b_w_out_b = b_w_in.astype(BF16), b_w_out.astype(BF16)
    w_kv_b = w_kv.astype(BF16)

    tab_p = _rope_tables(np.arange(T))
    h = x_prompt
    st_p = []
    for l in range(n_a):
        kv = (kv_norm, w_kv_b, tab_p) if l == n_a - 1 else None
        h, s, *kv_p = _hgrn_prompt_layer(h, a_norm[l], a_w_in_b[l], a_lb_logits, l, a_gnorm[l],
                                         a_w_out_b[l], kv)
        st_p.append(s)
    k_p, v_p = kv_p
    for l in range(n_b):
        h = _swa_prompt_layer(h, k_p, v_p, tab_p, b_norm[l], b_w_in_b[l], b_sinks[l], b_w_out_b[l],
                              final_norm, l == n_b - 1)
    y_prompt = h
    w_keep = min(WINDOW, T)
    k_win = k_p[:, T - w_keep:].reshape(B, w_keep, ATT_KVH, ATT_HD)
    v_win = v_p[:, T - w_keep:].reshape(B, w_keep, ATT_KVH, ATT_HD)

    tab_s = _rope_tables(PAST_LEN + np.arange(TD))
    hs = x_sample.reshape(BD * TD, D)
    hs, st_s = _hgrn_decode(hs, state_hgrn, a_norm, a_w_in_b, a_lb_logits, a_gnorm, a_w_out_b)
    k_s, v_s = _shared_kv(hs.reshape(1, BD, D), kv_norm, w_kv_b,
                          tuple(jnp.broadcast_to(t, (BD, LANES)) for t in tab_s))
    k_s, v_s = k_s.reshape(BD, ATT_KVD), v_s.reshape(BD, ATT_KVD)
    for l in range(n_b):
        hs = _swa_decode_layer(hs, k_s, v_s, cache_k_win, cache_v_win, tab_s, b_norm[l], b_w_in_b[l],
                               b_sinks[l], b_w_out_b[l], final_norm, l == n_b - 1)
    y_sample = hs.reshape(BD, TD, D)

    return (y_prompt, y_sample, jnp.stack(st_p), st_s, k_win, v_win,
            k_s.reshape(BD, TD, ATT_KVH, ATT_HD), v_s.reshape(BD, TD, ATT_KVH, ATT_HD))
```

```python
import functools

import numpy as np
import jax
import jax.numpy as jnp
from jax import lax
from jax.experimental import pallas as pl
from jax.experimental.pallas import tpu as pltpu

F32 = jnp.float32
BF16 = jnp.bfloat16

D_MODEL = 1024
HG_HEADS = 8
HG_DK = 128
HG_DV = 128
HG_KEY = HG_HEADS * HG_DK
HG_VAL = HG_HEADS * HG_DV
HG_CHUNK = 64
HG_HALF = HG_CHUNK // 2
HG_SUB = 16
HG_MAX_HALF_DECAY = 60.0
ATT_HD = 64
ATT_QH = 16
ATT_KVH = 2
ATT_GROUP = ATT_QH // ATT_KVH
ATT_QD = ATT_QH * ATT_HD
ATT_KVD = ATT_KVH * ATT_HD
WINDOW = 128
ROPE_THETA = 500000.0
ROT_DIM = ATT_HD // 4
NORM_EPS = 1e-6
MASK_VALUE = -1e30
NEG_BIG = -1e30
PAST_LEN = 8192

SUBLANES = 8
LANES = 128
PROMPT_TILE = 512
SWA_TILE = 512
DEC_BLOCK = 8
VMEM_LIMIT = 48 * 1024 * 1024


def _cparams(*sem):
    return pltpu.CompilerParams(dimension_semantics=sem, vmem_limit_bytes=VMEM_LIMIT)


def _mm(a, b):
    return jnp.dot(a.astype(BF16), b.astype(BF16), preferred_element_type=F32)


def _mm_nt(a, b):
    return lax.dot_general(a.astype(BF16), b.astype(BF16), (((1,), (1,)), ((), ())),
                           preferred_element_type=F32)


def _mm_tn(a, b):
    return lax.dot_general(a.astype(BF16), b.astype(BF16), (((0,), (0,)), ((), ())),
                           preferred_element_type=F32)


def _rms_scale(x, g):
    ms = jnp.mean(x * x, axis=-1, keepdims=True)
    return x * lax.rsqrt(ms + NORM_EPS) * g


def _silu(x):
    return x * (1.0 / (1.0 + jnp.exp(-x)))


def _lower_bound(lbl, layer):
    n = lbl.shape[0]
    rows = [lbl[i:i + 1, :] for i in range(n)]
    m = functools.reduce(jnp.maximum, rows)
    es = [jnp.exp(r - m) for r in rows]
    inv = 1.0 / functools.reduce(jnp.add, es)
    ps = [e * inv for e in es]
    return functools.reduce(jnp.add, ps[:layer + 1]) - ps[0]


def _forget_gates(f, lb):
    e = jnp.exp(-f)
    r = 1.0 / (1.0 + e)
    return lb + (1.0 - lb) * r, (1.0 - lb) * (e * r)


def _chunk_cumsum(x):
    rows = x.shape[0]
    pos = lax.broadcasted_iota(jnp.int32, x.shape, 0) % HG_CHUNK
    s = 1
    while s < HG_CHUNK:
        x = x + jnp.where(pos >= s, pltpu.roll(x, s, axis=0), 0.0)
        s *= 2
    del rows
    return x


def _diag_blocks(q, k, v, b):
    sub_iota = lax.broadcasted_iota(jnp.int32, (SUBLANES, LANES), 0)
    outs = []
    for blk in range(HG_CHUNK // HG_SUB):
        base = blk * HG_SUB
        groups = HG_SUB // SUBLANES
        qs = [q[base + g * SUBLANES: base + (g + 1) * SUBLANES] for g in range(groups)]
        bs = [b[base + g * SUBLANES: base + (g + 1) * SUBLANES] for g in range(groups)]
        accs = [jnp.zeros((SUBLANES, HG_DV), F32) for _ in range(groups)]
        for s in range(HG_SUB):
            row = base + s
            k_s = k[row:row + 1]
            b_s = b[row:row + 1]
            v_s = v[row:row + 1]
            for g in range(s // SUBLANES, groups):
                d = bs[g] - b_s
                if g == s // SUBLANES:
                    d = jnp.where(sub_iota >= (s % SUBLANES), d, NEG_BIG)
                p = qs[g] * (k_s * jnp.exp(d))
                a = jnp.sum(p, axis=-1, keepdims=True)
                accs[g] = accs[g] + a * v_s
        outs.extend(accs)
    return jnp.concatenate(outs, axis=0)


def _hgrn_chunk(q, k, v, b, st):
    C, H = HG_CHUNK, HG_HALF
    o_inter = _mm_nt(q * jnp.exp(b), st)
    r0 = b[H - 1:H]
    a0 = _mm_nt(q[H:] * jnp.exp(b[H:] - r0), k[:H] * jnp.exp(r0 - b[:H]))
    o_hi = _mm(a0, v[:H])
    Q = HG_SUB
    o_q = []
    for base in (0, H):
        r1 = b[base + Q - 1: base + Q]
        a1 = _mm_nt(q[base + Q: base + 2 * Q] * jnp.exp(b[base + Q: base + 2 * Q] - r1),
                    k[base: base + Q] * jnp.exp(r1 - b[base: base + Q]))
        o_q.append(_mm(a1, v[base: base + Q]))
    zero = jnp.zeros((Q, HG_DV), F32)
    o_off = jnp.concatenate([zero, o_q[0], o_hi[:Q], o_hi[Q:] + o_q[1]], axis=0)
    o = o_inter + o_off + _diag_blocks(q, k, v, b)
    b_end = b[C - 1:C]
    st_new = st * jnp.exp(b_end) + _mm_tn(v, k * jnp.exp(b_end - b))
    return o, st_new


def _hgrn_bounded_first(q, k, v, b, st):
    C, H = HG_CHUNK, HG_HALF
    r = b[H - 1:H]
    eq = jnp.exp(b - r)
    qt, kt = q * eq, k * jnp.exp(r - b)
    e_mid, e_hi = jnp.exp(r), eq[C - 1:C]
    scores = _mm_nt(qt, kt)
    v_t = v.T
    st_new = st * (e_mid * e_hi) + _mm(v_t, kt * e_hi)
    return scores, qt * e_mid, v_t, st_new


def _hgrn_bounded_second(scores, q_dec, v_t, st):
    t_i = lax.broadcasted_iota(jnp.int32, scores.shape, 0)
    s_i = lax.broadcasted_iota(jnp.int32, scores.shape, 1)
    a = jnp.where(s_i <= t_i, scores, 0.0)
    return _mm_nt(jnp.concatenate([q_dec, a], axis=1), jnp.concatenate([st, v_t], axis=1))


def _max_half_decay(b_s, tile):
    n = tile // HG_CHUNK
    worst = None
    for h in range(HG_HEADS):
        mid = b_s[h, pl.ds(HG_HALF - 1, n, stride=HG_CHUNK), :]
        end = b_s[h, pl.ds(HG_CHUNK - 1, n, stride=HG_CHUNK), :]
        w = jnp.maximum(-mid, mid - end)
        worst = w if worst is None else jnp.maximum(worst, w)
    return jnp.max(worst)


def _hgrn_prompt_kernel(layer, with_kv, *refs):
    h_ref, ng_ref, win_ref, lbl_ref, gn_ref, wout_ref = refs[:6]
    refs = refs[6:]
    if with_kv:
        kv_in, refs = refs[:5], refs[5:]
    out_ref, sfin_ref = refs[:2]
    refs = refs[2:]
    if with_kv:
        kv_out, refs = refs[:2], refs[2:]
    q_s, k_s, v_s, b_s, o_s, st_s = refs
    t = pl.program_id(1)
    nt = pl.num_programs(1)
    tile = h_ref.shape[1]

    @pl.when(t == 0)
    def _():
        st_s[...] = jnp.zeros_like(st_s)

    x = h_ref[0]
    u = _mm(_rms_scale(x, ng_ref[...]), win_ref[...])
    lb = _lower_bound(lbl_ref[...], layer)
    for h in range(HG_HEADS):
        sl = slice(h * HG_DK, (h + 1) * HG_DK)
        q_s[h] = _silu(u[:, sl])
        fg, kin = _forget_gates(u[:, HG_KEY + h * HG_DK: HG_KEY + (h + 1) * HG_DK], lb[:, sl])
        k_s[h] = kin
        b_s[h] = _chunk_cumsum(jnp.log(fg))
        v_s[h] = u[:, 2 * HG_KEY + h * HG_DV: 2 * HG_KEY + (h + 1) * HG_DV]

    def chunk_rows(c):
        return pl.ds(pl.multiple_of(c * HG_CHUNK, HG_CHUNK), HG_CHUNK)

    def load(c, h):
        rows = chunk_rows(c)
        return q_s[h, rows, :], k_s[h, rows, :], v_s[h, rows, :], b_s[h, rows, :], st_s[h]

    def bounded_chunks():
        pending = []
        for c in range(tile // HG_CHUNK):
            rows = slice(c * HG_CHUNK, (c + 1) * HG_CHUNK)
            started = []
            for h in range(HG_HEADS):
                st = st_s[h]
                scores, q_dec, v_t, st_new = _hgrn_bounded_first(
                    q_s[h, rows, :], k_s[h, rows, :], v_s[h, rows, :], b_s[h, rows, :], st)
                st_s[h] = st_new
                started.append((h, rows, scores, q_dec, v_t, st))
            for h, prev_rows, *second in pending:
                o_s[h, prev_rows, :] = _hgrn_bounded_second(*second)
            pending = started
        for h, prev_rows, *second in pending:
            o_s[h, prev_rows, :] = _hgrn_bounded_second(*second)

    def general_body(i, carry):
        c, h = i // HG_HEADS, i % HG_HEADS
        o, st_new = _hgrn_chunk(*load(c, h))
        o_s[h, chunk_rows(c), :] = o
        st_s[h] = st_new
        return carry

    bounded = _max_half_decay(b_s, tile) < HG_MAX_HALF_DECAY

    pl.when(bounded)(bounded_chunks)

    @pl.when(jnp.logical_not(bounded))
    def _():
        lax.fori_loop(0, (tile // HG_CHUNK) * HG_HEADS, general_body, 0)

    gn = gn_ref[...]
    gated = []
    for h in range(HG_HEADS):
        g = u[:, 2 * HG_KEY + HG_VAL + h * HG_DV: 2 * HG_KEY + HG_VAL + (h + 1) * HG_DV]
        gated.append(_rms_scale(o_s[h], gn) * _silu(g))
    y = x + _mm(jnp.concatenate(gated, axis=1), wout_ref[...])
    out_ref[0] = y
    if with_kv:
        _kv_project(y, *kv_in, *kv_out)

    @pl.when(t == nt - 1)
    def _():
        for h in range(HG_HEADS):
            sfin_ref[0, h] = st_s[h].T


def _hgrn_prompt_layer(h, norm_g, w_in, lb_logits, layer, gnorm, w_out, kv=None):
    B, T, D = h.shape
    tile = min(PROMPT_TILE, T)
    n_a = lb_logits.shape[0]
    const = lambda b, t: (0, 0)
    row_tile = lambda width: pl.BlockSpec((1, tile, width), lambda b, t: (b, t, 0))
    head_scratch = pltpu.VMEM((HG_HEADS, tile, HG_DK), F32)
    operands = [h, norm_g.reshape(1, D), w_in, lb_logits, gnorm.reshape(1, HG_DV), w_out]
    resident = lambda shape: pl.BlockSpec(shape, const, pipeline_mode=pl.Buffered(1))
    in_specs = [row_tile(D), pl.BlockSpec((1, D), const), resident(w_in.shape),
                pl.BlockSpec((n_a, HG_KEY), const), pl.BlockSpec((1, HG_DV), const),
                resident(w_out.shape)]
    out_specs = [row_tile(D), pl.BlockSpec((1, HG_HEADS, HG_DK, HG_DV), lambda b, t: (b, 0, 0, 0))]
    out_shape = [jax.ShapeDtypeStruct((B, T, D), F32),
                 jax.ShapeDtypeStruct((B, HG_HEADS, HG_DK, HG_DV), F32)]
    if kv is not None:
        kv_norm, w_kv, tables = kv
        operands += [kv_norm.reshape(1, D), w_kv, *tables]
        in_specs += [pl.BlockSpec((1, D), const), pl.BlockSpec(w_kv.shape, const)]
        in_specs += [pl.BlockSpec((tile, LANES), lambda b, t: (t, 0))] * 3
        out_specs += [row_tile(ATT_KVD)] * 2
        out_shape += [jax.ShapeDtypeStruct((B, T, ATT_KVD), F32)] * 2
    return pl.pallas_call(
        functools.partial(_hgrn_prompt_kernel, layer, kv is not None),
        grid=(B, T // tile),
        in_specs=in_specs,
        out_specs=out_specs,
        out_shape=out_shape,
        scratch_shapes=[head_scratch] * 5 + [pltpu.VMEM((HG_HEADS, HG_DV, HG_DK), F32)],
        compiler_params=_cparams("arbitrary", "arbitrary"),
        name=f"hgrn_prompt_l{layer}",
    )(*operands)


def _rope_tables(pos):
    half = ROT_DIM // 2
    pos = np.asarray(pos, np.float64)
    inv_freq = 1.0 / (ROPE_THETA ** (np.arange(half, dtype=np.float64) * 2.0 / ROT_DIM))
    ang = (pos[:, None].astype(np.float32) * inv_freq[None, :].astype(np.float32)).astype(np.float64)
    cos, sin = np.cos(ang), np.sin(ang)
    n = pos.shape[0]
    c = np.ones((n, ATT_HD)); s_up = np.zeros((n, ATT_HD)); s_dn = np.zeros((n, ATT_HD))
    c[:, :half] = cos; c[:, half:ROT_DIM] = cos
    s_dn[:, :half] = -sin
    s_up[:, half:ROT_DIM] = sin
    rep = LANES // ATT_HD
    tab = [np.tile(a, (1, rep)).astype(np.float32) for a in (c, s_up, s_dn)]
    return tuple(jnp.asarray(a) for a in tab)


def _rope_cols(x, cos, s_up, s_dn):
    half = ROT_DIM // 2
    cols = []
    for c in range(x.shape[1] // LANES):
        xc = x[:, c * LANES:(c + 1) * LANES]
        cols.append(xc * cos + pltpu.roll(xc, half, axis=1) * s_up
                    + pltpu.roll(xc, LANES - half, axis=1) * s_dn)
    return cols[0] if len(cols) == 1 else jnp.concatenate(cols, axis=1)


def _kv_kernel(h_ref, *refs):
    _kv_project(h_ref[0], *refs)


def _kv_project(x, ng_ref, w_ref, cos_ref, sup_ref, sdn_ref, k_ref, v_ref):
    u = _mm(_rms_scale(x, ng_ref[...]), w_ref[...])
    k_ref[0] = _rope_cols(u[:, :ATT_KVD], cos_ref[...], sup_ref[...], sdn_ref[...])
    v_ref[0] = u[:, ATT_KVD:]


def _shared_kv(h, kv_norm, w_kv, tables):
    B, T, D = h.shape
    tile = min(PROMPT_TILE, T)
    const = lambda b, t: (0, 0)
    tab = pl.BlockSpec((tile, LANES), lambda b, t: (t, 0))
    kv_spec = pl.BlockSpec((1, tile, ATT_KVD), lambda b, t: (b, t, 0))
    return pl.pallas_call(
        _kv_kernel,
        grid=(B, T // tile),
        in_specs=[pl.BlockSpec((1, tile, D), lambda b, t: (b, t, 0)),
                  pl.BlockSpec((1, D), const), pl.BlockSpec(w_kv.shape, const), tab, tab, tab],
        out_specs=[kv_spec, kv_spec],
        out_shape=[jax.ShapeDtypeStruct((B, T, ATT_KVD), F32)] * 2,
        compiler_params=_cparams("arbitrary", "arbitrary"),
        name="shared_kv",
    )(h, kv_norm.reshape(1, D), w_kv, *tables)


PAIRS_PER_KV = ATT_GROUP // 2


def _swa_attention(blocks, sinks_ref, side_product):
    low = lax.broadcasted_iota(jnp.int32, (WINDOW, LANES), 1) < ATT_HD
    low4 = jnp.concatenate([low] * PAIRS_PER_KV, axis=0)
    low_keys = lax.broadcasted_iota(jnp.int32, blocks[0][3][0].shape, 1) < ATT_HD
    groups =[(blk, kvh) for blk in range(len(blocks)) for kvh in range(ATT_KVH)]

    def score_products(g):
        blk, kvh = groups[g]
        q_cols, k_lo, k_hi, _, _, bias = blocks[blk]
        bias4 = jnp.concatenate([bias] * PAIRS_PER_KV, axis=0)
        qs = jnp.concatenate(q_cols[kvh * PAIRS_PER_KV:(kvh + 1) * PAIRS_PER_KV], axis=0)
        return [_mm_nt(qs, kk) + bias4 for kk in (k_lo[kvh], k_hi[kvh])]

    def softmax_terms(g, scores):
        kvh = groups[g][1]
        cols = range(kvh * PAIRS_PER_KV, (kvh + 1) * PAIRS_PER_KV)
        ps, excess = [], []
        for par, s in enumerate(scores):
            row_max = jnp.max(s, axis=-1, keepdims=True)
            m = jnp.concatenate(
                [jnp.maximum(row_max[i * WINDOW:(i + 1) * WINDOW], sinks_ref[0, 2 * c + par])
                 for i, c in enumerate(cols)], axis=0)
            ps.append(jnp.exp(s - m))
            excess.append(jnp.concatenate(
                [sinks_ref[0, 2 * c + par] - m[i * WINDOW:(i + 1) * WINDOW]
                 for i, c in enumerate(cols)], axis=0))
        return ps, jnp.exp(jnp.where(low4, excess[0], excess[1]))

    def value_products(g, ps, sink_term):
        blk, kvh = groups[g]
        v_lo, v_hi = blocks[blk][3], blocks[blk][4]
        ones = jnp.ones_like(v_lo[kvh])
        acc = (_mm(ps[0], jnp.concatenate([v_lo[kvh], jnp.where(low_keys, ones, 0.0)], axis=1))
               + _mm(ps[1], jnp.concatenate([v_hi[kvh], jnp.where(low_keys, 0.0, ones)], axis=1)))
        return acc[:, :LANES] * (1.0 / (acc[:, LANES:] + sink_term))

    scores = [score_products(g) for g in range(len(groups))]
    side = side_product()
    terms = [softmax_terms(g, s) for g, s in enumerate(scores)]
    outs = [value_products(g, ps, sink_term) for g, (ps, sink_term) in enumerate(terms)]
    res = []
    for b in range(len(blocks)):
        cols = []
        for kvh in range(ATT_KVH):
            o = outs[b * ATT_KVH + kvh]
            cols.extend(o[i * WINDOW:(i + 1) * WINDOW] for i in range(PAIRS_PER_KV))
        res.append(jnp.concatenate(cols, axis=1))
    return res, side


def _split_kv_heads(a, fill):
    lane = lax.broadcasted_iota(jnp.int32, a.shape, 1)
    low = lane < ATT_HD
    sw = pltpu.roll(a, ATT_HD, axis=1)
    lo = [jnp.where(low, a, fill), jnp.where(low, sw, fill)]
    hi = [jnp.where(low, fill, sw), jnp.where(low, fill, a)]
    return lo, hi


def _swa_prompt_kernel(final, h_ref, ng_ref, win_ref, sinks_ref, wout_ref, fin_ref,
                       kp_ref, kc_ref, vp_ref, vc_ref, cos_ref, sup_ref, sdn_ref, out_ref):
    t = pl.program_id(1)
    tile = h_ref.shape[1]
    nblk = tile // WINDOW
    x = h_ref[0]
    xn = _rms_scale(x, ng_ref[...]).astype(BF16)
    u = _mm(xn, win_ref[:, :ATT_QD])
    scale = ATT_HD ** -0.5
    cos, sup, sdn = cos_ref[...], sup_ref[...], sdn_ref[...]
    k_all = jnp.concatenate([kp_ref[0], kc_ref[0]], axis=0)
    v_all = jnp.concatenate([vp_ref[0], vc_ref[0]], axis=0)
    tq = lax.broadcasted_iota(jnp.int32, (WINDOW, 2 * WINDOW), 0)
    j = lax.broadcasted_iota(jnp.int32, (WINDOW, 2 * WINDOW), 1)
    rel = tq + WINDOW - j
    band = (rel >= 0) & (rel <= WINDOW)
    blocks = []
    for blk in range(nblk):
        rows = slice(blk * WINDOW, (blk + 1) * WINDOW)
        keys = slice(blk * WINDOW, (blk + 2) * WINDOW)
        k_lo, k_hi = _split_kv_heads(k_all[keys], 0.0)
        v_lo, v_hi = _split_kv_heads(v_all[keys], 0.0)
        if blk == 0:
            valid = band & (j >= WINDOW * (t == 0).astype(jnp.int32))
        else:
            valid = band
        q_cols = []
        for c in range(ATT_QD // LANES):
            qc = u[rows, c * LANES:(c + 1) * LANES]
            q_cols.append(_rope_cols(qc, cos[rows], sup[rows], sdn[rows]) * scale)
        blocks.append((q_cols, k_lo, k_hi, v_lo, v_hi, jnp.where(valid, 0.0, MASK_VALUE)))
    o_blocks, gate = _swa_attention(blocks, sinks_ref, lambda: _mm(xn, win_ref[:, ATT_QD:]))
    o = jnp.concatenate(o_blocks, axis=0) if nblk > 1 else o_blocks[0]
    y = x + _mm(o * _silu(gate), wout_ref[...])
    if final:
        y = _rms_scale(y, fin_ref[...])
    out_ref[0] = y


def _swa_prompt_layer(h, k, v, tables, norm_g, w_in, sinks, w_out, final_norm, final):
    B, T, D = h.shape
    tile = min(SWA_TILE, T)
    per = tile // WINDOW
    const = lambda b, t: (0, 0)
    tab = pl.BlockSpec((tile, LANES), lambda b, t: (t, 0))
    prev = pl.BlockSpec((1, WINDOW, ATT_KVD), lambda b, t: (b, jnp.maximum(t * per - 1, 0), 0))
    cur = pl.BlockSpec((1, tile, ATT_KVD), lambda b, t: (b, t, 0))
    return pl.pallas_call(
        functools.partial(_swa_prompt_kernel, final),
        grid=(B, T // tile),
        in_specs=[
            pl.BlockSpec((1, tile, D), lambda b, t: (b, t, 0)),
            pl.BlockSpec((1, D), const),
            pl.BlockSpec(w_in.shape, const),
            pl.BlockSpec(memory_space=pltpu.SMEM),
            pl.BlockSpec(w_out.shape, const),
            pl.BlockSpec((1, D), const),
            prev, cur, prev, cur, tab, tab, tab,
        ],
        out_specs=pl.BlockSpec((1, tile, D), lambda b, t: (b, t, 0)),
        out_shape=jax.ShapeDtypeStruct((B, T, D), F32),
        compiler_params=_cparams("arbitrary", "arbitrary"),
        name="swa_prompt_final" if final else "swa_prompt",
    )(h, norm_g.reshape(1, D), w_in, sinks.reshape(1, ATT_QH), w_out, final_norm.reshape(1, D),
      k, k, v, v, *tables)


def _col(row):
    n = row.shape[1]
    return jnp.broadcast_to(row, (n, n)).T


def _hgrn_decode_kernel(x_ref, s_ref, ng_ref, win_ref, lbl_ref, gn_ref, wout_ref,
                        snew_ref, hout_ref, h_s, q_s, f_s, v_s, g_s, o_s, oblk_s):
    l, i = pl.program_id(0), pl.program_id(1)
    n_layers, n_blocks = pl.num_programs(0), pl.num_programs(1)
    bb = s_ref.shape[1]

    @pl.when(i == 0)
    def _():
        @pl.when(l == 0)
        def _():
            h_s[...] = x_ref[...]

        u = _mm(_rms_scale(h_s[...], ng_ref[0]), win_ref[0])
        lbl = lbl_ref[...]
        lb = jnp.zeros_like(lbl[:1])
        for layer in range(lbl.shape[0]):
            lb = jnp.where(l == layer, _lower_bound(lbl, layer), lb)
        q_s[...] = _silu(u[:, :HG_KEY])
        f_s[...] = _forget_gates(u[:, HG_KEY:2 * HG_KEY], lb)[0]
        v_s[...] = u[:, 2 * HG_KEY:2 * HG_KEY + HG_VAL]
        g_s[...] = _silu(u[:, 2 * HG_KEY + HG_VAL:])

    rows = pl.ds(pl.multiple_of(i * bb, bb), bb)
    qb, fb, vb = q_s[rows, :], f_s[rows, :], v_s[rows, :]
    for j in range(bb):
        for h in range(HG_HEADS):
            sl = slice(h * HG_DK, (h + 1) * HG_DK)
            v_r = vb[j:j + 1, h * HG_DV:(h + 1) * HG_DV]
            s_new = _col(fb[j:j + 1, sl]) * (s_ref[0, j, h] - v_r) + v_r
            snew_ref[0, j, h] = s_new
            oblk_s[j:j + 1, h * HG_DV:(h + 1) * HG_DV] = _mm(qb[j:j + 1, sl], s_new)
    o_s[rows, :] = oblk_s[...]

    @pl.when(i == n_blocks - 1)
    def _():
        gn = gn_ref[0]
        gated = []
        for h in range(HG_HEADS):
            sl = slice(h * HG_DV, (h + 1) * HG_DV)
            gated.append(_rms_scale(o_s[:, sl], gn) * g_s[:, sl])
        h_new = h_s[...] + _mm(jnp.concatenate(gated, axis=1), wout_ref[0])
        h_s[...] = h_new

        @pl.when(l == n_layers - 1)
        def _():
            hout_ref[...] = h_new


def _whole(shape):
    return pl.BlockSpec(shape, lambda *_: (0,) * len(shape))


def _hgrn_decode(h, states, norms, w_in, lb_logits, gnorms, w_out):
    B, D = h.shape
    n_a = states.shape[0]
    bb = DEC_BLOCK
    per_layer = lambda *tail: pl.BlockSpec((1,) + tail, lambda l, i: (l,) + (0,) * len(tail))
    st_blk = pl.BlockSpec((1, bb, HG_HEADS, HG_DK, HG_DV), lambda l, i: (l, i, 0, 0, 0))
    wide = pltpu.VMEM((B, HG_KEY), F32)
    s_new, h_new = pl.pallas_call(
        _hgrn_decode_kernel,
        grid=(n_a, B // bb),
        in_specs=[_whole((B, D)), st_blk, per_layer(1, D), per_layer(*w_in.shape[1:]),
                  _whole(lb_logits.shape), per_layer(1, HG_DV), per_layer(*w_out.shape[1:])],
        out_specs=[st_blk, _whole((B, D))],
        out_shape=[jax.ShapeDtypeStruct(states.shape, F32), jax.ShapeDtypeStruct((B, D), F32)],
        scratch_shapes=[pltpu.VMEM((B, D), F32)] + [wide] * 5 + [pltpu.VMEM((bb, HG_VAL), F32)],
        compiler_params=_cparams("arbitrary", "arbitrary"),
        name="hgrn_decode",
    )(h, states, norms.reshape(n_a, 1, D), w_in, lb_logits, gnorms.reshape(n_a, 1, HG_DV), w_out)
    return h_new, s_new


def _bf16_round(x):
    return x.astype(BF16).astype(F32)


def _swa_decode_kernel(final, h_ref, ng_ref, win_ref, cos_ref, sup_ref, sdn_ref, kc_ref, vc_ref,
                       kn_ref, vn_ref, sinks_ref, wout_ref, fin_ref, out_ref, qp_s, g_s, os_s):
    i = pl.program_id(0)
    n_blocks = pl.num_programs(0)
    bb, W = kc_ref.shape[0], kc_ref.shape[1]
    B = h_ref.shape[0]
    lane = lax.broadcasted_iota(jnp.int32, (B, LANES), 1)

    @pl.when(i == 0)
    def _():
        u = _mm(_rms_scale(h_ref[...], ng_ref[...]), win_ref[...])
        q = _rope_cols(u[:, :ATT_QD], cos_ref[...], sup_ref[...], sdn_ref[...]) * (ATT_HD ** -0.5)
        g_s[...] = _silu(u[:, ATT_QD:])
        for head in range(ATT_QH):
            col = q[:, (head // 2) * LANES:(head // 2 + 1) * LANES]
            kvh = head // ATT_GROUP
            if head % 2 != kvh:
                col = pltpu.roll(col, ATT_HD, axis=1)
            qp_s[:, head, :] = jnp.where((lane >= ATT_HD) if kvh == 1 else (lane < ATT_HD), col, 0.0)

    rows = pl.ds(pl.multiple_of(i * bb, bb), bb)
    kn, vn = kn_ref[rows, :], vn_ref[rows, :]
    qp = [qp_s[i * bb + j] for j in range(bb)]
    s_c = jnp.concatenate([_mm_nt(qp[j], kc_ref[j]) for j in range(bb)], axis=0)
    qp_all = jnp.concatenate(qp, axis=0)
    expand = lambda a: jnp.concatenate(
        [jnp.broadcast_to(a[j:j + 1], (ATT_QH, ATT_KVD)) for j in range(bb)], axis=0)
    s_n = jnp.sum(_bf16_round(qp_all) * _bf16_round(expand(kn)), axis=-1, keepdims=True)
    sink = jnp.concatenate([sinks_ref[...]] * bb, axis=0)
    m = jnp.maximum(jnp.maximum(jnp.max(s_c, axis=-1, keepdims=True), s_n), sink)
    p_c = jnp.exp(s_c - m)
    p_n = jnp.exp(s_n - m)
    inv = 1.0 / (jnp.sum(p_c, axis=-1, keepdims=True) + p_n + jnp.exp(sink - m))
    p_c = p_c * inv
    o = jnp.concatenate([_mm(p_c[j * ATT_QH:(j + 1) * ATT_QH], vc_ref[j]) for j in range(bb)], axis=0)
    o = o + _bf16_round(p_n * inv) * _bf16_round(expand(vn))
    for j in range(bb):
        os_s[i * bb + j] = o[j * ATT_QH:(j + 1) * ATT_QH]

    @pl.when(i == n_blocks - 1)
    def _():
        cols = []
        for c in range(ATT_QD // LANES):
            halves = []
            for par in range(2):
                head = 2 * c + par
                t = os_s[:, head, :]
                if par != head // ATT_GROUP:
                    t = pltpu.roll(t, ATT_HD, axis=1)
                halves.append(t)
            cols.append(jnp.where(lane < ATT_HD, halves[0], halves[1]))
        y = h_ref[...] + _mm(jnp.concatenate(cols, axis=1) * g_s[...], wout_ref[...])
        if final:
            y = _rms_scale(y, fin_ref[...])
        out_ref[...] = y


def _swa_decode_layer(h, k_new, v_new, cache_k, cache_v, tables, norm_g, w_in, sinks, w_out,
                      final_norm, final):
    B, D = h.shape
    W = cache_k.shape[1]
    bb = DEC_BLOCK
    cache_blk = pl.BlockSpec((bb, W, ATT_KVD), lambda i: (i, 0, 0))
    head_rows = pltpu.VMEM((B, ATT_QH, ATT_KVD), F32)
    return pl.pallas_call(
        functools.partial(_swa_decode_kernel, final),
        grid=(B // bb,),
        in_specs=[_whole((B, D)), _whole((1, D)), _whole(w_in.shape)] + [_whole((1, LANES))] * 3
                 + [cache_blk, cache_blk, _whole((B, ATT_KVD)), _whole((B, ATT_KVD)),
                    _whole((ATT_QH, 1)), _whole(w_out.shape), _whole((1, D))],
        out_specs=_whole((B, D)),
        out_shape=jax.ShapeDtypeStruct((B, D), F32),
        scratch_shapes=[head_rows, pltpu.VMEM((B, ATT_QD), F32), head_rows],
        compiler_params=_cparams("arbitrary"),
        name="swa_decode_final" if final else "swa_decode",
    )(h, norm_g.reshape(1, D), w_in, *tables, cache_k.reshape(B, W, ATT_KVD),
      cache_v.reshape(B, W, ATT_KVD), k_new, v_new, sinks.reshape(ATT_QH, 1), w_out,
      final_norm.reshape(1, D))


def kernel(x_prompt, x_sample, state_hgrn, cache_k_win, cache_v_win, a_norm, a_w_in, a_lb_logits,
           a_gnorm, a_w_out, kv_norm, w_kv, b_norm, b_w_in, b_sinks, b_w_out, final_norm):
    B, T, D = x_prompt.shape
    BD, TD, _ = x_sample.shape
    n_a = a_w_in.shape[0]
    n_b = b_w_in.shape[0]
    a_w_in_b, a_w_out_b = a_w_in.astype(BF16), a_w_out.astype(BF16)
    b_w_in_b, b_w_out_b = b_w_in.astype(BF16), b_w_out.astype(BF16)
    w_kv_b = w_kv.astype(BF16)

    tab_p = _rope_tables(np.arange(T))
    h = x_prompt
    st_p = []
    for l in range(n_a):
        kv = (kv_norm, w_kv_b, tab_p) if l == n_a - 1 else None
        h, s, *kv_p = _hgrn_prompt_layer(h, a_norm[l], a_w_in_b[l], a_lb_logits, l, a_gnorm[l],
                                         a_w_out_b[l], kv)
        st_p.append(s)
    k_p, v_p = kv_p
    for l in range(n_b):
        h = _swa_prompt_layer(h, k_p, v_p, tab_p, b_norm[l], b_w_in_b[l], b_sinks[l], b_w_out_b[l],
                              final_norm, l == n_b - 1)
    y_prompt = h
    w_keep = min(WINDOW, T)
    k_win = k_p[:, T - w_keep:].reshape(B, w_keep, ATT_KVH, ATT_HD)
    v_win = v_p[:, T - w_keep:].reshape(B, w_keep, ATT_KVH, ATT_HD)

    tab_s = _rope_tables(PAST_LEN + np.arange(TD))
    hs = x_sample.reshape(BD * TD, D)
    hs, st_s = _hgrn_decode(hs, state_hgrn, a_norm, a_w_in_b, a_lb_logits, a_gnorm, a_w_out_b)
    k_s, v_s = _shared_kv(hs.reshape(1, BD, D), kv_norm, w_kv_b,
                          tuple(jnp.broadcast_to(t, (BD, LANES)) for t in tab_s))
    k_s, v_s = k_s.reshape(BD, ATT_KVD), v_s.reshape(BD, ATT_KVD)
    for l in range(n_b):
        hs = _swa_decode_layer(hs, k_s, v_s, cache_k_win, cache_v_win, tab_s, b_norm[l], b_w_in_b[l],
                               b_sinks[l], b_w_out_b[l], final_norm, l == n_b - 1)
    y_sample = hs.reshape(BD, TD, D)

    return (y_prompt, y_sample, jnp.stack(st_p), st_s, k_win, v_win,
            k_s.reshape(BD, TD, ATT_KVH, ATT_HD), v_s.reshape(BD, TD, ATT_KVH, ATT_HD))
```

```python
import functools

import numpy as np
import jax
import jax.numpy as jnp
from jax import lax
from jax.experimental import pallas as pl
from jax.experimental.pallas import tpu as pltpu

F32 = jnp.float32
BF16 = jnp.bfloat16

D_MODEL = 1024
HG_HEADS = 8
HG_DK = 128
HG_DV = 128
HG_KEY = HG_HEADS * HG_DK
HG_VAL = HG_HEADS * HG_DV
HG_CHUNK = 64
HG_HALF = HG_CHUNK // 2
HG_SUB = 16
HG_MAX_HALF_DECAY = 60.0
ATT_HD = 64
ATT_QH = 16
ATT_KVH = 2
ATT_GROUP = ATT_QH // ATT_KVH
ATT_QD = ATT_QH * ATT_HD
ATT_KVD = ATT_KVH * ATT_HD
WINDOW = 128
ROPE_THETA = 500000.0
ROT_DIM = ATT_HD // 4
NORM_EPS = 1e-6
MASK_VALUE = -1e30
NEG_BIG = -1e30
PAST_LEN = 8192

SUBLANES = 8
LANES = 128
PROMPT_TILE = 512
SWA_TILE = 512
DEC_BLOCK = 8
VMEM_LIMIT = 48 * 1024 * 1024


def _cparams(*sem):
    return pltpu.CompilerParams(dimension_semantics=sem, vmem_limit_bytes=VMEM_LIMIT)


def _layer_block(stacked, layer, single_buffer=False):
    tail = stacked.shape[1:]
    mode = dict(pipeline_mode=pl.Buffered(1)) if single_buffer else {}
    return pl.BlockSpec((None,) + tail, lambda *_: (layer,) + (0,) * len(tail), **mode)


def _mm(a, b):
    return jnp.dot(a.astype(BF16), b.astype(BF16), preferred_element_type=F32)


def _mm_nt(a, b):
    return lax.dot_general(a.astype(BF16), b.astype(BF16), (((1,), (1,)), ((), ())),
                           preferred_element_type=F32)


def _mm_tn(a, b):
    return lax.dot_general(a.astype(BF16), b.astype(BF16), (((0,), (0,)), ((), ())),
                           preferred_element_type=F32)


def _rms_scale(x, g):
    ms = jnp.mean(x * x, axis=-1, keepdims=True)
    return x * lax.rsqrt(ms + NORM_EPS) * g


def _silu(x):
    return x * (1.0 / (1.0 + jnp.exp(-x)))


def _lower_bound(lbl, layer):
    n = lbl.shape[0]
    rows = [lbl[i:i + 1, :] for i in range(n)]
    m = functools.reduce(jnp.maximum, rows)
    es = [jnp.exp(r - m) for r in rows]
    inv = 1.0 / functools.reduce(jnp.add, es)
    ps = [e * inv for e in es]
    return functools.reduce(jnp.add, ps[:layer + 1]) - ps[0]


def _forget_gates(f, lb):
    e = jnp.exp(-f)
    r = 1.0 / (1.0 + e)
    return lb + (1.0 - lb) * r, (1.0 - lb) * (e * r)


def _chunk_cumsum(x):
    rows = x.shape[0]
    pos = lax.broadcasted_iota(jnp.int32, x.shape, 0) % HG_CHUNK
    s = 1
    while s < HG_CHUNK:
        x = x + jnp.where(pos >= s, pltpu.roll(x, s, axis=0), 0.0)
        s *= 2
    del rows
    return x


def _diag_blocks(q, k, v, b):
    sub_iota = lax.broadcasted_iota(jnp.int32, (SUBLANES, LANES), 0)
    outs = []
    for blk in range(HG_CHUNK // HG_SUB):
        base = blk * HG_SUB
        groups = HG_SUB // SUBLANES
        qs = [q[base + g * SUBLANES: base + (g + 1) * SUBLANES] for g in range(groups)]
        bs = [b[base + g * SUBLANES: base + (g + 1) * SUBLANES] for g in range(groups)]
        accs = [jnp.zeros((SUBLANES, HG_DV), F32) for _ in range(groups)]
        for s in range(HG_SUB):
            row = base + s
            k_s = k[row:row + 1]
            b_s = b[row:row + 1]
            v_s = v[row:row + 1]
            for g in range(s // SUBLANES, groups):
                d = bs[g] - b_s
                if g == s // SUBLANES:
                    d = jnp.where(sub_iota >= (s % SUBLANES), d, NEG_BIG)
                p = qs[g] * (k_s * jnp.exp(d))
                a = jnp.sum(p, axis=-1, keepdims=True)
                accs[g] = accs[g] + a * v_s
        outs.extend(accs)
    return jnp.concatenate(outs, axis=0)


def _hgrn_chunk(q, k, v, b, st):
    C, H = HG_CHUNK, HG_HALF
    o_inter = _mm_nt(q * jnp.exp(b), st)
    r0 = b[H - 1:H]
    a0 = _mm_nt(q[H:] * jnp.exp(b[H:] - r0), k[:H] * jnp.exp(r0 - b[:H]))
    o_hi = _mm(a0, v[:H])
    Q = HG_SUB
    o_q = []
    for base in (0, H):
        r1 = b[base + Q - 1: base + Q]
        a1 = _mm_nt(q[base + Q: base + 2 * Q] * jnp.exp(b[base + Q: base + 2 * Q] - r1),
                    k[base: base + Q] * jnp.exp(r1 - b[base: base + Q]))
        o_q.append(_mm(a1, v[base: base + Q]))
    zero = jnp.zeros((Q, HG_DV), F32)
    o_off = jnp.concatenate([zero, o_q[0], o_hi[:Q], o_hi[Q:] + o_q[1]], axis=0)
    o = o_inter + o_off + _diag_blocks(q, k, v, b)
    b_end = b[C - 1:C]
    st_new = st * jnp.exp(b_end) + _mm_tn(v, k * jnp.exp(b_end - b))
    return o, st_new


def _hgrn_bounded_first(q, k, v, b, st):
    C, H = HG_CHUNK, HG_HALF
    r = b[H - 1:H]
    eq = jnp.exp(b - r)
    qt, kt = q * eq, k * jnp.exp(r - b)
    e_mid, e_hi = jnp.exp(r), eq[C - 1:C]
    scores = _mm_nt(qt, kt)
    v_t = v.T
    st_new = st * (e_mid * e_hi) + _mm(v_t, kt * e_hi)
    return scores, qt * e_mid, v_t, st_new


def _hgrn_bounded_second(scores, q_dec, v_t, st):
    t_i = lax.broadcasted_iota(jnp.int32, scores.shape, 0)
    s_i = lax.broadcasted_iota(jnp.int32, scores.shape, 1)
    a = jnp.where(s_i <= t_i, scores, 0.0)
    return _mm_nt(jnp.concatenate([q_dec, a], axis=1), jnp.concatenate([st, v_t], axis=1))


def _max_half_decay(b_s, tile):
    n = tile // HG_CHUNK
    worst = None
    for h in range(HG_HEADS):
        mid = b_s[h, pl.ds(HG_HALF - 1, n, stride=HG_CHUNK), :]
        end = b_s[h, pl.ds(HG_CHUNK - 1, n, stride=HG_CHUNK), :]
        w = jnp.maximum(-mid, mid - end)
        worst = w if worst is None else jnp.maximum(worst, w)
    return jnp.max(worst)


def _hgrn_prompt_kernel(layer, with_kv, *refs):
    h_ref, ng_ref, win_ref, lbl_ref, gn_ref, wout_ref = refs[:6]
    refs = refs[6:]
    if with_kv:
        kv_in, refs = refs[:5], refs[5:]
    out_ref, sfin_ref = refs[:2]
    refs = refs[2:]
    if with_kv:
        kv_out, refs = refs[:2], refs[2:]
    q_s, k_s, v_s, b_s, o_s, st_s = refs
    t = pl.program_id(1)
    nt = pl.num_programs(1)
    tile = h_ref.shape[1]

    @pl.when(t == 0)
    def _():
        st_s[...] = jnp.zeros_like(st_s)

    x = h_ref[0]
    u = _mm(_rms_scale(x, ng_ref[...]), win_ref[...])
    lb = _lower_bound(lbl_ref[...], layer)
    for h in range(HG_HEADS):
        sl = slice(h * HG_DK, (h + 1) * HG_DK)
        q_s[h] = _silu(u[:, sl])
        fg, kin = _forget_gates(u[:, HG_KEY + h * HG_DK: HG_KEY + (h + 1) * HG_DK], lb[:, sl])
        k_s[h] = kin
        b_s[h] = _chunk_cumsum(jnp.log(fg))
        v_s[h] = u[:, 2 * HG_KEY + h * HG_DV: 2 * HG_KEY + (h + 1) * HG_DV]

    def chunk_rows(c):
        return pl.ds(pl.multiple_of(c * HG_CHUNK, HG_CHUNK), HG_CHUNK)

    def load(c, h):
        rows = chunk_rows(c)
        return q_s[h, rows, :], k_s[h, rows, :], v_s[h, rows, :], b_s[h, rows, :], st_s[h]

    def bounded_chunks():
        pending = []
        for c in range(tile // HG_CHUNK):
            rows = slice(c * HG_CHUNK, (c + 1) * HG_CHUNK)
            started = []
            for h in range(HG_HEADS):
                st = st_s[h]
                scores, q_dec, v_t, st_new = _hgrn_bounded_first(
                    q_s[h, rows, :], k_s[h, rows, :], v_s[h, rows, :], b_s[h, rows, :], st)
                st_s[h] = st_new
                started.append((h, rows, scores, q_dec, v_t, st))
            for h, prev_rows, *second in pending:
                o_s[h, prev_rows, :] = _hgrn_bounded_second(*second)
            pending = started
        for h, prev_rows, *second in pending:
            o_s[h, prev_rows, :] = _hgrn_bounded_second(*second)

    def general_body(i, carry):
        c, h = i // HG_HEADS, i % HG_HEADS
        o, st_new = _hgrn_chunk(*load(c, h))
        o_s[h, chunk_rows(c), :] = o
        st_s[h] = st_new
        return carry

    bounded = _max_half_decay(b_s, tile) < HG_MAX_HALF_DECAY

    pl.when(bounded)(bounded_chunks)

    @pl.when(jnp.logical_not(bounded))
    def _():
        lax.fori_loop(0, (tile // HG_CHUNK) * HG_HEADS, general_body, 0)

    gn = gn_ref[...]
    gated = []
    for h in range(HG_HEADS):
        g = u[:, 2 * HG_KEY + HG_VAL + h * HG_DV: 2 * HG_KEY + HG_VAL + (h + 1) * HG_DV]
        gated.append(_rms_scale(o_s[h], gn) * _silu(g))
    y = x + _mm(jnp.concatenate(gated, axis=1), wout_ref[...])
    out_ref[0] = y
    if with_kv:
        _kv_project(y, *kv_in, *kv_out)

    @pl.when(t == nt - 1)
    def _():
        for h in range(HG_HEADS):
            sfin_ref[0, h] = st_s[h].T


def _hgrn_prompt_layer(h, norm_g, w_in, lb_logits, layer, gnorm, w_out, kv=None):
    B, T, D = h.shape
    tile = min(PROMPT_TILE, T)
    n_a = lb_logits.shape[0]
    const = lambda b, t: (0, 0)
    row_tile = lambda width: pl.BlockSpec((1, tile, width), lambda b, t: (b, t, 0))
    head_scratch = pltpu.VMEM((HG_HEADS, tile, HG_DK), F32)
    operands = [h, norm_g.reshape(1, D), w_in, lb_logits, gnorm.reshape(1, HG_DV), w_out]
    in_specs = [row_tile(D), pl.BlockSpec((1, D), const), _layer_block(w_in, layer, True),
                pl.BlockSpec((n_a, HG_KEY), const), pl.BlockSpec((1, HG_DV), const),
                _layer_block(w_out, layer, True)]
    out_specs = [row_tile(D), pl.BlockSpec((1, HG_HEADS, HG_DK, HG_DV), lambda b, t: (b, 0, 0, 0))]
    out_shape = [jax.ShapeDtypeStruct((B, T, D), F32),
                 jax.ShapeDtypeStruct((B, HG_HEADS, HG_DK, HG_DV), F32)]
    if kv is not None:
        kv_norm, w_kv, tables = kv
        operands += [kv_norm.reshape(1, D), w_kv, *tables]
        in_specs += [pl.BlockSpec((1, D), const), pl.BlockSpec(w_kv.shape, const)]
        in_specs += [pl.BlockSpec((tile, LANES), lambda b, t: (t, 0))] * 3
        out_specs += [row_tile(ATT_KVD)] * 2
        out_shape += [jax.ShapeDtypeStruct((B, T, ATT_KVD), F32)] * 2
    return pl.pallas_call(
        functools.partial(_hgrn_prompt_kernel, layer, kv is not None),
        grid=(B, T // tile),
        in_specs=in_specs,
        out_specs=out_specs,
        out_shape=out_shape,
        scratch_shapes=[head_scratch] * 5 + [pltpu.VMEM((HG_HEADS, HG_DV, HG_DK), F32)],
        compiler_params=_cparams("arbitrary", "arbitrary"),
        name=f"hgrn_prompt_l{layer}",
    )(*operands)


def _rope_tables(pos):
    half = ROT_DIM // 2
    pos = np.asarray(pos, np.float64)
    inv_freq = 1.0 / (ROPE_THETA ** (np.arange(half, dtype=np.float64) * 2.0 / ROT_DIM))
    ang = (pos[:, None].astype(np.float32) * inv_freq[None, :].astype(np.float32)).astype(np.float64)
    cos, sin = np.cos(ang), np.sin(ang)
    n = pos.shape[0]
    c = np.ones((n, ATT_HD)); s_up = np.zeros((n, ATT_HD)); s_dn = np.zeros((n, ATT_HD))
    c[:, :half] = cos; c[:, half:ROT_DIM] = cos
    s_dn[:, :half] = -sin
    s_up[:, half:ROT_DIM] = sin
    rep = LANES // ATT_HD
    tab = [np.tile(a, (1, rep)).astype(np.float32) for a in (c, s_up, s_dn)]
    return tuple(jnp.asarray(a) for a in tab)


def _rope_cols(x, cos, s_up, s_dn):
    half = ROT_DIM // 2
    cols = []
    for c in range(x.shape[1] // LANES):
        xc = x[:, c * LANES:(c + 1) * LANES]
        cols.append(xc * cos + pltpu.roll(xc, half, axis=1) * s_up
                    + pltpu.roll(xc, LANES - half, axis=1) * s_dn)
    return cols[0] if len(cols) == 1 else jnp.concatenate(cols, axis=1)


def _kv_kernel(h_ref, *refs):
    _kv_project(h_ref[0], *refs)


def _kv_project(x, ng_ref, w_ref, cos_ref, sup_ref, sdn_ref, k_ref, v_ref):
    u = _mm(_rms_scale(x, ng_ref[...]), w_ref[...])
    k_ref[0] = _rope_cols(u[:, :ATT_KVD], cos_ref[...], sup_ref[...], sdn_ref[...])
    v_ref[0] = u[:, ATT_KVD:]


def _shared_kv(h, kv_norm, w_kv, tables):
    B, T, D = h.shape
    tile = min(PROMPT_TILE, T)
    const = lambda b, t: (0, 0)
    tab = pl.BlockSpec((tile, LANES), lambda b, t: (t, 0))
    kv_spec = pl.BlockSpec((1, tile, ATT_KVD), lambda b, t: (b, t, 0))
    return pl.pallas_call(
        _kv_kernel,
        grid=(B, T // tile),
        in_specs=[pl.BlockSpec((1, tile, D), lambda b, t: (b, t, 0)),
                  pl.BlockSpec((1, D), const), pl.BlockSpec(w_kv.shape, const), tab, tab, tab],
        out_specs=[kv_spec, kv_spec],
        out_shape=[jax.ShapeDtypeStruct((B, T, ATT_KVD), F32)] * 2,
        compiler_params=_cparams("arbitrary", "arbitrary"),
        name="shared_kv",
    )(h, kv_norm.reshape(1, D), w_kv, *tables)


PAIRS_PER_KV = ATT_GROUP // 2


def _swa_attention(blocks, sinks_ref, side_product):
    groups = [(blk, kvh) for blk in range(len(blocks)) for kvh in range(ATT_KVH)]
    n_keys = blocks[0][4].shape[0]
    ones_rows = jnp.ones((SUBLANES, n_keys), F32)

    def score_products(g):
        blk, kvh = groups[g]
        q_cols, k_lo, k_hi, _, bias_t = blocks[blk]
        qs = jnp.concatenate(q_cols[kvh * PAIRS_PER_KV:(kvh + 1) * PAIRS_PER_KV], axis=0)
        keys = jnp.concatenate([k_lo[kvh], k_hi[kvh]], axis=0)
        bias = jnp.concatenate([jnp.concatenate([bias_t] * PAIRS_PER_KV, axis=1)] * 2, axis=0)
        return _mm_nt(keys, qs) + bias

    def softmax_terms(g, s_t):
        kvh = groups[g][1]
        cols = range(kvh * PAIRS_PER_KV, (kvh + 1) * PAIRS_PER_KV)
        terms = []
        for par in range(2):
            s = s_t[par * n_keys:(par + 1) * n_keys]
            sink = jnp.concatenate(
                [jnp.full((1, WINDOW), sinks_ref[0, 2 * c + par], F32) for c in cols], axis=1)
            m = jnp.maximum(jnp.max(s, axis=0, keepdims=True), sink)
            terms.append((jnp.exp(s - m), jnp.exp(sink - m)))
        return terms

    def value_products(g, terms):
        blk, kvh = groups[g]
        v_t = blocks[blk][3]
        lhs = jnp.concatenate([v_t[kvh * ATT_HD:(kvh + 1) * ATT_HD], ones_rows], axis=0)
        halves = []
        for p, sink_term in terms:
            r = _mm(lhs, p)
            halves.append(r[:ATT_HD] * (1.0 / (r[ATT_HD:ATT_HD + 1] + sink_term)))
        return halves

    scores = [score_products(g) for g in range(len(groups))]
    side = side_product()
    terms = [softmax_terms(g, s) for g, s in enumerate(scores)]
    outs = [value_products(g, t) for g, t in enumerate(terms)]
    res = []
    for b in range(len(blocks)):
        cols = []
        for kvh in range(ATT_KVH):
            even, odd = outs[b * ATT_KVH + kvh]
            for i in range(PAIRS_PER_KV):
                queries = slice(i * WINDOW, (i + 1) * WINDOW)
                cols.append(jnp.concatenate([even[:, queries], odd[:, queries]], axis=0).T)
        res.append(jnp.concatenate(cols, axis=1))
    return res, side


def _split_kv_heads(a, fill):
    lane = lax.broadcasted_iota(jnp.int32, a.shape, 1)
    low = lane < ATT_HD
    sw = pltpu.roll(a, ATT_HD, axis=1)
    lo = [jnp.where(low, a, fill), jnp.where(low, sw, fill)]
    hi = [jnp.where(low, fill, sw), jnp.where(low, fill, a)]
    return lo, hi


def _swa_prompt_kernel(final, h_ref, ng_ref, win_ref, sinks_ref, wout_ref, fin_ref,
                       kp_ref, kc_ref, vp_ref, vc_ref, cos_ref, sup_ref, sdn_ref, out_ref):
    t = pl.program_id(1)
    tile = h_ref.shape[1]
    nblk = tile // WINDOW
    x = h_ref[0]
    xn = _rms_scale(x, ng_ref[...]).astype(BF16)
    u = _mm(xn, win_ref[:, :ATT_QD])
    scale = ATT_HD ** -0.5
    cos, sup, sdn = cos_ref[...], sup_ref[...], sdn_ref[...]
    k_all = jnp.concatenate([kp_ref[0], kc_ref[0]], axis=0)
    v_all = jnp.concatenate([vp_ref[0], vc_ref[0]], axis=0)
    j = lax.broadcasted_iota(jnp.int32, (2 * WINDOW, WINDOW), 0)
    tq = lax.broadcasted_iota(jnp.int32, (2 * WINDOW, WINDOW), 1)
    rel = tq + WINDOW - j
    band = (rel >= 0) & (rel <= WINDOW)
    blocks = []
    for blk in range(nblk):
        rows = slice(blk * WINDOW, (blk + 1) * WINDOW)
        keys = slice(blk * WINDOW, (blk + 2) * WINDOW)
        k_lo, k_hi = _split_kv_heads(k_all[keys], 0.0)
        if blk == 0:
            valid = band & (j >= WINDOW * (t == 0).astype(jnp.int32))
        else:
            valid = band
        q_cols = []
        for c in range(ATT_QD // LANES):
            qc = u[rows, c * LANES:(c + 1) * LANES]
            q_cols.append(_rope_cols(qc, cos[rows], sup[rows], sdn[rows]) * scale)
        blocks.append((q_cols, k_lo, k_hi, v_all[keys].T, jnp.where(valid, 0.0, MASK_VALUE)))
    o_blocks, gate = _swa_attention(blocks, sinks_ref, lambda: _mm(xn, win_ref[:, ATT_QD:]))
    o = jnp.concatenate(o_blocks, axis=0) if nblk > 1 else o_blocks[0]
    y = x + _mm(o * _silu(gate), wout_ref[...])
    if final:
        y = _rms_scale(y, fin_ref[...])
    out_ref[0] = y


def _swa_prompt_layer(h, k, v, tables, norm_g, w_in, layer, sinks, w_out, final_norm, final):
    B, T, D = h.shape
    tile = min(SWA_TILE, T)
    per = tile // WINDOW
    const = lambda b, t: (0, 0)
    tab = pl.BlockSpec((tile, LANES), lambda b, t: (t, 0))
    prev = pl.BlockSpec((1, WINDOW, ATT_KVD), lambda b, t: (b, jnp.maximum(t * per - 1, 0), 0))
    cur = pl.BlockSpec((1, tile, ATT_KVD), lambda b, t: (b, t, 0))
    return pl.pallas_call(
        functools.partial(_swa_prompt_kernel, final),
        grid=(B, T // tile),
        in_specs=[
            pl.BlockSpec((1, tile, D), lambda b, t: (b, t, 0)),
            pl.BlockSpec((1, D), const),
            _layer_block(w_in, layer, True),
            pl.BlockSpec(memory_space=pltpu.SMEM),
            _layer_block(w_out, layer, True),
            pl.BlockSpec((1, D), const),
            prev, cur, prev, cur, tab, tab, tab,
        ],
        out_specs=pl.BlockSpec((1, tile, D), lambda b, t: (b, t, 0)),
        out_shape=jax.ShapeDtypeStruct((B, T, D), F32),
        compiler_params=_cparams("arbitrary", "arbitrary"),
        name="swa_prompt_final" if final else "swa_prompt",
    )(h, norm_g.reshape(1, D), w_in, sinks.reshape(1, ATT_QH), w_out, final_norm.reshape(1, D),
      k, k, v, v, *tables)


def _col(row):
    n = row.shape[1]
    return jnp.broadcast_to(row, (n, n)).T


def _hgrn_decode_kernel(x_ref, s_ref, ng_ref, win_ref, lbl_ref, gn_ref, wout_ref,
                        snew_ref, hout_ref, h_s, q_s, f_s, v_s, g_s, o_s, oblk_s):
    l, i = pl.program_id(0), pl.program_id(1)
    n_layers, n_blocks = pl.num_programs(0), pl.num_programs(1)
    bb = s_ref.shape[1]

    @pl.when(i == 0)
    def _():
        @pl.when(l == 0)
        def _():
            h_s[...] = x_ref[...]

        u = _mm(_rms_scale(h_s[...], ng_ref[0]), win_ref[0])
        lbl = lbl_ref[...]
        lb = jnp.zeros_like(lbl[:1])
        for layer in range(lbl.shape[0]):
            lb = jnp.where(l == layer, _lower_bound(lbl, layer), lb)
        q_s[...] = _silu(u[:, :HG_KEY])
        f_s[...] = _forget_gates(u[:, HG_KEY:2 * HG_KEY], lb)[0]
        v_s[...] = u[:, 2 * HG_KEY:2 * HG_KEY + HG_VAL]
        g_s[...] = _silu(u[:, 2 * HG_KEY + HG_VAL:])

    rows = pl.ds(pl.multiple_of(i * bb, bb), bb)
    qb, fb, vb = q_s[rows, :], f_s[rows, :], v_s[rows, :]
    for j in range(bb):
        for h in range(HG_HEADS):
            sl = slice(h * HG_DK, (h + 1) * HG_DK)
            v_r = vb[j:j + 1, h * HG_DV:(h + 1) * HG_DV]
            s_new = _col(fb[j:j + 1, sl]) * (s_ref[0, j, h] - v_r) + v_r
            snew_ref[0, j, h] = s_new
            oblk_s[j:j + 1, h * HG_DV:(h + 1) * HG_DV] = _mm(qb[j:j + 1, sl], s_new)
    o_s[rows, :] = oblk_s[...]

    @pl.when(i == n_blocks - 1)
    def _():
        gn = gn_ref[0]
        gated = []
        for h in range(HG_HEADS):
            sl = slice(h * HG_DV, (h + 1) * HG_DV)
            gated.append(_rms_scale(o_s[:, sl], gn) * g_s[:, sl])
        h_new = h_s[...] + _mm(jnp.concatenate(gated, axis=1), wout_ref[0])
        h_s[...] = h_new

        @pl.when(l == n_layers - 1)
        def _():
            hout_ref[...] = h_new


def _whole(shape):
    return pl.BlockSpec(shape, lambda *_: (0,) * len(shape))


def _hgrn_decode(h, states, norms, w_in, lb_logits, gnorms, w_out):
    B, D = h.shape
    n_a = states.shape[0]
    bb = DEC_BLOCK
    per_layer = lambda *tail: pl.BlockSpec((1,) + tail, lambda l, i: (l,) + (0,) * len(tail))
    st_blk = pl.BlockSpec((1, bb, HG_HEADS, HG_DK, HG_DV), lambda l, i: (l, i, 0, 0, 0))
    wide = pltpu.VMEM((B, HG_KEY), F32)
    s_new, h_new = pl.pallas_call(
        _hgrn_decode_kernel,
        grid=(n_a, B // bb),
        in_specs=[_whole((B, D)), st_blk, per_layer(1, D), per_layer(*w_in.shape[1:]),
                  _whole(lb_logits.shape), per_layer(1, HG_DV), per_layer(*w_out.shape[1:])],
        out_specs=[st_blk, _whole((B, D))],
        out_shape=[jax.ShapeDtypeStruct(states.shape, F32), jax.ShapeDtypeStruct((B, D), F32)],
        scratch_shapes=[pltpu.VMEM((B, D), F32)] + [wide] * 5 + [pltpu.VMEM((bb, HG_VAL), F32)],
        compiler_params=_cparams("arbitrary", "arbitrary"),
        name="hgrn_decode",
    )(h, states, norms.reshape(n_a, 1, D), w_in, lb_logits, gnorms.reshape(n_a, 1, HG_DV), w_out)
    return h_new, s_new


def _bf16_round(x):
    return x.astype(BF16).astype(F32)


def _swa_decode_kernel(final, h_ref, ng_ref, win_ref, cos_ref, sup_ref, sdn_ref, kc_ref, vc_ref,
                       kn_ref, vn_ref, sinks_ref, wout_ref, fin_ref, out_ref, qp_s, g_s, os_s):
    i = pl.program_id(0)
    n_blocks = pl.num_programs(0)
    bb, W = kc_ref.shape[0], kc_ref.shape[1]
    B = h_ref.shape[0]
    lane = lax.broadcasted_iota(jnp.int32, (B, LANES), 1)

    @pl.when(i == 0)
    def _():
        u = _mm(_rms_scale(h_ref[...], ng_ref[...]), win_ref[...])
        q = _rope_cols(u[:, :ATT_QD], cos_ref[...], sup_ref[...], sdn_ref[...]) * (ATT_HD ** -0.5)
        g_s[...] = _silu(u[:, ATT_QD:])
        for head in range(ATT_QH):
            col = q[:, (head // 2) * LANES:(head // 2 + 1) * LANES]
            kvh = head // ATT_GROUP
            if head % 2 != kvh:
                col = pltpu.roll(col, ATT_HD, axis=1)
            qp_s[:, head, :] = jnp.where((lane >= ATT_HD) if kvh == 1 else (lane < ATT_HD), col, 0.0)

    rows = pl.ds(pl.multiple_of(i * bb, bb), bb)
    kn, vn = kn_ref[rows, :], vn_ref[rows, :]
    qp = [qp_s[i * bb + j] for j in range(bb)]
    s_c = jnp.concatenate([_mm_nt(qp[j], kc_ref[j]) for j in range(bb)], axis=0)
    qp_all = jnp.concatenate(qp, axis=0)
    expand = lambda a: jnp.concatenate(
        [jnp.broadcast_to(a[j:j + 1], (ATT_QH, ATT_KVD)) for j in range(bb)], axis=0)
    s_n = jnp.sum(_bf16_round(qp_all) * _bf16_round(expand(kn)), axis=-1, keepdims=True)
    sink = jnp.concatenate([sinks_ref[...]] * bb, axis=0)
    m = jnp.maximum(jnp.maximum(jnp.max(s_c, axis=-1, keepdims=True), s_n), sink)
    p_c = jnp.exp(s_c - m)
    p_n = jnp.exp(s_n - m)
    inv = 1.0 / (jnp.sum(p_c, axis=-1, keepdims=True) + p_n + jnp.exp(sink - m))
    p_c = p_c * inv
    o = jnp.concatenate([_mm(p_c[j * ATT_QH:(j + 1) * ATT_QH], vc_ref[j]) for j in range(bb)], axis=0)
    o = o + _bf16_round(p_n * inv) * _bf16_round(expand(vn))
    for j in range(bb):
        os_s[i * bb + j] = o[j * ATT_QH:(j + 1) * ATT_QH]

    @pl.when(i == n_blocks - 1)
    def _():
        cols = []
        for c in range(ATT_QD // LANES):
            halves = []
            for par in range(2):
                head = 2 * c + par
                t = os_s[:, head, :]
                if par != head // ATT_GROUP:
                    t = pltpu.roll(t, ATT_HD, axis=1)
                halves.append(t)
            cols.append(jnp.where(lane < ATT_HD, halves[0], halves[1]))
        y = h_ref[...] + _mm(jnp.concatenate(cols, axis=1) * g_s[...], wout_ref[...])
        if final:
            y = _rms_scale(y, fin_ref[...])
        out_ref[...] = y


def _swa_decode_layer(h, k_new, v_new, cache_k, cache_v, tables, norm_g, w_in, layer, sinks, w_out,
                      final_norm, final):
    B, D = h.shape
    W = cache_k.shape[1]
    bb = DEC_BLOCK
    cache_blk = pl.BlockSpec((bb, W, ATT_KVD), lambda i: (i, 0, 0))
    head_rows = pltpu.VMEM((B, ATT_QH, ATT_KVD), F32)
    return pl.pallas_call(
        functools.partial(_swa_decode_kernel, final),
        grid=(B // bb,),
        in_specs=[_whole((B, D)), _whole((1, D)), _layer_block(w_in, layer)] + [_whole((1, LANES))] * 3
                 + [cache_blk, cache_blk, _whole((B, ATT_KVD)), _whole((B, ATT_KVD)),
                    _whole((ATT_QH, 1)), _layer_block(w_out, layer), _whole((1, D))],
        out_specs=_whole((B, D)),
        out_shape=jax.ShapeDtypeStruct((B, D), F32),
        scratch_shapes=[head_rows, pltpu.VMEM((B, ATT_QD), F32), head_rows],
        compiler_params=_cparams("arbitrary"),
        name="swa_decode_final" if final else "swa_decode",
    )(h, norm_g.reshape(1, D), w_in, *tables, cache_k.reshape(B, W, ATT_KVD),
      cache_v.reshape(B, W, ATT_KVD), k_new, v_new, sinks.reshape(ATT_QH, 1), w_out,
      final_norm.reshape(1, D))


def kernel(x_prompt, x_sample, state_hgrn, cache_k_win, cache_v_win, a_norm, a_w_in, a_lb_logits,
           a_gnorm, a_w_out, kv_norm, w_kv, b_norm, b_w_in, b_sinks, b_w_out, final_norm):
    B, T, D = x_prompt.shape
    BD, TD, _ = x_sample.shape
    n_a = a_w_in.shape[0]
    n_b = b_w_in.shape[0]
    a_w_in_b, a_w_out_b = a_w_in.astype(BF16), a_w_out.astype(BF16)
    b_w_in_b, b_w_out_b = b_w_in.astype(BF16), b_w_out.astype(BF16)
    w_kv_b = w_kv.astype(BF16)

    tab_p = _rope_tables(np.arange(T))
    h = x_prompt
    st_p = []
    for l in range(n_a):
        kv = (kv_norm, w_kv_b, tab_p) if l == n_a - 1 else None
        h, s, *kv_p = _hgrn_prompt_layer(h, a_norm[l], a_w_in_b, a_lb_logits, l, a_gnorm[l],
                                         a_w_out_b, kv)
        st_p.append(s)
    k_p, v_p = kv_p
    for l in range(n_b):
        h = _swa_prompt_layer(h, k_p, v_p, tab_p, b_norm[l], b_w_in_b, l, b_sinks[l], b_w_out_b,
                              final_norm, l == n_b - 1)
    y_prompt = h
    w_keep = min(WINDOW, T)
    k_win = k_p[:, T - w_keep:].reshape(B, w_keep, ATT_KVH, ATT_HD)
    v_win = v_p[:, T - w_keep:].reshape(B, w_keep, ATT_KVH, ATT_HD)

    tab_s = _rope_tables(PAST_LEN + np.arange(TD))
    hs = x_sample.reshape(BD * TD, D)
    hs, st_s = _hgrn_decode(hs, state_hgrn, a_norm, a_w_in_b, a_lb_logits, a_gnorm, a_w_out_b)
    k_s, v_s = _shared_kv(hs.reshape(1, BD, D), kv_norm, w_kv_b,
                          tuple(jnp.broadcast_to(t, (BD, LANES)) for t in tab_s))
    k_s, v_s = k_s.reshape(BD, ATT_KVD), v_s.reshape(BD, ATT_KVD)
    for l in range(n_b):
        hs = _swa_decode_layer(hs, k_s, v_s, cache_k_win, cache_v_win, tab_s, b_norm[l], b_w_in_b, l,
                               b_sinks[l], b_w_out_b, final_norm, l == n_b - 1)
    y_sample = hs.reshape(BD, TD, D)

    return (y_prompt, y_sample, jnp.stack(st_p), st_s, k_win, v_win,
            k_s.reshape(BD, TD, ATT_KVH, ATT_HD), v_s.reshape(BD, TD, ATT_KVH, ATT_HD))
```

```python
import functools

import numpy as np
import jax
import jax.numpy as jnp
from jax import lax
from jax.experimental import pallas as pl
from jax.experimental.pallas import tpu as pltpu

F32 = jnp.float32
BF16 = jnp.bfloat16

D_MODEL = 1024
HG_HEADS = 8
HG_DK = 128
HG_DV = 128
HG_KEY = HG_HEADS * HG_DK
HG_VAL = HG_HEADS * HG_DV
HG_CHUNK = 64
HG_HALF = HG_CHUNK // 2
HG_SUB = 16
HG_MAX_HALF_DECAY = 60.0
ATT_HD = 64
ATT_QH = 16
ATT_KVH = 2
ATT_GROUP = ATT_QH // ATT_KVH
ATT_QD = ATT_QH * ATT_HD
ATT_KVD = ATT_KVH * ATT_HD
WINDOW = 128
ROPE_THETA = 500000.0
ROT_DIM = ATT_HD // 4
NORM_EPS = 1e-6
MASK_VALUE = -1e30
NEG_BIG = -1e30
PAST_LEN = 8192

SUBLANES = 8
LANES = 128
PROMPT_TILE = 512
SWA_TILE = 512
DEC_BLOCK = 8
VMEM_LIMIT = 48 * 1024 * 1024


def _cparams(*sem):
    return pltpu.CompilerParams(dimension_semantics=sem, vmem_limit_bytes=VMEM_LIMIT)


def _layer_block(stacked, layer, single_buffer=False):
    tail = stacked.shape[1:]
    mode = dict(pipeline_mode=pl.Buffered(1)) if single_buffer else {}
    return pl.BlockSpec((None,) + tail, lambda *_: (layer,) + (0,) * len(tail), **mode)


def _mm(a, b):
    return jnp.dot(a.astype(BF16), b.astype(BF16), preferred_element_type=F32)


def _mm_nt(a, b):
    return lax.dot_general(a.astype(BF16), b.astype(BF16), (((1,), (1,)), ((), ())),
                           preferred_element_type=F32)


def _mm_tn(a, b):
    return lax.dot_general(a.astype(BF16), b.astype(BF16), (((0,), (0,)), ((), ())),
                           preferred_element_type=F32)


def _rms_scale(x, g):
    ms = jnp.mean(x * x, axis=-1, keepdims=True)
    return x * lax.rsqrt(ms + NORM_EPS) * g


def _silu(x):
    return x * (1.0 / (1.0 + jnp.exp(-x)))


def _lower_bound(lbl, layer):
    n = lbl.shape[0]
    rows = [lbl[i:i + 1, :] for i in range(n)]
    m = functools.reduce(jnp.maximum, rows)
    es = [jnp.exp(r - m) for r in rows]
    inv = 1.0 / functools.reduce(jnp.add, es)
    ps = [e * inv for e in es]
    return functools.reduce(jnp.add, ps[:layer + 1]) - ps[0]


HG_HEAD_COLS = 2 * HG_DK + 2 * HG_DV


def _per_head_columns(w_in):
    lead = w_in.shape[:-1]
    w = w_in.reshape(*lead, 4, HG_HEADS, HG_DK)
    return jnp.swapaxes(w, -3, -2).reshape(*lead, HG_HEADS * HG_HEAD_COLS)


def _split_head_columns(u):
    return [u[:, i * HG_DK:(i + 1) * HG_DK] for i in range(4)]


def _forget_gates(f, lb):
    e = jnp.exp(-f)
    r = 1.0 / (1.0 + e)
    return lb + (1.0 - lb) * r, (1.0 - lb) * (e * r)


def _chunk_cumsum(x):
    rows = x.shape[0]
    pos = lax.broadcasted_iota(jnp.int32, x.shape, 0) % HG_CHUNK
    s = 1
    while s < HG_CHUNK:
        x = x + jnp.where(pos >= s, pltpu.roll(x, s, axis=0), 0.0)
        s *= 2
    del rows
    return x


def _diag_blocks(q, k, v, b):
    sub_iota = lax.broadcasted_iota(jnp.int32, (SUBLANES, LANES), 0)
    outs = []
    for blk in range(HG_CHUNK // HG_SUB):
        base = blk * HG_SUB
        groups = HG_SUB // SUBLANES
        qs = [q[base + g * SUBLANES: base + (g + 1) * SUBLANES] for g in range(groups)]
        bs = [b[base + g * SUBLANES: base + (g + 1) * SUBLANES] for g in range(groups)]
        accs = [jnp.zeros((SUBLANES, HG_DV), F32) for _ in range(groups)]
        for s in range(HG_SUB):
            row = base + s
            k_s = k[row:row + 1]
            b_s = b[row:row + 1]
            v_s = v[row:row + 1]
            for g in range(s // SUBLANES, groups):
                d = bs[g] - b_s
                if g == s // SUBLANES:
                    d = jnp.where(sub_iota >= (s % SUBLANES), d, NEG_BIG)
                p = qs[g] * (k_s * jnp.exp(d))
                a = jnp.sum(p, axis=-1, keepdims=True)
                accs[g] = accs[g] + a * v_s
        outs.extend(accs)
    return jnp.concatenate(outs, axis=0)


def _hgrn_chunk(q, k, v, b, st):
    C, H = HG_CHUNK, HG_HALF
    o_inter = _mm_nt(q * jnp.exp(b), st)
    r0 = b[H - 1:H]
    a0 = _mm_nt(q[H:] * jnp.exp(b[H:] - r0), k[:H] * jnp.exp(r0 - b[:H]))
    o_hi = _mm(a0, v[:H])
    Q = HG_SUB
    o_q = []
    for base in (0, H):
        r1 = b[base + Q - 1: base + Q]
        a1 = _mm_nt(q[base + Q: base + 2 * Q] * jnp.exp(b[base + Q: base + 2 * Q] - r1),
                    k[base: base + Q] * jnp.exp(r1 - b[base: base + Q]))
        o_q.append(_mm(a1, v[base: base + Q]))
    zero = jnp.zeros((Q, HG_DV), F32)
    o_off = jnp.concatenate([zero, o_q[0], o_hi[:Q], o_hi[Q:] + o_q[1]], axis=0)
    o = o_inter + o_off + _diag_blocks(q, k, v, b)
    b_end = b[C - 1:C]
    st_new = st * jnp.exp(b_end) + _mm_tn(v, k * jnp.exp(b_end - b))
    return o, st_new


def _hgrn_bounded_first(q, k, v, b, st):
    C, H = HG_CHUNK, HG_HALF
    r = b[H - 1:H]
    eq = jnp.exp(b - r)
    qt, kt = q * eq, k * jnp.exp(r - b)
    e_mid, e_hi = jnp.exp(r), eq[C - 1:C]
    scores = _mm_nt(qt, kt)
    v_t = v.T
    st_new = st * (e_mid * e_hi) + _mm(v_t, kt * e_hi)
    return scores, qt * e_mid, v_t, st_new


def _hgrn_bounded_second(scores, q_dec, v_t, st):
    t_i = lax.broadcasted_iota(jnp.int32, scores.shape, 0)
    s_i = lax.broadcasted_iota(jnp.int32, scores.shape, 1)
    a = jnp.where(s_i <= t_i, scores, 0.0)
    return _mm_nt(jnp.concatenate([q_dec, a], axis=1), jnp.concatenate([st, v_t], axis=1))


def _max_half_decay(b_s, tile):
    n = tile // HG_CHUNK
    worst = None
    for h in range(HG_HEADS):
        mid = b_s[h, pl.ds(HG_HALF - 1, n, stride=HG_CHUNK), :]
        end = b_s[h, pl.ds(HG_CHUNK - 1, n, stride=HG_CHUNK), :]
        w = jnp.maximum(-mid, mid - end)
        worst = w if worst is None else jnp.maximum(worst, w)
    return jnp.max(worst)


def _hgrn_prompt_kernel(layer, with_kv, *refs):
    h_ref, ng_ref, win_ref, lbl_ref, gn_ref, wout_ref = refs[:6]
    refs = refs[6:]
    if with_kv:
        kv_in, refs = refs[:5], refs[5:]
    out_ref, sfin_ref = refs[:2]
    refs = refs[2:]
    if with_kv:
        kv_out, refs = refs[:2], refs[2:]
    q_s, k_s, v_s, b_s, g_s, o_s, st_s = refs
    t = pl.program_id(1)
    nt = pl.num_programs(1)
    tile = h_ref.shape[1]

    @pl.when(t == 0)
    def _():
        st_s[...] = jnp.zeros_like(st_s)

    x = h_ref[0]
    xn = _rms_scale(x, ng_ref[...]).astype(BF16)
    lb = _lower_bound(lbl_ref[...], layer)
    for h in range(HG_HEADS):
        u = _split_head_columns(_mm(xn, win_ref[:, h * HG_HEAD_COLS:(h + 1) * HG_HEAD_COLS]))
        q_s[h] = _silu(u[0])
        fg, kin = _forget_gates(u[1], lb[:, h * HG_DK:(h + 1) * HG_DK])
        k_s[h] = kin
        b_s[h] = _chunk_cumsum(jnp.log(fg))
        v_s[h] = u[2]
        g_s[h] = _silu(u[3])

    def chunk_rows(c):
        return pl.ds(pl.multiple_of(c * HG_CHUNK, HG_CHUNK), HG_CHUNK)

    def load(c, h):
        rows = chunk_rows(c)
        return q_s[h, rows, :], k_s[h, rows, :], v_s[h, rows, :], b_s[h, rows, :], st_s[h]

    def bounded_chunks():
        pending = []
        for c in range(tile // HG_CHUNK):
            rows = slice(c * HG_CHUNK, (c + 1) * HG_CHUNK)
            started = []
            for h in range(HG_HEADS):
                st = st_s[h]
                scores, q_dec, v_t, st_new = _hgrn_bounded_first(
                    q_s[h, rows, :], k_s[h, rows, :], v_s[h, rows, :], b_s[h, rows, :], st)
                st_s[h] = st_new
                started.append((h, rows, scores, q_dec, v_t, st))
            for h, prev_rows, *second in pending:
                o_s[h, prev_rows, :] = _hgrn_bounded_second(*second)
            pending = started
        for h, prev_rows, *second in pending:
            o_s[h, prev_rows, :] = _hgrn_bounded_second(*second)

    def general_body(i, carry):
        c, h = i // HG_HEADS, i % HG_HEADS
        o, st_new = _hgrn_chunk(*load(c, h))
        o_s[h, chunk_rows(c), :] = o
        st_s[h] = st_new
        return carry

    bounded = _max_half_decay(b_s, tile) < HG_MAX_HALF_DECAY

    pl.when(bounded)(bounded_chunks)

    @pl.when(jnp.logical_not(bounded))
    def _():
        lax.fori_loop(0, (tile // HG_CHUNK) * HG_HEADS, general_body, 0)

    gn = gn_ref[...]
    gated = []
    for h in range(HG_HEADS):
        gated.append(_rms_scale(o_s[h], gn) * g_s[h])
    y = x + _mm(jnp.concatenate(gated, axis=1), wout_ref[...])
    out_ref[0] = y
    if with_kv:
        _kv_project(y, *kv_in, *kv_out)

    @pl.when(t == nt - 1)
    def _():
        for h in range(HG_HEADS):
            sfin_ref[0, h] = st_s[h].T


def _hgrn_prompt_layer(h, norm_g, w_in, lb_logits, layer, gnorm, w_out, kv=None):
    B, T, D = h.shape
    tile = min(PROMPT_TILE, T)
    n_a = lb_logits.shape[0]
    const = lambda b, t: (0, 0)
    row_tile = lambda width: pl.BlockSpec((1, tile, width), lambda b, t: (b, t, 0))
    head_scratch = pltpu.VMEM((HG_HEADS, tile, HG_DK), F32)
    operands = [h, norm_g.reshape(1, D), w_in, lb_logits, gnorm.reshape(1, HG_DV), w_out]
    in_specs = [row_tile(D), pl.BlockSpec((1, D), const), _layer_block(w_in, layer, True),
                pl.BlockSpec((n_a, HG_KEY), const), pl.BlockSpec((1, HG_DV), const),
                _layer_block(w_out, layer, True)]
    out_specs = [row_tile(D), pl.BlockSpec((1, HG_HEADS, HG_DK, HG_DV), lambda b, t: (b, 0, 0, 0))]
    out_shape = [jax.ShapeDtypeStruct((B, T, D), F32),
                 jax.ShapeDtypeStruct((B, HG_HEADS, HG_DK, HG_DV), F32)]
    if kv is not None:
        kv_norm, w_kv, tables = kv
        operands += [kv_norm.reshape(1, D), w_kv, *tables]
        in_specs += [pl.BlockSpec((1, D), const), pl.BlockSpec(w_kv.shape, const)]
        in_specs += [pl.BlockSpec((tile, LANES), lambda b, t: (t, 0))] * 3
        out_specs += [row_tile(ATT_KVD)] * 2
        out_shape += [jax.ShapeDtypeStruct((B, T, ATT_KVD), F32)] * 2
    return pl.pallas_call(
        functools.partial(_hgrn_prompt_kernel, layer, kv is not None),
        grid=(B, T // tile),
        in_specs=in_specs,
        out_specs=out_specs,
        out_shape=out_shape,
        scratch_shapes=[head_scratch] * 6 + [pltpu.VMEM((HG_HEADS, HG_DV, HG_DK), F32)],
        compiler_params=_cparams("arbitrary", "arbitrary"),
        name=f"hgrn_prompt_l{layer}",
    )(*operands)


def _rope_tables(pos):
    half = ROT_DIM // 2
    pos = np.asarray(pos, np.float64)
    inv_freq = 1.0 / (ROPE_THETA ** (np.arange(half, dtype=np.float64) * 2.0 / ROT_DIM))
    ang = (pos[:, None].astype(np.float32) * inv_freq[None, :].astype(np.float32)).astype(np.float64)
    cos, sin = np.cos(ang), np.sin(ang)
    n = pos.shape[0]
    c = np.ones((n, ATT_HD)); s_up = np.zeros((n, ATT_HD)); s_dn = np.zeros((n, ATT_HD))
    c[:, :half] = cos; c[:, half:ROT_DIM] = cos
    s_dn[:, :half] = -sin
    s_up[:, half:ROT_DIM] = sin
    rep = LANES // ATT_HD
    tab = [np.tile(a, (1, rep)).astype(np.float32) for a in (c, s_up, s_dn)]
    return tuple(jnp.asarray(a) for a in tab)


def _rope_cols(x, cos, s_up, s_dn):
    half = ROT_DIM // 2
    cols = []
    for c in range(x.shape[1] // LANES):
        xc = x[:, c * LANES:(c + 1) * LANES]
        cols.append(xc * cos + pltpu.roll(xc, half, axis=1) * s_up
                    + pltpu.roll(xc, LANES - half, axis=1) * s_dn)
    return cols[0] if len(cols) == 1 else jnp.concatenate(cols, axis=1)


def _kv_kernel(h_ref, *refs):
    _kv_project(h_ref[0], *refs)


def _kv_project(x, ng_ref, w_ref, cos_ref, sup_ref, sdn_ref, k_ref, v_ref):
    u = _mm(_rms_scale(x, ng_ref[...]), w_ref[...])
    k_ref[0] = _rope_cols(u[:, :ATT_KVD], cos_ref[...], sup_ref[...], sdn_ref[...])
    v_ref[0] = u[:, ATT_KVD:]


def _shared_kv(h, kv_norm, w_kv, tables):
    B, T, D = h.shape
    tile = min(PROMPT_TILE, T)
    const = lambda b, t: (0, 0)
    tab = pl.BlockSpec((tile, LANES), lambda b, t: (t, 0))
    kv_spec = pl.BlockSpec((1, tile, ATT_KVD), lambda b, t: (b, t, 0))
    return pl.pallas_call(
        _kv_kernel,
        grid=(B, T // tile),
        in_specs=[pl.BlockSpec((1, tile, D), lambda b, t: (b, t, 0)),
                  pl.BlockSpec((1, D), const), pl.BlockSpec(w_kv.shape, const), tab, tab, tab],
        out_specs=[kv_spec, kv_spec],
        out_shape=[jax.ShapeDtypeStruct((B, T, ATT_KVD), F32)] * 2,
        compiler_params=_cparams("arbitrary", "arbitrary"),
        name="shared_kv",
    )(h, kv_norm.reshape(1, D), w_kv, *tables)


PAIRS_PER_KV = ATT_GROUP // 2


def _swa_attention(blocks, sinks_ref, side_product):
    groups = [(blk, kvh) for blk in range(len(blocks)) for kvh in range(ATT_KVH)]
    n_keys = blocks[0][4].shape[0]
    ones_rows = jnp.ones((SUBLANES, n_keys), F32)

    def score_products(g):
        blk, kvh = groups[g]
        q_cols, k_lo, k_hi, _, bias_t = blocks[blk]
        qs = jnp.concatenate(q_cols[kvh * PAIRS_PER_KV:(kvh + 1) * PAIRS_PER_KV], axis=0)
        keys = jnp.concatenate([k_lo[kvh], k_hi[kvh]], axis=0)
        bias = jnp.concatenate([jnp.concatenate([bias_t] * PAIRS_PER_KV, axis=1)] * 2, axis=0)
        return _mm_nt(keys, qs) + bias

    def softmax_terms(g, s_t):
        kvh = groups[g][1]
        cols = range(kvh * PAIRS_PER_KV, (kvh + 1) * PAIRS_PER_KV)
        terms = []
        for par in range(2):
            s = s_t[par * n_keys:(par + 1) * n_keys]
            sink = jnp.concatenate(
                [jnp.full((1, WINDOW), sinks_ref[0, 2 * c + par], F32) for c in cols], axis=1)
            m = jnp.maximum(jnp.max(s, axis=0, keepdims=True), sink)
            terms.append((jnp.exp(s - m), jnp.exp(sink - m)))
        return terms

    def value_products(g, terms):
        blk, kvh = groups[g]
        v_t = blocks[blk][3]
        lhs = jnp.concatenate([v_t[kvh * ATT_HD:(kvh + 1) * ATT_HD], ones_rows], axis=0)
        halves = []
        for p, sink_term in terms:
            r = _mm(lhs, p)
            halves.append(r[:ATT_HD] * (1.0 / (r[ATT_HD:ATT_HD + 1] + sink_term)))
        return halves

    scores = [score_products(g) for g in range(len(groups))]
    side = side_product()
    terms = [softmax_terms(g, s) for g, s in enumerate(scores)]
    outs = [value_products(g, t) for g, t in enumerate(terms)]
    res = []
    for b in range(len(blocks)):
        cols = []
        for kvh in range(ATT_KVH):
            even, odd = outs[b * ATT_KVH + kvh]
            for i in range(PAIRS_PER_KV):
                queries = slice(i * WINDOW, (i + 1) * WINDOW)
                cols.append(jnp.concatenate([even[:, queries], odd[:, queries]], axis=0).T)
        res.append(jnp.concatenate(cols, axis=1))
    return res, side


def _split_kv_heads(a, fill):
    lane = lax.broadcasted_iota(jnp.int32, a.shape, 1)
    low = lane < ATT_HD
    sw = pltpu.roll(a, ATT_HD, axis=1)
    lo = [jnp.where(low, a, fill), jnp.where(low, sw, fill)]
    hi = [jnp.where(low, fill, sw), jnp.where(low, fill, a)]
    return lo, hi


def _swa_prompt_kernel(final, h_ref, ng_ref, win_ref, sinks_ref, wout_ref, fin_ref,
                       kp_ref, kc_ref, vp_ref, vc_ref, cos_ref, sup_ref, sdn_ref, out_ref):
    t = pl.program_id(1)
    tile = h_ref.shape[1]
    nblk = tile // WINDOW
    x = h_ref[0]
    xn = _rms_scale(x, ng_ref[...]).astype(BF16)
    u = _mm(xn, win_ref[:, :ATT_QD])
    scale = ATT_HD ** -0.5
    cos, sup, sdn = cos_ref[...], sup_ref[...], sdn_ref[...]
    k_all = jnp.concatenate([kp_ref[0], kc_ref[0]], axis=0)
    v_all = jnp.concatenate([vp_ref[0], vc_ref[0]], axis=0)
    j = lax.broadcasted_iota(jnp.int32, (2 * WINDOW, WINDOW), 0)
    tq = lax.broadcasted_iota(jnp.int32, (2 * WINDOW, WINDOW), 1)
    rel = tq + WINDOW - j
    band = (rel >= 0) & (rel <= WINDOW)
    blocks = []
    for blk in range(nblk):
        rows = slice(blk * WINDOW, (blk + 1) * WINDOW)
        keys = slice(blk * WINDOW, (blk + 2) * WINDOW)
        k_lo, k_hi = _split_kv_heads(k_all[keys], 0.0)
        if blk == 0:
            valid = band & (j >= WINDOW * (t == 0).astype(jnp.int32))
        else:
            valid = band
        q_cols = []
        for c in range(ATT_QD // LANES):
            qc = u[rows, c * LANES:(c + 1) * LANES]
            q_cols.append(_rope_cols(qc, cos[rows], sup[rows], sdn[rows]) * scale)
        blocks.append((q_cols, k_lo, k_hi, v_all[keys].T, jnp.where(valid, 0.0, MASK_VALUE)))
    o_blocks, gate = _swa_attention(blocks, sinks_ref, lambda: _mm(xn, win_ref[:, ATT_QD:]))
    o = jnp.concatenate(o_blocks, axis=0) if nblk > 1 else o_blocks[0]
    y = x + _mm(o * _silu(gate), wout_ref[...])
    if final:
        y = _rms_scale(y, fin_ref[...])
    out_ref[0] = y


def _swa_prompt_layer(h, k, v, tables, norm_g, w_in, layer, sinks, w_out, final_norm, final):
    B, T, D = h.shape
    tile = min(SWA_TILE, T)
    per = tile // WINDOW
    const = lambda b, t: (0, 0)
    tab = pl.BlockSpec((tile, LANES), lambda b, t: (t, 0))
    prev = pl.BlockSpec((1, WINDOW, ATT_KVD), lambda b, t: (b, jnp.maximum(t * per - 1, 0), 0))
    cur = pl.BlockSpec((1, tile, ATT_KVD), lambda b, t: (b, t, 0))
    return pl.pallas_call(
        functools.partial(_swa_prompt_kernel, final),
        grid=(B, T // tile),
        in_specs=[
            pl.BlockSpec((1, tile, D), lambda b, t: (b, t, 0)),
            pl.BlockSpec((1, D), const),
            _layer_block(w_in, layer, True),
            pl.BlockSpec(memory_space=pltpu.SMEM),
            _layer_block(w_out, layer, True),
            pl.BlockSpec((1, D), const),
            prev, cur, prev, cur, tab, tab, tab,
        ],
        out_specs=pl.BlockSpec((1, tile, D), lambda b, t: (b, t, 0)),
        out_shape=jax.ShapeDtypeStruct((B, T, D), F32),
        compiler_params=_cparams("arbitrary", "arbitrary"),
        name="swa_prompt_final" if final else "swa_prompt",
    )(h, norm_g.reshape(1, D), w_in, sinks.reshape(1, ATT_QH), w_out, final_norm.reshape(1, D),
      k, k, v, v, *tables)


def _col(row):
    n = row.shape[1]
    return jnp.broadcast_to(row, (n, n)).T


def _hgrn_decode_kernel(x_ref, s_ref, ng_ref, win_ref, lbl_ref, gn_ref, wout_ref,
                        snew_ref, hout_ref, h_s, q_s, f_s, v_s, g_s, o_s, oblk_s):
    l, i = pl.program_id(0), pl.program_id(1)
    n_layers, n_blocks = pl.num_programs(0), pl.num_programs(1)
    bb = s_ref.shape[1]

    @pl.when(i == 0)
    def _():
        @pl.when(l == 0)
        def _():
            h_s[...] = x_ref[...]

        u = _mm(_rms_scale(h_s[...], ng_ref[0]), win_ref[0])
        lbl = lbl_ref[...]
        lb = jnp.zeros_like(lbl[:1])
        for layer in range(lbl.shape[0]):
            lb = jnp.where(l == layer, _lower_bound(lbl, layer), lb)
        q, f, v, g = (jnp.concatenate(cols, axis=1) for cols in zip(*(
            _split_head_columns(u[:, h * HG_HEAD_COLS:(h + 1) * HG_HEAD_COLS]) for h in range(HG_HEADS))))
        q_s[...] = _silu(q)
        f_s[...] = _forget_gates(f, lb)[0]
        v_s[...] = v
        g_s[...] = _silu(g)

    rows = pl.ds(pl.multiple_of(i * bb, bb), bb)
    qb, fb, vb = q_s[rows, :], f_s[rows, :], v_s[rows, :]
    for j in range(bb):
        for h in range(HG_HEADS):
            sl = slice(h * HG_DK, (h + 1) * HG_DK)
            v_r = vb[j:j + 1, h * HG_DV:(h + 1) * HG_DV]
            s_new = _col(fb[j:j + 1, sl]) * (s_ref[0, j, h] - v_r) + v_r
            snew_ref[0, j, h] = s_new
            oblk_s[j:j + 1, h * HG_DV:(h + 1) * HG_DV] = _mm(qb[j:j + 1, sl], s_new)
    o_s[rows, :] = oblk_s[...]

    @pl.when(i == n_blocks - 1)
    def _():
        gn = gn_ref[0]
        gated = []
        for h in range(HG_HEADS):
            sl = slice(h * HG_DV, (h + 1) * HG_DV)
            gated.append(_rms_scale(o_s[:, sl], gn) * g_s[:, sl])
        h_new = h_s[...] + _mm(jnp.concatenate(gated, axis=1), wout_ref[0])
        h_s[...] = h_new

        @pl.when(l == n_layers - 1)
        def _():
            hout_ref[...] = h_new


def _whole(shape):
    return pl.BlockSpec(shape, lambda *_: (0,) * len(shape))


def _hgrn_decode(h, states, norms, w_in, lb_logits, gnorms, w_out):
    B, D = h.shape
    n_a = states.shape[0]
    bb = DEC_BLOCK
    per_layer = lambda *tail: pl.BlockSpec((1,) + tail, lambda l, i: (l,) + (0,) * len(tail))
    st_blk = pl.BlockSpec((1, bb, HG_HEADS, HG_DK, HG_DV), lambda l, i: (l, i, 0, 0, 0))
    wide = pltpu.VMEM((B, HG_KEY), F32)
    s_new, h_new = pl.pallas_call(
        _hgrn_decode_kernel,
        grid=(n_a, B // bb),
        in_specs=[_whole((B, D)), st_blk, per_layer(1, D), per_layer(*w_in.shape[1:]),
                  _whole(lb_logits.shape), per_layer(1, HG_DV), per_layer(*w_out.shape[1:])],
        out_specs=[st_blk, _whole((B, D))],
        out_shape=[jax.ShapeDtypeStruct(states.shape, F32), jax.ShapeDtypeStruct((B, D), F32)],
        scratch_shapes=[pltpu.VMEM((B, D), F32)] + [wide] * 5 + [pltpu.VMEM((bb, HG_VAL), F32)],
        compiler_params=_cparams("arbitrary", "arbitrary"),
        name="hgrn_decode",
    )(h, states, norms.reshape(n_a, 1, D), w_in, lb_logits, gnorms.reshape(n_a, 1, HG_DV), w_out)
    return h_new, s_new


def _bf16_round(x):
    return x.astype(BF16).astype(F32)


def _swa_decode_kernel(final, h_ref, ng_ref, win_ref, cos_ref, sup_ref, sdn_ref, kc_ref, vc_ref,
                       kn_ref, vn_ref, sinks_ref, wout_ref, fin_ref, out_ref, qp_s, g_s, os_s):
    i = pl.program_id(0)
    n_blocks = pl.num_programs(0)
    bb, W = kc_ref.shape[0], kc_ref.shape[1]
    B = h_ref.shape[0]
    lane = lax.broadcasted_iota(jnp.int32, (B, LANES), 1)

    @pl.when(i == 0)
    def _():
        u = _mm(_rms_scale(h_ref[...], ng_ref[...]), win_ref[...])
        q = _rope_cols(u[:, :ATT_QD], cos_ref[...], sup_ref[...], sdn_ref[...]) * (ATT_HD ** -0.5)
        g_s[...] = _silu(u[:, ATT_QD:])
        for head in range(ATT_QH):
            col = q[:, (head // 2) * LANES:(head // 2 + 1) * LANES]
            kvh = head // ATT_GROUP
            if head % 2 != kvh:
                col = pltpu.roll(col, ATT_HD, axis=1)
            qp_s[:, head, :] = jnp.where((lane >= ATT_HD) if kvh == 1 else (lane < ATT_HD), col, 0.0)

    rows = pl.ds(pl.multiple_of(i * bb, bb), bb)
    kn, vn = kn_ref[rows, :], vn_ref[rows, :]
    qp = [qp_s[i * bb + j] for j in range(bb)]
    s_c = jnp.concatenate([_mm_nt(qp[j], kc_ref[j]) for j in range(bb)], axis=0)
    qp_all = jnp.concatenate(qp, axis=0)
    expand = lambda a: jnp.concatenate(
        [jnp.broadcast_to(a[j:j + 1], (ATT_QH, ATT_KVD)) for j in range(bb)], axis=0)
    s_n = jnp.sum(_bf16_round(qp_all) * _bf16_round(expand(kn)), axis=-1, keepdims=True)
    sink = jnp.concatenate([sinks_ref[...]] * bb, axis=0)
    m = jnp.maximum(jnp.maximum(jnp.max(s_c, axis=-1, keepdims=True), s_n), sink)
    p_c = jnp.exp(s_c - m)
    p_n = jnp.exp(s_n - m)
    inv = 1.0 / (jnp.sum(p_c, axis=-1, keepdims=True) + p_n + jnp.exp(sink - m))
    p_c = p_c * inv
    o = jnp.concatenate([_mm(p_c[j * ATT_QH:(j + 1) * ATT_QH], vc_ref[j]) for j in range(bb)], axis=0)
    o = o + _bf16_round(p_n * inv) * _bf16_round(expand(vn))
    for j in range(bb):
        os_s[i * bb + j] = o[j * ATT_QH:(j + 1) * ATT_QH]

    @pl.when(i == n_blocks - 1)
    def _():
        cols = []
        for c in range(ATT_QD // LANES):
            halves = []
            for par in range(2):
                head = 2 * c + par
                t = os_s[:, head, :]
                if par != head // ATT_GROUP:
                    t = pltpu.roll(t, ATT_HD, axis=1)
                halves.append(t)
            cols.append(jnp.where(lane < ATT_HD, halves[0], halves[1]))
        y = h_ref[...] + _mm(jnp.concatenate(cols, axis=1) * g_s[...], wout_ref[...])
        if final:
            y = _rms_scale(y, fin_ref[...])
        out_ref[...] = y


def _swa_decode_layer(h, k_new, v_new, cache_k, cache_v, tables, norm_g, w_in, layer, sinks, w_out,
                      final_norm, final):
    B, D = h.shape
    W = cache_k.shape[1]
    bb = DEC_BLOCK
    cache_blk = pl.BlockSpec((bb, W, ATT_KVD), lambda i: (i, 0, 0))
    head_rows = pltpu.VMEM((B, ATT_QH, ATT_KVD), F32)
    return pl.pallas_call(
        functools.partial(_swa_decode_kernel, final),
        grid=(B // bb,),
        in_specs=[_whole((B, D)), _whole((1, D)), _layer_block(w_in, layer)] + [_whole((1, LANES))] * 3
                 + [cache_blk, cache_blk, _whole((B, ATT_KVD)), _whole((B, ATT_KVD)),
                    _whole((ATT_QH, 1)), _layer_block(w_out, layer), _whole((1, D))],
        out_specs=_whole((B, D)),
        out_shape=jax.ShapeDtypeStruct((B, D), F32),
        scratch_shapes=[head_rows, pltpu.VMEM((B, ATT_QD), F32), head_rows],
        compiler_params=_cparams("arbitrary"),
        name="swa_decode_final" if final else "swa_decode",
    )(h, norm_g.reshape(1, D), w_in, *tables, cache_k.reshape(B, W, ATT_KVD),
      cache_v.reshape(B, W, ATT_KVD), k_new, v_new, sinks.reshape(ATT_QH, 1), w_out,
      final_norm.reshape(1, D))


def kernel(x_prompt, x_sample, state_hgrn, cache_k_win, cache_v_win, a_norm, a_w_in, a_lb_logits,
           a_gnorm, a_w_out, kv_norm, w_kv, b_norm, b_w_in, b_sinks, b_w_out, final_norm):
    B, T, D = x_prompt.shape
    BD, TD, _ = x_sample.shape
    n_a = a_w_in.shape[0]
    n_b = b_w_in.shape[0]
    a_w_in_b, a_w_out_b = _per_head_columns(a_w_in.astype(BF16)), a_w_out.astype(BF16)
    b_w_in_b, b_w_out_b = b_w_in.astype(BF16), b_w_out.astype(BF16)
    w_kv_b = w_kv.astype(BF16)

    tab_p = _rope_tables(np.arange(T))
    h = x_prompt
    st_p = []
    for l in range(n_a):
        kv = (kv_norm, w_kv_b, tab_p) if l == n_a - 1 else None
        h, s, *kv_p = _hgrn_prompt_layer(h, a_norm[l], a_w_in_b, a_lb_logits, l, a_gnorm[l],
                                         a_w_out_b, kv)
        st_p.append(s)
    k_p, v_p = kv_p
    for l in range(n_b):
        h = _swa_prompt_layer(h, k_p, v_p, tab_p, b_norm[l], b_w_in_b, l, b_sinks[l], b_w_out_b,
                              final_norm, l == n_b - 1)
    y_prompt = h
    w_keep = min(WINDOW, T)
    k_win = k_p[:, T - w_keep:].reshape(B, w_keep, ATT_KVH, ATT_HD)
    v_win = v_p[:, T - w_keep:].reshape(B, w_keep, ATT_KVH, ATT_HD)

    tab_s = _rope_tables(PAST_LEN + np.arange(TD))
    hs = x_sample.reshape(BD * TD, D)
    hs, st_s = _hgrn_decode(hs, state_hgrn, a_norm, a_w_in_b, a_lb_logits, a_gnorm, a_w_out_b)
    k_s, v_s = _shared_kv(hs.reshape(1, BD, D), kv_norm, w_kv_b,
                          tuple(jnp.broadcast_to(t, (BD, LANES)) for t in tab_s))
    k_s, v_s = k_s.reshape(BD, ATT_KVD), v_s.reshape(BD, ATT_KVD)
    for l in range(n_b):
        hs = _swa_decode_layer(hs, k_s, v_s, cache_k_win, cache_v_win, tab_s, b_norm[l], b_w_in_b, l,
                               b_sinks[l], b_w_out_b, final_norm, l == n_b - 1)
    y_sample = hs.reshape(BD, TD, D)

    return (y_prompt, y_sample, jnp.stack(st_p), st_s, k_win, v_win,
            k_s.reshape(BD, TD, ATT_KVH, ATT_HD), v_s.reshape(BD, TD, ATT_KVH, ATT_HD))
```

```python
import functools

import numpy as np
import jax
import jax.numpy as jnp
from jax import lax
from jax.experimental import pallas as pl
from jax.experimental.pallas import tpu as pltpu

F32 = jnp.float32
BF16 = jnp.bfloat16

D_MODEL = 1024
HG_HEADS = 8
HG_DK = 128
HG_DV = 128
HG_KEY = HG_HEADS * HG_DK
HG_VAL = HG_HEADS * HG_DV
HG_CHUNK = 64
HG_HALF = HG_CHUNK // 2
HG_SUB = 16
HG_MAX_HALF_DECAY = 60.0
ATT_HD = 64
ATT_QH = 16
ATT_KVH = 2
ATT_GROUP = ATT_QH // ATT_KVH
ATT_QD = ATT_QH * ATT_HD
ATT_KVD = ATT_KVH * ATT_HD
WINDOW = 128
ROPE_THETA = 500000.0
ROT_DIM = ATT_HD // 4
NORM_EPS = 1e-6
MASK_VALUE = -1e30
NEG_BIG = -1e30
PAST_LEN = 8192

SUBLANES = 8
LANES = 128
PROMPT_TILE = 512
SWA_TILE = 512
DEC_BLOCK = 8
SWA_DEC_BLOCK = 16
VMEM_LIMIT = 48 * 1024 * 1024


def _cparams(*sem):
    return pltpu.CompilerParams(dimension_semantics=sem, vmem_limit_bytes=VMEM_LIMIT)


def _layer_block(stacked, layer, single_buffer=False):
    tail = stacked.shape[1:]
    mode = dict(pipeline_mode=pl.Buffered(1)) if single_buffer else {}
    return pl.BlockSpec((None,) + tail, lambda *_: (layer,) + (0,) * len(tail), **mode)


def _mm(a, b):
    return jnp.dot(a.astype(BF16), b.astype(BF16), preferred_element_type=F32)


def _mm_nt(a, b):
    return lax.dot_general(a.astype(BF16), b.astype(BF16), (((1,), (1,)), ((), ())),
                           preferred_element_type=F32)


def _mm_tn(a, b):
    return lax.dot_general(a.astype(BF16), b.astype(BF16), (((0,), (0,)), ((), ())),
                           preferred_element_type=F32)


def _rms_scale(x, g):
    ms = jnp.mean(x * x, axis=-1, keepdims=True)
    return x * lax.rsqrt(ms + NORM_EPS) * g


def _silu(x):
    return x * (1.0 / (1.0 + jnp.exp(-x)))


def _lower_bound(lbl, layer):
    n = lbl.shape[0]
    rows = [lbl[i:i + 1, :] for i in range(n)]
    m = functools.reduce(jnp.maximum, rows)
    es = [jnp.exp(r - m) for r in rows]
    inv = 1.0 / functools.reduce(jnp.add, es)
    ps = [e * inv for e in es]
    return functools.reduce(jnp.add, ps[:layer + 1]) - ps[0]


def _forget_gates(f, lb):
    e = jnp.exp(-f)
    r = 1.0 / (1.0 + e)
    return lb + (1.0 - lb) * r, (1.0 - lb) * (e * r)


def _chunk_cumsum(x):
    rows = x.shape[0]
    pos = lax.broadcasted_iota(jnp.int32, x.shape, 0) % HG_CHUNK
    s = 1
    while s < HG_CHUNK:
        x = x + jnp.where(pos >= s, pltpu.roll(x, s, axis=0), 0.0)
        s *= 2
    del rows
    return x


def _diag_blocks(q, k, v, b):
    sub_iota = lax.broadcasted_iota(jnp.int32, (SUBLANES, LANES), 0)
    outs = []
    for blk in range(HG_CHUNK // HG_SUB):
        base = blk * HG_SUB
        groups = HG_SUB // SUBLANES
        qs = [q[base + g * SUBLANES: base + (g + 1) * SUBLANES] for g in range(groups)]
        bs = [b[base + g * SUBLANES: base + (g + 1) * SUBLANES] for g in range(groups)]
        accs = [jnp.zeros((SUBLANES, HG_DV), F32) for _ in range(groups)]
        for s in range(HG_SUB):
            row = base + s
            k_s = k[row:row + 1]
            b_s = b[row:row + 1]
            v_s = v[row:row + 1]
            for g in range(s // SUBLANES, groups):
                d = bs[g] - b_s
                if g == s // SUBLANES:
                    d = jnp.where(sub_iota >= (s % SUBLANES), d, NEG_BIG)
                p = qs[g] * (k_s * jnp.exp(d))
                a = jnp.sum(p, axis=-1, keepdims=True)
                accs[g] = accs[g] + a * v_s
        outs.extend(accs)
    return jnp.concatenate(outs, axis=0)


def _hgrn_chunk(q, k, v, b, st):
    C, H = HG_CHUNK, HG_HALF
    o_inter = _mm_nt(q * jnp.exp(b), st)
    r0 = b[H - 1:H]
    a0 = _mm_nt(q[H:] * jnp.exp(b[H:] - r0), k[:H] * jnp.exp(r0 - b[:H]))
    o_hi = _mm(a0, v[:H])
    Q = HG_SUB
    o_q = []
    for base in (0, H):
        r1 = b[base + Q - 1: base + Q]
        a1 = _mm_nt(q[base + Q: base + 2 * Q] * jnp.exp(b[base + Q: base + 2 * Q] - r1),
                    k[base: base + Q] * jnp.exp(r1 - b[base: base + Q]))
        o_q.append(_mm(a1, v[base: base + Q]))
    zero = jnp.zeros((Q, HG_DV), F32)
    o_off = jnp.concatenate([zero, o_q[0], o_hi[:Q], o_hi[Q:] + o_q[1]], axis=0)
    o = o_inter + o_off + _diag_blocks(q, k, v, b)
    b_end = b[C - 1:C]
    st_new = st * jnp.exp(b_end) + _mm_tn(v, k * jnp.exp(b_end - b))
    return o, st_new


def _hgrn_bounded_first(q, k, v, b, st):
    C, H = HG_CHUNK, HG_HALF
    r = b[H - 1:H]
    eq = jnp.exp(b - r)
    qt, kt = q * eq, k * jnp.exp(r - b)
    e_mid, e_hi = jnp.exp(r), eq[C - 1:C]
    scores = _mm_nt(qt, kt)
    v_t = v.T
    st_new = st * (e_mid * e_hi) + _mm(v_t, kt * e_hi)
    return scores, qt * e_mid, v_t, st_new


def _hgrn_bounded_second(scores, q_dec, v_t, st):
    t_i = lax.broadcasted_iota(jnp.int32, scores.shape, 0)
    s_i = lax.broadcasted_iota(jnp.int32, scores.shape, 1)
    a = jnp.where(s_i <= t_i, scores, 0.0)
    return _mm_nt(jnp.concatenate([q_dec, a], axis=1), jnp.concatenate([st, v_t], axis=1))


def _max_half_decay(b_s, tile):
    n = tile // HG_CHUNK
    worst = None
    for h in range(HG_HEADS):
        mid = b_s[h, pl.ds(HG_HALF - 1, n, stride=HG_CHUNK), :]
        end = b_s[h, pl.ds(HG_CHUNK - 1, n, stride=HG_CHUNK), :]
        w = jnp.maximum(-mid, mid - end)
        worst = w if worst is None else jnp.maximum(worst, w)
    return jnp.max(worst)


def _hgrn_prompt_kernel(layer, with_kv, *refs):
    h_ref, ng_ref, win_ref, lbl_ref, gn_ref, wout_ref = refs[:6]
    refs = refs[6:]
    if with_kv:
        kv_in, refs = refs[:5], refs[5:]
    out_ref, sfin_ref = refs[:2]
    refs = refs[2:]
    if with_kv:
        kv_out, refs = refs[:2], refs[2:]
    q_s, k_s, v_s, b_s, o_s, st_s = refs
    t = pl.program_id(1)
    nt = pl.num_programs(1)
    tile = h_ref.shape[1]

    @pl.when(t == 0)
    def _():
        st_s[...] = jnp.zeros_like(st_s)

    x = h_ref[0]
    u = _mm(_rms_scale(x, ng_ref[...]), win_ref[...])
    lb = _lower_bound(lbl_ref[...], layer)
    for h in range(HG_HEADS):
        sl = slice(h * HG_DK, (h + 1) * HG_DK)
        q_s[h] = _silu(u[:, sl])
        fg, kin = _forget_gates(u[:, HG_KEY + h * HG_DK: HG_KEY + (h + 1) * HG_DK], lb[:, sl])
        k_s[h] = kin
        b_s[h] = _chunk_cumsum(jnp.log(fg))
        v_s[h] = u[:, 2 * HG_KEY + h * HG_DV: 2 * HG_KEY + (h + 1) * HG_DV]

    def chunk_rows(c):
        return pl.ds(pl.multiple_of(c * HG_CHUNK, HG_CHUNK), HG_CHUNK)

    def load(c, h):
        rows = chunk_rows(c)
        return q_s[h, rows, :], k_s[h, rows, :], v_s[h, rows, :], b_s[h, rows, :], st_s[h]

    def bounded_chunks():
        pending = []
        for c in range(tile // HG_CHUNK):
            rows = slice(c * HG_CHUNK, (c + 1) * HG_CHUNK)
            started = []
            for h in range(HG_HEADS):
                st = st_s[h]
                scores, q_dec, v_t, st_new = _hgrn_bounded_first(
                    q_s[h, rows, :], k_s[h, rows, :], v_s[h, rows, :], b_s[h, rows, :], st)
                st_s[h] = st_new
                started.append((h, rows, scores, q_dec, v_t, st))
            for h, prev_rows, *second in pending:
                o_s[h, prev_rows, :] = _hgrn_bounded_second(*second)
            pending = started
        for h, prev_rows, *second in pending:
            o_s[h, prev_rows, :] = _hgrn_bounded_second(*second)

    def general_body(i, carry):
        c, h = i // HG_HEADS, i % HG_HEADS
        o, st_new = _hgrn_chunk(*load(c, h))
        o_s[h, chunk_rows(c), :] = o
        st_s[h] = st_new
        return carry

    bounded = _max_half_decay(b_s, tile) < HG_MAX_HALF_DECAY

    pl.when(bounded)(bounded_chunks)

    @pl.when(jnp.logical_not(bounded))
    def _():
        lax.fori_loop(0, (tile // HG_CHUNK) * HG_HEADS, general_body, 0)

    gn = gn_ref[...]
    gated = []
    for h in range(HG_HEADS):
        g = u[:, 2 * HG_KEY + HG_VAL + h * HG_DV: 2 * HG_KEY + HG_VAL + (h + 1) * HG_DV]
        gated.append(_rms_scale(o_s[h], gn) * _silu(g))
    y = x + _mm(jnp.concatenate(gated, axis=1), wout_ref[...])
    out_ref[0] = y
    if with_kv:
        _kv_project(y, *kv_in, *kv_out)

    @pl.when(t == nt - 1)
    def _():
        for h in range(HG_HEADS):
            sfin_ref[0, h] = st_s[h].T


def _hgrn_prompt_layer(h, norm_g, w_in, lb_logits, layer, gnorm, w_out, kv=None):
    B, T, D = h.shape
    tile = min(PROMPT_TILE, T)
    n_a = lb_logits.shape[0]
    const = lambda b, t: (0, 0)
    row_tile = lambda width: pl.BlockSpec((1, tile, width), lambda b, t: (b, t, 0))
    head_scratch = pltpu.VMEM((HG_HEADS, tile, HG_DK), F32)
    operands = [h, norm_g.reshape(1, D), w_in, lb_logits, gnorm.reshape(1, HG_DV), w_out]
    in_specs = [row_tile(D), pl.BlockSpec((1, D), const), _layer_block(w_in, layer, True),
                pl.BlockSpec((n_a, HG_KEY), const), pl.BlockSpec((1, HG_DV), const),
                _layer_block(w_out, layer, True)]
    out_specs = [row_tile(D), pl.BlockSpec((1, HG_HEADS, HG_DK, HG_DV), lambda b, t: (b, 0, 0, 0))]
    out_shape = [jax.ShapeDtypeStruct((B, T, D), F32),
                 jax.ShapeDtypeStruct((B, HG_HEADS, HG_DK, HG_DV), F32)]
    if kv is not None:
        kv_norm, w_kv, tables = kv
        operands += [kv_norm.reshape(1, D), w_kv, *tables]
        in_specs += [pl.BlockSpec((1, D), const), pl.BlockSpec(w_kv.shape, const)]
        in_specs += [pl.BlockSpec((tile, LANES), lambda b, t: (t, 0))] * 3
        out_specs += [row_tile(ATT_KVD)] * 2
        out_shape += [jax.ShapeDtypeStruct((B, T, ATT_KVD), F32)] * 2
    return pl.pallas_call(
        functools.partial(_hgrn_prompt_kernel, layer, kv is not None),
        grid=(B, T // tile),
        in_specs=in_specs,
        out_specs=out_specs,
        out_shape=out_shape,
        scratch_shapes=[head_scratch] * 5 + [pltpu.VMEM((HG_HEADS, HG_DV, HG_DK), F32)],
        compiler_params=_cparams("arbitrary", "arbitrary"),
        name=f"hgrn_prompt_l{layer}",
    )(*operands)


def _rope_tables(pos):
    half = ROT_DIM // 2
    pos = np.asarray(pos, np.float64)
    inv_freq = 1.0 / (ROPE_THETA ** (np.arange(half, dtype=np.float64) * 2.0 / ROT_DIM))
    ang = (pos[:, None].astype(np.float32) * inv_freq[None, :].astype(np.float32)).astype(np.float64)
    cos, sin = np.cos(ang), np.sin(ang)
    n = pos.shape[0]
    c = np.ones((n, ATT_HD)); s_up = np.zeros((n, ATT_HD)); s_dn = np.zeros((n, ATT_HD))
    c[:, :half] = cos; c[:, half:ROT_DIM] = cos
    s_dn[:, :half] = -sin
    s_up[:, half:ROT_DIM] = sin
    rep = LANES // ATT_HD
    tab = [np.tile(a, (1, rep)).astype(np.float32) for a in (c, s_up, s_dn)]
    return tuple(jnp.asarray(a) for a in tab)


def _rope_cols(x, cos, s_up, s_dn):
    half = ROT_DIM // 2
    cols = []
    for c in range(x.shape[1] // LANES):
        xc = x[:, c * LANES:(c + 1) * LANES]
        cols.append(xc * cos + pltpu.roll(xc, half, axis=1) * s_up
                    + pltpu.roll(xc, LANES - half, axis=1) * s_dn)
    return cols[0] if len(cols) == 1 else jnp.concatenate(cols, axis=1)


def _kv_kernel(h_ref, *refs):
    _kv_project(h_ref[0], *refs)


def _kv_project(x, ng_ref, w_ref, cos_ref, sup_ref, sdn_ref, k_ref, v_ref):
    u = _mm(_rms_scale(x, ng_ref[...]), w_ref[...])
    k_ref[0] = _rope_cols(u[:, :ATT_KVD], cos_ref[...], sup_ref[...], sdn_ref[...])
    v_ref[0] = u[:, ATT_KVD:]


def _shared_kv(h, kv_norm, w_kv, tables):
    B, T, D = h.shape
    tile = min(PROMPT_TILE, T)
    const = lambda b, t: (0, 0)
    tab = pl.BlockSpec((tile, LANES), lambda b, t: (t, 0))
    kv_spec = pl.BlockSpec((1, tile, ATT_KVD), lambda b, t: (b, t, 0))
    return pl.pallas_call(
        _kv_kernel,
        grid=(B, T // tile),
        in_specs=[pl.BlockSpec((1, tile, D), lambda b, t: (b, t, 0)),
                  pl.BlockSpec((1, D), const), pl.BlockSpec(w_kv.shape, const), tab, tab, tab],
        out_specs=[kv_spec, kv_spec],
        out_shape=[jax.ShapeDtypeStruct((B, T, ATT_KVD), F32)] * 2,
        compiler_params=_cparams("arbitrary", "arbitrary"),
        name="shared_kv",
    )(h, kv_norm.reshape(1, D), w_kv, *tables)


PAIRS_PER_KV = ATT_GROUP // 2


def _swa_attention(blocks, sinks_ref, side_product):
    groups = [(blk, kvh) for blk in range(len(blocks)) for kvh in range(ATT_KVH)]
    n_keys = blocks[0][4].shape[0]
    ones_rows = jnp.ones((SUBLANES, n_keys), F32)

    def score_products(g):
        blk, kvh = groups[g]
        q_cols, k_lo, k_hi, _, bias_t = blocks[blk]
        qs = jnp.concatenate(q_cols[kvh * PAIRS_PER_KV:(kvh + 1) * PAIRS_PER_KV], axis=0)
        keys = jnp.concatenate([k_lo[kvh], k_hi[kvh]], axis=0)
        bias = jnp.concatenate([jnp.concatenate([bias_t] * PAIRS_PER_KV, axis=1)] * 2, axis=0)
        return _mm_nt(keys, qs) + bias

    def softmax_terms(g, s_t):
        kvh = groups[g][1]
        cols = range(kvh * PAIRS_PER_KV, (kvh + 1) * PAIRS_PER_KV)
        terms = []
        for par in range(2):
            s = s_t[par * n_keys:(par + 1) * n_keys]
            sink = jnp.concatenate(
                [jnp.full((1, WINDOW), sinks_ref[0, 2 * c + par], F32) for c in cols], axis=1)
            m = jnp.maximum(jnp.max(s, axis=0, keepdims=True), sink)
            terms.append((jnp.exp(s - m), jnp.exp(sink - m)))
        return terms

    def value_products(g, terms):
        blk, kvh = groups[g]
        v_t = blocks[blk][3]
        lhs = jnp.concatenate([v_t[kvh * ATT_HD:(kvh + 1) * ATT_HD], ones_rows], axis=0)
        halves = []
        for p, sink_term in terms:
            r = _mm(lhs, p)
            halves.append(r[:ATT_HD] * (1.0 / (r[ATT_HD:ATT_HD + 1] + sink_term)))
        return halves

    scores = [score_products(g) for g in range(len(groups))]
    side = side_product()
    terms = [softmax_terms(g, s) for g, s in enumerate(scores)]
    outs = [value_products(g, t) for g, t in enumerate(terms)]
    res = []
    for b in range(len(blocks)):
        cols = []
        for kvh in range(ATT_KVH):
            even, odd = outs[b * ATT_KVH + kvh]
            for i in range(PAIRS_PER_KV):
                queries = slice(i * WINDOW, (i + 1) * WINDOW)
                cols.append(jnp.concatenate([even[:, queries], odd[:, queries]], axis=0).T)
        res.append(jnp.concatenate(cols, axis=1))
    return res, side


def _split_kv_heads(a, fill):
    lane = lax.broadcasted_iota(jnp.int32, a.shape, 1)
    low = lane < ATT_HD
    sw = pltpu.roll(a, ATT_HD, axis=1)
    lo = [jnp.where(low, a, fill), jnp.where(low, sw, fill)]
    hi = [jnp.where(low, fill, sw), jnp.where(low, fill, a)]
    return lo, hi


def _swa_prompt_kernel(final, h_ref, ng_ref, win_ref, sinks_ref, wout_ref, fin_ref,
                       kp_ref, kc_ref, vp_ref, vc_ref, cos_ref, sup_ref, sdn_ref, out_ref):
    t = pl.program_id(1)
    tile = h_ref.shape[1]
    nblk = tile // WINDOW
    x = h_ref[0]
    xn = _rms_scale(x, ng_ref[...]).astype(BF16)
    u = _mm(xn, win_ref[:, :ATT_QD])
    scale = ATT_HD ** -0.5
    cos, sup, sdn = cos_ref[...], sup_ref[...], sdn_ref[...]
    k_all = jnp.concatenate([kp_ref[0], kc_ref[0]], axis=0)
    v_all = jnp.concatenate([vp_ref[0], vc_ref[0]], axis=0)
    j = lax.broadcasted_iota(jnp.int32, (2 * WINDOW, WINDOW), 0)
    tq = lax.broadcasted_iota(jnp.int32, (2 * WINDOW, WINDOW), 1)
    rel = tq + WINDOW - j
    band = (rel >= 0) & (rel <= WINDOW)
    blocks = []
    for blk in range(nblk):
        rows = slice(blk * WINDOW, (blk + 1) * WINDOW)
        keys = slice(blk * WINDOW, (blk + 2) * WINDOW)
        k_lo, k_hi = _split_kv_heads(k_all[keys], 0.0)
        if blk == 0:
            valid = band & (j >= WINDOW * (t == 0).astype(jnp.int32))
        else:
            valid = band
        q_cols = []
        for c in range(ATT_QD // LANES):
            qc = u[rows, c * LANES:(c + 1) * LANES]
            q_cols.append(_rope_cols(qc, cos[rows], sup[rows], sdn[rows]) * scale)
        blocks.append((q_cols, k_lo, k_hi, v_all[keys].T, jnp.where(valid, 0.0, MASK_VALUE)))
    o_blocks, gate = _swa_attention(blocks, sinks_ref, lambda: _mm(xn, win_ref[:, ATT_QD:]))
    o = jnp.concatenate(o_blocks, axis=0) if nblk > 1 else o_blocks[0]
    y = x + _mm(o * _silu(gate), wout_ref[...])
    if final:
        y = _rms_scale(y, fin_ref[...])
    out_ref[0] = y


def _swa_prompt_layer(h, k, v, tables, norm_g, w_in, layer, sinks, w_out, final_norm, final):
    B, T, D = h.shape
    tile = min(SWA_TILE, T)
    per = tile // WINDOW
    const = lambda b, t: (0, 0)
    tab = pl.BlockSpec((tile, LANES), lambda b, t: (t, 0))
    prev = pl.BlockSpec((1, WINDOW, ATT_KVD), lambda b, t: (b, jnp.maximum(t * per - 1, 0), 0))
    cur = pl.BlockSpec((1, tile, ATT_KVD), lambda b, t: (b, t, 0))
    return pl.pallas_call(
        functools.partial(_swa_prompt_kernel, final),
        grid=(B, T // tile),
        in_specs=[
            pl.BlockSpec((1, tile, D), lambda b, t: (b, t, 0)),
            pl.BlockSpec((1, D), const),
            _layer_block(w_in, layer, True),
            pl.BlockSpec(memory_space=pltpu.SMEM),
            _layer_block(w_out, layer, True),
            pl.BlockSpec((1, D), const),
            prev, cur, prev, cur, tab, tab, tab,
        ],
        out_specs=pl.BlockSpec((1, tile, D), lambda b, t: (b, t, 0)),
        out_shape=jax.ShapeDtypeStruct((B, T, D), F32),
        compiler_params=_cparams("arbitrary", "arbitrary"),
        name="swa_prompt_final" if final else "swa_prompt",
    )(h, norm_g.reshape(1, D), w_in, sinks.reshape(1, ATT_QH), w_out, final_norm.reshape(1, D),
      k, k, v, v, *tables)


def _col(row):
    n = row.shape[1]
    return jnp.broadcast_to(row, (n, n)).T


def _hgrn_decode_kernel(x_ref, s_ref, ng_ref, win_ref, lbl_ref, gn_ref, wout_ref,
                        snew_ref, hout_ref, h_s, q_s, f_s, v_s, g_s, o_s, oblk_s):
    l, i = pl.program_id(0), pl.program_id(1)
    n_layers, n_blocks = pl.num_programs(0), pl.num_programs(1)
    bb = s_ref.shape[1]

    @pl.when(i == 0)
    def _():
        @pl.when(l == 0)
        def _():
            h_s[...] = x_ref[...]

        u = _mm(_rms_scale(h_s[...], ng_ref[0]), win_ref[0])
        lbl = lbl_ref[...]
        lb = jnp.zeros_like(lbl[:1])
        for layer in range(lbl.shape[0]):
            lb = jnp.where(l == layer, _lower_bound(lbl, layer), lb)
        q_s[...] = _silu(u[:, :HG_KEY])
        f_s[...] = _forget_gates(u[:, HG_KEY:2 * HG_KEY], lb)[0]
        v_s[...] = u[:, 2 * HG_KEY:2 * HG_KEY + HG_VAL]
        g_s[...] = _silu(u[:, 2 * HG_KEY + HG_VAL:])

    rows = pl.ds(pl.multiple_of(i * bb, bb), bb)
    qb, fb, vb = q_s[rows, :], f_s[rows, :], v_s[rows, :]
    for j in range(bb):
        for h in range(HG_HEADS):
            sl = slice(h * HG_DK, (h + 1) * HG_DK)
            v_r = vb[j:j + 1, h * HG_DV:(h + 1) * HG_DV]
            s_new = _col(fb[j:j + 1, sl]) * (s_ref[0, j, h] - v_r) + v_r
            snew_ref[0, j, h] = s_new
            oblk_s[j:j + 1, h * HG_DV:(h + 1) * HG_DV] = _mm(qb[j:j + 1, sl], s_new)
    o_s[rows, :] = oblk_s[...]

    @pl.when(i == n_blocks - 1)
    def _():
        gn = gn_ref[0]
        gated = []
        for h in range(HG_HEADS):
            sl = slice(h * HG_DV, (h + 1) * HG_DV)
            gated.append(_rms_scale(o_s[:, sl], gn) * g_s[:, sl])
        h_new = h_s[...] + _mm(jnp.concatenate(gated, axis=1), wout_ref[0])
        h_s[...] = h_new

        @pl.when(l == n_layers - 1)
        def _():
            hout_ref[...] = h_new


def _whole(shape):
    return pl.BlockSpec(shape, lambda *_: (0,) * len(shape))


def _hgrn_decode(h, states, norms, w_in, lb_logits, gnorms, w_out):
    B, D = h.shape
    n_a = states.shape[0]
    bb = DEC_BLOCK
    per_layer = lambda *tail: pl.BlockSpec((1,) + tail, lambda l, i: (l,) + (0,) * len(tail))
    st_blk = pl.BlockSpec((1, bb, HG_HEADS, HG_DK, HG_DV), lambda l, i: (l, i, 0, 0, 0))
    wide = pltpu.VMEM((B, HG_KEY), F32)
    s_new, h_new = pl.pallas_call(
        _hgrn_decode_kernel,
        grid=(n_a, B // bb),
        in_specs=[_whole((B, D)), st_blk, per_layer(1, D), per_layer(*w_in.shape[1:]),
                  _whole(lb_logits.shape), per_layer(1, HG_DV), per_layer(*w_out.shape[1:])],
        out_specs=[st_blk, _whole((B, D))],
        out_shape=[jax.ShapeDtypeStruct(states.shape, F32), jax.ShapeDtypeStruct((B, D), F32)],
        scratch_shapes=[pltpu.VMEM((B, D), F32)] + [wide] * 5 + [pltpu.VMEM((bb, HG_VAL), F32)],
        compiler_params=_cparams("arbitrary", "arbitrary"),
        name="hgrn_decode",
    )(h, states, norms.reshape(n_a, 1, D), w_in, lb_logits, gnorms.reshape(n_a, 1, HG_DV), w_out)
    return h_new, s_new


def _bf16_round(x):
    return x.astype(BF16).astype(F32)


def _swa_decode_kernel(final, h_ref, ng_ref, win_ref, cos_ref, sup_ref, sdn_ref, kc_ref, vc_ref,
                       kn_ref, vn_ref, sinks_ref, wout_ref, fin_ref, out_ref, qp_s, g_s, os_s):
    i = pl.program_id(0)
    n_blocks = pl.num_programs(0)
    bb, W = kc_ref.shape[0], kc_ref.shape[1]
    B = h_ref.shape[0]
    lane = lax.broadcasted_iota(jnp.int32, (B, LANES), 1)

    @pl.when(i == 0)
    def _():
        u = _mm(_rms_scale(h_ref[...], ng_ref[...]), win_ref[...])
        q = _rope_cols(u[:, :ATT_QD], cos_ref[...], sup_ref[...], sdn_ref[...]) * (ATT_HD ** -0.5)
        g_s[...] = _silu(u[:, ATT_QD:])
        for head in range(ATT_QH):
            col = q[:, (head // 2) * LANES:(head // 2 + 1) * LANES]
            kvh = head // ATT_GROUP
            if head % 2 != kvh:
                col = pltpu.roll(col, ATT_HD, axis=1)
            qp_s[:, head, :] = jnp.where((lane >= ATT_HD) if kvh == 1 else (lane < ATT_HD), col, 0.0)

    rows = pl.ds(pl.multiple_of(i * bb, bb), bb)
    kn, vn = kn_ref[rows, :], vn_ref[rows, :]
    qp = [qp_s[i * bb + j] for j in range(bb)]
    s_c = jnp.concatenate([_mm_nt(qp[j], kc_ref[j]) for j in range(bb)], axis=0)
    qp_all = jnp.concatenate(qp, axis=0)
    expand = lambda a: jnp.concatenate(
        [jnp.broadcast_to(a[j:j + 1], (ATT_QH, ATT_KVD)) for j in range(bb)], axis=0)
    s_n = jnp.sum(_bf16_round(qp_all) * _bf16_round(expand(kn)), axis=-1, keepdims=True)
    sink = jnp.concatenate([sinks_ref[...]] * bb, axis=0)
    m = jnp.maximum(jnp.maximum(jnp.max(s_c, axis=-1, keepdims=True), s_n), sink)
    p_c = jnp.exp(s_c - m)
    p_n = jnp.exp(s_n - m)
    inv = 1.0 / (jnp.sum(p_c, axis=-1, keepdims=True) + p_n + jnp.exp(sink - m))
    p_c = p_c * inv
    o = jnp.concatenate([_mm(p_c[j * ATT_QH:(j + 1) * ATT_QH], vc_ref[j]) for j in range(bb)], axis=0)
    o = o + _bf16_round(p_n * inv) * _bf16_round(expand(vn))
    for j in range(bb):
        os_s[i * bb + j] = o[j * ATT_QH:(j + 1) * ATT_QH]

    @pl.when(i == n_blocks - 1)
    def _():
        cols = []
        for c in range(ATT_QD // LANES):
            halves = []
            for par in range(2):
                head = 2 * c + par
                t = os_s[:, head, :]
                if par != head // ATT_GROUP:
                    t = pltpu.roll(t, ATT_HD, axis=1)
                halves.append(t)
            cols.append(jnp.where(lane < ATT_HD, halves[0], halves[1]))
        y = h_ref[...] + _mm(jnp.concatenate(cols, axis=1) * g_s[...], wout_ref[...])
        if final:
            y = _rms_scale(y, fin_ref[...])
        out_ref[...] = y


def _swa_decode_layer(h, k_new, v_new, cache_k, cache_v, tables, norm_g, w_in, layer, sinks, w_out,
                      final_norm, final):
    B, D = h.shape
    W = cache_k.shape[1]
    bb = min(SWA_DEC_BLOCK, B)
    cache_blk = pl.BlockSpec((bb, W, ATT_KVD), lambda i: (i, 0, 0))
    head_rows = pltpu.VMEM((B, ATT_QH, ATT_KVD), F32)
    return pl.pallas_call(
        functools.partial(_swa_decode_kernel, final),
        grid=(B // bb,),
        in_specs=[_whole((B, D)), _whole((1, D)), _layer_block(w_in, layer)] + [_whole((1, LANES))] * 3
                 + [cache_blk, cache_blk, _whole((B, ATT_KVD)), _whole((B, ATT_KVD)),
                    _whole((ATT_QH, 1)), _layer_block(w_out, layer), _whole((1, D))],
        out_specs=_whole((B, D)),
        out_shape=jax.ShapeDtypeStruct((B, D), F32),
        scratch_shapes=[head_rows, pltpu.VMEM((B, ATT_QD), F32), head_rows],
        compiler_params=_cparams("arbitrary"),
        name="swa_decode_final" if final else "swa_decode",
    )(h, norm_g.reshape(1, D), w_in, *tables, cache_k.reshape(B, W, ATT_KVD),
      cache_v.reshape(B, W, ATT_KVD), k_new, v_new, sinks.reshape(ATT_QH, 1), w_out,
      final_norm.reshape(1, D))


def kernel(x_prompt, x_sample, state_hgrn, cache_k_win, cache_v_win, a_norm, a_w_in, a_lb_logits,
           a_gnorm, a_w_out, kv_norm, w_kv, b_norm, b_w_in, b_sinks, b_w_out, final_norm):
    B, T, D = x_prompt.shape
    BD, TD, _ = x_sample.shape
    n_a = a_w_in.shape[0]
    n_b = b_w_in.shape[0]
    a_w_in_b, a_w_out_b = a_w_in.astype(BF16), a_w_out.astype(BF16)
    b_w_in_b, b_w_out_b = b_w_in.astype(BF16), b_w_out.astype(BF16)
    w_kv_b = w_kv.astype(BF16)

    tab_p = _rope_tables(np.arange(T))
    h = x_prompt
    st_p = []
    for l in range(n_a):
        kv = (kv_norm, w_kv_b, tab_p) if l == n_a - 1 else None
        h, s, *kv_p = _hgrn_prompt_layer(h, a_norm[l], a_w_in_b, a_lb_logits, l, a_gnorm[l],
                                         a_w_out_b, kv)
        st_p.append(s)
    k_p, v_p = kv_p
    for l in range(n_b):
        h = _swa_prompt_layer(h, k_p, v_p, tab_p, b_norm[l], b_w_in_b, l, b_sinks[l], b_w_out_b,
                              final_norm, l == n_b - 1)
    y_prompt = h
    w_keep = min(WINDOW, T)
    k_win = k_p[:, T - w_keep:].reshape(B, w_keep, ATT_KVH, ATT_HD)
    v_win = v_p[:, T - w_keep:].reshape(B, w_keep, ATT_KVH, ATT_HD)

    tab_s = _rope_tables(PAST_LEN + np.arange(TD))
    hs = x_sample.reshape(BD * TD, D)
    hs, st_s = _hgrn_decode(hs, state_hgrn, a_norm, a_w_in_b, a_lb_logits, a_gnorm, a_w_out_b)
    k_s, v_s = _shared_kv(hs.reshape(1, BD, D), kv_norm, w_kv_b,
                          tuple(jnp.broadcast_to(t, (BD, LANES)) for t in tab_s))
    k_s, v_s = k_s.reshape(BD, ATT_KVD), v_s.reshape(BD, ATT_KVD)
    for l in range(n_b):
        hs = _swa_decode_layer(hs, k_s, v_s, cache_k_win, cache_v_win, tab_s, b_norm[l], b_w_in_b, l,
                               b_sinks[l], b_w_out_b, final_norm, l == n_b - 1)
    y_sample = hs.reshape(BD, TD, D)

    return (y_prompt, y_sample, jnp.stack(st_p), st_s, k_win, v_win,
            k_s.reshape(BD, TD, ATT_KVH, ATT_HD), v_s.reshape(BD, TD, ATT_KVH, ATT_HD))
```

```python
import functools

import numpy as np
import jax
import jax.numpy as jnp
from jax import lax
from jax.experimental import pallas as pl
from jax.experimental.pallas import tpu as pltpu

F32 = jnp.float32
BF16 = jnp.bfloat16

D_MODEL = 1024
HG_HEADS = 8
HG_DK = 128
HG_DV = 128
HG_KEY = HG_HEADS * HG_DK
HG_VAL = HG_HEADS * HG_DV
HG_CHUNK = 64
HG_HALF = HG_CHUNK // 2
HG_SUB = 16
HG_MAX_HALF_DECAY = 60.0
ATT_HD = 64
ATT_QH = 16
ATT_KVH = 2
ATT_GROUP = ATT_QH // ATT_KVH
ATT_QD = ATT_QH * ATT_HD
ATT_KVD = ATT_KVH * ATT_HD
WINDOW = 128
ROPE_THETA = 500000.0
ROT_DIM = ATT_HD // 4
NORM_EPS = 1e-6
MASK_VALUE = -1e30
NEG_BIG = -1e30
PAST_LEN = 8192

SUBLANES = 8
LANES = 128
PROMPT_TILE = 512
SWA_TILE = 1024
DEC_BLOCK = 8
SWA_DEC_BLOCK = 16
VMEM_LIMIT = 48 * 1024 * 1024


def _cparams(*sem):
    return pltpu.CompilerParams(dimension_semantics=sem, vmem_limit_bytes=VMEM_LIMIT)


def _layer_block(stacked, layer, single_buffer=False):
    tail = stacked.shape[1:]
    mode = dict(pipeline_mode=pl.Buffered(1)) if single_buffer else {}
    return pl.BlockSpec((None,) + tail, lambda *_: (layer,) + (0,) * len(tail), **mode)


def _mm(a, b):
    return jnp.dot(a.astype(BF16), b.astype(BF16), preferred_element_type=F32)


def _mm_nt(a, b):
    return lax.dot_general(a.astype(BF16), b.astype(BF16), (((1,), (1,)), ((), ())),
                           preferred_element_type=F32)


def _mm_tn(a, b):
    return lax.dot_general(a.astype(BF16), b.astype(BF16), (((0,), (0,)), ((), ())),
                           preferred_element_type=F32)


def _rms_scale(x, g):
    ms = jnp.mean(x * x, axis=-1, keepdims=True)
    return x * lax.rsqrt(ms + NORM_EPS) * g


def _silu(x):
    return x * (1.0 / (1.0 + jnp.exp(-x)))


def _lower_bound(lbl, layer):
    n = lbl.shape[0]
    rows = [lbl[i:i + 1, :] for i in range(n)]
    m = functools.reduce(jnp.maximum, rows)
    es = [jnp.exp(r - m) for r in rows]
    inv = 1.0 / functools.reduce(jnp.add, es)
    ps = [e * inv for e in es]
    return functools.reduce(jnp.add, ps[:layer + 1]) - ps[0]


def _forget_gates(f, lb):
    e = jnp.exp(-f)
    r = 1.0 / (1.0 + e)
    return lb + (1.0 - lb) * r, (1.0 - lb) * (e * r)


def _chunk_cumsum(x):
    rows = x.shape[0]
    pos = lax.broadcasted_iota(jnp.int32, x.shape, 0) % HG_CHUNK
    s = 1
    while s < HG_CHUNK:
        x = x + jnp.where(pos >= s, pltpu.roll(x, s, axis=0), 0.0)
        s *= 2
    del rows
    return x


def _diag_blocks(q, k, v, b):
    sub_iota = lax.broadcasted_iota(jnp.int32, (SUBLANES, LANES), 0)
    outs = []
    for blk in range(HG_CHUNK // HG_SUB):
        base = blk * HG_SUB
        groups = HG_SUB // SUBLANES
        qs = [q[base + g * SUBLANES: base + (g + 1) * SUBLANES] for g in range(groups)]
        bs = [b[base + g * SUBLANES: base + (g + 1) * SUBLANES] for g in range(groups)]
        accs = [jnp.zeros((SUBLANES, HG_DV), F32) for _ in range(groups)]
        for s in range(HG_SUB):
            row = base + s
            k_s = k[row:row + 1]
            b_s = b[row:row + 1]
            v_s = v[row:row + 1]
            for g in range(s // SUBLANES, groups):
                d = bs[g] - b_s
                if g == s // SUBLANES:
                    d = jnp.where(sub_iota >= (s % SUBLANES), d, NEG_BIG)
                p = qs[g] * (k_s * jnp.exp(d))
                a = jnp.sum(p, axis=-1, keepdims=True)
                accs[g] = accs[g] + a * v_s
        outs.extend(accs)
    return jnp.concatenate(outs, axis=0)


def _hgrn_chunk(q, k, v, b, st):
    C, H = HG_CHUNK, HG_HALF
    o_inter = _mm_nt(q * jnp.exp(b), st)
    r0 = b[H - 1:H]
    a0 = _mm_nt(q[H:] * jnp.exp(b[H:] - r0), k[:H] * jnp.exp(r0 - b[:H]))
    o_hi = _mm(a0, v[:H])
    Q = HG_SUB
    o_q = []
    for base in (0, H):
        r1 = b[base + Q - 1: base + Q]
        a1 = _mm_nt(q[base + Q: base + 2 * Q] * jnp.exp(b[base + Q: base + 2 * Q] - r1),
                    k[base: base + Q] * jnp.exp(r1 - b[base: base + Q]))
        o_q.append(_mm(a1, v[base: base + Q]))
    zero = jnp.zeros((Q, HG_DV), F32)
    o_off = jnp.concatenate([zero, o_q[0], o_hi[:Q], o_hi[Q:] + o_q[1]], axis=0)
    o = o_inter + o_off + _diag_blocks(q, k, v, b)
    b_end = b[C - 1:C]
    st_new = st * jnp.exp(b_end) + _mm_tn(v, k * jnp.exp(b_end - b))
    return o, st_new


def _hgrn_bounded_first(q, k, v, b, st):
    C, H = HG_CHUNK, HG_HALF
    r = b[H - 1:H]
    eq = jnp.exp(b - r)
    qt, kt = q * eq, k * jnp.exp(r - b)
    e_mid, e_hi = jnp.exp(r), eq[C - 1:C]
    scores = _mm_nt(qt, kt)
    v_t = v.T
    st_new = st * (e_mid * e_hi) + _mm(v_t, kt * e_hi)
    return scores, qt * e_mid, v_t, st_new


def _hgrn_bounded_second(scores, q_dec, v_t, st):
    t_i = lax.broadcasted_iota(jnp.int32, scores.shape, 0)
    s_i = lax.broadcasted_iota(jnp.int32, scores.shape, 1)
    a = jnp.where(s_i <= t_i, scores, 0.0)
    return _mm_nt(jnp.concatenate([q_dec, a], axis=1), jnp.concatenate([st, v_t], axis=1))


def _max_half_decay(b_s, tile):
    n = tile // HG_CHUNK
    worst = None
    for h in range(HG_HEADS):
        mid = b_s[h, pl.ds(HG_HALF - 1, n, stride=HG_CHUNK), :]
        end = b_s[h, pl.ds(HG_CHUNK - 1, n, stride=HG_CHUNK), :]
        w = jnp.maximum(-mid, mid - end)
        worst = w if worst is None else jnp.maximum(worst, w)
    return jnp.max(worst)


def _hgrn_prompt_kernel(layer, with_kv, *refs):
    h_ref, ng_ref, win_ref, lbl_ref, gn_ref, wout_ref = refs[:6]
    refs = refs[6:]
    if with_kv:
        kv_in, refs = refs[:5], refs[5:]
    out_ref, sfin_ref = refs[:2]
    refs = refs[2:]
    if with_kv:
        kv_out, refs = refs[:2], refs[2:]
    q_s, k_s, v_s, b_s, o_s, st_s = refs
    t = pl.program_id(1)
    nt = pl.num_programs(1)
    tile = h_ref.shape[1]

    @pl.when(t == 0)
    def _():
        st_s[...] = jnp.zeros_like(st_s)

    x = h_ref[0]
    u = _mm(_rms_scale(x, ng_ref[...]), win_ref[...])
    lb = _lower_bound(lbl_ref[...], layer)
    for h in range(HG_HEADS):
        sl = slice(h * HG_DK, (h + 1) * HG_DK)
        q_s[h] = _silu(u[:, sl])
        fg, kin = _forget_gates(u[:, HG_KEY + h * HG_DK: HG_KEY + (h + 1) * HG_DK], lb[:, sl])
        k_s[h] = kin
        b_s[h] = _chunk_cumsum(jnp.log(fg))
        v_s[h] = u[:, 2 * HG_KEY + h * HG_DV: 2 * HG_KEY + (h + 1) * HG_DV]

    def chunk_rows(c):
        return pl.ds(pl.multiple_of(c * HG_CHUNK, HG_CHUNK), HG_CHUNK)

    def load(c, h):
        rows = chunk_rows(c)
        return q_s[h, rows, :], k_s[h, rows, :], v_s[h, rows, :], b_s[h, rows, :], st_s[h]

    def bounded_chunks():
        pending = []
        for c in range(tile // HG_CHUNK):
            rows = slice(c * HG_CHUNK, (c + 1) * HG_CHUNK)
            started = []
            for h in range(HG_HEADS):
                st = st_s[h]
                scores, q_dec, v_t, st_new = _hgrn_bounded_first(
                    q_s[h, rows, :], k_s[h, rows, :], v_s[h, rows, :], b_s[h, rows, :], st)
                st_s[h] = st_new
                started.append((h, rows, scores, q_dec, v_t, st))
            for h, prev_rows, *second in pending:
                o_s[h, prev_rows, :] = _hgrn_bounded_second(*second)
            pending = started
        for h, prev_rows, *second in pending:
            o_s[h, prev_rows, :] = _hgrn_bounded_second(*second)

    def general_body(i, carry):
        c, h = i // HG_HEADS, i % HG_HEADS
        o, st_new = _hgrn_chunk(*load(c, h))
        o_s[h, chunk_rows(c), :] = o
        st_s[h] = st_new
        return carry

    bounded = _max_half_decay(b_s, tile) < HG_MAX_HALF_DECAY

    pl.when(bounded)(bounded_chunks)

    @pl.when(jnp.logical_not(bounded))
    def _():
        lax.fori_loop(0, (tile // HG_CHUNK) * HG_HEADS, general_body, 0)

    gn = gn_ref[...]
    gated = []
    for h in range(HG_HEADS):
        g = u[:, 2 * HG_KEY + HG_VAL + h * HG_DV: 2 * HG_KEY + HG_VAL + (h + 1) * HG_DV]
        gated.append(_rms_scale(o_s[h], gn) * _silu(g))
    y = x + _mm(jnp.concatenate(gated, axis=1), wout_ref[...])
    out_ref[0] = y
    if with_kv:
        _kv_project(y, *kv_in, *kv_out)

    @pl.when(t == nt - 1)
    def _():
        for h in range(HG_HEADS):
            sfin_ref[0, h] = st_s[h].T


def _hgrn_prompt_layer(h, norm_g, w_in, lb_logits, layer, gnorm, w_out, kv=None):
    B, T, D = h.shape
    tile = min(PROMPT_TILE, T)
    n_a = lb_logits.shape[0]
    const = lambda b, t: (0, 0)
    row_tile = lambda width: pl.BlockSpec((1, tile, width), lambda b, t: (b, t, 0))
    head_scratch = pltpu.VMEM((HG_HEADS, tile, HG_DK), F32)
    operands = [h, norm_g.reshape(1, D), w_in, lb_logits, gnorm.reshape(1, HG_DV), w_out]
    in_specs = [row_tile(D), pl.BlockSpec((1, D), const), _layer_block(w_in, layer, True),
                pl.BlockSpec((n_a, HG_KEY), const), pl.BlockSpec((1, HG_DV), const),
                _layer_block(w_out, layer, True)]
    out_specs = [row_tile(D), pl.BlockSpec((1, HG_HEADS, HG_DK, HG_DV), lambda b, t: (b, 0, 0, 0))]
    out_shape = [jax.ShapeDtypeStruct((B, T, D), F32),
                 jax.ShapeDtypeStruct((B, HG_HEADS, HG_DK, HG_DV), F32)]
    if kv is not None:
        kv_norm, w_kv, tables = kv
        operands += [kv_norm.reshape(1, D), w_kv, *tables]
        in_specs += [pl.BlockSpec((1, D), const), pl.BlockSpec(w_kv.shape, const)]
        in_specs += [pl.BlockSpec((tile, LANES), lambda b, t: (t, 0))] * 3
        out_specs += [row_tile(ATT_KVD)] * 2
        out_shape += [jax.ShapeDtypeStruct((B, T, ATT_KVD), F32)] * 2
    return pl.pallas_call(
        functools.partial(_hgrn_prompt_kernel, layer, kv is not None),
        grid=(B, T // tile),
        in_specs=in_specs,
        out_specs=out_specs,
        out_shape=out_shape,
        scratch_shapes=[head_scratch] * 5 + [pltpu.VMEM((HG_HEADS, HG_DV, HG_DK), F32)],
        compiler_params=_cparams("arbitrary", "arbitrary"),
        name=f"hgrn_prompt_l{layer}",
    )(*operands)


def _rope_tables(pos):
    half = ROT_DIM // 2
    pos = np.asarray(pos, np.float64)
    inv_freq = 1.0 / (ROPE_THETA ** (np.arange(half, dtype=np.float64) * 2.0 / ROT_DIM))
    ang = (pos[:, None].astype(np.float32) * inv_freq[None, :].astype(np.float32)).astype(np.float64)
    cos, sin = np.cos(ang), np.sin(ang)
    n = pos.shape[0]
    c = np.ones((n, ATT_HD)); s_up = np.zeros((n, ATT_HD)); s_dn = np.zeros((n, ATT_HD))
    c[:, :half] = cos; c[:, half:ROT_DIM] = cos
    s_dn[:, :half] = -sin
    s_up[:, half:ROT_DIM] = sin
    rep = LANES // ATT_HD
    tab = [np.tile(a, (1, rep)).astype(np.float32) for a in (c, s_up, s_dn)]
    return tuple(jnp.asarray(a) for a in tab)


def _rope_cols(x, cos, s_up, s_dn):
    half = ROT_DIM // 2
    cols = []
    for c in range(x.shape[1] // LANES):
        xc = x[:, c * LANES:(c + 1) * LANES]
        cols.append(xc * cos + pltpu.roll(xc, half, axis=1) * s_up
                    + pltpu.roll(xc, LANES - half, axis=1) * s_dn)
    return cols[0] if len(cols) == 1 else jnp.concatenate(cols, axis=1)


def _kv_kernel(h_ref, *refs):
    _kv_project(h_ref[0], *refs)


def _kv_project(x, ng_ref, w_ref, cos_ref, sup_ref, sdn_ref, k_ref, v_ref):
    u = _mm(_rms_scale(x, ng_ref[...]), w_ref[...])
    k_ref[0] = _rope_cols(u[:, :ATT_KVD], cos_ref[...], sup_ref[...], sdn_ref[...])
    v_ref[0] = u[:, ATT_KVD:]


def _shared_kv(h, kv_norm, w_kv, tables):
    B, T, D = h.shape
    tile = min(PROMPT_TILE, T)
    const = lambda b, t: (0, 0)
    tab = pl.BlockSpec((tile, LANES), lambda b, t: (t, 0))
    kv_spec = pl.BlockSpec((1, tile, ATT_KVD), lambda b, t: (b, t, 0))
    return pl.pallas_call(
        _kv_kernel,
        grid=(B, T // tile),
        in_specs=[pl.BlockSpec((1, tile, D), lambda b, t: (b, t, 0)),
                  pl.BlockSpec((1, D), const), pl.BlockSpec(w_kv.shape, const), tab, tab, tab],
        out_specs=[kv_spec, kv_spec],
        out_shape=[jax.ShapeDtypeStruct((B, T, ATT_KVD), F32)] * 2,
        compiler_params=_cparams("arbitrary", "arbitrary"),
        name="shared_kv",
    )(h, kv_norm.reshape(1, D), w_kv, *tables)


PAIRS_PER_KV = ATT_GROUP // 2


def _swa_attention(blocks, sinks_ref, side_product):
    groups = [(blk, kvh) for blk in range(len(blocks)) for kvh in range(ATT_KVH)]
    n_keys = blocks[0][4].shape[0]
    ones_rows = jnp.ones((SUBLANES, n_keys), F32)

    def score_products(g):
        blk, kvh = groups[g]
        q_cols, k_lo, k_hi, _, bias_t = blocks[blk]
        qs = jnp.concatenate(q_cols[kvh * PAIRS_PER_KV:(kvh + 1) * PAIRS_PER_KV], axis=0)
        keys = jnp.concatenate([k_lo[kvh], k_hi[kvh]], axis=0)
        bias = jnp.concatenate([jnp.concatenate([bias_t] * PAIRS_PER_KV, axis=1)] * 2, axis=0)
        return _mm_nt(keys, qs) + bias

    def softmax_terms(g, s_t):
        kvh = groups[g][1]
        cols = range(kvh * PAIRS_PER_KV, (kvh + 1) * PAIRS_PER_KV)
        terms = []
        for par in range(2):
            s = s_t[par * n_keys:(par + 1) * n_keys]
            sink = jnp.concatenate(
                [jnp.full((1, WINDOW), sinks_ref[0, 2 * c + par], F32) for c in cols], axis=1)
            m = jnp.maximum(jnp.max(s, axis=0, keepdims=True), sink)
            terms.append((jnp.exp(s - m), jnp.exp(sink - m)))
        return terms

    def value_products(g, terms):
        blk, kvh = groups[g]
        v_t = blocks[blk][3]
        lhs = jnp.concatenate([v_t[kvh * ATT_HD:(kvh + 1) * ATT_HD], ones_rows], axis=0)
        halves = []
        for p, sink_term in terms:
            r = _mm(lhs, p)
            halves.append(r[:ATT_HD] * (1.0 / (r[ATT_HD:ATT_HD + 1] + sink_term)))
        return halves

    scores = [score_products(g) for g in range(len(groups))]
    side = side_product()
    terms = [softmax_terms(g, s) for g, s in enumerate(scores)]
    outs = [value_products(g, t) for g, t in enumerate(terms)]
    res = []
    for b in range(len(blocks)):
        cols = []
        for kvh in range(ATT_KVH):
            even, odd = outs[b * ATT_KVH + kvh]
            for i in range(PAIRS_PER_KV):
                queries = slice(i * WINDOW, (i + 1) * WINDOW)
                cols.append(jnp.concatenate([even[:, queries], odd[:, queries]], axis=0).T)
        res.append(jnp.concatenate(cols, axis=1))
    return res, side


def _split_kv_heads(a, fill):
    lane = lax.broadcasted_iota(jnp.int32, a.shape, 1)
    low = lane < ATT_HD
    sw = pltpu.roll(a, ATT_HD, axis=1)
    lo = [jnp.where(low, a, fill), jnp.where(low, sw, fill)]
    hi = [jnp.where(low, fill, sw), jnp.where(low, fill, a)]
    return lo, hi


def _swa_prompt_kernel(final, h_ref, ng_ref, win_ref, sinks_ref, wout_ref, fin_ref,
                       kp_ref, kc_ref, vp_ref, vc_ref, cos_ref, sup_ref, sdn_ref, out_ref):
    t = pl.program_id(1)
    tile = h_ref.shape[1]
    nblk = tile // WINDOW
    x = h_ref[0]
    xn = _rms_scale(x, ng_ref[...]).astype(BF16)
    u = _mm(xn, win_ref[:, :ATT_QD])
    scale = ATT_HD ** -0.5
    cos, sup, sdn = cos_ref[...], sup_ref[...], sdn_ref[...]
    k_all = jnp.concatenate([kp_ref[0], kc_ref[0]], axis=0)
    v_all = jnp.concatenate([vp_ref[0], vc_ref[0]], axis=0)
    j = lax.broadcasted_iota(jnp.int32, (2 * WINDOW, WINDOW), 0)
    tq = lax.broadcasted_iota(jnp.int32, (2 * WINDOW, WINDOW), 1)
    rel = tq + WINDOW - j
    band = (rel >= 0) & (rel <= WINDOW)
    blocks = []
    for blk in range(nblk):
        rows = slice(blk * WINDOW, (blk + 1) * WINDOW)
        keys = slice(blk * WINDOW, (blk + 2) * WINDOW)
        k_lo, k_hi = _split_kv_heads(k_all[keys], 0.0)
        if blk == 0:
            valid = band & (j >= WINDOW * (t == 0).astype(jnp.int32))
        else:
            valid = band
        q_cols = []
        for c in range(ATT_QD // LANES):
            qc = u[rows, c * LANES:(c + 1) * LANES]
            q_cols.append(_rope_cols(qc, cos[rows], sup[rows], sdn[rows]) * scale)
        blocks.append((q_cols, k_lo, k_hi, v_all[keys].T, jnp.where(valid, 0.0, MASK_VALUE)))
    o_blocks, gate = _swa_attention(blocks, sinks_ref, lambda: _mm(xn, win_ref[:, ATT_QD:]))
    o = jnp.concatenate(o_blocks, axis=0) if nblk > 1 else o_blocks[0]
    y = x + _mm(o * _silu(gate), wout_ref[...])
    if final:
        y = _rms_scale(y, fin_ref[...])
    out_ref[0] = y


def _swa_prompt_layer(h, k, v, tables, norm_g, w_in, layer, sinks, w_out, final_norm, final):
    B, T, D = h.shape
    tile = min(SWA_TILE, T)
    per = tile // WINDOW
    const = lambda b, t: (0, 0)
    tab = pl.BlockSpec((tile, LANES), lambda b, t: (t, 0))
    prev = pl.BlockSpec((1, WINDOW, ATT_KVD), lambda b, t: (b, jnp.maximum(t * per - 1, 0), 0))
    cur = pl.BlockSpec((1, tile, ATT_KVD), lambda b, t: (b, t, 0))
    return pl.pallas_call(
        functools.partial(_swa_prompt_kernel, final),
        grid=(B, T // tile),
        in_specs=[
            pl.BlockSpec((1, tile, D), lambda b, t: (b, t, 0)),
            pl.BlockSpec((1, D), const),
            _layer_block(w_in, layer, True),
            pl.BlockSpec(memory_space=pltpu.SMEM),
            _layer_block(w_out, layer, True),
            pl.BlockSpec((1, D), const),
            prev, cur, prev, cur, tab, tab, tab,
        ],
        out_specs=pl.BlockSpec((1, tile, D), lambda b, t: (b, t, 0)),
        out_shape=jax.ShapeDtypeStruct((B, T, D), F32),
        compiler_params=_cparams("arbitrary", "arbitrary"),
        name="swa_prompt_final" if final else "swa_prompt",
    )(h, norm_g.reshape(1, D), w_in, sinks.reshape(1, ATT_QH), w_out, final_norm.reshape(1, D),
      k, k, v, v, *tables)


def _col(row):
    n = row.shape[1]
    return jnp.broadcast_to(row, (n, n)).T


def _hgrn_decode_kernel(x_ref, s_ref, ng_ref, win_ref, lbl_ref, gn_ref, wout_ref,
                        snew_ref, hout_ref, h_s, q_s, f_s, v_s, g_s, o_s, oblk_s):
    l, i = pl.program_id(0), pl.program_id(1)
    n_layers, n_blocks = pl.num_programs(0), pl.num_programs(1)
    bb = s_ref.shape[1]

    @pl.when(i == 0)
    def _():
        @pl.when(l == 0)
        def _():
            h_s[...] = x_ref[...]

        u = _mm(_rms_scale(h_s[...], ng_ref[0]), win_ref[0])
        lbl = lbl_ref[...]
        lb = jnp.zeros_like(lbl[:1])
        for layer in range(lbl.shape[0]):
            lb = jnp.where(l == layer, _lower_bound(lbl, layer), lb)
        q_s[...] = _silu(u[:, :HG_KEY])
        f_s[...] = _forget_gates(u[:, HG_KEY:2 * HG_KEY], lb)[0]
        v_s[...] = u[:, 2 * HG_KEY:2 * HG_KEY + HG_VAL]
        g_s[...] = _silu(u[:, 2 * HG_KEY + HG_VAL:])

    rows = pl.ds(pl.multiple_of(i * bb, bb), bb)
    qb, fb, vb = q_s[rows, :], f_s[rows, :], v_s[rows, :]
    for j in range(bb):
        for h in range(HG_HEADS):
            sl = slice(h * HG_DK, (h + 1) * HG_DK)
            v_r = vb[j:j + 1, h * HG_DV:(h + 1) * HG_DV]
            s_new = _col(fb[j:j + 1, sl]) * (s_ref[0, j, h] - v_r) + v_r
            snew_ref[0, j, h] = s_new
            oblk_s[j:j + 1, h * HG_DV:(h + 1) * HG_DV] = _mm(qb[j:j + 1, sl], s_new)
    o_s[rows, :] = oblk_s[...]

    @pl.when(i == n_blocks - 1)
    def _():
        gn = gn_ref[0]
        gated = []
        for h in range(HG_HEADS):
            sl = slice(h * HG_DV, (h + 1) * HG_DV)
            gated.append(_rms_scale(o_s[:, sl], gn) * g_s[:, sl])
        h_new = h_s[...] + _mm(jnp.concatenate(gated, axis=1), wout_ref[0])
        h_s[...] = h_new

        @pl.when(l == n_layers - 1)
        def _():
            hout_ref[...] = h_new


def _whole(shape):
    return pl.BlockSpec(shape, lambda *_: (0,) * len(shape))


def _hgrn_decode(h, states, norms, w_in, lb_logits, gnorms, w_out):
    B, D = h.shape
    n_a = states.shape[0]
    bb = DEC_BLOCK
    per_layer = lambda *tail: pl.BlockSpec((1,) + tail, lambda l, i: (l,) + (0,) * len(tail))
    st_blk = pl.BlockSpec((1, bb, HG_HEADS, HG_DK, HG_DV), lambda l, i: (l, i, 0, 0, 0))
    wide = pltpu.VMEM((B, HG_KEY), F32)
    s_new, h_new = pl.pallas_call(
        _hgrn_decode_kernel,
        grid=(n_a, B // bb),
        in_specs=[_whole((B, D)), st_blk, per_layer(1, D), per_layer(*w_in.shape[1:]),
                  _whole(lb_logits.shape), per_layer(1, HG_DV), per_layer(*w_out.shape[1:])],
        out_specs=[st_blk, _whole((B, D))],
        out_shape=[jax.ShapeDtypeStruct(states.shape, F32), jax.ShapeDtypeStruct((B, D), F32)],
        scratch_shapes=[pltpu.VMEM((B, D), F32)] + [wide] * 5 + [pltpu.VMEM((bb, HG_VAL), F32)],
        compiler_params=_cparams("arbitrary", "arbitrary"),
        name="hgrn_decode",
    )(h, states, norms.reshape(n_a, 1, D), w_in, lb_logits, gnorms.reshape(n_a, 1, HG_DV), w_out)
    return h_new, s_new


def _bf16_round(x):
    return x.astype(BF16).astype(F32)


def _swa_decode_kernel(final, h_ref, ng_ref, win_ref, cos_ref, sup_ref, sdn_ref, kc_ref, vc_ref,
                       kn_ref, vn_ref, sinks_ref, wout_ref, fin_ref, out_ref, qp_s, g_s, os_s):
    i = pl.program_id(0)
    n_blocks = pl.num_programs(0)
    bb, W = kc_ref.shape[0], kc_ref.shape[1]
    B = h_ref.shape[0]
    lane = lax.broadcasted_iota(jnp.int32, (B, LANES), 1)

    @pl.when(i == 0)
    def _():
        u = _mm(_rms_scale(h_ref[...], ng_ref[...]), win_ref[...])
        q = _rope_cols(u[:, :ATT_QD], cos_ref[...], sup_ref[...], sdn_ref[...]) * (ATT_HD ** -0.5)
        g_s[...] = _silu(u[:, ATT_QD:])
        for head in range(ATT_QH):
            col = q[:, (head // 2) * LANES:(head // 2 + 1) * LANES]
            kvh = head // ATT_GROUP
            if head % 2 != kvh:
                col = pltpu.roll(col, ATT_HD, axis=1)
            qp_s[:, head, :] = jnp.where((lane >= ATT_HD) if kvh == 1 else (lane < ATT_HD), col, 0.0)

    rows = pl.ds(pl.multiple_of(i * bb, bb), bb)
    kn, vn = kn_ref[rows, :], vn_ref[rows, :]
    qp = [qp_s[i * bb + j] for j in range(bb)]
    s_c = jnp.concatenate([_mm_nt(qp[j], kc_ref[j]) for j in range(bb)], axis=0)
    qp_all = jnp.concatenate(qp, axis=0)
    expand = lambda a: jnp.concatenate(
        [jnp.broadcast_to(a[j:j + 1], (ATT_QH, ATT_KVD)) for j in range(bb)], axis=0)
    s_n = jnp.sum(_bf16_round(qp_all) * _bf16_round(expand(kn)), axis=-1, keepdims=True)
    sink = jnp.concatenate([sinks_ref[...]] * bb, axis=0)
    m = jnp.maximum(jnp.maximum(jnp.max(s_c, axis=-1, keepdims=True), s_n), sink)
    p_c = jnp.exp(s_c - m)
    p_n = jnp.exp(s_n - m)
    inv = 1.0 / (jnp.sum(p_c, axis=-1, keepdims=True) + p_n + jnp.exp(sink - m))
    p_c = p_c * inv
    o = jnp.concatenate([_mm(p_c[j * ATT_QH:(j + 1) * ATT_QH], vc_ref[j]) for j in range(bb)], axis=0)
    o = o + _bf16_round(p_n * inv) * _bf16_round(expand(vn))
    for j in range(bb):
        os_s[i * bb + j] = o[j * ATT_QH:(j + 1) * ATT_QH]

    @pl.when(i == n_blocks - 1)
    def _():
        cols = []
        for c in range(ATT_QD // LANES):
            halves = []
            for par in range(2):
                head = 2 * c + par
                t = os_s[:, head, :]
                if par != head // ATT_GROUP:
                    t = pltpu.roll(t, ATT_HD, axis=1)
                halves.append(t)
            cols.append(jnp.where(lane < ATT_HD, halves[0], halves[1]))
        y = h_ref[...] + _mm(jnp.concatenate(cols, axis=1) * g_s[...], wout_ref[...])
        if final:
            y = _rms_scale(y, fin_ref[...])
        out_ref[...] = y


def _swa_decode_layer(h, k_new, v_new, cache_k, cache_v, tables, norm_g, w_in, layer, sinks, w_out,
                      final_norm, final):
    B, D = h.shape
    W = cache_k.shape[1]
    bb = min(SWA_DEC_BLOCK, B)
    cache_blk = pl.BlockSpec((bb, W, ATT_KVD), lambda i: (i, 0, 0))
    head_rows = pltpu.VMEM((B, ATT_QH, ATT_KVD), F32)
    return pl.pallas_call(
        functools.partial(_swa_decode_kernel, final),
        grid=(B // bb,),
        in_specs=[_whole((B, D)), _whole((1, D)), _layer_block(w_in, layer)] + [_whole((1, LANES))] * 3
                 + [cache_blk, cache_blk, _whole((B, ATT_KVD)), _whole((B, ATT_KVD)),
                    _whole((ATT_QH, 1)), _layer_block(w_out, layer), _whole((1, D))],
        out_specs=_whole((B, D)),
        out_shape=jax.ShapeDtypeStruct((B, D), F32),
        scratch_shapes=[head_rows, pltpu.VMEM((B, ATT_QD), F32), head_rows],
        compiler_params=_cparams("arbitrary"),
        name="swa_decode_final" if final else "swa_decode",
    )(h, norm_g.reshape(1, D), w_in, *tables, cache_k.reshape(B, W, ATT_KVD),
      cache_v.reshape(B, W, ATT_KVD), k_new, v_new, sinks.reshape(ATT_QH, 1), w_out,
      final_norm.reshape(1, D))


def kernel(x_prompt, x_sample, state_hgrn, cache_k_win, cache_v_win, a_norm, a_w_in, a_lb_logits,
           a_gnorm, a_w_out, kv_norm, w_kv, b_norm, b_w_in, b_sinks, b_w_out, final_norm):
    B, T, D = x_prompt.shape
    BD, TD, _ = x_sample.shape
    n_a = a_w_in.shape[0]
    n_b = b_w_in.shape[0]
    a_w_in_b, a_w_out_b = a_w_in.astype(BF16), a_w_out.astype(BF16)
    b_w_in_b, b_w_out_b = b_w_in.astype(BF16), b_w_out.astype(BF16)
    w_kv_b = w_kv.astype(BF16)

    tab_p = _rope_tables(np.arange(T))
    h = x_prompt
    st_p = []
    for l in range(n_a):
        kv = (kv_norm, w_kv_b, tab_p) if l == n_a - 1 else None
        h, s, *kv_p = _hgrn_prompt_layer(h, a_norm[l], a_w_in_b, a_lb_logits, l, a_gnorm[l],
                                         a_w_out_b, kv)
        st_p.append(s)
    k_p, v_p = kv_p
    for l in range(n_b):
        h = _swa_prompt_layer(h, k_p, v_p, tab_p, b_norm[l], b_w_in_b, l, b_sinks[l], b_w_out_b,
                              final_norm, l == n_b - 1)
    y_prompt = h
    w_keep = min(WINDOW, T)
    k_win = k_p[:, T - w_keep:].reshape(B, w_keep, ATT_KVH, ATT_HD)
    v_win = v_p[:, T - w_keep:].reshape(B, w_keep, ATT_KVH, ATT_HD)

    tab_s = _rope_tables(PAST_LEN + np.arange(TD))
    hs = x_sample.reshape(BD * TD, D)
    hs, st_s = _hgrn_decode(hs, state_hgrn, a_norm, a_w_in_b, a_lb_logits, a_gnorm, a_w_out_b)
    k_s, v_s = _shared_kv(hs.reshape(1, BD, D), kv_norm, w_kv_b,
                          tuple(jnp.broadcast_to(t, (BD, LANES)) for t in tab_s))
    k_s, v_s = k_s.reshape(BD, ATT_KVD), v_s.reshape(BD, ATT_KVD)
    for l in range(n_b):
        hs = _swa_decode_layer(hs, k_s, v_s, cache_k_win, cache_v_win, tab_s, b_norm[l], b_w_in_b, l,
                               b_sinks[l], b_w_out_b, final_norm, l == n_b - 1)
    y_sample = hs.reshape(BD, TD, D)

    return (y_prompt, y_sample, jnp.stack(st_p), st_s, k_win, v_win,
            k_s.reshape(BD, TD, ATT_KVH, ATT_HD), v_s.reshape(BD, TD, ATT_KVH, ATT_HD))
```

```python
import functools

import numpy as np
import jax
import jax.numpy as jnp
from jax import lax
from jax.experimental import pallas as pl
from jax.experimental.pallas import tpu as pltpu

F32 = jnp.float32
BF16 = jnp.bfloat16

D_MODEL = 1024
HG_HEADS = 8
HG_DK = 128
HG_DV = 128
HG_KEY = HG_HEADS * HG_DK
HG_VAL = HG_HEADS * HG_DV
HG_CHUNK = 64
HG_HALF = HG_CHUNK // 2
HG_SUB = 16
HG_MAX_HALF_DECAY = 60.0
ATT_HD = 64
ATT_QH = 16
ATT_KVH = 2
ATT_GROUP = ATT_QH // ATT_KVH
ATT_QD = ATT_QH * ATT_HD
ATT_KVD = ATT_KVH * ATT_HD
WINDOW = 128
ROPE_THETA = 500000.0
ROT_DIM = ATT_HD // 4
NORM_EPS = 1e-6
MASK_VALUE = -1e30
NEG_BIG = -1e30
PAST_LEN = 8192

SUBLANES = 8
LANES = 128
PROMPT_TILE = 512
SWA_TILE = 1024
DEC_BLOCK = 8
SWA_DEC_BLOCK = 16
V7X_VMEM_BYTES = 64 * 1024 * 1024
VMEM_LIMIT = V7X_VMEM_BYTES * 3 // 4


def _cparams(*sem):
    return pltpu.CompilerParams(dimension_semantics=sem, vmem_limit_bytes=VMEM_LIMIT)


def _layer_block(stacked, layer, single_buffer=False):
    tail = stacked.shape[1:]
    mode = dict(pipeline_mode=pl.Buffered(1)) if single_buffer else {}
    return pl.BlockSpec((None,) + tail, lambda *_: (layer,) + (0,) * len(tail), **mode)


def _mm(a, b):
    return jnp.dot(a.astype(BF16), b.astype(BF16), preferred_element_type=F32)


def _mm_nt(a, b):
    return lax.dot_general(a.astype(BF16), b.astype(BF16), (((1,), (1,)), ((), ())),
                           preferred_element_type=F32)


def _mm_tn(a, b):
    return lax.dot_general(a.astype(BF16), b.astype(BF16), (((0,), (0,)), ((), ())),
                           preferred_element_type=F32)


def _rms_scale(x, g):
    ms = jnp.mean(x * x, axis=-1, keepdims=True)
    return x * lax.rsqrt(ms + NORM_EPS) * g


def _silu(x):
    return x * (1.0 / (1.0 + jnp.exp(-x)))


def _lower_bound(lbl, layer):
    n = lbl.shape[0]
    rows = [lbl[i:i + 1, :] for i in range(n)]
    m = functools.reduce(jnp.maximum, rows)
    es = [jnp.exp(r - m) for r in rows]
    inv = 1.0 / functools.reduce(jnp.add, es)
    ps = [e * inv for e in es]
    return functools.reduce(jnp.add, ps[:layer + 1]) - ps[0]


def _forget_gates(f, lb):
    e = jnp.exp(-f)
    r = 1.0 / (1.0 + e)
    return lb + (1.0 - lb) * r, (1.0 - lb) * (e * r)


def _chunk_cumsum(x):
    pos = lax.broadcasted_iota(jnp.int32, x.shape, 0) % HG_CHUNK
    s = 1
    while s < HG_CHUNK:
        x = x + jnp.where(pos >= s, pltpu.roll(x, s, axis=0), 0.0)
        s *= 2
    return x


def _diag_blocks(q, k, v, b):
    sub_iota = lax.broadcasted_iota(jnp.int32, (SUBLANES, LANES), 0)
    outs = []
    for blk in range(HG_CHUNK // HG_SUB):
        base = blk * HG_SUB
        groups = HG_SUB // SUBLANES
        qs = [q[base + g * SUBLANES: base + (g + 1) * SUBLANES] for g in range(groups)]
        bs = [b[base + g * SUBLANES: base + (g + 1) * SUBLANES] for g in range(groups)]
        accs = [jnp.zeros((SUBLANES, HG_DV), F32) for _ in range(groups)]
        for s in range(HG_SUB):
            row = base + s
            k_s = k[row:row + 1]
            b_s = b[row:row + 1]
            v_s = v[row:row + 1]
            for g in range(s // SUBLANES, groups):
                d = bs[g] - b_s
                if g == s // SUBLANES:
                    d = jnp.where(sub_iota >= (s % SUBLANES), d, NEG_BIG)
                p = qs[g] * (k_s * jnp.exp(d))
                a = jnp.sum(p, axis=-1, keepdims=True)
                accs[g] = accs[g] + a * v_s
        outs.extend(accs)
    return jnp.concatenate(outs, axis=0)


def _hgrn_chunk(q, k, v, b, st):
    C, H = HG_CHUNK, HG_HALF
    o_inter = _mm_nt(q * jnp.exp(b), st)
    r0 = b[H - 1:H]
    a0 = _mm_nt(q[H:] * jnp.exp(b[H:] - r0), k[:H] * jnp.exp(r0 - b[:H]))
    o_hi = _mm(a0, v[:H])
    Q = HG_SUB
    o_q = []
    for base in (0, H):
        r1 = b[base + Q - 1: base + Q]
        a1 = _mm_nt(q[base + Q: base + 2 * Q] * jnp.exp(b[base + Q: base + 2 * Q] - r1),
                    k[base: base + Q] * jnp.exp(r1 - b[base: base + Q]))
        o_q.append(_mm(a1, v[base: base + Q]))
    zero = jnp.zeros((Q, HG_DV), F32)
    o_off = jnp.concatenate([zero, o_q[0], o_hi[:Q], o_hi[Q:] + o_q[1]], axis=0)
    o = o_inter + o_off + _diag_blocks(q, k, v, b)
    b_end = b[C - 1:C]
    st_new = st * jnp.exp(b_end) + _mm_tn(v, k * jnp.exp(b_end - b))
    return o, st_new


def _hgrn_bounded_first(q, k, v, b, st):
    C, H = HG_CHUNK, HG_HALF
    r = b[H - 1:H]
    eq = jnp.exp(b - r)
    qt, kt = q * eq, k * jnp.exp(r - b)
    e_mid, e_hi = jnp.exp(r), eq[C - 1:C]
    scores = _mm_nt(qt, kt)
    v_t = v.T
    st_new = st * (e_mid * e_hi) + _mm(v_t, kt * e_hi)
    return scores, qt * e_mid, v_t, st_new


def _hgrn_bounded_second(scores, q_dec, v_t, st):
    t_i = lax.broadcasted_iota(jnp.int32, scores.shape, 0)
    s_i = lax.broadcasted_iota(jnp.int32, scores.shape, 1)
    a = jnp.where(s_i <= t_i, scores, 0.0)
    return _mm_nt(jnp.concatenate([q_dec, a], axis=1), jnp.concatenate([st, v_t], axis=1))


def _max_half_decay(b_s, tile):
    n = tile // HG_CHUNK
    worst = None
    for h in range(HG_HEADS):
        mid = b_s[h, pl.ds(HG_HALF - 1, n, stride=HG_CHUNK), :]
        end = b_s[h, pl.ds(HG_CHUNK - 1, n, stride=HG_CHUNK), :]
        w = jnp.maximum(-mid, mid - end)
        worst = w if worst is None else jnp.maximum(worst, w)
    return jnp.max(worst)


def _hgrn_prompt_kernel(layer, with_kv, *refs):
    h_ref, ng_ref, win_ref, lbl_ref, gn_ref, wout_ref = refs[:6]
    refs = refs[6:]
    if with_kv:
        kv_in, refs = refs[:5], refs[5:]
    out_ref, sfin_ref = refs[:2]
    refs = refs[2:]
    if with_kv:
        kv_out, refs = refs[:2], refs[2:]
    q_s, k_s, v_s, b_s, o_s, st_s = refs
    t = pl.program_id(1)
    nt = pl.num_programs(1)
    tile = h_ref.shape[1]

    @pl.when(t == 0)
    def _():
        st_s[...] = jnp.zeros_like(st_s)

    x = h_ref[0]
    u = _mm(_rms_scale(x, ng_ref[...]), win_ref[...])
    lb = _lower_bound(lbl_ref[...], layer)
    for h in range(HG_HEADS):
        sl = slice(h * HG_DK, (h + 1) * HG_DK)
        q_s[h] = _silu(u[:, sl])
        fg, kin = _forget_gates(u[:, HG_KEY + h * HG_DK: HG_KEY + (h + 1) * HG_DK], lb[:, sl])
        k_s[h] = kin
        b_s[h] = _chunk_cumsum(jnp.log(fg))
        v_s[h] = u[:, 2 * HG_KEY + h * HG_DV: 2 * HG_KEY + (h + 1) * HG_DV]

    def chunk_rows(c):
        return pl.ds(pl.multiple_of(c * HG_CHUNK, HG_CHUNK), HG_CHUNK)

    def load(c, h):
        rows = chunk_rows(c)
        return q_s[h, rows, :], k_s[h, rows, :], v_s[h, rows, :], b_s[h, rows, :], st_s[h]

    def bounded_chunks():
        pending = []
        for c in range(tile // HG_CHUNK):
            rows = slice(c * HG_CHUNK, (c + 1) * HG_CHUNK)
            started = []
            for h in range(HG_HEADS):
                st = st_s[h]
                scores, q_dec, v_t, st_new = _hgrn_bounded_first(
                    q_s[h, rows, :], k_s[h, rows, :], v_s[h, rows, :], b_s[h, rows, :], st)
                st_s[h] = st_new
                started.append((h, rows, scores, q_dec, v_t, st))
            for h, prev_rows, *second in pending:
                o_s[h, prev_rows, :] = _hgrn_bounded_second(*second)
            pending = started
        for h, prev_rows, *second in pending:
            o_s[h, prev_rows, :] = _hgrn_bounded_second(*second)

    def general_body(i, carry):
        c, h = i // HG_HEADS, i % HG_HEADS
        o, st_new = _hgrn_chunk(*load(c, h))
        o_s[h, chunk_rows(c), :] = o
        st_s[h] = st_new
        return carry

    bounded = _max_half_decay(b_s, tile) < HG_MAX_HALF_DECAY

    pl.when(bounded)(bounded_chunks)

    @pl.when(jnp.logical_not(bounded))
    def _():
        lax.fori_loop(0, (tile // HG_CHUNK) * HG_HEADS, general_body, 0)

    gn = gn_ref[...]
    gated = []
    for h in range(HG_HEADS):
        g = u[:, 2 * HG_KEY + HG_VAL + h * HG_DV: 2 * HG_KEY + HG_VAL + (h + 1) * HG_DV]
        gated.append(_rms_scale(o_s[h], gn) * _silu(g))
    y = x + _mm(jnp.concatenate(gated, axis=1), wout_ref[...])
    out_ref[0] = y
    if with_kv:
        _kv_project(y, *kv_in, *kv_out)

    @pl.when(t == nt - 1)
    def _():
        for h in range(HG_HEADS):
            sfin_ref[0, h] = st_s[h].T


def _hgrn_prompt_layer(h, norm_g, w_in, lb_logits, layer, gnorm, w_out, kv=None):
    B, T, D = h.shape
    tile = min(PROMPT_TILE, T)
    assert T % tile == 0 and tile % HG_CHUNK == 0 and HG_DK == HG_DV == LANES
    n_a = lb_logits.shape[0]
    const = lambda b, t: (0, 0)
    row_tile = lambda width: pl.BlockSpec((1, tile, width), lambda b, t: (b, t, 0))
    head_scratch = pltpu.VMEM((HG_HEADS, tile, HG_DK), F32)
    operands = [h, norm_g.reshape(1, D), w_in, lb_logits, gnorm.reshape(1, HG_DV), w_out]
    in_specs = [row_tile(D), pl.BlockSpec((1, D), const), _layer_block(w_in, layer, True),
                pl.BlockSpec((n_a, HG_KEY), const), pl.BlockSpec((1, HG_DV), const),
                _layer_block(w_out, layer, True)]
    out_specs = [row_tile(D), pl.BlockSpec((1, HG_HEADS, HG_DK, HG_DV), lambda b, t: (b, 0, 0, 0))]
    out_shape = [jax.ShapeDtypeStruct((B, T, D), F32),
                 jax.ShapeDtypeStruct((B, HG_HEADS, HG_DK, HG_DV), F32)]
    if kv is not None:
        kv_norm, w_kv, tables = kv
        operands += [kv_norm.reshape(1, D), w_kv, *tables]
        in_specs += [pl.BlockSpec((1, D), const), pl.BlockSpec(w_kv.shape, const)]
        in_specs += [pl.BlockSpec((tile, LANES), lambda b, t: (t, 0))] * 3
        out_specs += [row_tile(ATT_KVD)] * 2
        out_shape += [jax.ShapeDtypeStruct((B, T, ATT_KVD), F32)] * 2
    return pl.pallas_call(
        functools.partial(_hgrn_prompt_kernel, layer, kv is not None),
        grid=(B, T // tile),
        in_specs=in_specs,
        out_specs=out_specs,
        out_shape=out_shape,
        scratch_shapes=[head_scratch] * 5 + [pltpu.VMEM((HG_HEADS, HG_DV, HG_DK), F32)],
        compiler_params=_cparams("arbitrary", "arbitrary"),
        name=f"hgrn_prompt_l{layer}",
    )(*operands)


def _rope_tables(pos):
    half = ROT_DIM // 2
    pos = np.asarray(pos, np.float64)
    inv_freq = 1.0 / (ROPE_THETA ** (np.arange(half, dtype=np.float64) * 2.0 / ROT_DIM))
    ang = (pos[:, None].astype(np.float32) * inv_freq[None, :].astype(np.float32)).astype(np.float64)
    cos, sin = np.cos(ang), np.sin(ang)
    n = pos.shape[0]
    c = np.ones((n, ATT_HD)); s_up = np.zeros((n, ATT_HD)); s_dn = np.zeros((n, ATT_HD))
    c[:, :half] = cos; c[:, half:ROT_DIM] = cos
    s_dn[:, :half] = -sin
    s_up[:, half:ROT_DIM] = sin
    rep = LANES // ATT_HD
    tab = [np.tile(a, (1, rep)).astype(np.float32) for a in (c, s_up, s_dn)]
    return tuple(jnp.asarray(a) for a in tab)


def _rope_cols(x, cos, s_up, s_dn):
    half = ROT_DIM // 2
    cols = []
    for c in range(x.shape[1] // LANES):
        xc = x[:, c * LANES:(c + 1) * LANES]
        cols.append(xc * cos + pltpu.roll(xc, half, axis=1) * s_up
                    + pltpu.roll(xc, LANES - half, axis=1) * s_dn)
    return cols[0] if len(cols) == 1 else jnp.concatenate(cols, axis=1)


def _kv_kernel(h_ref, *refs):
    _kv_project(h_ref[0], *refs)


def _kv_project(x, ng_ref, w_ref, cos_ref, sup_ref, sdn_ref, k_ref, v_ref):
    u = _mm(_rms_scale(x, ng_ref[...]), w_ref[...])
    k_ref[0] = _rope_cols(u[:, :ATT_KVD], cos_ref[...], sup_ref[...], sdn_ref[...])
    v_ref[0] = u[:, ATT_KVD:]


def _shared_kv(h, kv_norm, w_kv, tables):
    B, T, D = h.shape
    tile = min(PROMPT_TILE, T)
    const = lambda b, t: (0, 0)
    tab = pl.BlockSpec((tile, LANES), lambda b, t: (t, 0))
    kv_spec = pl.BlockSpec((1, tile, ATT_KVD), lambda b, t: (b, t, 0))
    return pl.pallas_call(
        _kv_kernel,
        grid=(B, T // tile),
        in_specs=[pl.BlockSpec((1, tile, D), lambda b, t: (b, t, 0)),
                  pl.BlockSpec((1, D), const), pl.BlockSpec(w_kv.shape, const), tab, tab, tab],
        out_specs=[kv_spec, kv_spec],
        out_shape=[jax.ShapeDtypeStruct((B, T, ATT_KVD), F32)] * 2,
        compiler_params=_cparams("arbitrary", "arbitrary"),
        name="shared_kv",
    )(h, kv_norm.reshape(1, D), w_kv, *tables)


PAIRS_PER_KV = ATT_GROUP // 2


def _swa_attention(blocks, sinks_ref, side_product):
    groups = [(blk, kvh) for blk in range(len(blocks)) for kvh in range(ATT_KVH)]
    n_keys = blocks[0][4].shape[0]
    ones_rows = jnp.ones((SUBLANES, n_keys), F32)

    def score_products(g):
        blk, kvh = groups[g]
        q_cols, k_lo, k_hi, _, bias_t = blocks[blk]
        qs = jnp.concatenate(q_cols[kvh * PAIRS_PER_KV:(kvh + 1) * PAIRS_PER_KV], axis=0)
        keys = jnp.concatenate([k_lo[kvh], k_hi[kvh]], axis=0)
        bias = jnp.concatenate([jnp.concatenate([bias_t] * PAIRS_PER_KV, axis=1)] * 2, axis=0)
        return _mm_nt(keys, qs) + bias

    def softmax_terms(g, s_t):
        kvh = groups[g][1]
        cols = range(kvh * PAIRS_PER_KV, (kvh + 1) * PAIRS_PER_KV)
        terms = []
        for par in range(2):
            s = s_t[par * n_keys:(par + 1) * n_keys]
            sink = jnp.concatenate(
                [jnp.full((1, WINDOW), sinks_ref[0, 2 * c + par], F32) for c in cols], axis=1)
            m = jnp.maximum(jnp.max(s, axis=0, keepdims=True), sink)
            terms.append((jnp.exp(s - m), jnp.exp(sink - m)))
        return terms

    def value_products(g, terms):
        blk, kvh = groups[g]
        v_t = blocks[blk][3]
        lhs = jnp.concatenate([v_t[kvh * ATT_HD:(kvh + 1) * ATT_HD], ones_rows], axis=0)
        halves = []
        for p, sink_term in terms:
            r = _mm(lhs, p)
            halves.append(r[:ATT_HD] * (1.0 / (r[ATT_HD:ATT_HD + 1] + sink_term)))
        return halves

    scores = [score_products(g) for g in range(len(groups))]
    side = side_product()
    terms = [softmax_terms(g, s) for g, s in enumerate(scores)]
    outs = [value_products(g, t) for g, t in enumerate(terms)]
    res = []
    for b in range(len(blocks)):
        cols = []
        for kvh in range(ATT_KVH):
            even, odd = outs[b * ATT_KVH + kvh]
            for i in range(PAIRS_PER_KV):
                queries = slice(i * WINDOW, (i + 1) * WINDOW)
                cols.append(jnp.concatenate([even[:, queries], odd[:, queries]], axis=0).T)
        res.append(jnp.concatenate(cols, axis=1))
    return res, side


def _split_kv_heads(a, fill):
    lane = lax.broadcasted_iota(jnp.int32, a.shape, 1)
    low = lane < ATT_HD
    sw = pltpu.roll(a, ATT_HD, axis=1)
    lo = [jnp.where(low, a, fill), jnp.where(low, sw, fill)]
    hi = [jnp.where(low, fill, sw), jnp.where(low, fill, a)]
    return lo, hi


def _swa_prompt_kernel(final, h_ref, ng_ref, win_ref, sinks_ref, wout_ref, fin_ref,
                       kp_ref, kc_ref, vp_ref, vc_ref, cos_ref, sup_ref, sdn_ref, out_ref):
    t = pl.program_id(1)
    tile = h_ref.shape[1]
    nblk = tile // WINDOW
    x = h_ref[0]
    xn = _rms_scale(x, ng_ref[...]).astype(BF16)
    u = _mm(xn, win_ref[:, :ATT_QD])
    scale = ATT_HD ** -0.5
    cos, sup, sdn = cos_ref[...], sup_ref[...], sdn_ref[...]
    k_all = jnp.concatenate([kp_ref[0], kc_ref[0]], axis=0)
    v_all = jnp.concatenate([vp_ref[0], vc_ref[0]], axis=0)
    j = lax.broadcasted_iota(jnp.int32, (2 * WINDOW, WINDOW), 0)
    tq = lax.broadcasted_iota(jnp.int32, (2 * WINDOW, WINDOW), 1)
    rel = tq + WINDOW - j
    band = (rel >= 0) & (rel <= WINDOW)
    blocks = []
    for blk in range(nblk):
        rows = slice(blk * WINDOW, (blk + 1) * WINDOW)
        keys = slice(blk * WINDOW, (blk + 2) * WINDOW)
        k_lo, k_hi = _split_kv_heads(k_all[keys], 0.0)
        if blk == 0:
            valid = band & (j >= WINDOW * (t == 0).astype(jnp.int32))
        else:
            valid = band
        q_cols = []
        for c in range(ATT_QD // LANES):
            qc = u[rows, c * LANES:(c + 1) * LANES]
            q_cols.append(_rope_cols(qc, cos[rows], sup[rows], sdn[rows]) * scale)
        blocks.append((q_cols, k_lo, k_hi, v_all[keys].T, jnp.where(valid, 0.0, MASK_VALUE)))
    o_blocks, gate = _swa_attention(blocks, sinks_ref, lambda: _mm(xn, win_ref[:, ATT_QD:]))
    o = jnp.concatenate(o_blocks, axis=0) if nblk > 1 else o_blocks[0]
    y = x + _mm(o * _silu(gate), wout_ref[...])
    if final:
        y = _rms_scale(y, fin_ref[...])
    out_ref[0] = y


def _swa_prompt_layer(h, k, v, tables, norm_g, w_in, layer, sinks, w_out, final_norm, final):
    B, T, D = h.shape
    tile = min(SWA_TILE, T)
    assert T % tile == 0 and tile % WINDOW == 0
    per = tile // WINDOW
    const = lambda b, t: (0, 0)
    tab = pl.BlockSpec((tile, LANES), lambda b, t: (t, 0))
    prev = pl.BlockSpec((1, WINDOW, ATT_KVD), lambda b, t: (b, jnp.maximum(t * per - 1, 0), 0))
    cur = pl.BlockSpec((1, tile, ATT_KVD), lambda b, t: (b, t, 0))
    return pl.pallas_call(
        functools.partial(_swa_prompt_kernel, final),
        grid=(B, T // tile),
        in_specs=[
            pl.BlockSpec((1, tile, D), lambda b, t: (b, t, 0)),
            pl.BlockSpec((1, D), const),
            _layer_block(w_in, layer, True),
            pl.BlockSpec(memory_space=pltpu.SMEM),
            _layer_block(w_out, layer, True),
            pl.BlockSpec((1, D), const),
            prev, cur, prev, cur, tab, tab, tab,
        ],
        out_specs=pl.BlockSpec((1, tile, D), lambda b, t: (b, t, 0)),
        out_shape=jax.ShapeDtypeStruct((B, T, D), F32),
        compiler_params=_cparams("arbitrary", "arbitrary"),
        name="swa_prompt_final" if final else "swa_prompt",
    )(h, norm_g.reshape(1, D), w_in, sinks.reshape(1, ATT_QH), w_out, final_norm.reshape(1, D),
      k, k, v, v, *tables)


def _col(row):
    n = row.shape[1]
    return jnp.broadcast_to(row, (n, n)).T


def _hgrn_decode_kernel(x_ref, s_ref, ng_ref, win_ref, lbl_ref, gn_ref, wout_ref,
                        snew_ref, hout_ref, h_s, q_s, f_s, v_s, g_s, o_s, oblk_s):
    l, i = pl.program_id(0), pl.program_id(1)
    n_layers, n_blocks = pl.num_programs(0), pl.num_programs(1)
    bb = s_ref.shape[1]

    @pl.when(i == 0)
    def _():
        @pl.when(l == 0)
        def _():
            h_s[...] = x_ref[...]

        u = _mm(_rms_scale(h_s[...], ng_ref[0]), win_ref[0])
        lbl = lbl_ref[...]
        lb = jnp.zeros_like(lbl[:1])
        for layer in range(lbl.shape[0]):
            lb = jnp.where(l == layer, _lower_bound(lbl, layer), lb)
        q_s[...] = _silu(u[:, :HG_KEY])
        f_s[...] = _forget_gates(u[:, HG_KEY:2 * HG_KEY], lb)[0]
        v_s[...] = u[:, 2 * HG_KEY:2 * HG_KEY + HG_VAL]
        g_s[...] = _silu(u[:, 2 * HG_KEY + HG_VAL:])

    rows = pl.ds(pl.multiple_of(i * bb, bb), bb)
    qb, fb, vb = q_s[rows, :], f_s[rows, :], v_s[rows, :]
    for j in range(bb):
        for h in range(HG_HEADS):
            sl = slice(h * HG_DK, (h + 1) * HG_DK)
            v_r = vb[j:j + 1, h * HG_DV:(h + 1) * HG_DV]
            s_new = _col(fb[j:j + 1, sl]) * (s_ref[0, j, h] - v_r) + v_r
            snew_ref[0, j, h] = s_new
            oblk_s[j:j + 1, h * HG_DV:(h + 1) * HG_DV] = _mm(qb[j:j + 1, sl], s_new)
    o_s[rows, :] = oblk_s[...]

    @pl.when(i == n_blocks - 1)
    def _():
        gn = gn_ref[0]
        gated = []
        for h in range(HG_HEADS):
            sl = slice(h * HG_DV, (h + 1) * HG_DV)
            gated.append(_rms_scale(o_s[:, sl], gn) * g_s[:, sl])
        h_new = h_s[...] + _mm(jnp.concatenate(gated, axis=1), wout_ref[0])
        h_s[...] = h_new

        @pl.when(l == n_layers - 1)
        def _():
            hout_ref[...] = h_new


def _whole(shape):
    return pl.BlockSpec(shape, lambda *_: (0,) * len(shape))


def _hgrn_decode(h, states, norms, w_in, lb_logits, gnorms, w_out):
    B, D = h.shape
    n_a = states.shape[0]
    bb = min(DEC_BLOCK, B)
    assert B % bb == 0
    per_layer = lambda *tail: pl.BlockSpec((1,) + tail, lambda l, i: (l,) + (0,) * len(tail))
    st_blk = pl.BlockSpec((1, bb, HG_HEADS, HG_DK, HG_DV), lambda l, i: (l, i, 0, 0, 0))
    wide = pltpu.VMEM((B, HG_KEY), F32)
    s_new, h_new = pl.pallas_call(
        _hgrn_decode_kernel,
        grid=(n_a, B // bb),
        in_specs=[_whole((B, D)), st_blk, per_layer(1, D), per_layer(*w_in.shape[1:]),
                  _whole(lb_logits.shape), per_layer(1, HG_DV), per_layer(*w_out.shape[1:])],
        out_specs=[st_blk, _whole((B, D))],
        out_shape=[jax.ShapeDtypeStruct(states.shape, F32), jax.ShapeDtypeStruct((B, D), F32)],
        scratch_shapes=[pltpu.VMEM((B, D), F32)] + [wide] * 5 + [pltpu.VMEM((bb, HG_VAL), F32)],
        compiler_params=_cparams("arbitrary", "arbitrary"),
        name="hgrn_decode",
    )(h, states, norms.reshape(n_a, 1, D), w_in, lb_logits, gnorms.reshape(n_a, 1, HG_DV), w_out)
    return h_new, s_new


def _bf16_round(x):
    return x.astype(BF16).astype(F32)


def _swa_decode_kernel(final, h_ref, ng_ref, win_ref, cos_ref, sup_ref, sdn_ref, kc_ref, vc_ref,
                       kn_ref, vn_ref, sinks_ref, wout_ref, fin_ref, out_ref, qp_s, g_s, os_s):
    i = pl.program_id(0)
    n_blocks = pl.num_programs(0)
    bb, W = kc_ref.shape[0], kc_ref.shape[1]
    B = h_ref.shape[0]
    lane = lax.broadcasted_iota(jnp.int32, (B, LANES), 1)

    @pl.when(i == 0)
    def _():
        u = _mm(_rms_scale(h_ref[...], ng_ref[...]), win_ref[...])
        q = _rope_cols(u[:, :ATT_QD], cos_ref[...], sup_ref[...], sdn_ref[...]) * (ATT_HD ** -0.5)
        g_s[...] = _silu(u[:, ATT_QD:])
        for head in range(ATT_QH):
            col = q[:, (head // 2) * LANES:(head // 2 + 1) * LANES]
            kvh = head // ATT_GROUP
            if head % 2 != kvh:
                col = pltpu.roll(col, ATT_HD, axis=1)
            qp_s[:, head, :] = jnp.where((lane >= ATT_HD) if kvh == 1 else (lane < ATT_HD), col, 0.0)

    rows = pl.ds(pl.multiple_of(i * bb, bb), bb)
    kn, vn = kn_ref[rows, :], vn_ref[rows, :]
    qp = [qp_s[i * bb + j] for j in range(bb)]
    s_c = jnp.concatenate([_mm_nt(qp[j], kc_ref[j]) for j in range(bb)], axis=0)
    qp_all = jnp.concatenate(qp, axis=0)
    expand = lambda a: jnp.concatenate(
        [jnp.broadcast_to(a[j:j + 1], (ATT_QH, ATT_KVD)) for j in range(bb)], axis=0)
    s_n = jnp.sum(_bf16_round(qp_all) * _bf16_round(expand(kn)), axis=-1, keepdims=True)
    sink = jnp.concatenate([sinks_ref[...]] * bb, axis=0)
    m = jnp.maximum(jnp.maximum(jnp.max(s_c, axis=-1, keepdims=True), s_n), sink)
    p_c = jnp.exp(s_c - m)
    p_n = jnp.exp(s_n - m)
    inv = 1.0 / (jnp.sum(p_c, axis=-1, keepdims=True) + p_n + jnp.exp(sink - m))
    p_c = p_c * inv
    o = jnp.concatenate([_mm(p_c[j * ATT_QH:(j + 1) * ATT_QH], vc_ref[j]) for j in range(bb)], axis=0)
    o = o + _bf16_round(p_n * inv) * _bf16_round(expand(vn))
    for j in range(bb):
        os_s[i * bb + j] = o[j * ATT_QH:(j + 1) * ATT_QH]

    @pl.when(i == n_blocks - 1)
    def _():
        cols = []
        for c in range(ATT_QD // LANES):
            halves = []
            for par in range(2):
                head = 2 * c + par
                t = os_s[:, head, :]
                if par != head // ATT_GROUP:
                    t = pltpu.roll(t, ATT_HD, axis=1)
                halves.append(t)
            cols.append(jnp.where(lane < ATT_HD, halves[0], halves[1]))
        y = h_ref[...] + _mm(jnp.concatenate(cols, axis=1) * g_s[...], wout_ref[...])
        if final:
            y = _rms_scale(y, fin_ref[...])
        out_ref[...] = y


def _swa_decode_layer(h, k_new, v_new, cache_k, cache_v, tables, norm_g, w_in, layer, sinks, w_out,
                      final_norm, final):
    B, D = h.shape
    W = cache_k.shape[1]
    bb = min(SWA_DEC_BLOCK, B)
    assert B % bb == 0
    cache_blk = pl.BlockSpec((bb, W, ATT_KVD), lambda i: (i, 0, 0))
    head_rows = pltpu.VMEM((B, ATT_QH, ATT_KVD), F32)
    return pl.pallas_call(
        functools.partial(_swa_decode_kernel, final),
        grid=(B // bb,),
        in_specs=[_whole((B, D)), _whole((1, D)), _layer_block(w_in, layer)] + [_whole((1, LANES))] * 3
                 + [cache_blk, cache_blk, _whole((B, ATT_KVD)), _whole((B, ATT_KVD)),
                    _whole((ATT_QH, 1)), _layer_block(w_out, layer), _whole((1, D))],
        out_specs=_whole((B, D)),
        out_shape=jax.ShapeDtypeStruct((B, D), F32),
        scratch_shapes=[head_rows, pltpu.VMEM((B, ATT_QD), F32), head_rows],
        compiler_params=_cparams("arbitrary"),
        name="swa_decode_final" if final else "swa_decode",
    )(h, norm_g.reshape(1, D), w_in, *tables, cache_k.reshape(B, W, ATT_KVD),
      cache_v.reshape(B, W, ATT_KVD), k_new, v_new, sinks.reshape(ATT_QH, 1), w_out,
      final_norm.reshape(1, D))


def kernel(x_prompt, x_sample, state_hgrn, cache_k_win, cache_v_win, a_norm, a_w_in, a_lb_logits,
           a_gnorm, a_w_out, kv_norm, w_kv, b_norm, b_w_in, b_sinks, b_w_out, final_norm):
    B, T, D = x_prompt.shape
    BD, TD, _ = x_sample.shape
    n_a = a_w_in.shape[0]
    n_b = b_w_in.shape[0]
    a_w_in_b, a_w_out_b = a_w_in.astype(BF16), a_w_out.astype(BF16)
    b_w_in_b, b_w_out_b = b_w_in.astype(BF16), b_w_out.astype(BF16)
    w_kv_b = w_kv.astype(BF16)

    tab_p = _rope_tables(np.arange(T))
    h = x_prompt
    st_p = []
    for l in range(n_a):
        kv = (kv_norm, w_kv_b, tab_p) if l == n_a - 1 else None
        h, s, *kv_p = _hgrn_prompt_layer(h, a_norm[l], a_w_in_b, a_lb_logits, l, a_gnorm[l],
                                         a_w_out_b, kv)
        st_p.append(s)
    k_p, v_p = kv_p
    for l in range(n_b):
        h = _swa_prompt_layer(h, k_p, v_p, tab_p, b_norm[l], b_w_in_b, l, b_sinks[l], b_w_out_b,
                              final_norm, l == n_b - 1)
    y_prompt = h
    w_keep = min(WINDOW, T)
    k_win = k_p[:, T - w_keep:].reshape(B, w_keep, ATT_KVH, ATT_HD)
    v_win = v_p[:, T - w_keep:].reshape(B, w_keep, ATT_KVH, ATT_HD)

    tab_s = _rope_tables(PAST_LEN + np.arange(TD))
    hs = x_sample.reshape(BD * TD, D)
    hs, st_s = _hgrn_decode(hs, state_hgrn, a_norm, a_w_in_b, a_lb_logits, a_gnorm, a_w_out_b)
    k_s, v_s = _shared_kv(hs.reshape(1, BD, D), kv_norm, w_kv_b,
                          tuple(jnp.broadcast_to(t, (BD, LANES)) for t in tab_s))
    k_s, v_s = k_s.reshape(BD, ATT_KVD), v_s.reshape(BD, ATT_KVD)
    for l in range(n_b):
        hs = _swa_decode_layer(hs, k_s, v_s, cache_k_win, cache_v_win, tab_s, b_norm[l], b_w_in_b, l,
                               b_sinks[l], b_w_out_b, final_norm, l == n_b - 1)
    y_sample = hs.reshape(BD, TD, D)

    return (y_prompt, y_sample, jnp.stack(st_p), st_s, k_win, v_win,
            k_s.reshape(BD, TD, ATT_KVH, ATT_HD), v_s.reshape(BD, TD, ATT_KVH, ATT_HD))
```

```python
import functools

import numpy as np
import jax
import jax.numpy as jnp
from jax import lax
from jax.experimental import pallas as pl
from jax.experimental.pallas import tpu as pltpu

F32 = jnp.float32
BF16 = jnp.bfloat16

D_MODEL = 1024
HG_HEADS = 8
HG_DK = 128
HG_DV = 128
HG_KEY = HG_HEADS * HG_DK
HG_VAL = HG_HEADS * HG_DV
HG_CHUNK = 64
HG_HALF = HG_CHUNK // 2
HG_SUB = 16
HG_MAX_HALF_DECAY = 60.0
ATT_HD = 64
ATT_QH = 16
ATT_KVH = 2
ATT_GROUP = ATT_QH // ATT_KVH
ATT_QD = ATT_QH * ATT_HD
ATT_KVD = ATT_KVH * ATT_HD
WINDOW = 128
ROPE_THETA = 500000.0
ROT_DIM = ATT_HD // 4
NORM_EPS = 1e-6
MASK_VALUE = -1e30
NEG_BIG = -1e30
LOG2E = 1.4426950408889634
PAST_LEN = 8192

SUBLANES = 8
LANES = 128
PROMPT_TILE = 512
SWA_TILE = 1024
DEC_BLOCK = 8
SWA_DEC_BLOCK = 16
V7X_VMEM_BYTES = 64 * 1024 * 1024
VMEM_LIMIT = V7X_VMEM_BYTES * 3 // 4


def _cparams(*sem):
    return pltpu.CompilerParams(dimension_semantics=sem, vmem_limit_bytes=VMEM_LIMIT)


def _layer_block(stacked, layer, single_buffer=False):
    tail = stacked.shape[1:]
    mode = dict(pipeline_mode=pl.Buffered(1)) if single_buffer else {}
    return pl.BlockSpec((None,) + tail, lambda *_: (layer,) + (0,) * len(tail), **mode)


def _mm(a, b):
    return jnp.dot(a.astype(BF16), b.astype(BF16), preferred_element_type=F32)


def _mm_nt(a, b):
    return lax.dot_general(a.astype(BF16), b.astype(BF16), (((1,), (1,)), ((), ())),
                           preferred_element_type=F32)


def _mm_tn(a, b):
    return lax.dot_general(a.astype(BF16), b.astype(BF16), (((0,), (0,)), ((), ())),
                           preferred_element_type=F32)


def _rms_scale(x, g):
    ms = jnp.mean(x * x, axis=-1, keepdims=True)
    return x * lax.rsqrt(ms + NORM_EPS) * g


def _silu(x):
    return x * (1.0 / (1.0 + jnp.exp(-x)))


def _lower_bound(lbl, layer):
    n = lbl.shape[0]
    rows = [lbl[i:i + 1, :] for i in range(n)]
    m = functools.reduce(jnp.maximum, rows)
    es = [jnp.exp(r - m) for r in rows]
    inv = 1.0 / functools.reduce(jnp.add, es)
    ps = [e * inv for e in es]
    return functools.reduce(jnp.add, ps[:layer + 1]) - ps[0]


def _forget_gates(f, lb):
    e = jnp.exp(-f)
    r = 1.0 / (1.0 + e)
    return lb + (1.0 - lb) * r, (1.0 - lb) * (e * r)


def _decay(x):
    return jnp.exp(x)


def _chunk_cumsum(x):
    pos = lax.broadcasted_iota(jnp.int32, x.shape, 0) % HG_CHUNK
    s = 1
    while s < HG_CHUNK:
        x = x + jnp.where(pos >= s, pltpu.roll(x, s, axis=0), 0.0)
        s *= 2
    return x


def _diag_blocks(q, k, v, b):
    sub_iota = lax.broadcasted_iota(jnp.int32, (SUBLANES, LANES), 0)
    outs = []
    for blk in range(HG_CHUNK // HG_SUB):
        base = blk * HG_SUB
        groups = HG_SUB // SUBLANES
        qs = [q[base + g * SUBLANES: base + (g + 1) * SUBLANES] for g in range(groups)]
        bs = [b[base + g * SUBLANES: base + (g + 1) * SUBLANES] for g in range(groups)]
        accs = [jnp.zeros((SUBLANES, HG_DV), F32) for _ in range(groups)]
        for s in range(HG_SUB):
            row = base + s
            k_s = k[row:row + 1]
            b_s = b[row:row + 1]
            v_s = v[row:row + 1]
            for g in range(s // SUBLANES, groups):
                d = bs[g] - b_s
                if g == s // SUBLANES:
                    d = jnp.where(sub_iota >= (s % SUBLANES), d, NEG_BIG)
                p = qs[g] * (k_s * _decay(d))
                a = jnp.sum(p, axis=-1, keepdims=True)
                accs[g] = accs[g] + a * v_s
        outs.extend(accs)
    return jnp.concatenate(outs, axis=0)


def _hgrn_chunk(q, k, v, b, st):
    C, H = HG_CHUNK, HG_HALF
    o_inter = _mm_nt(q * _decay(b), st)
    r0 = b[H - 1:H]
    a0 = _mm_nt(q[H:] * _decay(b[H:] - r0), k[:H] * _decay(r0 - b[:H]))
    o_hi = _mm(a0, v[:H])
    Q = HG_SUB
    o_q = []
    for base in (0, H):
        r1 = b[base + Q - 1: base + Q]
        a1 = _mm_nt(q[base + Q: base + 2 * Q] * _decay(b[base + Q: base + 2 * Q] - r1),
                    k[base: base + Q] * _decay(r1 - b[base: base + Q]))
        o_q.append(_mm(a1, v[base: base + Q]))
    zero = jnp.zeros((Q, HG_DV), F32)
    o_off = jnp.concatenate([zero, o_q[0], o_hi[:Q], o_hi[Q:] + o_q[1]], axis=0)
    o = o_inter + o_off + _diag_blocks(q, k, v, b)
    b_end = b[C - 1:C]
    st_new = st * _decay(b_end) + _mm_tn(v, k * _decay(b_end - b))
    return o, st_new


def _hgrn_bounded_first(q, k, v, b, st):
    C, H = HG_CHUNK, HG_HALF
    r = b[H - 1:H]
    eq = _decay(b - r)
    qt, kt = q * eq, k * _decay(r - b)
    e_mid, e_hi = _decay(r), eq[C - 1:C]
    scores = _mm_nt(qt, kt)
    v_t = v.T
    st_new = st * (e_mid * e_hi) + _mm(v_t, kt * e_hi)
    return scores, qt * e_mid, v_t, st_new


def _hgrn_bounded_second(scores, q_dec, v_t, st):
    t_i = lax.broadcasted_iota(jnp.int32, scores.shape, 0)
    s_i = lax.broadcasted_iota(jnp.int32, scores.shape, 1)
    a = jnp.where(s_i <= t_i, scores, 0.0)
    return _mm_nt(jnp.concatenate([q_dec, a], axis=1), jnp.concatenate([st, v_t], axis=1))


def _max_half_decay(b_s, tile):
    n = tile // HG_CHUNK
    worst = None
    for h in range(HG_HEADS):
        mid = b_s[h, pl.ds(HG_HALF - 1, n, stride=HG_CHUNK), :]
        end = b_s[h, pl.ds(HG_CHUNK - 1, n, stride=HG_CHUNK), :]
        w = jnp.maximum(-mid, mid - end)
        worst = w if worst is None else jnp.maximum(worst, w)
    return jnp.max(worst)


def _hgrn_prompt_kernel(layer, with_kv, *refs):
    h_ref, ng_ref, win_ref, lbl_ref, gn_ref, wout_ref = refs[:6]
    refs = refs[6:]
    if with_kv:
        kv_in, refs = refs[:5], refs[5:]
    out_ref, sfin_ref = refs[:2]
    refs = refs[2:]
    if with_kv:
        kv_out, refs = refs[:2], refs[2:]
    q_s, k_s, v_s, b_s, o_s, st_s = refs
    t = pl.program_id(1)
    nt = pl.num_programs(1)
    tile = h_ref.shape[1]

    @pl.when(t == 0)
    def _():
        st_s[...] = jnp.zeros_like(st_s)

    x = h_ref[0]
    u = _mm(_rms_scale(x, ng_ref[...]), win_ref[...])
    lb = _lower_bound(lbl_ref[...], layer)
    for h in range(HG_HEADS):
        sl = slice(h * HG_DK, (h + 1) * HG_DK)
        q_s[h] = _silu(u[:, sl])
        fg, kin = _forget_gates(u[:, HG_KEY + h * HG_DK: HG_KEY + (h + 1) * HG_DK], lb[:, sl])
        k_s[h] = kin
        b_s[h] = _chunk_cumsum(jnp.log(fg))
        v_s[h] = u[:, 2 * HG_KEY + h * HG_DV: 2 * HG_KEY + (h + 1) * HG_DV]

    def chunk_rows(c):
        return pl.ds(pl.multiple_of(c * HG_CHUNK, HG_CHUNK), HG_CHUNK)

    def load(c, h):
        rows = chunk_rows(c)
        return q_s[h, rows, :], k_s[h, rows, :], v_s[h, rows, :], b_s[h, rows, :], st_s[h]

    def bounded_chunks():
        pending = []
        for c in range(tile // HG_CHUNK):
            rows = slice(c * HG_CHUNK, (c + 1) * HG_CHUNK)
            started = []
            for h in range(HG_HEADS):
                st = st_s[h]
                scores, q_dec, v_t, st_new = _hgrn_bounded_first(
                    q_s[h, rows, :], k_s[h, rows, :], v_s[h, rows, :], b_s[h, rows, :], st)
                st_s[h] = st_new
                started.append((h, rows, scores, q_dec, v_t, st))
            for h, prev_rows, *second in pending:
                o_s[h, prev_rows, :] = _hgrn_bounded_second(*second)
            pending = started
        for h, prev_rows, *second in pending:
            o_s[h, prev_rows, :] = _hgrn_bounded_second(*second)

    def general_body(i, carry):
        c, h = i // HG_HEADS, i % HG_HEADS
        o, st_new = _hgrn_chunk(*load(c, h))
        o_s[h, chunk_rows(c), :] = o
        st_s[h] = st_new
        return carry

    bounded = _max_half_decay(b_s, tile) < HG_MAX_HALF_DECAY

    pl.when(bounded)(bounded_chunks)

    @pl.when(jnp.logical_not(bounded))
    def _():
        lax.fori_loop(0, (tile // HG_CHUNK) * HG_HEADS, general_body, 0)

    gn = gn_ref[...]
    gated = []
    for h in range(HG_HEADS):
        g = u[:, 2 * HG_KEY + HG_VAL + h * HG_DV: 2 * HG_KEY + HG_VAL + (h + 1) * HG_DV]
        gated.append(_rms_scale(o_s[h], gn) * _silu(g))
    y = x + _mm(jnp.concatenate(gated, axis=1), wout_ref[...])
    out_ref[0] = y
    if with_kv:
        _kv_project(y, *kv_in, *kv_out)

    @pl.when(t == nt - 1)
    def _():
        for h in range(HG_HEADS):
            sfin_ref[0, h] = st_s[h].T


def _hgrn_prompt_layer(h, norm_g, w_in, lb_logits, layer, gnorm, w_out, kv=None):
    B, T, D = h.shape
    tile = min(PROMPT_TILE, T)
    assert T % tile == 0 and tile % HG_CHUNK == 0 and HG_DK == HG_DV == LANES
    n_a = lb_logits.shape[0]
    const = lambda b, t: (0, 0)
    row_tile = lambda width: pl.BlockSpec((1, tile, width), lambda b, t: (b, t, 0))
    head_scratch = pltpu.VMEM((HG_HEADS, tile, HG_DK), F32)
    operands = [h, norm_g.reshape(1, D), w_in, lb_logits, gnorm.reshape(1, HG_DV), w_out]
    in_specs = [row_tile(D), pl.BlockSpec((1, D), const), _layer_block(w_in, layer, True),
                pl.BlockSpec((n_a, HG_KEY), const), pl.BlockSpec((1, HG_DV), const),
                _layer_block(w_out, layer, True)]
    out_specs = [row_tile(D), pl.BlockSpec((1, HG_HEADS, HG_DK, HG_DV), lambda b, t: (b, 0, 0, 0))]
    out_shape = [jax.ShapeDtypeStruct((B, T, D), F32),
                 jax.ShapeDtypeStruct((B, HG_HEADS, HG_DK, HG_DV), F32)]
    if kv is not None:
        kv_norm, w_kv, tables = kv
        operands += [kv_norm.reshape(1, D), w_kv, *tables]
        in_specs += [pl.BlockSpec((1, D), const), pl.BlockSpec(w_kv.shape, const)]
        in_specs += [pl.BlockSpec((tile, LANES), lambda b, t: (t, 0))] * 3
        out_specs += [row_tile(ATT_KVD)] * 2
        out_shape += [jax.ShapeDtypeStruct((B, T, ATT_KVD), F32)] * 2
    return pl.pallas_call(
        functools.partial(_hgrn_prompt_kernel, layer, kv is not None),
        grid=(B, T // tile),
        in_specs=in_specs,
        out_specs=out_specs,
        out_shape=out_shape,
        scratch_shapes=[head_scratch] * 5 + [pltpu.VMEM((HG_HEADS, HG_DV, HG_DK), F32)],
        compiler_params=_cparams("arbitrary", "arbitrary"),
        name=f"hgrn_prompt_l{layer}",
    )(*operands)


def _rope_tables(pos):
    half = ROT_DIM // 2
    pos = np.asarray(pos, np.float64)
    inv_freq = 1.0 / (ROPE_THETA ** (np.arange(half, dtype=np.float64) * 2.0 / ROT_DIM))
    ang = (pos[:, None].astype(np.float32) * inv_freq[None, :].astype(np.float32)).astype(np.float64)
    cos, sin = np.cos(ang), np.sin(ang)
    n = pos.shape[0]
    c = np.ones((n, ATT_HD)); s_up = np.zeros((n, ATT_HD)); s_dn = np.zeros((n, ATT_HD))
    c[:, :half] = cos; c[:, half:ROT_DIM] = cos
    s_dn[:, :half] = -sin
    s_up[:, half:ROT_DIM] = sin
    rep = LANES // ATT_HD
    tab = [np.tile(a, (1, rep)).astype(np.float32) for a in (c, s_up, s_dn)]
    return tuple(jnp.asarray(a) for a in tab)


def _rope_cols(x, cos, s_up, s_dn):
    half = ROT_DIM // 2
    cols = []
    for c in range(x.shape[1] // LANES):
        xc = x[:, c * LANES:(c + 1) * LANES]
        cols.append(xc * cos + pltpu.roll(xc, half, axis=1) * s_up
                    + pltpu.roll(xc, LANES - half, axis=1) * s_dn)
    return cols[0] if len(cols) == 1 else jnp.concatenate(cols, axis=1)


def _kv_kernel(h_ref, *refs):
    _kv_project(h_ref[0], *refs)


def _kv_project(x, ng_ref, w_ref, cos_ref, sup_ref, sdn_ref, k_ref, v_ref):
    u = _mm(_rms_scale(x, ng_ref[...]), w_ref[...])
    k_ref[0] = _rope_cols(u[:, :ATT_KVD], cos_ref[...], sup_ref[...], sdn_ref[...])
    v_ref[0] = u[:, ATT_KVD:]


def _shared_kv(h, kv_norm, w_kv, tables):
    B, T, D = h.shape
    tile = min(PROMPT_TILE, T)
    const = lambda b, t: (0, 0)
    tab = pl.BlockSpec((tile, LANES), lambda b, t: (t, 0))
    kv_spec = pl.BlockSpec((1, tile, ATT_KVD), lambda b, t: (b, t, 0))
    return pl.pallas_call(
        _kv_kernel,
        grid=(B, T // tile),
        in_specs=[pl.BlockSpec((1, tile, D), lambda b, t: (b, t, 0)),
                  pl.BlockSpec((1, D), const), pl.BlockSpec(w_kv.shape, const), tab, tab, tab],
        out_specs=[kv_spec, kv_spec],
        out_shape=[jax.ShapeDtypeStruct((B, T, ATT_KVD), F32)] * 2,
        compiler_params=_cparams("arbitrary", "arbitrary"),
        name="shared_kv",
    )(h, kv_norm.reshape(1, D), w_kv, *tables)


PAIRS_PER_KV = ATT_GROUP // 2


def _swa_attention(blocks, sinks_ref, side_product):
    groups = [(blk, kvh) for blk in range(len(blocks)) for kvh in range(ATT_KVH)]
    n_keys = blocks[0][4].shape[0]
    ones_rows = jnp.ones((SUBLANES, n_keys), F32)

    def score_products(g):
        blk, kvh = groups[g]
        q_cols, k_lo, k_hi, _, bias_t = blocks[blk]
        qs = jnp.concatenate(q_cols[kvh * PAIRS_PER_KV:(kvh + 1) * PAIRS_PER_KV], axis=0)
        keys = jnp.concatenate([k_lo[kvh], k_hi[kvh]], axis=0)
        bias = jnp.concatenate([jnp.concatenate([bias_t] * PAIRS_PER_KV, axis=1)] * 2, axis=0)
        return _mm_nt(keys, qs) + bias

    def softmax_terms(g, s_t):
        kvh = groups[g][1]
        cols = range(kvh * PAIRS_PER_KV, (kvh + 1) * PAIRS_PER_KV)
        terms = []
        for par in range(2):
            s = s_t[par * n_keys:(par + 1) * n_keys]
            sink = jnp.concatenate(
                [jnp.full((1, WINDOW), sinks_ref[0, 2 * c + par] * LOG2E, F32) for c in cols], axis=1)
            m = jnp.maximum(jnp.max(s, axis=0, keepdims=True), sink)
            terms.append((jnp.exp2(s - m), jnp.exp2(sink - m)))
        return terms

    def value_products(g, terms):
        blk, kvh = groups[g]
        v_t = blocks[blk][3]
        lhs = jnp.concatenate([v_t[kvh * ATT_HD:(kvh + 1) * ATT_HD], ones_rows], axis=0)
        halves = []
        for p, sink_term in terms:
            r = _mm(lhs, p)
            halves.append(r[:ATT_HD] * (1.0 / (r[ATT_HD:ATT_HD + 1] + sink_term)))
        return halves

    scores = [score_products(g) for g in range(len(groups))]
    side = side_product()
    terms = [softmax_terms(g, s) for g, s in enumerate(scores)]
    outs = [value_products(g, t) for g, t in enumerate(terms)]
    res = []
    for b in range(len(blocks)):
        cols = []
        for kvh in range(ATT_KVH):
            even, odd = outs[b * ATT_KVH + kvh]
            for i in range(PAIRS_PER_KV):
                queries = slice(i * WINDOW, (i + 1) * WINDOW)
                cols.append(jnp.concatenate([even[:, queries], odd[:, queries]], axis=0).T)
        res.append(jnp.concatenate(cols, axis=1))
    return res, side


def _split_kv_heads(a, fill):
    lane = lax.broadcasted_iota(jnp.int32, a.shape, 1)
    low = lane < ATT_HD
    sw = pltpu.roll(a, ATT_HD, axis=1)
    lo = [jnp.where(low, a, fill), jnp.where(low, sw, fill)]
    hi = [jnp.where(low, fill, sw), jnp.where(low, fill, a)]
    return lo, hi


def _swa_prompt_kernel(final, h_ref, ng_ref, win_ref, sinks_ref, wout_ref, fin_ref,
                       kp_ref, kc_ref, vp_ref, vc_ref, cos_ref, sup_ref, sdn_ref, out_ref):
    t = pl.program_id(1)
    tile = h_ref.shape[1]
    nblk = tile // WINDOW
    x = h_ref[0]
    xn = _rms_scale(x, ng_ref[...]).astype(BF16)
    u = _mm(xn, win_ref[:, :ATT_QD])
    scale = ATT_HD ** -0.5 * LOG2E
    cos, sup, sdn = cos_ref[...], sup_ref[...], sdn_ref[...]
    k_all = jnp.concatenate([kp_ref[0], kc_ref[0]], axis=0)
    v_all = jnp.concatenate([vp_ref[0], vc_ref[0]], axis=0)
    j = lax.broadcasted_iota(jnp.int32, (2 * WINDOW, WINDOW), 0)
    tq = lax.broadcasted_iota(jnp.int32, (2 * WINDOW, WINDOW), 1)
    rel = tq + WINDOW - j
    band = (rel >= 0) & (rel <= WINDOW)
    blocks = []
    for blk in range(nblk):
        rows = slice(blk * WINDOW, (blk + 1) * WINDOW)
        keys = slice(blk * WINDOW, (blk + 2) * WINDOW)
        k_lo, k_hi = _split_kv_heads(k_all[keys], 0.0)
        if blk == 0:
            valid = band & (j >= WINDOW * (t == 0).astype(jnp.int32))
        else:
            valid = band
        q_cols = []
        for c in range(ATT_QD // LANES):
            qc = u[rows, c * LANES:(c + 1) * LANES]
            q_cols.append(_rope_cols(qc, cos[rows], sup[rows], sdn[rows]) * scale)
        blocks.append((q_cols, k_lo, k_hi, v_all[keys].T, jnp.where(valid, 0.0, MASK_VALUE)))
    o_blocks, gate = _swa_attention(blocks, sinks_ref, lambda: _mm(xn, win_ref[:, ATT_QD:]))
    o = jnp.concatenate(o_blocks, axis=0) if nblk > 1 else o_blocks[0]
    y = x + _mm(o * _silu(gate), wout_ref[...])
    if final:
        y = _rms_scale(y, fin_ref[...])
    out_ref[0] = y


def _swa_prompt_layer(h, k, v, tables, norm_g, w_in, layer, sinks, w_out, final_norm, final):
    B, T, D = h.shape
    tile = min(SWA_TILE, T)
    assert T % tile == 0 and tile % WINDOW == 0
    per = tile // WINDOW
    const = lambda b, t: (0, 0)
    tab = pl.BlockSpec((tile, LANES), lambda b, t: (t, 0))
    prev = pl.BlockSpec((1, WINDOW, ATT_KVD), lambda b, t: (b, jnp.maximum(t * per - 1, 0), 0))
    cur = pl.BlockSpec((1, tile, ATT_KVD), lambda b, t: (b, t, 0))
    return pl.pallas_call(
        functools.partial(_swa_prompt_kernel, final),
        grid=(B, T // tile),
        in_specs=[
            pl.BlockSpec((1, tile, D), lambda b, t: (b, t, 0)),
            pl.BlockSpec((1, D), const),
            _layer_block(w_in, layer, True),
            pl.BlockSpec(memory_space=pltpu.SMEM),
            _layer_block(w_out, layer, True),
            pl.BlockSpec((1, D), const),
            prev, cur, prev, cur, tab, tab, tab,
        ],
        out_specs=pl.BlockSpec((1, tile, D), lambda b, t: (b, t, 0)),
        out_shape=jax.ShapeDtypeStruct((B, T, D), F32),
        compiler_params=_cparams("arbitrary", "arbitrary"),
        name="swa_prompt_final" if final else "swa_prompt",
    )(h, norm_g.reshape(1, D), w_in, sinks.reshape(1, ATT_QH), w_out, final_norm.reshape(1, D),
      k, k, v, v, *tables)


def _col(row):
    n = row.shape[1]
    return jnp.broadcast_to(row, (n, n)).T


def _hgrn_decode_kernel(x_ref, s_ref, ng_ref, win_ref, lbl_ref, gn_ref, wout_ref,
                        snew_ref, hout_ref, h_s, q_s, f_s, v_s, g_s, o_s, oblk_s):
    l, i = pl.program_id(0), pl.program_id(1)
    n_layers, n_blocks = pl.num_programs(0), pl.num_programs(1)
    bb = s_ref.shape[1]

    @pl.when(i == 0)
    def _():
        @pl.when(l == 0)
        def _():
            h_s[...] = x_ref[...]

        u = _mm(_rms_scale(h_s[...], ng_ref[0]), win_ref[0])
        lbl = lbl_ref[...]
        lb = jnp.zeros_like(lbl[:1])
        for layer in range(lbl.shape[0]):
            lb = jnp.where(l == layer, _lower_bound(lbl, layer), lb)
        q_s[...] = _silu(u[:, :HG_KEY])
        f_s[...] = _forget_gates(u[:, HG_KEY:2 * HG_KEY], lb)[0]
        v_s[...] = u[:, 2 * HG_KEY:2 * HG_KEY + HG_VAL]
        g_s[...] = _silu(u[:, 2 * HG_KEY + HG_VAL:])

    rows = pl.ds(pl.multiple_of(i * bb, bb), bb)
    qb, fb, vb = q_s[rows, :], f_s[rows, :], v_s[rows, :]
    for j in range(bb):
        for h in range(HG_HEADS):
            sl = slice(h * HG_DK, (h + 1) * HG_DK)
            v_r = vb[j:j + 1, h * HG_DV:(h + 1) * HG_DV]
            s_new = _col(fb[j:j + 1, sl]) * (s_ref[0, j, h] - v_r) + v_r
            snew_ref[0, j, h] = s_new
            oblk_s[j:j + 1, h * HG_DV:(h + 1) * HG_DV] = _mm(qb[j:j + 1, sl], s_new)
    o_s[rows, :] = oblk_s[...]

    @pl.when(i == n_blocks - 1)
    def _():
        gn = gn_ref[0]
        gated = []
        for h in range(HG_HEADS):
            sl = slice(h * HG_DV, (h + 1) * HG_DV)
            gated.append(_rms_scale(o_s[:, sl], gn) * g_s[:, sl])
        h_new = h_s[...] + _mm(jnp.concatenate(gated, axis=1), wout_ref[0])
        h_s[...] = h_new

        @pl.when(l == n_layers - 1)
        def _():
            hout_ref[...] = h_new


def _whole(shape):
    return pl.BlockSpec(shape, lambda *_: (0,) * len(shape))


def _hgrn_decode(h, states, norms, w_in, lb_logits, gnorms, w_out):
    B, D = h.shape
    n_a = states.shape[0]
    bb = min(DEC_BLOCK, B)
    assert B % bb == 0
    per_layer = lambda *tail: pl.BlockSpec((1,) + tail, lambda l, i: (l,) + (0,) * len(tail))
    st_blk = pl.BlockSpec((1, bb, HG_HEADS, HG_DK, HG_DV), lambda l, i: (l, i, 0, 0, 0))
    wide = pltpu.VMEM((B, HG_KEY), F32)
    s_new, h_new = pl.pallas_call(
        _hgrn_decode_kernel,
        grid=(n_a, B // bb),
        in_specs=[_whole((B, D)), st_blk, per_layer(1, D), per_layer(*w_in.shape[1:]),
                  _whole(lb_logits.shape), per_layer(1, HG_DV), per_layer(*w_out.shape[1:])],
        out_specs=[st_blk, _whole((B, D))],
        out_shape=[jax.ShapeDtypeStruct(states.shape, F32), jax.ShapeDtypeStruct((B, D), F32)],
        scratch_shapes=[pltpu.VMEM((B, D), F32)] + [wide] * 5 + [pltpu.VMEM((bb, HG_VAL), F32)],
        compiler_params=_cparams("arbitrary", "arbitrary"),
        name="hgrn_decode",
    )(h, states, norms.reshape(n_a, 1, D), w_in, lb_logits, gnorms.reshape(n_a, 1, HG_DV), w_out)
    return h_new, s_new


def _bf16_round(x):
    return x.astype(BF16).astype(F32)


def _swa_decode_kernel(final, h_ref, ng_ref, win_ref, cos_ref, sup_ref, sdn_ref, kc_ref, vc_ref,
                       kn_ref, vn_ref, sinks_ref, wout_ref, fin_ref, out_ref, qp_s, g_s, os_s):
    i = pl.program_id(0)
    n_blocks = pl.num_programs(0)
    bb, W = kc_ref.shape[0], kc_ref.shape[1]
    B = h_ref.shape[0]
    lane = lax.broadcasted_iota(jnp.int32, (B, LANES), 1)

    @pl.when(i == 0)
    def _():
        u = _mm(_rms_scale(h_ref[...], ng_ref[...]), win_ref[...])
        q = _rope_cols(u[:, :ATT_QD], cos_ref[...], sup_ref[...], sdn_ref[...]) * (ATT_HD ** -0.5)
        g_s[...] = _silu(u[:, ATT_QD:])
        for head in range(ATT_QH):
            col = q[:, (head // 2) * LANES:(head // 2 + 1) * LANES]
            kvh = head // ATT_GROUP
            if head % 2 != kvh:
                col = pltpu.roll(col, ATT_HD, axis=1)
            qp_s[:, head, :] = jnp.where((lane >= ATT_HD) if kvh == 1 else (lane < ATT_HD), col, 0.0)

    rows = pl.ds(pl.multiple_of(i * bb, bb), bb)
    kn, vn = kn_ref[rows, :], vn_ref[rows, :]
    qp = [qp_s[i * bb + j] for j in range(bb)]
    s_c = jnp.concatenate([_mm_nt(qp[j], kc_ref[j]) for j in range(bb)], axis=0)
    qp_all = jnp.concatenate(qp, axis=0)
    expand = lambda a: jnp.concatenate(
        [jnp.broadcast_to(a[j:j + 1], (ATT_QH, ATT_KVD)) for j in range(bb)], axis=0)
    s_n = jnp.sum(_bf16_round(qp_all) * _bf16_round(expand(kn)), axis=-1, keepdims=True)
    sink = jnp.concatenate([sinks_ref[...]] * bb, axis=0)
    m = jnp.maximum(jnp.maximum(jnp.max(s_c, axis=-1, keepdims=True), s_n), sink)
    p_c = jnp.exp(s_c - m)
    p_n = jnp.exp(s_n - m)
    inv = 1.0 / (jnp.sum(p_c, axis=-1, keepdims=True) + p_n + jnp.exp(sink - m))
    p_c = p_c * inv
    o = jnp.concatenate([_mm(p_c[j * ATT_QH:(j + 1) * ATT_QH], vc_ref[j]) for j in range(bb)], axis=0)
    o = o + _bf16_round(p_n * inv) * _bf16_round(expand(vn))
    for j in range(bb):
        os_s[i * bb + j] = o[j * ATT_QH:(j + 1) * ATT_QH]

    @pl.when(i == n_blocks - 1)
    def _():
        cols = []
        for c in range(ATT_QD // LANES):
            halves = []
            for par in range(2):
                head = 2 * c + par
                t = os_s[:, head, :]
                if par != head // ATT_GROUP:
                    t = pltpu.roll(t, ATT_HD, axis=1)
                halves.append(t)
            cols.append(jnp.where(lane < ATT_HD, halves[0], halves[1]))
        y = h_ref[...] + _mm(jnp.concatenate(cols, axis=1) * g_s[...], wout_ref[...])
        if final:
            y = _rms_scale(y, fin_ref[...])
        out_ref[...] = y


def _swa_decode_layer(h, k_new, v_new, cache_k, cache_v, tables, norm_g, w_in, layer, sinks, w_out,
                      final_norm, final):
    B, D = h.shape
    W = cache_k.shape[1]
    bb = min(SWA_DEC_BLOCK, B)
    assert B % bb == 0
    cache_blk = pl.BlockSpec((bb, W, ATT_KVD), lambda i: (i, 0, 0))
    head_rows = pltpu.VMEM((B, ATT_QH, ATT_KVD), F32)
    return pl.pallas_call(
        functools.partial(_swa_decode_kernel, final),
        grid=(B // bb,),
        in_specs=[_whole((B, D)), _whole((1, D)), _layer_block(w_in, layer)] + [_whole((1, LANES))] * 3
                 + [cache_blk, cache_blk, _whole((B, ATT_KVD)), _whole((B, ATT_KVD)),
                    _whole((ATT_QH, 1)), _layer_block(w_out, layer), _whole((1, D))],
        out_specs=_whole((B, D)),
        out_shape=jax.ShapeDtypeStruct((B, D), F32),
        scratch_shapes=[head_rows, pltpu.VMEM((B, ATT_QD), F32), head_rows],
        compiler_params=_cparams("arbitrary"),
        name="swa_decode_final" if final else "swa_decode",
    )(h, norm_g.reshape(1, D), w_in, *tables, cache_k.reshape(B, W, ATT_KVD),
      cache_v.reshape(B, W, ATT_KVD), k_new, v_new, sinks.reshape(ATT_QH, 1), w_out,
      final_norm.reshape(1, D))


def kernel(x_prompt, x_sample, state_hgrn, cache_k_win, cache_v_win, a_norm, a_w_in, a_lb_logits,
           a_gnorm, a_w_out, kv_norm, w_kv, b_norm, b_w_in, b_sinks, b_w_out, final_norm):
    B, T, D = x_prompt.shape
    BD, TD, _ = x_sample.shape
    n_a = a_w_in.shape[0]
    n_b = b_w_in.shape[0]
    a_w_in_b, a_w_out_b = a_w_in.astype(BF16), a_w_out.astype(BF16)
    b_w_in_b, b_w_out_b = b_w_in.astype(BF16), b_w_out.astype(BF16)
    w_kv_b = w_kv.astype(BF16)

    tab_p = _rope_tables(np.arange(T))
    h = x_prompt
    st_p = []
    for l in range(n_a):
        kv = (kv_norm, w_kv_b, tab_p) if l == n_a - 1 else None
        h, s, *kv_p = _hgrn_prompt_layer(h, a_norm[l], a_w_in_b, a_lb_logits, l, a_gnorm[l],
                                         a_w_out_b, kv)
        st_p.append(s)
    k_p, v_p = kv_p
    for l in range(n_b):
        h = _swa_prompt_layer(h, k_p, v_p, tab_p, b_norm[l], b_w_in_b, l, b_sinks[l], b_w_out_b,
                              final_norm, l == n_b - 1)
    y_prompt = h
    w_keep = min(WINDOW, T)
    k_win = k_p[:, T - w_keep:].reshape(B, w_keep, ATT_KVH, ATT_HD)
    v_win = v_p[:, T - w_keep:].reshape(B, w_keep, ATT_KVH, ATT_HD)

    tab_s = _rope_tables(PAST_LEN + np.arange(TD))
    hs = x_sample.reshape(BD * TD, D)
    hs, st_s = _hgrn_decode(hs, state_hgrn, a_norm, a_w_in_b, a_lb_logits, a_gnorm, a_w_out_b)
    k_s, v_s = _shared_kv(hs.reshape(1, BD, D), kv_norm, w_kv_b,
                          tuple(jnp.broadcast_to(t, (BD, LANES)) for t in tab_s))
    k_s, v_s = k_s.reshape(BD, ATT_KVD), v_s.reshape(BD, ATT_KVD)
    for l in range(n_b):
        hs = _swa_decode_layer(hs, k_s, v_s, cache_k_win, cache_v_win, tab_s, b_norm[l], b_w_in_b, l,
                               b_sinks[l], b_w_out_b, final_norm, l == n_b - 1)
    y_sample = hs.reshape(BD, TD, D)

    return (y_prompt, y_sample, jnp.stack(st_p), st_s, k_win, v_win,
            k_s.reshape(BD, TD, ATT_KVH, ATT_HD), v_s.reshape(BD, TD, ATT_KVH, ATT_HD))
```

```python
import functools

import numpy as np
import jax
import jax.numpy as jnp
from jax import lax
from jax.experimental import pallas as pl
from jax.experimental.pallas import tpu as pltpu

F32 = jnp.float32
BF16 = jnp.bfloat16

D_MODEL = 1024
HG_HEADS = 8
HG_DK = 128
HG_DV = 128
HG_KEY = HG_HEADS * HG_DK
HG_VAL = HG_HEADS * HG_DV
HG_CHUNK = 64
HG_HALF = HG_CHUNK // 2
HG_SUB = 16
HG_MAX_HALF_DECAY = 60.0
ATT_HD = 64
ATT_QH = 16
ATT_KVH = 2
ATT_GROUP = ATT_QH // ATT_KVH
ATT_QD = ATT_QH * ATT_HD
ATT_KVD = ATT_KVH * ATT_HD
WINDOW = 128
ROPE_THETA = 500000.0
ROT_DIM = ATT_HD // 4
NORM_EPS = 1e-6
MASK_VALUE = -1e30
NEG_BIG = -1e30
LOG2E = 1.4426950408889634
PAST_LEN = 8192

SUBLANES = 8
LANES = 128
PROMPT_TILE = 512
SWA_TILE = 1024
DEC_BLOCK = 8
SWA_DEC_BLOCK = 16
V7X_VMEM_BYTES = 64 * 1024 * 1024
VMEM_LIMIT = V7X_VMEM_BYTES * 3 // 4


def _cparams(*sem):
    return pltpu.CompilerParams(dimension_semantics=sem, vmem_limit_bytes=VMEM_LIMIT)


def _layer_block(stacked, layer, single_buffer=False):
    tail = stacked.shape[1:]
    mode = dict(pipeline_mode=pl.Buffered(1)) if single_buffer else {}
    return pl.BlockSpec((None,) + tail, lambda *_: (layer,) + (0,) * len(tail), **mode)


def _mm(a, b):
    return jnp.dot(a.astype(BF16), b.astype(BF16), preferred_element_type=F32)


def _mm_nt(a, b):
    return lax.dot_general(a.astype(BF16), b.astype(BF16), (((1,), (1,)), ((), ())),
                           preferred_element_type=F32)


def _mm_tn(a, b):
    return lax.dot_general(a.astype(BF16), b.astype(BF16), (((0,), (0,)), ((), ())),
                           preferred_element_type=F32)


def _rms_scale(x, g):
    ms = jnp.mean(x * x, axis=-1, keepdims=True)
    return x * lax.rsqrt(ms + NORM_EPS) * g


def _silu(x):
    return x * (0.5 + 0.5 * jnp.tanh(0.5 * x))


def _lower_bound(lbl, layer):
    n = lbl.shape[0]
    rows = [lbl[i:i + 1, :] for i in range(n)]
    m = functools.reduce(jnp.maximum, rows)
    es = [jnp.exp(r - m) for r in rows]
    inv = 1.0 / functools.reduce(jnp.add, es)
    ps = [e * inv for e in es]
    return functools.reduce(jnp.add, ps[:layer + 1]) - ps[0]


def _forget_gates(f, lb):
    e = jnp.exp(-f)
    r = 1.0 / (1.0 + e)
    return lb + (1.0 - lb) * r, (1.0 - lb) * (e * r)


def _decay(x):
    return jnp.exp(x)


def _chunk_cumsum(x):
    pos = lax.broadcasted_iota(jnp.int32, x.shape, 0) % HG_CHUNK
    s = 1
    while s < HG_CHUNK:
        x = x + jnp.where(pos >= s, pltpu.roll(x, s, axis=0), 0.0)
        s *= 2
    return x


def _diag_blocks(q, k, v, b):
    sub_iota = lax.broadcasted_iota(jnp.int32, (SUBLANES, LANES), 0)
    outs = []
    for blk in range(HG_CHUNK // HG_SUB):
        base = blk * HG_SUB
        groups = HG_SUB // SUBLANES
        qs = [q[base + g * SUBLANES: base + (g + 1) * SUBLANES] for g in range(groups)]
        bs = [b[base + g * SUBLANES: base + (g + 1) * SUBLANES] for g in range(groups)]
        accs = [jnp.zeros((SUBLANES, HG_DV), F32) for _ in range(groups)]
        for s in range(HG_SUB):
            row = base + s
            k_s = k[row:row + 1]
            b_s = b[row:row + 1]
            v_s = v[row:row + 1]
            for g in range(s // SUBLANES, groups):
                d = bs[g] - b_s
                if g == s // SUBLANES:
                    d = jnp.where(sub_iota >= (s % SUBLANES), d, NEG_BIG)
                p = qs[g] * (k_s * _decay(d))
                a = jnp.sum(p, axis=-1, keepdims=True)
                accs[g] = accs[g] + a * v_s
        outs.extend(accs)
    return jnp.concatenate(outs, axis=0)


def _hgrn_chunk(q, k, v, b, st):
    C, H = HG_CHUNK, HG_HALF
    o_inter = _mm_nt(q * _decay(b), st)
    r0 = b[H - 1:H]
    a0 = _mm_nt(q[H:] * _decay(b[H:] - r0), k[:H] * _decay(r0 - b[:H]))
    o_hi = _mm(a0, v[:H])
    Q = HG_SUB
    o_q = []
    for base in (0, H):
        r1 = b[base + Q - 1: base + Q]
        a1 = _mm_nt(q[base + Q: base + 2 * Q] * _decay(b[base + Q: base + 2 * Q] - r1),
                    k[base: base + Q] * _decay(r1 - b[base: base + Q]))
        o_q.append(_mm(a1, v[base: base + Q]))
    zero = jnp.zeros((Q, HG_DV), F32)
    o_off = jnp.concatenate([zero, o_q[0], o_hi[:Q], o_hi[Q:] + o_q[1]], axis=0)
    o = o_inter + o_off + _diag_blocks(q, k, v, b)
    b_end = b[C - 1:C]
    st_new = st * _decay(b_end) + _mm_tn(v, k * _decay(b_end - b))
    return o, st_new


def _hgrn_bounded_first(q, k, v, b, st):
    C, H = HG_CHUNK, HG_HALF
    r = b[H - 1:H]
    eq = _decay(b - r)
    qt, kt = q * eq, k * _decay(r - b)
    e_mid, e_hi = _decay(r), eq[C - 1:C]
    scores = _mm_nt(qt, kt)
    v_t = v.T
    st_new = st * (e_mid * e_hi) + _mm(v_t, kt * e_hi)
    return scores, qt * e_mid, v_t, st_new


def _hgrn_bounded_second(scores, q_dec, v_t, st):
    t_i = lax.broadcasted_iota(jnp.int32, scores.shape, 0)
    s_i = lax.broadcasted_iota(jnp.int32, scores.shape, 1)
    a = jnp.where(s_i <= t_i, scores, 0.0)
    return _mm_nt(jnp.concatenate([q_dec, a], axis=1), jnp.concatenate([st, v_t], axis=1))


def _max_half_decay(b_s, tile):
    n = tile // HG_CHUNK
    worst = None
    for h in range(HG_HEADS):
        mid = b_s[h, pl.ds(HG_HALF - 1, n, stride=HG_CHUNK), :]
        end = b_s[h, pl.ds(HG_CHUNK - 1, n, stride=HG_CHUNK), :]
        w = jnp.maximum(-mid, mid - end)
        worst = w if worst is None else jnp.maximum(worst, w)
    return jnp.max(worst)


def _hgrn_prompt_kernel(layer, with_kv, *refs):
    h_ref, ng_ref, win_ref, lbl_ref, gn_ref, wout_ref = refs[:6]
    refs = refs[6:]
    if with_kv:
        kv_in, refs = refs[:5], refs[5:]
    out_ref, sfin_ref = refs[:2]
    refs = refs[2:]
    if with_kv:
        kv_out, refs = refs[:2], refs[2:]
    q_s, k_s, v_s, b_s, o_s, st_s = refs
    t = pl.program_id(1)
    nt = pl.num_programs(1)
    tile = h_ref.shape[1]

    @pl.when(t == 0)
    def _():
        st_s[...] = jnp.zeros_like(st_s)

    x = h_ref[0]
    u = _mm(_rms_scale(x, ng_ref[...]), win_ref[...])
    lb = _lower_bound(lbl_ref[...], layer)
    for h in range(HG_HEADS):
        sl = slice(h * HG_DK, (h + 1) * HG_DK)
        q_s[h] = _silu(u[:, sl])
        fg, kin = _forget_gates(u[:, HG_KEY + h * HG_DK: HG_KEY + (h + 1) * HG_DK], lb[:, sl])
        k_s[h] = kin
        b_s[h] = _chunk_cumsum(jnp.log(fg))
        v_s[h] = u[:, 2 * HG_KEY + h * HG_DV: 2 * HG_KEY + (h + 1) * HG_DV]

    def chunk_rows(c):
        return pl.ds(pl.multiple_of(c * HG_CHUNK, HG_CHUNK), HG_CHUNK)

    def load(c, h):
        rows = chunk_rows(c)
        return q_s[h, rows, :], k_s[h, rows, :], v_s[h, rows, :], b_s[h, rows, :], st_s[h]

    def bounded_chunks():
        pending = []
        for c in range(tile // HG_CHUNK):
            rows = slice(c * HG_CHUNK, (c + 1) * HG_CHUNK)
            started = []
            for h in range(HG_HEADS):
                st = st_s[h]
                scores, q_dec, v_t, st_new = _hgrn_bounded_first(
                    q_s[h, rows, :], k_s[h, rows, :], v_s[h, rows, :], b_s[h, rows, :], st)
                st_s[h] = st_new
                started.append((h, rows, scores, q_dec, v_t, st))
            for h, prev_rows, *second in pending:
                o_s[h, prev_rows, :] = _hgrn_bounded_second(*second)
            pending = started
        for h, prev_rows, *second in pending:
            o_s[h, prev_rows, :] = _hgrn_bounded_second(*second)

    def general_body(i, carry):
        c, h = i // HG_HEADS, i % HG_HEADS
        o, st_new = _hgrn_chunk(*load(c, h))
        o_s[h, chunk_rows(c), :] = o
        st_s[h] = st_new
        return carry

    bounded = _max_half_decay(b_s, tile) < HG_MAX_HALF_DECAY

    pl.when(bounded)(bounded_chunks)

    @pl.when(jnp.logical_not(bounded))
    def _():
        lax.fori_loop(0, (tile // HG_CHUNK) * HG_HEADS, general_body, 0)

    gn = gn_ref[...]
    gated = []
    for h in range(HG_HEADS):
        g = u[:, 2 * HG_KEY + HG_VAL + h * HG_DV: 2 * HG_KEY + HG_VAL + (h + 1) * HG_DV]
        gated.append(_rms_scale(o_s[h], gn) * _silu(g))
    y = x + _mm(jnp.concatenate(gated, axis=1), wout_ref[...])
    out_ref[0] = y
    if with_kv:
        _kv_project(y, *kv_in, *kv_out)

    @pl.when(t == nt - 1)
    def _():
        for h in range(HG_HEADS):
            sfin_ref[0, h] = st_s[h].T


def _hgrn_prompt_layer(h, norm_g, w_in, lb_logits, layer, gnorm, w_out, kv=None):
    B, T, D = h.shape
    tile = min(PROMPT_TILE, T)
    assert T % tile == 0 and tile % HG_CHUNK == 0 and HG_DK == HG_DV == LANES
    n_a = lb_logits.shape[0]
    const = lambda b, t: (0, 0)
    row_tile = lambda width: pl.BlockSpec((1, tile, width), lambda b, t: (b, t, 0))
    head_scratch = pltpu.VMEM((HG_HEADS, tile, HG_DK), F32)
    operands = [h, norm_g.reshape(1, D), w_in, lb_logits, gnorm.reshape(1, HG_DV), w_out]
    in_specs = [row_tile(D), pl.BlockSpec((1, D), const), _layer_block(w_in, layer, True),
                pl.BlockSpec((n_a, HG_KEY), const), pl.BlockSpec((1, HG_DV), const),
                _layer_block(w_out, layer, True)]
    out_specs = [row_tile(D), pl.BlockSpec((1, HG_HEADS, HG_DK, HG_DV), lambda b, t: (b, 0, 0, 0))]
    out_shape = [jax.ShapeDtypeStruct((B, T, D), F32),
                 jax.ShapeDtypeStruct((B, HG_HEADS, HG_DK, HG_DV), F32)]
    if kv is not None:
        kv_norm, w_kv, tables = kv
        operands += [kv_norm.reshape(1, D), w_kv, *tables]
        in_specs += [pl.BlockSpec((1, D), const), pl.BlockSpec(w_kv.shape, const)]
        in_specs += [pl.BlockSpec((tile, LANES), lambda b, t: (t, 0))] * 3
        out_specs += [row_tile(ATT_KVD)] * 2
        out_shape += [jax.ShapeDtypeStruct((B, T, ATT_KVD), F32)] * 2
    return pl.pallas_call(
        functools.partial(_hgrn_prompt_kernel, layer, kv is not None),
        grid=(B, T // tile),
        in_specs=in_specs,
        out_specs=out_specs,
        out_shape=out_shape,
        scratch_shapes=[head_scratch] * 5 + [pltpu.VMEM((HG_HEADS, HG_DV, HG_DK), F32)],
        compiler_params=_cparams("arbitrary", "arbitrary"),
        name=f"hgrn_prompt_l{layer}",
    )(*operands)


def _rope_tables(pos):
    half = ROT_DIM // 2
    pos = np.asarray(pos, np.float64)
    inv_freq = 1.0 / (ROPE_THETA ** (np.arange(half, dtype=np.float64) * 2.0 / ROT_DIM))
    ang = (pos[:, None].astype(np.float32) * inv_freq[None, :].astype(np.float32)).astype(np.float64)
    cos, sin = np.cos(ang), np.sin(ang)
    n = pos.shape[0]
    c = np.ones((n, ATT_HD)); s_up = np.zeros((n, ATT_HD)); s_dn = np.zeros((n, ATT_HD))
    c[:, :half] = cos; c[:, half:ROT_DIM] = cos
    s_dn[:, :half] = -sin
    s_up[:, half:ROT_DIM] = sin
    rep = LANES // ATT_HD
    tab = [np.tile(a, (1, rep)).astype(np.float32) for a in (c, s_up, s_dn)]
    return tuple(jnp.asarray(a) for a in tab)


def _rope_cols(x, cos, s_up, s_dn):
    half = ROT_DIM // 2
    cols = []
    for c in range(x.shape[1] // LANES):
        xc = x[:, c * LANES:(c + 1) * LANES]
        cols.append(xc * cos + pltpu.roll(xc, half, axis=1) * s_up
                    + pltpu.roll(xc, LANES - half, axis=1) * s_dn)
    return cols[0] if len(cols) == 1 else jnp.concatenate(cols, axis=1)


def _kv_kernel(h_ref, *refs):
    _kv_project(h_ref[0], *refs)


def _kv_project(x, ng_ref, w_ref, cos_ref, sup_ref, sdn_ref, k_ref, v_ref):
    u = _mm(_rms_scale(x, ng_ref[...]), w_ref[...])
    k_ref[0] = _rope_cols(u[:, :ATT_KVD], cos_ref[...], sup_ref[...], sdn_ref[...])
    v_ref[0] = u[:, ATT_KVD:]


def _shared_kv(h, kv_norm, w_kv, tables):
    B, T, D = h.shape
    tile = min(PROMPT_TILE, T)
    const = lambda b, t: (0, 0)
    tab = pl.BlockSpec((tile, LANES), lambda b, t: (t, 0))
    kv_spec = pl.BlockSpec((1, tile, ATT_KVD), lambda b, t: (b, t, 0))
    return pl.pallas_call(
        _kv_kernel,
        grid=(B, T // tile),
        in_specs=[pl.BlockSpec((1, tile, D), lambda b, t: (b, t, 0)),
                  pl.BlockSpec((1, D), const), pl.BlockSpec(w_kv.shape, const), tab, tab, tab],
        out_specs=[kv_spec, kv_spec],
        out_shape=[jax.ShapeDtypeStruct((B, T, ATT_KVD), F32)] * 2,
        compiler_params=_cparams("arbitrary", "arbitrary"),
        name="shared_kv",
    )(h, kv_norm.reshape(1, D), w_kv, *tables)


PAIRS_PER_KV = ATT_GROUP // 2


def _swa_attention(blocks, sinks_ref, side_product):
    groups = [(blk, kvh) for blk in range(len(blocks)) for kvh in range(ATT_KVH)]
    n_keys = blocks[0][4].shape[0]
    ones_rows = jnp.ones((SUBLANES, n_keys), F32)

    def score_products(g):
        blk, kvh = groups[g]
        q_cols, k_lo, k_hi, _, bias_t = blocks[blk]
        qs = jnp.concatenate(q_cols[kvh * PAIRS_PER_KV:(kvh + 1) * PAIRS_PER_KV], axis=0)
        keys = jnp.concatenate([k_lo[kvh], k_hi[kvh]], axis=0)
        bias = jnp.concatenate([jnp.concatenate([bias_t] * PAIRS_PER_KV, axis=1)] * 2, axis=0)
        return _mm_nt(keys, qs) + bias

    def softmax_terms(g, s_t):
        kvh = groups[g][1]
        cols = range(kvh * PAIRS_PER_KV, (kvh + 1) * PAIRS_PER_KV)
        terms = []
        for par in range(2):
            s = s_t[par * n_keys:(par + 1) * n_keys]
            sink = jnp.concatenate(
                [jnp.full((1, WINDOW), sinks_ref[0, 2 * c + par] * LOG2E, F32) for c in cols], axis=1)
            m = jnp.maximum(jnp.max(s, axis=0, keepdims=True), sink)
            terms.append((jnp.exp2(s - m), jnp.exp2(sink - m)))
        return terms

    def value_products(g, terms):
        blk, kvh = groups[g]
        v_t = blocks[blk][3]
        lhs = jnp.concatenate([v_t[kvh * ATT_HD:(kvh + 1) * ATT_HD], ones_rows], axis=0)
        halves = []
        for p, sink_term in terms:
            r = _mm(lhs, p)
            halves.append(r[:ATT_HD] * (1.0 / (r[ATT_HD:ATT_HD + 1] + sink_term)))
        return halves

    scores = [score_products(g) for g in range(len(groups))]
    side = side_product()
    terms = [softmax_terms(g, s) for g, s in enumerate(scores)]
    outs = [value_products(g, t) for g, t in enumerate(terms)]
    res = []
    for b in range(len(blocks)):
        cols = []
        for kvh in range(ATT_KVH):
            even, odd = outs[b * ATT_KVH + kvh]
            for i in range(PAIRS_PER_KV):
                queries = slice(i * WINDOW, (i + 1) * WINDOW)
                cols.append(jnp.concatenate([even[:, queries], odd[:, queries]], axis=0).T)
        res.append(jnp.concatenate(cols, axis=1))
    return res, side


def _split_kv_heads(a, fill):
    lane = lax.broadcasted_iota(jnp.int32, a.shape, 1)
    low = lane < ATT_HD
    sw = pltpu.roll(a, ATT_HD, axis=1)
    lo = [jnp.where(low, a, fill), jnp.where(low, sw, fill)]
    hi = [jnp.where(low, fill, sw), jnp.where(low, fill, a)]
    return lo, hi


def _swa_prompt_kernel(final, h_ref, ng_ref, win_ref, sinks_ref, wout_ref, fin_ref,
                       kp_ref, kc_ref, vp_ref, vc_ref, cos_ref, sup_ref, sdn_ref, out_ref):
    t = pl.program_id(1)
    tile = h_ref.shape[1]
    nblk = tile // WINDOW
    x = h_ref[0]
    xn = _rms_scale(x, ng_ref[...]).astype(BF16)
    u = _mm(xn, win_ref[:, :ATT_QD])
    scale = ATT_HD ** -0.5 * LOG2E
    cos, sup, sdn = cos_ref[...], sup_ref[...], sdn_ref[...]
    k_all = jnp.concatenate([kp_ref[0], kc_ref[0]], axis=0)
    v_all = jnp.concatenate([vp_ref[0], vc_ref[0]], axis=0)
    j = lax.broadcasted_iota(jnp.int32, (2 * WINDOW, WINDOW), 0)
    tq = lax.broadcasted_iota(jnp.int32, (2 * WINDOW, WINDOW), 1)
    rel = tq + WINDOW - j
    band = (rel >= 0) & (rel <= WINDOW)
    blocks = []
    for blk in range(nblk):
        rows = slice(blk * WINDOW, (blk + 1) * WINDOW)
        keys = slice(blk * WINDOW, (blk + 2) * WINDOW)
        k_lo, k_hi = _split_kv_heads(k_all[keys], 0.0)
        if blk == 0:
            valid = band & (j >= WINDOW * (t == 0).astype(jnp.int32))
        else:
            valid = band
        q_cols = []
        for c in range(ATT_QD // LANES):
            qc = u[rows, c * LANES:(c + 1) * LANES]
            q_cols.append(_rope_cols(qc, cos[rows], sup[rows], sdn[rows]) * scale)
        blocks.append((q_cols, k_lo, k_hi, v_all[keys].T, jnp.where(valid, 0.0, MASK_VALUE)))
    o_blocks, gate = _swa_attention(blocks, sinks_ref, lambda: _mm(xn, win_ref[:, ATT_QD:]))
    o = jnp.concatenate(o_blocks, axis=0) if nblk > 1 else o_blocks[0]
    y = x + _mm(o * _silu(gate), wout_ref[...])
    if final:
        y = _rms_scale(y, fin_ref[...])
    out_ref[0] = y


def _swa_prompt_layer(h, k, v, tables, norm_g, w_in, layer, sinks, w_out, final_norm, final):
    B, T, D = h.shape
    tile = min(SWA_TILE, T)
    assert T % tile == 0 and tile % WINDOW == 0
    per = tile // WINDOW
    const = lambda b, t: (0, 0)
    tab = pl.BlockSpec((tile, LANES), lambda b, t: (t, 0))
    prev = pl.BlockSpec((1, WINDOW, ATT_KVD), lambda b, t: (b, jnp.maximum(t * per - 1, 0), 0))
    cur = pl.BlockSpec((1, tile, ATT_KVD), lambda b, t: (b, t, 0))
    return pl.pallas_call(
        functools.partial(_swa_prompt_kernel, final),
        grid=(B, T // tile),
        in_specs=[
            pl.BlockSpec((1, tile, D), lambda b, t: (b, t, 0)),
            pl.BlockSpec((1, D), const),
            _layer_block(w_in, layer, True),
            pl.BlockSpec(memory_space=pltpu.SMEM),
            _layer_block(w_out, layer, True),
            pl.BlockSpec((1, D), const),
            prev, cur, prev, cur, tab, tab, tab,
        ],
        out_specs=pl.BlockSpec((1, tile, D), lambda b, t: (b, t, 0)),
        out_shape=jax.ShapeDtypeStruct((B, T, D), F32),
        compiler_params=_cparams("arbitrary", "arbitrary"),
        name="swa_prompt_final" if final else "swa_prompt",
    )(h, norm_g.reshape(1, D), w_in, sinks.reshape(1, ATT_QH), w_out, final_norm.reshape(1, D),
      k, k, v, v, *tables)


def _col(row):
    n = row.shape[1]
    return jnp.broadcast_to(row, (n, n)).T


def _hgrn_decode_kernel(x_ref, s_ref, ng_ref, win_ref, lbl_ref, gn_ref, wout_ref,
                        snew_ref, hout_ref, h_s, q_s, f_s, v_s, g_s, o_s, oblk_s):
    l, i = pl.program_id(0), pl.program_id(1)
    n_layers, n_blocks = pl.num_programs(0), pl.num_programs(1)
    bb = s_ref.shape[1]

    @pl.when(i == 0)
    def _():
        @pl.when(l == 0)
        def _():
            h_s[...] = x_ref[...]

        u = _mm(_rms_scale(h_s[...], ng_ref[0]), win_ref[0])
        lbl = lbl_ref[...]
        lb = jnp.zeros_like(lbl[:1])
        for layer in range(lbl.shape[0]):
            lb = jnp.where(l == layer, _lower_bound(lbl, layer), lb)
        q_s[...] = _silu(u[:, :HG_KEY])
        f_s[...] = _forget_gates(u[:, HG_KEY:2 * HG_KEY], lb)[0]
        v_s[...] = u[:, 2 * HG_KEY:2 * HG_KEY + HG_VAL]
        g_s[...] = _silu(u[:, 2 * HG_KEY + HG_VAL:])

    rows = pl.ds(pl.multiple_of(i * bb, bb), bb)
    qb, fb, vb = q_s[rows, :], f_s[rows, :], v_s[rows, :]
    for j in range(bb):
        for h in range(HG_HEADS):
            sl = slice(h * HG_DK, (h + 1) * HG_DK)
            v_r = vb[j:j + 1, h * HG_DV:(h + 1) * HG_DV]
            s_new = _col(fb[j:j + 1, sl]) * (s_ref[0, j, h] - v_r) + v_r
            snew_ref[0, j, h] = s_new
            oblk_s[j:j + 1, h * HG_DV:(h + 1) * HG_DV] = _mm(qb[j:j + 1, sl], s_new)
    o_s[rows, :] = oblk_s[...]

    @pl.when(i == n_blocks - 1)
    def _():
        gn = gn_ref[0]
        gated = []
        for h in range(HG_HEADS):
            sl = slice(h * HG_DV, (h + 1) * HG_DV)
            gated.append(_rms_scale(o_s[:, sl], gn) * g_s[:, sl])
        h_new = h_s[...] + _mm(jnp.concatenate(gated, axis=1), wout_ref[0])
        h_s[...] = h_new

        @pl.when(l == n_layers - 1)
        def _():
            hout_ref[...] = h_new


def _whole(shape):
    return pl.BlockSpec(shape, lambda *_: (0,) * len(shape))


def _hgrn_decode(h, states, norms, w_in, lb_logits, gnorms, w_out):
    B, D = h.shape
    n_a = states.shape[0]
    bb = min(DEC_BLOCK, B)
    assert B % bb == 0
    per_layer = lambda *tail: pl.BlockSpec((1,) + tail, lambda l, i: (l,) + (0,) * len(tail))
    st_blk = pl.BlockSpec((1, bb, HG_HEADS, HG_DK, HG_DV), lambda l, i: (l, i, 0, 0, 0))
    wide = pltpu.VMEM((B, HG_KEY), F32)
    s_new, h_new = pl.pallas_call(
        _hgrn_decode_kernel,
        grid=(n_a, B // bb),
        in_specs=[_whole((B, D)), st_blk, per_layer(1, D), per_layer(*w_in.shape[1:]),
                  _whole(lb_logits.shape), per_layer(1, HG_DV), per_layer(*w_out.shape[1:])],
        out_specs=[st_blk, _whole((B, D))],
        out_shape=[jax.ShapeDtypeStruct(states.shape, F32), jax.ShapeDtypeStruct((B, D), F32)],
        scratch_shapes=[pltpu.VMEM((B, D), F32)] + [wide] * 5 + [pltpu.VMEM((bb, HG_VAL), F32)],
        compiler_params=_cparams("arbitrary", "arbitrary"),
        name="hgrn_decode",
    )(h, states, norms.reshape(n_a, 1, D), w_in, lb_logits, gnorms.reshape(n_a, 1, HG_DV), w_out)
    return h_new, s_new


def _bf16_round(x):
    return x.astype(BF16).astype(F32)


def _swa_decode_kernel(final, h_ref, ng_ref, win_ref, cos_ref, sup_ref, sdn_ref, kc_ref, vc_ref,
                       kn_ref, vn_ref, sinks_ref, wout_ref, fin_ref, out_ref, qp_s, g_s, os_s):
    i = pl.program_id(0)
    n_blocks = pl.num_programs(0)
    bb, W = kc_ref.shape[0], kc_ref.shape[1]
    B = h_ref.shape[0]
    lane = lax.broadcasted_iota(jnp.int32, (B, LANES), 1)

    @pl.when(i == 0)
    def _():
        u = _mm(_rms_scale(h_ref[...], ng_ref[...]), win_ref[...])
        q = _rope_cols(u[:, :ATT_QD], cos_ref[...], sup_ref[...], sdn_ref[...]) * (ATT_HD ** -0.5)
        g_s[...] = _silu(u[:, ATT_QD:])
        for head in range(ATT_QH):
            col = q[:, (head // 2) * LANES:(head // 2 + 1) * LANES]
            kvh = head // ATT_GROUP
            if head % 2 != kvh:
                col = pltpu.roll(col, ATT_HD, axis=1)
            qp_s[:, head, :] = jnp.where((lane >= ATT_HD) if kvh == 1 else (lane < ATT_HD), col, 0.0)

    rows = pl.ds(pl.multiple_of(i * bb, bb), bb)
    kn, vn = kn_ref[rows, :], vn_ref[rows, :]
    qp = [qp_s[i * bb + j] for j in range(bb)]
    s_c = jnp.concatenate([_mm_nt(qp[j], kc_ref[j]) for j in range(bb)], axis=0)
    qp_all = jnp.concatenate(qp, axis=0)
    expand = lambda a: jnp.concatenate(
        [jnp.broadcast_to(a[j:j + 1], (ATT_QH, ATT_KVD)) for j in range(bb)], axis=0)
    s_n = jnp.sum(_bf16_round(qp_all) * _bf16_round(expand(kn)), axis=-1, keepdims=True)
    sink = jnp.concatenate([sinks_ref[...]] * bb, axis=0)
    m = jnp.maximum(jnp.maximum(jnp.max(s_c, axis=-1, keepdims=True), s_n), sink)
    p_c = jnp.exp(s_c - m)
    p_n = jnp.exp(s_n - m)
    inv = 1.0 / (jnp.sum(p_c, axis=-1, keepdims=True) + p_n + jnp.exp(sink - m))
    p_c = p_c * inv
    o = jnp.concatenate([_mm(p_c[j * ATT_QH:(j + 1) * ATT_QH], vc_ref[j]) for j in range(bb)], axis=0)
    o = o + _bf16_round(p_n * inv) * _bf16_round(expand(vn))
    for j in range(bb):
        os_s[i * bb + j] = o[j * ATT_QH:(j + 1) * ATT_QH]

    @pl.when(i == n_blocks - 1)
    def _():
        cols = []
        for c in range(ATT_QD // LANES):
            halves = []
            for par in range(2):
                head = 2 * c + par
                t = os_s[:, head, :]
                if par != head // ATT_GROUP:
                    t = pltpu.roll(t, ATT_HD, axis=1)
                halves.append(t)
            cols.append(jnp.where(lane < ATT_HD, halves[0], halves[1]))
        y = h_ref[...] + _mm(jnp.concatenate(cols, axis=1) * g_s[...], wout_ref[...])
        if final:
            y = _rms_scale(y, fin_ref[...])
        out_ref[...] = y


def _swa_decode_layer(h, k_new, v_new, cache_k, cache_v, tables, norm_g, w_in, layer, sinks, w_out,
                      final_norm, final):
    B, D = h.shape
    W = cache_k.shape[1]
    bb = min(SWA_DEC_BLOCK, B)
    assert B % bb == 0
    cache_blk = pl.BlockSpec((bb, W, ATT_KVD), lambda i: (i, 0, 0))
    head_rows = pltpu.VMEM((B, ATT_QH, ATT_KVD), F32)
    return pl.pallas_call(
        functools.partial(_swa_decode_kernel, final),
        grid=(B // bb,),
        in_specs=[_whole((B, D)), _whole((1, D)), _layer_block(w_in, layer)] + [_whole((1, LANES))] * 3
                 + [cache_blk, cache_blk, _whole((B, ATT_KVD)), _whole((B, ATT_KVD)),
                    _whole((ATT_QH, 1)), _layer_block(w_out, layer), _whole((1, D))],
        out_specs=_whole((B, D)),
        out_shape=jax.ShapeDtypeStruct((B, D), F32),
        scratch_shapes=[head_rows, pltpu.VMEM((B, ATT_QD), F32), head_rows],
        compiler_params=_cparams("arbitrary"),
        name="swa_decode_final" if final else "swa_decode",
    )(h, norm_g.reshape(1, D), w_in, *tables, cache_k.reshape(B, W, ATT_KVD),
      cache_v.reshape(B, W, ATT_KVD), k_new, v_new, sinks.reshape(ATT_QH, 1), w_out,
      final_norm.reshape(1, D))


def kernel(x_prompt, x_sample, state_hgrn, cache_k_win, cache_v_win, a_norm, a_w_in, a_lb_logits,
           a_gnorm, a_w_out, kv_norm, w_kv, b_norm, b_w_in, b_sinks, b_w_out, final_norm):
    B, T, D = x_prompt.shape
    BD, TD, _ = x_sample.shape
    n_a = a_w_in.shape[0]
    n_b = b_w_in.shape[0]
    a_w_in_b, a_w_out_b = a_w_in.astype(BF16), a_w_out.astype(BF16)
    b_w_in_b, b_w_out_b = b_w_in.astype(BF16), b_w_out.astype(BF16)
    w_kv_b = w_kv.astype(BF16)

    tab_p = _rope_tables(np.arange(T))
    h = x_prompt
    st_p = []
    for l in range(n_a):
        kv = (kv_norm, w_kv_b, tab_p) if l == n_a - 1 else None
        h, s, *kv_p = _hgrn_prompt_layer(h, a_norm[l], a_w_in_b, a_lb_logits, l, a_gnorm[l],
                                         a_w_out_b, kv)
        st_p.append(s)
    k_p, v_p = kv_p
    for l in range(n_b):
        h = _swa_prompt_layer(h, k_p, v_p, tab_p, b_norm[l], b_w_in_b, l, b_sinks[l], b_w_out_b,
                              final_norm, l == n_b - 1)
    y_prompt = h
    w_keep = min(WINDOW, T)
    k_win = k_p[:, T - w_keep:].reshape(B, w_keep, ATT_KVH, ATT_HD)
    v_win = v_p[:, T - w_keep:].reshape(B, w_keep, ATT_KVH, ATT_HD)

    tab_s = _rope_tables(PAST_LEN + np.arange(TD))
    hs = x_sample.reshape(BD * TD, D)
    hs, st_s = _hgrn_decode(hs, state_hgrn, a_norm, a_w_in_b, a_lb_logits, a_gnorm, a_w_out_b)
    k_s, v_s = _shared_kv(hs.reshape(1, BD, D), kv_norm, w_kv_b,
                          tuple(jnp.broadcast_to(t, (BD, LANES)) for t in tab_s))
    k_s, v_s = k_s.reshape(BD, ATT_KVD), v_s.reshape(BD, ATT_KVD)
    for l in range(n_b):
        hs = _swa_decode_layer(hs, k_s, v_s, cache_k_win, cache_v_win, tab_s, b_norm[l], b_w_in_b, l,
                               b_sinks[l], b_w_out_b, final_norm, l == n_b - 1)
    y_sample = hs.reshape(BD, TD, D)

    return (y_prompt, y_sample, jnp.stack(st_p), st_s, k_win, v_win,
            k_s.reshape(BD, TD, ATT_KVH, ATT_HD), v_s.reshape(BD, TD, ATT_KVH, ATT_HD))
```

```python
import functools

import numpy as np
import jax
import jax.numpy as jnp
from jax import lax
from jax.experimental import pallas as pl
from jax.experimental.pallas import tpu as pltpu

F32 = jnp.float32
BF16 = jnp.bfloat16

D_MODEL = 1024
HG_HEADS = 8
HG_DK = 128
HG_DV = 128
HG_KEY = HG_HEADS * HG_DK
HG_VAL = HG_HEADS * HG_DV
HG_CHUNK = 64
HG_HALF = HG_CHUNK // 2
HG_SUB = 16
HG_MAX_HALF_DECAY = 60.0
ATT_HD = 64
ATT_QH = 16
ATT_KVH = 2
ATT_GROUP = ATT_QH // ATT_KVH
ATT_QD = ATT_QH * ATT_HD
ATT_KVD = ATT_KVH * ATT_HD
WINDOW = 128
ROPE_THETA = 500000.0
ROT_DIM = ATT_HD // 4
NORM_EPS = 1e-6
MASK_VALUE = -1e30
NEG_BIG = -1e30
LOG2E = 1.4426950408889634
PAST_LEN = 8192

SUBLANES = 8
LANES = 128
PROMPT_TILE = 512
SWA_TILE = 1024
DEC_BLOCK = 8
SWA_DEC_BLOCK = 16
V7X_VMEM_BYTES = 64 * 1024 * 1024
VMEM_LIMIT = V7X_VMEM_BYTES * 3 // 4


def _cparams(*sem):
    return pltpu.CompilerParams(dimension_semantics=sem, vmem_limit_bytes=VMEM_LIMIT)


def _layer_block(stacked, layer, single_buffer=False):
    tail = stacked.shape[1:]
    mode = dict(pipeline_mode=pl.Buffered(1)) if single_buffer else {}
    return pl.BlockSpec((None,) + tail, lambda *_: (layer,) + (0,) * len(tail), **mode)


def _mm(a, b):
    return jnp.dot(a.astype(BF16), b.astype(BF16), preferred_element_type=F32)


def _mm_nt(a, b):
    return lax.dot_general(a.astype(BF16), b.astype(BF16), (((1,), (1,)), ((), ())),
                           preferred_element_type=F32)


def _mm_tn(a, b):
    return lax.dot_general(a.astype(BF16), b.astype(BF16), (((0,), (0,)), ((), ())),
                           preferred_element_type=F32)


def _rms_scale(x, g):
    ms = jnp.mean(x * x, axis=-1, keepdims=True)
    return x * lax.rsqrt(ms + NORM_EPS) * g


def _silu(x):
    return x * (0.5 + 0.5 * jnp.tanh(0.5 * x))


def _lower_bound(lbl, layer):
    n = lbl.shape[0]
    rows = [lbl[i:i + 1, :] for i in range(n)]
    m = functools.reduce(jnp.maximum, rows)
    es = [jnp.exp(r - m) for r in rows]
    inv = 1.0 / functools.reduce(jnp.add, es)
    ps = [e * inv for e in es]
    return functools.reduce(jnp.add, ps[:layer + 1]) - ps[0]


def _forget_gates(f, lb):
    e = jnp.exp(-f)
    r = 1.0 / (1.0 + e)
    return lb + (1.0 - lb) * r, (1.0 - lb) * (e * r)


def _decay(x):
    return jnp.exp(x)


def _chunk_cumsum(x):
    sub = lax.broadcasted_iota(jnp.int32, (SUBLANES, x.shape[1]), 0)
    out = []
    for r0 in range(0, x.shape[0], SUBLANES):
        t = x[r0:r0 + SUBLANES]
        s = 1
        while s < SUBLANES:
            t = t + jnp.where(sub >= s, pltpu.roll(t, s, axis=0), 0.0)
            s *= 2
        if r0 % HG_CHUNK:
            t = t + jnp.broadcast_to(out[-1][SUBLANES - 1:], t.shape)
        out.append(t)
    return jnp.concatenate(out, axis=0)


def _diag_blocks(q, k, v, b):
    sub_iota = lax.broadcasted_iota(jnp.int32, (SUBLANES, LANES), 0)
    outs = []
    for blk in range(HG_CHUNK // HG_SUB):
        base = blk * HG_SUB
        groups = HG_SUB // SUBLANES
        qs = [q[base + g * SUBLANES: base + (g + 1) * SUBLANES] for g in range(groups)]
        bs = [b[base + g * SUBLANES: base + (g + 1) * SUBLANES] for g in range(groups)]
        accs = [jnp.zeros((SUBLANES, HG_DV), F32) for _ in range(groups)]
        for s in range(HG_SUB):
            row = base + s
            k_s = k[row:row + 1]
            b_s = b[row:row + 1]
            v_s = v[row:row + 1]
            for g in range(s // SUBLANES, groups):
                d = bs[g] - b_s
                if g == s // SUBLANES:
                    d = jnp.where(sub_iota >= (s % SUBLANES), d, NEG_BIG)
                p = qs[g] * (k_s * _decay(d))
                a = jnp.sum(p, axis=-1, keepdims=True)
                accs[g] = accs[g] + a * v_s
        outs.extend(accs)
    return jnp.concatenate(outs, axis=0)


def _hgrn_chunk(q, k, v, b, st):
    C, H = HG_CHUNK, HG_HALF
    o_inter = _mm_nt(q * _decay(b), st)
    r0 = b[H - 1:H]
    a0 = _mm_nt(q[H:] * _decay(b[H:] - r0), k[:H] * _decay(r0 - b[:H]))
    o_hi = _mm(a0, v[:H])
    Q = HG_SUB
    o_q = []
    for base in (0, H):
        r1 = b[base + Q - 1: base + Q]
        a1 = _mm_nt(q[base + Q: base + 2 * Q] * _decay(b[base + Q: base + 2 * Q] - r1),
                    k[base: base + Q] * _decay(r1 - b[base: base + Q]))
        o_q.append(_mm(a1, v[base: base + Q]))
    zero = jnp.zeros((Q, HG_DV), F32)
    o_off = jnp.concatenate([zero, o_q[0], o_hi[:Q], o_hi[Q:] + o_q[1]], axis=0)
    o = o_inter + o_off + _diag_blocks(q, k, v, b)
    b_end = b[C - 1:C]
    st_new = st * _decay(b_end) + _mm_tn(v, k * _decay(b_end - b))
    return o, st_new


def _hgrn_bounded_first(q, k, v, b, st):
    C, H = HG_CHUNK, HG_HALF
    r = b[H - 1:H]
    eq = _decay(b - r)
    qt, kt = q * eq, k * _decay(r - b)
    e_mid, e_hi = _decay(r), eq[C - 1:C]
    scores = _mm_nt(qt, kt)
    v_t = v.T
    st_new = st * (e_mid * e_hi) + _mm(v_t, kt * e_hi)
    return scores, qt * e_mid, v_t, st_new


def _hgrn_bounded_second(scores, q_dec, v_t, st):
    t_i = lax.broadcasted_iota(jnp.int32, scores.shape, 0)
    s_i = lax.broadcasted_iota(jnp.int32, scores.shape, 1)
    a = jnp.where(s_i <= t_i, scores, 0.0)
    return _mm_nt(jnp.concatenate([q_dec, a], axis=1), jnp.concatenate([st, v_t], axis=1))


def _max_half_decay(b_s, tile):
    n = tile // HG_CHUNK
    worst = None
    for h in range(HG_HEADS):
        mid = b_s[h, pl.ds(HG_HALF - 1, n, stride=HG_CHUNK), :]
        end = b_s[h, pl.ds(HG_CHUNK - 1, n, stride=HG_CHUNK), :]
        w = jnp.maximum(-mid, mid - end)
        worst = w if worst is None else jnp.maximum(worst, w)
    return jnp.max(worst)


def _hgrn_prompt_kernel(layer, with_kv, *refs):
    h_ref, ng_ref, win_ref, lbl_ref, gn_ref, wout_ref = refs[:6]
    refs = refs[6:]
    if with_kv:
        kv_in, refs = refs[:5], refs[5:]
    out_ref, sfin_ref = refs[:2]
    refs = refs[2:]
    if with_kv:
        kv_out, refs = refs[:2], refs[2:]
    q_s, k_s, v_s, b_s, o_s, st_s = refs
    t = pl.program_id(1)
    nt = pl.num_programs(1)
    tile = h_ref.shape[1]

    @pl.when(t == 0)
    def _():
        st_s[...] = jnp.zeros_like(st_s)

    x = h_ref[0]
    u = _mm(_rms_scale(x, ng_ref[...]), win_ref[...])
    lb = _lower_bound(lbl_ref[...], layer)
    for h in range(HG_HEADS):
        sl = slice(h * HG_DK, (h + 1) * HG_DK)
        q_s[h] = _silu(u[:, sl])
        fg, kin = _forget_gates(u[:, HG_KEY + h * HG_DK: HG_KEY + (h + 1) * HG_DK], lb[:, sl])
        k_s[h] = kin
        b_s[h] = _chunk_cumsum(jnp.log(fg))
        v_s[h] = u[:, 2 * HG_KEY + h * HG_DV: 2 * HG_KEY + (h + 1) * HG_DV]

    def chunk_rows(c):
        return pl.ds(pl.multiple_of(c * HG_CHUNK, HG_CHUNK), HG_CHUNK)

    def load(c, h):
        rows = chunk_rows(c)
        return q_s[h, rows, :], k_s[h, rows, :], v_s[h, rows, :], b_s[h, rows, :], st_s[h]

    def bounded_chunks():
        pending = []
        for c in range(tile // HG_CHUNK):
            rows = slice(c * HG_CHUNK, (c + 1) * HG_CHUNK)
            started = []
            for h in range(HG_HEADS):
                st = st_s[h]
                scores, q_dec, v_t, st_new = _hgrn_bounded_first(
                    q_s[h, rows, :], k_s[h, rows, :], v_s[h, rows, :], b_s[h, rows, :], st)
                st_s[h] = st_new
                started.append((h, rows, scores, q_dec, v_t, st))
            for h, prev_rows, *second in pending:
                o_s[h, prev_rows, :] = _hgrn_bounded_second(*second)
            pending = started
        for h, prev_rows, *second in pending:
            o_s[h, prev_rows, :] = _hgrn_bounded_second(*second)

    def general_body(i, carry):
        c, h = i // HG_HEADS, i % HG_HEADS
        o, st_new = _hgrn_chunk(*load(c, h))
        o_s[h, chunk_rows(c), :] = o
        st_s[h] = st_new
        return carry

    bounded = _max_half_decay(b_s, tile) < HG_MAX_HALF_DECAY

    pl.when(bounded)(bounded_chunks)

    @pl.when(jnp.logical_not(bounded))
    def _():
        lax.fori_loop(0, (tile // HG_CHUNK) * HG_HEADS, general_body, 0)

    gn = gn_ref[...]
    gated = []
    for h in range(HG_HEADS):
        g = u[:, 2 * HG_KEY + HG_VAL + h * HG_DV: 2 * HG_KEY + HG_VAL + (h + 1) * HG_DV]
        gated.append(_rms_scale(o_s[h], gn) * _silu(g))
    y = x + _mm(jnp.concatenate(gated, axis=1), wout_ref[...])
    out_ref[0] = y
    if with_kv:
        _kv_project(y, *kv_in, *kv_out)

    @pl.when(t == nt - 1)
    def _():
        for h in range(HG_HEADS):
            sfin_ref[0, h] = st_s[h].T


def _hgrn_prompt_layer(h, norm_g, w_in, lb_logits, layer, gnorm, w_out, kv=None):
    B, T, D = h.shape
    tile = min(PROMPT_TILE, T)
    assert T % tile == 0 and tile % HG_CHUNK == 0 and HG_DK == HG_DV == LANES
    n_a = lb_logits.shape[0]
    const = lambda b, t: (0, 0)
    row_tile = lambda width: pl.BlockSpec((1, tile, width), lambda b, t: (b, t, 0))
    head_scratch = pltpu.VMEM((HG_HEADS, tile, HG_DK), F32)
    operands = [h, norm_g.reshape(1, D), w_in, lb_logits, gnorm.reshape(1, HG_DV), w_out]
    in_specs = [row_tile(D), pl.BlockSpec((1, D), const), _layer_block(w_in, layer, True),
                pl.BlockSpec((n_a, HG_KEY), const), pl.BlockSpec((1, HG_DV), const),
                _layer_block(w_out, layer, True)]
    out_specs = [row_tile(D), pl.BlockSpec((1, HG_HEADS, HG_DK, HG_DV), lambda b, t: (b, 0, 0, 0))]
    out_shape = [jax.ShapeDtypeStruct((B, T, D), F32),
                 jax.ShapeDtypeStruct((B, HG_HEADS, HG_DK, HG_DV), F32)]
    if kv is not None:
        kv_norm, w_kv, tables = kv
        operands += [kv_norm.reshape(1, D), w_kv, *tables]
        in_specs += [pl.BlockSpec((1, D), const), pl.BlockSpec(w_kv.shape, const)]
        in_specs += [pl.BlockSpec((tile, LANES), lambda b, t: (t, 0))] * 3
        out_specs += [row_tile(ATT_KVD)] * 2
        out_shape += [jax.ShapeDtypeStruct((B, T, ATT_KVD), F32)] * 2
    return pl.pallas_call(
        functools.partial(_hgrn_prompt_kernel, layer, kv is not None),
        grid=(B, T // tile),
        in_specs=in_specs,
        out_specs=out_specs,
        out_shape=out_shape,
        scratch_shapes=[head_scratch] * 5 + [pltpu.VMEM((HG_HEADS, HG_DV, HG_DK), F32)],
        compiler_params=_cparams("arbitrary", "arbitrary"),
        name=f"hgrn_prompt_l{layer}",
    )(*operands)


def _rope_tables(pos):
    half = ROT_DIM // 2
    pos = np.asarray(pos, np.float64)
    inv_freq = 1.0 / (ROPE_THETA ** (np.arange(half, dtype=np.float64) * 2.0 / ROT_DIM))
    ang = (pos[:, None].astype(np.float32) * inv_freq[None, :].astype(np.float32)).astype(np.float64)
    cos, sin = np.cos(ang), np.sin(ang)
    n = pos.shape[0]
    c = np.ones((n, ATT_HD)); s_up = np.zeros((n, ATT_HD)); s_dn = np.zeros((n, ATT_HD))
    c[:, :half] = cos; c[:, half:ROT_DIM] = cos
    s_dn[:, :half] = -sin
    s_up[:, half:ROT_DIM] = sin
    rep = LANES // ATT_HD
    tab = [np.tile(a, (1, rep)).astype(np.float32) for a in (c, s_up, s_dn)]
    return tuple(jnp.asarray(a) for a in tab)


def _rope_cols(x, cos, s_up, s_dn):
    half = ROT_DIM // 2
    cols = []
    for c in range(x.shape[1] // LANES):
        xc = x[:, c * LANES:(c + 1) * LANES]
        cols.append(xc * cos + pltpu.roll(xc, half, axis=1) * s_up
                    + pltpu.roll(xc, LANES - half, axis=1) * s_dn)
    return cols[0] if len(cols) == 1 else jnp.concatenate(cols, axis=1)


def _kv_kernel(h_ref, *refs):
    _kv_project(h_ref[0], *refs)


def _kv_project(x, ng_ref, w_ref, cos_ref, sup_ref, sdn_ref, k_ref, v_ref):
    u = _mm(_rms_scale(x, ng_ref[...]), w_ref[...])
    k_ref[0] = _rope_cols(u[:, :ATT_KVD], cos_ref[...], sup_ref[...], sdn_ref[...])
    v_ref[0] = u[:, ATT_KVD:]


def _shared_kv(h, kv_norm, w_kv, tables):
    B, T, D = h.shape
    tile = min(PROMPT_TILE, T)
    const = lambda b, t: (0, 0)
    tab = pl.BlockSpec((tile, LANES), lambda b, t: (t, 0))
    kv_spec = pl.BlockSpec((1, tile, ATT_KVD), lambda b, t: (b, t, 0))
    return pl.pallas_call(
        _kv_kernel,
        grid=(B, T // tile),
        in_specs=[pl.BlockSpec((1, tile, D), lambda b, t: (b, t, 0)),
                  pl.BlockSpec((1, D), const), pl.BlockSpec(w_kv.shape, const), tab, tab, tab],
        out_specs=[kv_spec, kv_spec],
        out_shape=[jax.ShapeDtypeStruct((B, T, ATT_KVD), F32)] * 2,
        compiler_params=_cparams("arbitrary", "arbitrary"),
        name="shared_kv",
    )(h, kv_norm.reshape(1, D), w_kv, *tables)


PAIRS_PER_KV = ATT_GROUP // 2


def _swa_attention(blocks, sinks_ref, side_product):
    groups = [(blk, kvh) for blk in range(len(blocks)) for kvh in range(ATT_KVH)]
    n_keys = blocks[0][4].shape[0]
    ones_rows = jnp.ones((SUBLANES, n_keys), F32)

    def score_products(g):
        blk, kvh = groups[g]
        q_cols, k_lo, k_hi, _, bias_t = blocks[blk]
        qs = jnp.concatenate(q_cols[kvh * PAIRS_PER_KV:(kvh + 1) * PAIRS_PER_KV], axis=0)
        keys = jnp.concatenate([k_lo[kvh], k_hi[kvh]], axis=0)
        bias = jnp.concatenate([jnp.concatenate([bias_t] * PAIRS_PER_KV, axis=1)] * 2, axis=0)
        return _mm_nt(keys, qs) + bias

    def softmax_terms(g, s_t):
        kvh = groups[g][1]
        cols = range(kvh * PAIRS_PER_KV, (kvh + 1) * PAIRS_PER_KV)
        terms = []
        for par in range(2):
            s = s_t[par * n_keys:(par + 1) * n_keys]
            sink = jnp.concatenate(
                [jnp.full((1, WINDOW), sinks_ref[0, 2 * c + par] * LOG2E, F32) for c in cols], axis=1)
            m = jnp.maximum(jnp.max(s, axis=0, keepdims=True), sink)
            terms.append((jnp.exp2(s - m), jnp.exp2(sink - m)))
        return terms

    def value_products(g, terms):
        blk, kvh = groups[g]
        v_t = blocks[blk][3]
        lhs = jnp.concatenate([v_t[kvh * ATT_HD:(kvh + 1) * ATT_HD], ones_rows], axis=0)
        halves = []
        for p, sink_term in terms:
            r = _mm(lhs, p)
            halves.append(r[:ATT_HD] * (1.0 / (r[ATT_HD:ATT_HD + 1] + sink_term)))
        return halves

    scores = [score_products(g) for g in range(len(groups))]
    side = side_product()
    terms = [softmax_terms(g, s) for g, s in enumerate(scores)]
    outs = [value_products(g, t) for g, t in enumerate(terms)]
    res = []
    for b in range(len(blocks)):
        cols = []
        for kvh in range(ATT_KVH):
            even, odd = outs[b * ATT_KVH + kvh]
            for i in range(PAIRS_PER_KV):
                queries = slice(i * WINDOW, (i + 1) * WINDOW)
                cols.append(jnp.concatenate([even[:, queries], odd[:, queries]], axis=0).T)
        res.append(jnp.concatenate(cols, axis=1))
    return res, side


def _split_kv_heads(a, fill):
    lane = lax.broadcasted_iota(jnp.int32, a.shape, 1)
    low = lane < ATT_HD
    sw = pltpu.roll(a, ATT_HD, axis=1)
    lo = [jnp.where(low, a, fill), jnp.where(low, sw, fill)]
    hi = [jnp.where(low, fill, sw), jnp.where(low, fill, a)]
    return lo, hi


def _swa_prompt_kernel(final, h_ref, ng_ref, win_ref, sinks_ref, wout_ref, fin_ref,
                       kp_ref, kc_ref, vp_ref, vc_ref, cos_ref, sup_ref, sdn_ref, out_ref):
    t = pl.program_id(1)
    tile = h_ref.shape[1]
    nblk = tile // WINDOW
    x = h_ref[0]
    xn = _rms_scale(x, ng_ref[...]).astype(BF16)
    u = _mm(xn, win_ref[:, :ATT_QD])
    scale = ATT_HD ** -0.5 * LOG2E
    cos, sup, sdn = cos_ref[...], sup_ref[...], sdn_ref[...]
    k_all = jnp.concatenate([kp_ref[0], kc_ref[0]], axis=0)
    v_all = jnp.concatenate([vp_ref[0], vc_ref[0]], axis=0)
    j = lax.broadcasted_iota(jnp.int32, (2 * WINDOW, WINDOW), 0)
    tq = lax.broadcasted_iota(jnp.int32, (2 * WINDOW, WINDOW), 1)
    rel = tq + WINDOW - j
    band = (rel >= 0) & (rel <= WINDOW)
    blocks = []
    for blk in range(nblk):
        rows = slice(blk * WINDOW, (blk + 1) * WINDOW)
        keys = slice(blk * WINDOW, (blk + 2) * WINDOW)
        k_lo, k_hi = _split_kv_heads(k_all[keys], 0.0)
        if blk == 0:
            valid = band & (j >= WINDOW * (t == 0).astype(jnp.int32))
        else:
            valid = band
        q_cols = []
        for c in range(ATT_QD // LANES):
            qc = u[rows, c * LANES:(c + 1) * LANES]
            q_cols.append(_rope_cols(qc, cos[rows], sup[rows], sdn[rows]) * scale)
        blocks.append((q_cols, k_lo, k_hi, v_all[keys].T, jnp.where(valid, 0.0, MASK_VALUE)))
    o_blocks, gate = _swa_attention(blocks, sinks_ref, lambda: _mm(xn, win_ref[:, ATT_QD:]))
    o = jnp.concatenate(o_blocks, axis=0) if nblk > 1 else o_blocks[0]
    y = x + _mm(o * _silu(gate), wout_ref[...])
    if final:
        y = _rms_scale(y, fin_ref[...])
    out_ref[0] = y


def _swa_prompt_layer(h, k, v, tables, norm_g, w_in, layer, sinks, w_out, final_norm, final):
    B, T, D = h.shape
    tile = min(SWA_TILE, T)
    assert T % tile == 0 and tile % WINDOW == 0
    per = tile // WINDOW
    const = lambda b, t: (0, 0)
    tab = pl.BlockSpec((tile, LANES), lambda b, t: (t, 0))
    prev = pl.BlockSpec((1, WINDOW, ATT_KVD), lambda b, t: (b, jnp.maximum(t * per - 1, 0), 0))
    cur = pl.BlockSpec((1, tile, ATT_KVD), lambda b, t: (b, t, 0))
    return pl.pallas_call(
        functools.partial(_swa_prompt_kernel, final),
        grid=(B, T // tile),
        in_specs=[
            pl.BlockSpec((1, tile, D), lambda b, t: (b, t, 0)),
            pl.BlockSpec((1, D), const),
            _layer_block(w_in, layer, True),
            pl.BlockSpec(memory_space=pltpu.SMEM),
            _layer_block(w_out, layer, True),
            pl.BlockSpec((1, D), const),
            prev, cur, prev, cur, tab, tab, tab,
        ],
        out_specs=pl.BlockSpec((1, tile, D), lambda b, t: (b, t, 0)),
        out_shape=jax.ShapeDtypeStruct((B, T, D), F32),
        compiler_params=_cparams("arbitrary", "arbitrary"),
        name="swa_prompt_final" if final else "swa_prompt",
    )(h, norm_g.reshape(1, D), w_in, sinks.reshape(1, ATT_QH), w_out, final_norm.reshape(1, D),
      k, k, v, v, *tables)


def _col(row):
    n = row.shape[1]
    return jnp.broadcast_to(row, (n, n)).T


def _hgrn_decode_kernel(x_ref, s_ref, ng_ref, win_ref, lbl_ref, gn_ref, wout_ref,
                        snew_ref, hout_ref, h_s, q_s, f_s, v_s, g_s, o_s, oblk_s):
    l, i = pl.program_id(0), pl.program_id(1)
    n_layers, n_blocks = pl.num_programs(0), pl.num_programs(1)
    bb = s_ref.shape[1]

    @pl.when(i == 0)
    def _():
        @pl.when(l == 0)
        def _():
            h_s[...] = x_ref[...]

        u = _mm(_rms_scale(h_s[...], ng_ref[0]), win_ref[0])
        lbl = lbl_ref[...]
        lb = jnp.zeros_like(lbl[:1])
        for layer in range(lbl.shape[0]):
            lb = jnp.where(l == layer, _lower_bound(lbl, layer), lb)
        q_s[...] = _silu(u[:, :HG_KEY])
        f_s[...] = _forget_gates(u[:, HG_KEY:2 * HG_KEY], lb)[0]
        v_s[...] = u[:, 2 * HG_KEY:2 * HG_KEY + HG_VAL]
        g_s[...] = _silu(u[:, 2 * HG_KEY + HG_VAL:])

    rows = pl.ds(pl.multiple_of(i * bb, bb), bb)
    qb, fb, vb = q_s[rows, :], f_s[rows, :], v_s[rows, :]
    for j in range(bb):
        for h in range(HG_HEADS):
            sl = slice(h * HG_DK, (h + 1) * HG_DK)
            v_r = vb[j:j + 1, h * HG_DV:(h + 1) * HG_DV]
            s_new = _col(fb[j:j + 1, sl]) * (s_ref[0, j, h] - v_r) + v_r
            snew_ref[0, j, h] = s_new
            oblk_s[j:j + 1, h * HG_DV:(h + 1) * HG_DV] = _mm(qb[j:j + 1, sl], s_new)
    o_s[rows, :] = oblk_s[...]

    @pl.when(i == n_blocks - 1)
    def _():
        gn = gn_ref[0]
        gated = []
        for h in range(HG_HEADS):
            sl = slice(h * HG_DV, (h + 1) * HG_DV)
            gated.append(_rms_scale(o_s[:, sl], gn) * g_s[:, sl])
        h_new = h_s[...] + _mm(jnp.concatenate(gated, axis=1), wout_ref[0])
        h_s[...] = h_new

        @pl.when(l == n_layers - 1)
        def _():
            hout_ref[...] = h_new


def _whole(shape):
    return pl.BlockSpec(shape, lambda *_: (0,) * len(shape))


def _hgrn_decode(h, states, norms, w_in, lb_logits, gnorms, w_out):
    B, D = h.shape
    n_a = states.shape[0]
    bb = min(DEC_BLOCK, B)
    assert B % bb == 0
    per_layer = lambda *tail: pl.BlockSpec((1,) + tail, lambda l, i: (l,) + (0,) * len(tail))
    st_blk = pl.BlockSpec((1, bb, HG_HEADS, HG_DK, HG_DV), lambda l, i: (l, i, 0, 0, 0))
    wide = pltpu.VMEM((B, HG_KEY), F32)
    s_new, h_new = pl.pallas_call(
        _hgrn_decode_kernel,
        grid=(n_a, B // bb),
        in_specs=[_whole((B, D)), st_blk, per_layer(1, D), per_layer(*w_in.shape[1:]),
                  _whole(lb_logits.shape), per_layer(1, HG_DV), per_layer(*w_out.shape[1:])],
        out_specs=[st_blk, _whole((B, D))],
        out_shape=[jax.ShapeDtypeStruct(states.shape, F32), jax.ShapeDtypeStruct((B, D), F32)],
        scratch_shapes=[pltpu.VMEM((B, D), F32)] + [wide] * 5 + [pltpu.VMEM((bb, HG_VAL), F32)],
        compiler_params=_cparams("arbitrary", "arbitrary"),
        name="hgrn_decode",
    )(h, states, norms.reshape(n_a, 1, D), w_in, lb_logits, gnorms.reshape(n_a, 1, HG_DV), w_out)
    return h_new, s_new


def _bf16_round(x):
    return x.astype(BF16).astype(F32)


def _swa_decode_kernel(final, h_ref, ng_ref, win_ref, cos_ref, sup_ref, sdn_ref, kc_ref, vc_ref,
                       kn_ref, vn_ref, sinks_ref, wout_ref, fin_ref, out_ref, qp_s, g_s, os_s):
    i = pl.program_id(0)
    n_blocks = pl.num_programs(0)
    bb, W = kc_ref.shape[0], kc_ref.shape[1]
    B = h_ref.shape[0]
    lane = lax.broadcasted_iota(jnp.int32, (B, LANES), 1)

    @pl.when(i == 0)
    def _():
        u = _mm(_rms_scale(h_ref[...], ng_ref[...]), win_ref[...])
        q = _rope_cols(u[:, :ATT_QD], cos_ref[...], sup_ref[...], sdn_ref[...]) * (ATT_HD ** -0.5)
        g_s[...] = _silu(u[:, ATT_QD:])
        for head in range(ATT_QH):
            col = q[:, (head // 2) * LANES:(head // 2 + 1) * LANES]
            kvh = head // ATT_GROUP
            if head % 2 != kvh:
                col = pltpu.roll(col, ATT_HD, axis=1)
            qp_s[:, head, :] = jnp.where((lane >= ATT_HD) if kvh == 1 else (lane < ATT_HD), col, 0.0)

    rows = pl.ds(pl.multiple_of(i * bb, bb), bb)
    kn, vn = kn_ref[rows, :], vn_ref[rows, :]
    qp = [qp_s[i * bb + j] for j in range(bb)]
    s_c = jnp.concatenate([_mm_nt(qp[j], kc_ref[j]) for j in range(bb)], axis=0)
    qp_all = jnp.concatenate(qp, axis=0)
    expand = lambda a: jnp.concatenate(
        [jnp.broadcast_to(a[j:j + 1], (ATT_QH, ATT_KVD)) for j in range(bb)], axis=0)
    s_n = jnp.sum(_bf16_round(qp_all) * _bf16_round(expand(kn)), axis=-1, keepdims=True)
    sink = jnp.concatenate([sinks_ref[...]] * bb, axis=0)
    m = jnp.maximum(jnp.maximum(jnp.max(s_c, axis=-1, keepdims=True), s_n), sink)
    p_c = jnp.exp(s_c - m)
    p_n = jnp.exp(s_n - m)
    inv = 1.0 / (jnp.sum(p_c, axis=-1, keepdims=True) + p_n + jnp.exp(sink - m))
    p_c = p_c * inv
    o = jnp.concatenate([_mm(p_c[j * ATT_QH:(j + 1) * ATT_QH], vc_ref[j]) for j in range(bb)], axis=0)
    o = o + _bf16_round(p_n * inv) * _bf16_round(expand(vn))
    for j in range(bb):
        os_s[i * bb + j] = o[j * ATT_QH:(j + 1) * ATT_QH]

    @pl.when(i == n_blocks - 1)
    def _():
        cols = []
        for c in range(ATT_QD // LANES):
            halves = []
            for par in range(2):
                head = 2 * c + par
                t = os_s[:, head, :]
                if par != head // ATT_GROUP:
                    t = pltpu.roll(t, ATT_HD, axis=1)
                halves.append(t)
            cols.append(jnp.where(lane < ATT_HD, halves[0], halves[1]))
        y = h_ref[...] + _mm(jnp.concatenate(cols, axis=1) * g_s[...], wout_ref[...])
        if final:
            y = _rms_scale(y, fin_ref[...])
        out_ref[...] = y


def _swa_decode_layer(h, k_new, v_new, cache_k, cache_v, tables, norm_g, w_in, layer, sinks, w_out,
                      final_norm, final):
    B, D = h.shape
    W = cache_k.shape[1]
    bb = min(SWA_DEC_BLOCK, B)
    assert B % bb == 0
    cache_blk = pl.BlockSpec((bb, W, ATT_KVD), lambda i: (i, 0, 0))
    head_rows = pltpu.VMEM((B, ATT_QH, ATT_KVD), F32)
    return pl.pallas_call(
        functools.partial(_swa_decode_kernel, final),
        grid=(B // bb,),
        in_specs=[_whole((B, D)), _whole((1, D)), _layer_block(w_in, layer)] + [_whole((1, LANES))] * 3
                 + [cache_blk, cache_blk, _whole((B, ATT_KVD)), _whole((B, ATT_KVD)),
                    _whole((ATT_QH, 1)), _layer_block(w_out, layer), _whole((1, D))],
        out_specs=_whole((B, D)),
        out_shape=jax.ShapeDtypeStruct((B, D), F32),
        scratch_shapes=[head_rows, pltpu.VMEM((B, ATT_QD), F32), head_rows],
        compiler_params=_cparams("arbitrary"),
        name="swa_decode_final" if final else "swa_decode",
    )(h, norm_g.reshape(1, D), w_in, *tables, cache_k.reshape(B, W, ATT_KVD),
      cache_v.reshape(B, W, ATT_KVD), k_new, v_new, sinks.reshape(ATT_QH, 1), w_out,
      final_norm.reshape(1, D))


def kernel(x_prompt, x_sample, state_hgrn, cache_k_win, cache_v_win, a_norm, a_w_in, a_lb_logits,
           a_gnorm, a_w_out, kv_norm, w_kv, b_norm, b_w_in, b_sinks, b_w_out, final_norm):
    B, T, D = x_prompt.shape
    BD, TD, _ = x_sample.shape
    n_a = a_w_in.shape[0]
    n_b = b_w_in.shape[0]
    a_w_in_b, a_w_out_b = a_w_in.astype(BF16), a_w_out.astype(BF16)
    b_w_in_b, b_w_out_b = b_w_in.astype(BF16), b_w_out.astype(BF16)
    w_kv_b = w_kv.astype(BF16)

    tab_p = _rope_tables(np.arange(T))
    h = x_prompt
    st_p = []
    for l in range(n_a):
        kv = (kv_norm, w_kv_b, tab_p) if l == n_a - 1 else None
        h, s, *kv_p = _hgrn_prompt_layer(h, a_norm[l], a_w_in_b, a_lb_logits, l, a_gnorm[l],
                                         a_w_out_b, kv)
        st_p.append(s)
    k_p, v_p = kv_p
    for l in range(n_b):
        h = _swa_prompt_layer(h, k_p, v_p, tab_p, b_norm[l], b_w_in_b, l, b_sinks[l], b_w_out_b,
                              final_norm, l == n_b - 1)
    y_prompt = h
    w_keep = min(WINDOW, T)
    k_win = k_p[:, T - w_keep:].reshape(B, w_keep, ATT_KVH, ATT_HD)
    v_win = v_p[:, T - w_keep:].reshape(B, w_keep, ATT_KVH, ATT_HD)

    tab_s = _rope_tables(PAST_LEN + np.arange(TD))
    hs = x_sample.reshape(BD * TD, D)
    hs, st_s = _hgrn_decode(hs, state_hgrn, a_norm, a_w_in_b, a_lb_logits, a_gnorm, a_w_out_b)
    k_s, v_s = _shared_kv(hs.reshape(1, BD, D), kv_norm, w_kv_b,
                          tuple(jnp.broadcast_to(t, (BD, LANES)) for t in tab_s))
    k_s, v_s = k_s.reshape(BD, ATT_KVD), v_s.reshape(BD, ATT_KVD)
    for l in range(n_b):
        hs = _swa_decode_layer(hs, k_s, v_s, cache_k_win, cache_v_win, tab_s, b_norm[l], b_w_in_b, l,
                               b_sinks[l], b_w_out_b, final_norm, l == n_b - 1)
    y_sample = hs.reshape(BD, TD, D)

    return (y_prompt, y_sample, jnp.stack(st_p), st_s, k_win, v_win,
            k_s.reshape(BD, TD, ATT_KVH, ATT_HD), v_s.reshape(BD, TD, ATT_KVH, ATT_HD))
```

```python
import functools

import numpy as np
import jax
import jax.numpy as jnp
from jax import lax
from jax.experimental import pallas as pl
from jax.experimental.pallas import tpu as pltpu

F32 = jnp.float32
BF16 = jnp.bfloat16

D_MODEL = 1024
HG_HEADS = 8
HG_DK = 128
HG_DV = 128
HG_KEY = HG_HEADS * HG_DK
HG_VAL = HG_HEADS * HG_DV
HG_CHUNK = 64
HG_HALF = HG_CHUNK // 2
HG_SUB = 16
HG_MAX_HALF_DECAY = 60.0
ATT_HD = 64
ATT_QH = 16
ATT_KVH = 2
ATT_GROUP = ATT_QH // ATT_KVH
ATT_QD = ATT_QH * ATT_HD
ATT_KVD = ATT_KVH * ATT_HD
WINDOW = 128
ROPE_THETA = 500000.0
ROT_DIM = ATT_HD // 4
NORM_EPS = 1e-6
MASK_VALUE = -1e30
NEG_BIG = -1e30
LOG2E = 1.4426950408889634
PAST_LEN = 8192

SUBLANES = 8
LANES = 128
PROMPT_TILE = 512
SWA_TILE = 1024
DEC_BLOCK = 8
SWA_DEC_BLOCK = 16
V7X_VMEM_BYTES = 64 * 1024 * 1024
VMEM_LIMIT = V7X_VMEM_BYTES * 3 // 4


def _cparams(*sem):
    return pltpu.CompilerParams(dimension_semantics=sem, vmem_limit_bytes=VMEM_LIMIT)


def _layer_block(stacked, layer, single_buffer=False):
    tail = stacked.shape[1:]
    mode = dict(pipeline_mode=pl.Buffered(1)) if single_buffer else {}
    return pl.BlockSpec((None,) + tail, lambda *_: (layer,) + (0,) * len(tail), **mode)


def _mm(a, b):
    return jnp.dot(a.astype(BF16), b.astype(BF16), preferred_element_type=F32)


def _mm_nt(a, b):
    return lax.dot_general(a.astype(BF16), b.astype(BF16), (((1,), (1,)), ((), ())),
                           preferred_element_type=F32)


def _mm_tn(a, b):
    return lax.dot_general(a.astype(BF16), b.astype(BF16), (((0,), (0,)), ((), ())),
                           preferred_element_type=F32)


def _rms_scale(x, g):
    ms = jnp.mean(x * x, axis=-1, keepdims=True)
    return x * lax.rsqrt(ms + NORM_EPS) * g


def _silu(x):
    half = 0.5 * x
    return half + half * jnp.tanh(half)


def _lower_bound(lbl, layer):
    n = lbl.shape[0]
    rows = [lbl[i:i + 1, :] for i in range(n)]
    m = functools.reduce(jnp.maximum, rows)
    es = [jnp.exp(r - m) for r in rows]
    inv = 1.0 / functools.reduce(jnp.add, es)
    ps = [e * inv for e in es]
    return functools.reduce(jnp.add, ps[:layer + 1]) - ps[0]


def _forget_gates(f, lb):
    e = jnp.exp(-f)
    r = 1.0 / (1.0 + e)
    return lb + (1.0 - lb) * r, (1.0 - lb) * (e * r)


def _decay(x):
    return jnp.exp(x)


def _chunk_cumsum(x):
    sub = lax.broadcasted_iota(jnp.int32, (SUBLANES, x.shape[1]), 0)
    out = []
    for r0 in range(0, x.shape[0], SUBLANES):
        t = x[r0:r0 + SUBLANES]
        s = 1
        while s < SUBLANES:
            t = t + jnp.where(sub >= s, pltpu.roll(t, s, axis=0), 0.0)
            s *= 2
        if r0 % HG_CHUNK:
            t = t + jnp.broadcast_to(out[-1][SUBLANES - 1:], t.shape)
        out.append(t)
    return jnp.concatenate(out, axis=0)


def _diag_blocks(q, k, v, b):
    sub_iota = lax.broadcasted_iota(jnp.int32, (SUBLANES, LANES), 0)
    outs = []
    for blk in range(HG_CHUNK // HG_SUB):
        base = blk * HG_SUB
        groups = HG_SUB // SUBLANES
        qs = [q[base + g * SUBLANES: base + (g + 1) * SUBLANES] for g in range(groups)]
        bs = [b[base + g * SUBLANES: base + (g + 1) * SUBLANES] for g in range(groups)]
        accs = [jnp.zeros((SUBLANES, HG_DV), F32) for _ in range(groups)]
        for s in range(HG_SUB):
            row = base + s
            k_s = k[row:row + 1]
            b_s = b[row:row + 1]
            v_s = v[row:row + 1]
            for g in range(s // SUBLANES, groups):
                d = bs[g] - b_s
                if g == s // SUBLANES:
                    d = jnp.where(sub_iota >= (s % SUBLANES), d, NEG_BIG)
                p = qs[g] * (k_s * _decay(d))
                a = jnp.sum(p, axis=-1, keepdims=True)
                accs[g] = accs[g] + a * v_s
        outs.extend(accs)
    return jnp.concatenate(outs, axis=0)


def _hgrn_chunk(q, k, v, b, st):
    C, H = HG_CHUNK, HG_HALF
    o_inter = _mm_nt(q * _decay(b), st)
    r0 = b[H - 1:H]
    a0 = _mm_nt(q[H:] * _decay(b[H:] - r0), k[:H] * _decay(r0 - b[:H]))
    o_hi = _mm(a0, v[:H])
    Q = HG_SUB
    o_q = []
    for base in (0, H):
        r1 = b[base + Q - 1: base + Q]
        a1 = _mm_nt(q[base + Q: base + 2 * Q] * _decay(b[base + Q: base + 2 * Q] - r1),
                    k[base: base + Q] * _decay(r1 - b[base: base + Q]))
        o_q.append(_mm(a1, v[base: base + Q]))
    zero = jnp.zeros((Q, HG_DV), F32)
    o_off = jnp.concatenate([zero, o_q[0], o_hi[:Q], o_hi[Q:] + o_q[1]], axis=0)
    o = o_inter + o_off + _diag_blocks(q, k, v, b)
    b_end = b[C - 1:C]
    st_new = st * _decay(b_end) + _mm_tn(v, k * _decay(b_end - b))
    return o, st_new


def _hgrn_bounded_first(q, k, v, b, st):
    C, H = HG_CHUNK, HG_HALF
    r = b[H - 1:H]
    eq = _decay(b - r)
    qt, kt = q * eq, k * _decay(r - b)
    e_mid, e_hi = _decay(r), eq[C - 1:C]
    scores = _mm_nt(qt, kt)
    v_t = v.T
    st_new = st * (e_mid * e_hi) + _mm(v_t, kt * e_hi)
    return scores, qt * e_mid, v_t, st_new


def _hgrn_bounded_second(scores, q_dec, v_t, st):
    t_i = lax.broadcasted_iota(jnp.int32, scores.shape, 0)
    s_i = lax.broadcasted_iota(jnp.int32, scores.shape, 1)
    a = jnp.where(s_i <= t_i, scores, 0.0)
    return _mm_nt(jnp.concatenate([q_dec, a], axis=1), jnp.concatenate([st, v_t], axis=1))


def _max_half_decay(b_s, tile):
    n = tile // HG_CHUNK
    worst = None
    for h in range(HG_HEADS):
        mid = b_s[h, pl.ds(HG_HALF - 1, n, stride=HG_CHUNK), :]
        end = b_s[h, pl.ds(HG_CHUNK - 1, n, stride=HG_CHUNK), :]
        w = jnp.maximum(-mid, mid - end)
        worst = w if worst is None else jnp.maximum(worst, w)
    return jnp.max(worst)


def _hgrn_prompt_kernel(layer, with_kv, *refs):
    h_ref, ng_ref, win_ref, lbl_ref, gn_ref, wout_ref = refs[:6]
    refs = refs[6:]
    if with_kv:
        kv_in, refs = refs[:5], refs[5:]
    out_ref, sfin_ref = refs[:2]
    refs = refs[2:]
    if with_kv:
        kv_out, refs = refs[:2], refs[2:]
    u_s, q_s, k_s, v_s, b_s, o_s, st_s = refs
    t = pl.program_id(1)
    nt = pl.num_programs(1)
    tile = h_ref.shape[1]

    @pl.when(t == 0)
    def _():
        st_s[...] = jnp.zeros_like(st_s)

    x = h_ref[0]
    u_s[...] = _mm(_rms_scale(x, ng_ref[...]), win_ref[...])
    u = u_s
    lb = _lower_bound(lbl_ref[...], layer)
    for h in range(HG_HEADS):
        sl = slice(h * HG_DK, (h + 1) * HG_DK)
        q_s[h] = _silu(u[:, sl])
        fg, kin = _forget_gates(u[:, HG_KEY + h * HG_DK: HG_KEY + (h + 1) * HG_DK], lb[:, sl])
        k_s[h] = kin
        b_s[h] = _chunk_cumsum(jnp.log(fg))
        v_s[h] = u[:, 2 * HG_KEY + h * HG_DV: 2 * HG_KEY + (h + 1) * HG_DV]

    def chunk_rows(c):
        return pl.ds(pl.multiple_of(c * HG_CHUNK, HG_CHUNK), HG_CHUNK)

    def load(c, h):
        rows = chunk_rows(c)
        return q_s[h, rows, :], k_s[h, rows, :], v_s[h, rows, :], b_s[h, rows, :], st_s[h]

    def bounded_chunks():
        pending = []
        for c in range(tile // HG_CHUNK):
            rows = slice(c * HG_CHUNK, (c + 1) * HG_CHUNK)
            started = []
            for h in range(HG_HEADS):
                st = st_s[h]
                scores, q_dec, v_t, st_new = _hgrn_bounded_first(
                    q_s[h, rows, :], k_s[h, rows, :], v_s[h, rows, :], b_s[h, rows, :], st)
                st_s[h] = st_new
                started.append((h, rows, scores, q_dec, v_t, st))
            for h, prev_rows, *second in pending:
                o_s[h, prev_rows, :] = _hgrn_bounded_second(*second)
            pending = started
        for h, prev_rows, *second in pending:
            o_s[h, prev_rows, :] = _hgrn_bounded_second(*second)

    def general_body(i, carry):
        c, h = i // HG_HEADS, i % HG_HEADS
        o, st_new = _hgrn_chunk(*load(c, h))
        o_s[h, chunk_rows(c), :] = o
        st_s[h] = st_new
        return carry

    bounded = _max_half_decay(b_s, tile) < HG_MAX_HALF_DECAY

    pl.when(bounded)(bounded_chunks)

    @pl.when(jnp.logical_not(bounded))
    def _():
        lax.fori_loop(0, (tile // HG_CHUNK) * HG_HEADS, general_body, 0)

    gn = gn_ref[...]
    gated = []
    for h in range(HG_HEADS):
        g = u[:, 2 * HG_KEY + HG_VAL + h * HG_DV: 2 * HG_KEY + HG_VAL + (h + 1) * HG_DV]
        gated.append(_rms_scale(o_s[h], gn) * _silu(g))
    y = x + _mm(jnp.concatenate(gated, axis=1), wout_ref[...])
    out_ref[0] = y
    if with_kv:
        _kv_project(y, *kv_in, *kv_out)

    @pl.when(t == nt - 1)
    def _():
        for h in range(HG_HEADS):
            sfin_ref[0, h] = st_s[h].T


def _hgrn_prompt_layer(h, norm_g, w_in, lb_logits, layer, gnorm, w_out, kv=None):
    B, T, D = h.shape
    tile = min(PROMPT_TILE, T)
    assert T % tile == 0 and tile % HG_CHUNK == 0 and HG_DK == HG_DV == LANES
    n_a = lb_logits.shape[0]
    const = lambda b, t: (0, 0)
    row_tile = lambda width: pl.BlockSpec((1, tile, width), lambda b, t: (b, t, 0))
    head_scratch = pltpu.VMEM((HG_HEADS, tile, HG_DK), F32)
    operands = [h, norm_g.reshape(1, D), w_in, lb_logits, gnorm.reshape(1, HG_DV), w_out]
    in_specs = [row_tile(D), pl.BlockSpec((1, D), const), _layer_block(w_in, layer, True),
                pl.BlockSpec((n_a, HG_KEY), const), pl.BlockSpec((1, HG_DV), const),
                _layer_block(w_out, layer, True)]
    out_specs = [row_tile(D), pl.BlockSpec((1, HG_HEADS, HG_DK, HG_DV), lambda b, t: (b, 0, 0, 0))]
    out_shape = [jax.ShapeDtypeStruct((B, T, D), F32),
                 jax.ShapeDtypeStruct((B, HG_HEADS, HG_DK, HG_DV), F32)]
    if kv is not None:
        kv_norm, w_kv, tables = kv
        operands += [kv_norm.reshape(1, D), w_kv, *tables]
        in_specs += [pl.BlockSpec((1, D), const), pl.BlockSpec(w_kv.shape, const)]
        in_specs += [pl.BlockSpec((tile, LANES), lambda b, t: (t, 0))] * 3
        out_specs += [row_tile(ATT_KVD)] * 2
        out_shape += [jax.ShapeDtypeStruct((B, T, ATT_KVD), F32)] * 2
    return pl.pallas_call(
        functools.partial(_hgrn_prompt_kernel, layer, kv is not None),
        grid=(B, T // tile),
        in_specs=in_specs,
        out_specs=out_specs,
        out_shape=out_shape,
        scratch_shapes=[pltpu.VMEM((tile, w_in.shape[-1]), F32)] + [head_scratch] * 5
                       + [pltpu.VMEM((HG_HEADS, HG_DV, HG_DK), F32)],
        compiler_params=_cparams("arbitrary", "arbitrary"),
        name=f"hgrn_prompt_l{layer}",
    )(*operands)


def _rope_tables(pos):
    half = ROT_DIM // 2
    pos = np.asarray(pos, np.float64)
    inv_freq = 1.0 / (ROPE_THETA ** (np.arange(half, dtype=np.float64) * 2.0 / ROT_DIM))
    ang = (pos[:, None].astype(np.float32) * inv_freq[None, :].astype(np.float32)).astype(np.float64)
    cos, sin = np.cos(ang), np.sin(ang)
    n = pos.shape[0]
    c = np.ones((n, ATT_HD)); s_up = np.zeros((n, ATT_HD)); s_dn = np.zeros((n, ATT_HD))
    c[:, :half] = cos; c[:, half:ROT_DIM] = cos
    s_dn[:, :half] = -sin
    s_up[:, half:ROT_DIM] = sin
    rep = LANES // ATT_HD
    tab = [np.tile(a, (1, rep)).astype(np.float32) for a in (c, s_up, s_dn)]
    return tuple(jnp.asarray(a) for a in tab)


def _rope_cols(x, cos, s_up, s_dn):
    half = ROT_DIM // 2
    cols = []
    for c in range(x.shape[1] // LANES):
        xc = x[:, c * LANES:(c + 1) * LANES]
        cols.append(xc * cos + pltpu.roll(xc, half, axis=1) * s_up
                    + pltpu.roll(xc, LANES - half, axis=1) * s_dn)
    return cols[0] if len(cols) == 1 else jnp.concatenate(cols, axis=1)


def _kv_kernel(h_ref, *refs):
    _kv_project(h_ref[0], *refs)


def _kv_project(x, ng_ref, w_ref, cos_ref, sup_ref, sdn_ref, k_ref, v_ref):
    u = _mm(_rms_scale(x, ng_ref[...]), w_ref[...])
    k_ref[0] = _rope_cols(u[:, :ATT_KVD], cos_ref[...], sup_ref[...], sdn_ref[...])
    v_ref[0] = u[:, ATT_KVD:]


def _shared_kv(h, kv_norm, w_kv, tables):
    B, T, D = h.shape
    tile = min(PROMPT_TILE, T)
    const = lambda b, t: (0, 0)
    tab = pl.BlockSpec((tile, LANES), lambda b, t: (t, 0))
    kv_spec = pl.BlockSpec((1, tile, ATT_KVD), lambda b, t: (b, t, 0))
    return pl.pallas_call(
        _kv_kernel,
        grid=(B, T // tile),
        in_specs=[pl.BlockSpec((1, tile, D), lambda b, t: (b, t, 0)),
                  pl.BlockSpec((1, D), const), pl.BlockSpec(w_kv.shape, const), tab, tab, tab],
        out_specs=[kv_spec, kv_spec],
        out_shape=[jax.ShapeDtypeStruct((B, T, ATT_KVD), F32)] * 2,
        compiler_params=_cparams("arbitrary", "arbitrary"),
        name="shared_kv",
    )(h, kv_norm.reshape(1, D), w_kv, *tables)


PAIRS_PER_KV = ATT_GROUP // 2


def _swa_attention(blocks, sinks_ref, side_product):
    groups = [(blk, kvh) for blk in range(len(blocks)) for kvh in range(ATT_KVH)]
    n_keys = blocks[0][4].shape[0]
    ones_rows = jnp.ones((SUBLANES, n_keys), F32)

    def score_products(g):
        blk, kvh = groups[g]
        q_cols, k_lo, k_hi, _, bias_t = blocks[blk]
        qs = jnp.concatenate(q_cols[kvh * PAIRS_PER_KV:(kvh + 1) * PAIRS_PER_KV], axis=0)
        keys = jnp.concatenate([k_lo[kvh], k_hi[kvh]], axis=0)
        bias = jnp.concatenate([jnp.concatenate([bias_t] * PAIRS_PER_KV, axis=1)] * 2, axis=0)
        return _mm_nt(keys, qs) + bias

    def softmax_terms(g, s_t):
        kvh = groups[g][1]
        cols = range(kvh * PAIRS_PER_KV, (kvh + 1) * PAIRS_PER_KV)
        terms = []
        for par in range(2):
            s = s_t[par * n_keys:(par + 1) * n_keys]
            sink = jnp.concatenate(
                [jnp.full((1, WINDOW), sinks_ref[0, 2 * c + par] * LOG2E, F32) for c in cols], axis=1)
            m = jnp.maximum(jnp.max(s, axis=0, keepdims=True), sink)
            terms.append((jnp.exp2(s - m), jnp.exp2(sink - m)))
        return terms

    def value_products(g, terms):
        blk, kvh = groups[g]
        v_t = blocks[blk][3]
        lhs = jnp.concatenate([v_t[kvh * ATT_HD:(kvh + 1) * ATT_HD], ones_rows], axis=0)
        halves = []
        for p, sink_term in terms:
            r = _mm(lhs, p)
            halves.append(r[:ATT_HD] * (1.0 / (r[ATT_HD:ATT_HD + 1] + sink_term)))
        return halves

    scores = [score_products(g) for g in range(len(groups))]
    side = side_product()
    terms = [softmax_terms(g, s) for g, s in enumerate(scores)]
    outs = [value_products(g, t) for g, t in enumerate(terms)]
    res = []
    for b in range(len(blocks)):
        cols = []
        for kvh in range(ATT_KVH):
            even, odd = outs[b * ATT_KVH + kvh]
            for i in range(PAIRS_PER_KV):
                queries = slice(i * WINDOW, (i + 1) * WINDOW)
                cols.append(jnp.concatenate([even[:, queries], odd[:, queries]], axis=0).T)
        res.append(jnp.concatenate(cols, axis=1))
    return res, side


def _split_kv_heads(a, fill):
    lane = lax.broadcasted_iota(jnp.int32, a.shape, 1)
    low = lane < ATT_HD
    sw = pltpu.roll(a, ATT_HD, axis=1)
    lo = [jnp.where(low, a, fill), jnp.where(low, sw, fill)]
    hi = [jnp.where(low, fill, sw), jnp.where(low, fill, a)]
    return lo, hi


def _swa_prompt_kernel(final, h_ref, ng_ref, win_ref, sinks_ref, wout_ref, fin_ref,
                       kp_ref, kc_ref, vp_ref, vc_ref, cos_ref, sup_ref, sdn_ref, out_ref):
    t = pl.program_id(1)
    tile = h_ref.shape[1]
    nblk = tile // WINDOW
    x = h_ref[0]
    xn = _rms_scale(x, ng_ref[...]).astype(BF16)
    u = _mm(xn, win_ref[:, :ATT_QD])
    scale = ATT_HD ** -0.5 * LOG2E
    cos, sup, sdn = cos_ref[...], sup_ref[...], sdn_ref[...]
    k_all = jnp.concatenate([kp_ref[0], kc_ref[0]], axis=0)
    v_all = jnp.concatenate([vp_ref[0], vc_ref[0]], axis=0)
    j = lax.broadcasted_iota(jnp.int32, (2 * WINDOW, WINDOW), 0)
    tq = lax.broadcasted_iota(jnp.int32, (2 * WINDOW, WINDOW), 1)
    rel = tq + WINDOW - j
    band = (rel >= 0) & (rel <= WINDOW)
    blocks = []
    for blk in range(nblk):
        rows = slice(blk * WINDOW, (blk + 1) * WINDOW)
        keys = slice(blk * WINDOW, (blk + 2) * WINDOW)
        k_lo, k_hi = _split_kv_heads(k_all[keys], 0.0)
        if blk == 0:
            valid = band & (j >= WINDOW * (t == 0).astype(jnp.int32))
        else:
            valid = band
        q_cols = []
        for c in range(ATT_QD // LANES):
            qc = u[rows, c * LANES:(c + 1) * LANES]
            q_cols.append(_rope_cols(qc, cos[rows], sup[rows], sdn[rows]) * scale)
        blocks.append((q_cols, k_lo, k_hi, v_all[keys].T, jnp.where(valid, 0.0, MASK_VALUE)))
    o_blocks, gate = _swa_attention(blocks, sinks_ref, lambda: _mm(xn, win_ref[:, ATT_QD:]))
    o = jnp.concatenate(o_blocks, axis=0) if nblk > 1 else o_blocks[0]
    y = x + _mm(o * _silu(gate), wout_ref[...])
    if final:
        y = _rms_scale(y, fin_ref[...])
    out_ref[0] = y


def _swa_prompt_layer(h, k, v, tables, norm_g, w_in, layer, sinks, w_out, final_norm, final):
    B, T, D = h.shape
    tile = min(SWA_TILE, T)
    assert T % tile == 0 and tile % WINDOW == 0
    per = tile // WINDOW
    const = lambda b, t: (0, 0)
    tab = pl.BlockSpec((tile, LANES), lambda b, t: (t, 0))
    prev = pl.BlockSpec((1, WINDOW, ATT_KVD), lambda b, t: (b, jnp.maximum(t * per - 1, 0), 0))
    cur = pl.BlockSpec((1, tile, ATT_KVD), lambda b, t: (b, t, 0))
    return pl.pallas_call(
        functools.partial(_swa_prompt_kernel, final),
        grid=(B, T // tile),
        in_specs=[
            pl.BlockSpec((1, tile, D), lambda b, t: (b, t, 0)),
            pl.BlockSpec((1, D), const),
            _layer_block(w_in, layer, True),
            pl.BlockSpec(memory_space=pltpu.SMEM),
            _layer_block(w_out, layer, True),
            pl.BlockSpec((1, D), const),
            prev, cur, prev, cur, tab, tab, tab,
        ],
        out_specs=pl.BlockSpec((1, tile, D), lambda b, t: (b, t, 0)),
        out_shape=jax.ShapeDtypeStruct((B, T, D), F32),
        compiler_params=_cparams("arbitrary", "arbitrary"),
        name="swa_prompt_final" if final else "swa_prompt",
    )(h, norm_g.reshape(1, D), w_in, sinks.reshape(1, ATT_QH), w_out, final_norm.reshape(1, D),
      k, k, v, v, *tables)


def _col(row):
    n = row.shape[1]
    return jnp.broadcast_to(row, (n, n)).T


def _hgrn_decode_kernel(x_ref, s_ref, ng_ref, win_ref, lbl_ref, gn_ref, wout_ref,
                        snew_ref, hout_ref, h_s, q_s, f_s, v_s, g_s, o_s, oblk_s):
    l, i = pl.program_id(0), pl.program_id(1)
    n_layers, n_blocks = pl.num_programs(0), pl.num_programs(1)
    bb = s_ref.shape[1]

    @pl.when(i == 0)
    def _():
        @pl.when(l == 0)
        def _():
            h_s[...] = x_ref[...]

        u = _mm(_rms_scale(h_s[...], ng_ref[0]), win_ref[0])
        lbl = lbl_ref[...]
        lb = jnp.zeros_like(lbl[:1])
        for layer in range(lbl.shape[0]):
            lb = jnp.where(l == layer, _lower_bound(lbl, layer), lb)
        q_s[...] = _silu(u[:, :HG_KEY])
        f_s[...] = _forget_gates(u[:, HG_KEY:2 * HG_KEY], lb)[0]
        v_s[...] = u[:, 2 * HG_KEY:2 * HG_KEY + HG_VAL]
        g_s[...] = _silu(u[:, 2 * HG_KEY + HG_VAL:])

    rows = pl.ds(pl.multiple_of(i * bb, bb), bb)
    qb, fb, vb = q_s[rows, :], f_s[rows, :], v_s[rows, :]
    for j in range(bb):
        for h in range(HG_HEADS):
            sl = slice(h * HG_DK, (h + 1) * HG_DK)
            v_r = vb[j:j + 1, h * HG_DV:(h + 1) * HG_DV]
            s_new = _col(fb[j:j + 1, sl]) * (s_ref[0, j, h] - v_r) + v_r
            snew_ref[0, j, h] = s_new
            oblk_s[j:j + 1, h * HG_DV:(h + 1) * HG_DV] = _mm(qb[j:j + 1, sl], s_new)
    o_s[rows, :] = oblk_s[...]

    @pl.when(i == n_blocks - 1)
    def _():
        gn = gn_ref[0]
        gated = []
        for h in range(HG_HEADS):
            sl = slice(h * HG_DV, (h + 1) * HG_DV)
            gated.append(_rms_scale(o_s[:, sl], gn) * g_s[:, sl])
        h_new = h_s[...] + _mm(jnp.concatenate(gated, axis=1), wout_ref[0])
        h_s[...] = h_new

        @pl.when(l == n_layers - 1)
        def _():
            hout_ref[...] = h_new


def _whole(shape):
    return pl.BlockSpec(shape, lambda *_: (0,) * len(shape))


def _hgrn_decode(h, states, norms, w_in, lb_logits, gnorms, w_out):
    B, D = h.shape
    n_a = states.shape[0]
    bb = min(DEC_BLOCK, B)
    assert B % bb == 0
    per_layer = lambda *tail: pl.BlockSpec((1,) + tail, lambda l, i: (l,) + (0,) * len(tail))
    st_blk = pl.BlockSpec((1, bb, HG_HEADS, HG_DK, HG_DV), lambda l, i: (l, i, 0, 0, 0))
    wide = pltpu.VMEM((B, HG_KEY), F32)
    s_new, h_new = pl.pallas_call(
        _hgrn_decode_kernel,
        grid=(n_a, B // bb),
        in_specs=[_whole((B, D)), st_blk, per_layer(1, D), per_layer(*w_in.shape[1:]),
                  _whole(lb_logits.shape), per_layer(1, HG_DV), per_layer(*w_out.shape[1:])],
        out_specs=[st_blk, _whole((B, D))],
        out_shape=[jax.ShapeDtypeStruct(states.shape, F32), jax.ShapeDtypeStruct((B, D), F32)],
        scratch_shapes=[pltpu.VMEM((B, D), F32)] + [wide] * 5 + [pltpu.VMEM((bb, HG_VAL), F32)],
        compiler_params=_cparams("arbitrary", "arbitrary"),
        name="hgrn_decode",
    )(h, states, norms.reshape(n_a, 1, D), w_in, lb_logits, gnorms.reshape(n_a, 1, HG_DV), w_out)
    return h_new, s_new


def _bf16_round(x):
    return x.astype(BF16).astype(F32)


def _swa_decode_kernel(final, h_ref, ng_ref, win_ref, cos_ref, sup_ref, sdn_ref, kc_ref, vc_ref,
                       kn_ref, vn_ref, sinks_ref, wout_ref, fin_ref, out_ref, qp_s, g_s, os_s):
    i = pl.program_id(0)
    n_blocks = pl.num_programs(0)
    bb, W = kc_ref.shape[0], kc_ref.shape[1]
    B = h_ref.shape[0]
    lane = lax.broadcasted_iota(jnp.int32, (B, LANES), 1)

    @pl.when(i == 0)
    def _():
        u = _mm(_rms_scale(h_ref[...], ng_ref[...]), win_ref[...])
        q = _rope_cols(u[:, :ATT_QD], cos_ref[...], sup_ref[...], sdn_ref[...]) * (ATT_HD ** -0.5)
        g_s[...] = _silu(u[:, ATT_QD:])
        for head in range(ATT_QH):
            col = q[:, (head // 2) * LANES:(head // 2 + 1) * LANES]
            kvh = head // ATT_GROUP
            if head % 2 != kvh:
                col = pltpu.roll(col, ATT_HD, axis=1)
            qp_s[:, head, :] = jnp.where((lane >= ATT_HD) if kvh == 1 else (lane < ATT_HD), col, 0.0)

    rows = pl.ds(pl.multiple_of(i * bb, bb), bb)
    kn, vn = kn_ref[rows, :], vn_ref[rows, :]
    qp = [qp_s[i * bb + j] for j in range(bb)]
    s_c = jnp.concatenate([_mm_nt(qp[j], kc_ref[j]) for j in range(bb)], axis=0)
    qp_all = jnp.concatenate(qp, axis=0)
    expand = lambda a: jnp.concatenate(
        [jnp.broadcast_to(a[j:j + 1], (ATT_QH, ATT_KVD)) for j in range(bb)], axis=0)
    s_n = jnp.sum(_bf16_round(qp_all) * _bf16_round(expand(kn)), axis=-1, keepdims=True)
    sink = jnp.concatenate([sinks_ref[...]] * bb, axis=0)
    m = jnp.maximum(jnp.maximum(jnp.max(s_c, axis=-1, keepdims=True), s_n), sink)
    p_c = jnp.exp(s_c - m)
    p_n = jnp.exp(s_n - m)
    inv = 1.0 / (jnp.sum(p_c, axis=-1, keepdims=True) + p_n + jnp.exp(sink - m))
    p_c = p_c * inv
    o = jnp.concatenate([_mm(p_c[j * ATT_QH:(j + 1) * ATT_QH], vc_ref[j]) for j in range(bb)], axis=0)
    o = o + _bf16_round(p_n * inv) * _bf16_round(expand(vn))
    for j in range(bb):
        os_s[i * bb + j] = o[j * ATT_QH:(j + 1) * ATT_QH]

    @pl.when(i == n_blocks - 1)
    def _():
        cols = []
        for c in range(ATT_QD // LANES):
            halves = []
            for par in range(2):
                head = 2 * c + par
                t = os_s[:, head, :]
                if par != head // ATT_GROUP:
                    t = pltpu.roll(t, ATT_HD, axis=1)
                halves.append(t)
            cols.append(jnp.where(lane < ATT_HD, halves[0], halves[1]))
        y = h_ref[...] + _mm(jnp.concatenate(cols, axis=1) * g_s[...], wout_ref[...])
        if final:
            y = _rms_scale(y, fin_ref[...])
        out_ref[...] = y


def _swa_decode_layer(h, k_new, v_new, cache_k, cache_v, tables, norm_g, w_in, layer, sinks, w_out,
                      final_norm, final):
    B, D = h.shape
    W = cache_k.shape[1]
    bb = min(SWA_DEC_BLOCK, B)
    assert B % bb == 0
    cache_blk = pl.BlockSpec((bb, W, ATT_KVD), lambda i: (i, 0, 0))
    head_rows = pltpu.VMEM((B, ATT_QH, ATT_KVD), F32)
    return pl.pallas_call(
        functools.partial(_swa_decode_kernel, final),
        grid=(B // bb,),
        in_specs=[_whole((B, D)), _whole((1, D)), _layer_block(w_in, layer)] + [_whole((1, LANES))] * 3
                 + [cache_blk, cache_blk, _whole((B, ATT_KVD)), _whole((B, ATT_KVD)),
                    _whole((ATT_QH, 1)), _layer_block(w_out, layer), _whole((1, D))],
        out_specs=_whole((B, D)),
        out_shape=jax.ShapeDtypeStruct((B, D), F32),
        scratch_shapes=[head_rows, pltpu.VMEM((B, ATT_QD), F32), head_rows],
        compiler_params=_cparams("arbitrary"),
        name="swa_decode_final" if final else "swa_decode",
    )(h, norm_g.reshape(1, D), w_in, *tables, cache_k.reshape(B, W, ATT_KVD),
      cache_v.reshape(B, W, ATT_KVD), k_new, v_new, sinks.reshape(ATT_QH, 1), w_out,
      final_norm.reshape(1, D))


def kernel(x_prompt, x_sample, state_hgrn, cache_k_win, cache_v_win, a_norm, a_w_in, a_lb_logits,
           a_gnorm, a_w_out, kv_norm, w_kv, b_norm, b_w_in, b_sinks, b_w_out, final_norm):
    B, T, D = x_prompt.shape
    BD, TD, _ = x_sample.shape
    n_a = a_w_in.shape[0]
    n_b = b_w_in.shape[0]
    a_w_in_b, a_w_out_b = a_w_in.astype(BF16), a_w_out.astype(BF16)
    b_w_in_b, b_w_out_b = b_w_in.astype(BF16), b_w_out.astype(BF16)
    w_kv_b = w_kv.astype(BF16)

    tab_p = _rope_tables(np.arange(T))
    h = x_prompt
    st_p = []
    for l in range(n_a):
        kv = (kv_norm, w_kv_b, tab_p) if l == n_a - 1 else None
        h, s, *kv_p = _hgrn_prompt_layer(h, a_norm[l], a_w_in_b, a_lb_logits, l, a_gnorm[l],
                                         a_w_out_b, kv)
        st_p.append(s)
    k_p, v_p = kv_p
    for l in range(n_b):
        h = _swa_prompt_layer(h, k_p, v_p, tab_p, b_norm[l], b_w_in_b, l, b_sinks[l], b_w_out_b,
                              final_norm, l == n_b - 1)
    y_prompt = h
    w_keep = min(WINDOW, T)
    k_win = k_p[:, T - w_keep:].reshape(B, w_keep, ATT_KVH, ATT_HD)
    v_win = v_p[:, T - w_keep:].reshape(B, w_keep, ATT_KVH, ATT_HD)

    tab_s = _rope_tables(PAST_LEN + np.arange(TD))
    hs = x_sample.reshape(BD * TD, D)
    hs, st_s = _hgrn_decode(hs, state_hgrn, a_norm, a_w_in_b, a_lb_logits, a_gnorm, a_w_out_b)
    k_s, v_s = _shared_kv(hs.reshape(1, BD, D), kv_norm, w_kv_b,
                          tuple(jnp.broadcast_to(t, (BD, LANES)) for t in tab_s))
    k_s, v_s = k_s.reshape(BD, ATT_KVD), v_s.reshape(BD, ATT_KVD)
    for l in range(n_b):
        hs = _swa_decode_layer(hs, k_s, v_s, cache_k_win, cache_v_win, tab_s, b_norm[l], b_w_in_b, l,
                               b_sinks[l], b_w_out_b, final_norm, l == n_b - 1)
    y_sample = hs.reshape(BD, TD, D)

    return (y_prompt, y_sample, jnp.stack(st_p), st_s, k_win, v_win,
            k_s.reshape(BD, TD, ATT_KVH, ATT_HD), v_s.reshape(BD, TD, ATT_KVH, ATT_HD))
```

```python
import functools

import numpy as np
import jax
import jax.numpy as jnp
from jax import lax
from jax.experimental import pallas as pl
from jax.experimental.pallas import tpu as pltpu

F32 = jnp.float32
BF16 = jnp.bfloat16

D_MODEL = 1024
HG_HEADS = 8
HG_DK = 128
HG_DV = 128
HG_KEY = HG_HEADS * HG_DK
HG_VAL = HG_HEADS * HG_DV
HG_CHUNK = 64
HG_HALF = HG_CHUNK // 2
HG_SUB = 16
HG_MAX_HALF_DECAY = 60.0
ATT_HD = 64
ATT_QH = 16
ATT_KVH = 2
ATT_GROUP = ATT_QH // ATT_KVH
ATT_QD = ATT_QH * ATT_HD
ATT_KVD = ATT_KVH * ATT_HD
WINDOW = 128
ROPE_THETA = 500000.0
ROT_DIM = ATT_HD // 4
NORM_EPS = 1e-6
MASK_VALUE = -1e30
NEG_BIG = -1e30
LOG2E = 1.4426950408889634
PAST_LEN = 8192

SUBLANES = 8
LANES = 128
PROMPT_TILE = 512
SWA_TILE = 1024
DEC_BLOCK = 16
SWA_DEC_BLOCK = 16
V7X_VMEM_BYTES = 64 * 1024 * 1024
VMEM_LIMIT = V7X_VMEM_BYTES * 3 // 4


def _cparams(*sem):
    return pltpu.CompilerParams(dimension_semantics=sem, vmem_limit_bytes=VMEM_LIMIT)


def _layer_block(stacked, layer, single_buffer=False):
    tail = stacked.shape[1:]
    mode = dict(pipeline_mode=pl.Buffered(1)) if single_buffer else {}
    return pl.BlockSpec((None,) + tail, lambda *_: (layer,) + (0,) * len(tail), **mode)


def _mm(a, b):
    return jnp.dot(a.astype(BF16), b.astype(BF16), preferred_element_type=F32)


def _mm_nt(a, b):
    return lax.dot_general(a.astype(BF16), b.astype(BF16), (((1,), (1,)), ((), ())),
                           preferred_element_type=F32)


def _mm_tn(a, b):
    return lax.dot_general(a.astype(BF16), b.astype(BF16), (((0,), (0,)), ((), ())),
                           preferred_element_type=F32)


def _rms_scale(x, g):
    ms = jnp.mean(x * x, axis=-1, keepdims=True)
    return x * lax.rsqrt(ms + NORM_EPS) * g


def _silu(x):
    half = 0.5 * x
    return half + half * jnp.tanh(half)


def _lower_bound(lbl, layer):
    n = lbl.shape[0]
    rows = [lbl[i:i + 1, :] for i in range(n)]
    m = functools.reduce(jnp.maximum, rows)
    es = [jnp.exp(r - m) for r in rows]
    inv = 1.0 / functools.reduce(jnp.add, es)
    ps = [e * inv for e in es]
    return functools.reduce(jnp.add, ps[:layer + 1]) - ps[0]


def _forget_gates(f, lb):
    e = jnp.exp(-f)
    r = 1.0 / (1.0 + e)
    return lb + (1.0 - lb) * r, (1.0 - lb) * (e * r)


def _decay(x):
    return jnp.exp(x)


def _chunk_cumsum(x):
    sub = lax.broadcasted_iota(jnp.int32, (SUBLANES, x.shape[1]), 0)
    out = []
    for r0 in range(0, x.shape[0], SUBLANES):
        t = x[r0:r0 + SUBLANES]
        s = 1
        while s < SUBLANES:
            t = t + jnp.where(sub >= s, pltpu.roll(t, s, axis=0), 0.0)
            s *= 2
        if r0 % HG_CHUNK:
            t = t + jnp.broadcast_to(out[-1][SUBLANES - 1:], t.shape)
        out.append(t)
    return jnp.concatenate(out, axis=0)


def _diag_blocks(q, k, v, b):
    sub_iota = lax.broadcasted_iota(jnp.int32, (SUBLANES, LANES), 0)
    outs = []
    for blk in range(HG_CHUNK // HG_SUB):
        base = blk * HG_SUB
        groups = HG_SUB // SUBLANES
        qs = [q[base + g * SUBLANES: base + (g + 1) * SUBLANES] for g in range(groups)]
        bs = [b[base + g * SUBLANES: base + (g + 1) * SUBLANES] for g in range(groups)]
        accs = [jnp.zeros((SUBLANES, HG_DV), F32) for _ in range(groups)]
        for s in range(HG_SUB):
            row = base + s
            k_s = k[row:row + 1]
            b_s = b[row:row + 1]
            v_s = v[row:row + 1]
            for g in range(s // SUBLANES, groups):
                d = bs[g] - b_s
                if g == s // SUBLANES:
                    d = jnp.where(sub_iota >= (s % SUBLANES), d, NEG_BIG)
                p = qs[g] * (k_s * _decay(d))
                a = jnp.sum(p, axis=-1, keepdims=True)
                accs[g] = accs[g] + a * v_s
        outs.extend(accs)
    return jnp.concatenate(outs, axis=0)


def _hgrn_chunk(q, k, v, b, st):
    C, H = HG_CHUNK, HG_HALF
    o_inter = _mm_nt(q * _decay(b), st)
    r0 = b[H - 1:H]
    a0 = _mm_nt(q[H:] * _decay(b[H:] - r0), k[:H] * _decay(r0 - b[:H]))
    o_hi = _mm(a0, v[:H])
    Q = HG_SUB
    o_q = []
    for base in (0, H):
        r1 = b[base + Q - 1: base + Q]
        a1 = _mm_nt(q[base + Q: base + 2 * Q] * _decay(b[base + Q: base + 2 * Q] - r1),
                    k[base: base + Q] * _decay(r1 - b[base: base + Q]))
        o_q.append(_mm(a1, v[base: base + Q]))
    zero = jnp.zeros((Q, HG_DV), F32)
    o_off = jnp.concatenate([zero, o_q[0], o_hi[:Q], o_hi[Q:] + o_q[1]], axis=0)
    o = o_inter + o_off + _diag_blocks(q, k, v, b)
    b_end = b[C - 1:C]
    st_new = st * _decay(b_end) + _mm_tn(v, k * _decay(b_end - b))
    return o, st_new


def _hgrn_bounded_first(q, k, v, b, st):
    C, H = HG_CHUNK, HG_HALF
    r = b[H - 1:H]
    eq = _decay(b - r)
    qt, kt = q * eq, k * _decay(r - b)
    e_mid, e_hi = _decay(r), eq[C - 1:C]
    scores = _mm_nt(qt, kt)
    v_t = v.T
    st_new = st * (e_mid * e_hi) + _mm(v_t, kt * e_hi)
    return scores, qt * e_mid, v_t, st_new


def _hgrn_bounded_second(scores, q_dec, v_t, st):
    t_i = lax.broadcasted_iota(jnp.int32, scores.shape, 0)
    s_i = lax.broadcasted_iota(jnp.int32, scores.shape, 1)
    a = jnp.where(s_i <= t_i, scores, 0.0)
    return _mm_nt(jnp.concatenate([q_dec, a], axis=1), jnp.concatenate([st, v_t], axis=1))


def _max_half_decay(b_s, tile):
    n = tile // HG_CHUNK
    worst = None
    for h in range(HG_HEADS):
        mid = b_s[h, pl.ds(HG_HALF - 1, n, stride=HG_CHUNK), :]
        end = b_s[h, pl.ds(HG_CHUNK - 1, n, stride=HG_CHUNK), :]
        w = jnp.maximum(-mid, mid - end)
        worst = w if worst is None else jnp.maximum(worst, w)
    return jnp.max(worst)


def _hgrn_prompt_kernel(layer, with_kv, *refs):
    h_ref, ng_ref, win_ref, lbl_ref, gn_ref, wout_ref = refs[:6]
    refs = refs[6:]
    if with_kv:
        kv_in, refs = refs[:5], refs[5:]
    out_ref, sfin_ref = refs[:2]
    refs = refs[2:]
    if with_kv:
        kv_out, refs = refs[:2], refs[2:]
    u_s, q_s, k_s, v_s, b_s, o_s, st_s = refs
    t = pl.program_id(1)
    nt = pl.num_programs(1)
    tile = h_ref.shape[1]

    @pl.when(t == 0)
    def _():
        st_s[...] = jnp.zeros_like(st_s)

    x = h_ref[0]
    u_s[...] = _mm(_rms_scale(x, ng_ref[...]), win_ref[...])
    u = u_s
    lb = _lower_bound(lbl_ref[...], layer)
    for h in range(HG_HEADS):
        sl = slice(h * HG_DK, (h + 1) * HG_DK)
        q_s[h] = _silu(u[:, sl])
        fg, kin = _forget_gates(u[:, HG_KEY + h * HG_DK: HG_KEY + (h + 1) * HG_DK], lb[:, sl])
        k_s[h] = kin
        b_s[h] = _chunk_cumsum(jnp.log(fg))
        v_s[h] = u[:, 2 * HG_KEY + h * HG_DV: 2 * HG_KEY + (h + 1) * HG_DV]

    def chunk_rows(c):
        return pl.ds(pl.multiple_of(c * HG_CHUNK, HG_CHUNK), HG_CHUNK)

    def load(c, h):
        rows = chunk_rows(c)
        return q_s[h, rows, :], k_s[h, rows, :], v_s[h, rows, :], b_s[h, rows, :], st_s[h]

    def bounded_chunks():
        pending = []
        for c in range(tile // HG_CHUNK):
            rows = slice(c * HG_CHUNK, (c + 1) * HG_CHUNK)
            started = []
            for h in range(HG_HEADS):
                st = st_s[h]
                scores, q_dec, v_t, st_new = _hgrn_bounded_first(
                    q_s[h, rows, :], k_s[h, rows, :], v_s[h, rows, :], b_s[h, rows, :], st)
                st_s[h] = st_new
                started.append((h, rows, scores, q_dec, v_t, st))
            for h, prev_rows, *second in pending:
                o_s[h, prev_rows, :] = _hgrn_bounded_second(*second)
            pending = started
        for h, prev_rows, *second in pending:
            o_s[h, prev_rows, :] = _hgrn_bounded_second(*second)

    def general_body(i, carry):
        c, h = i // HG_HEADS, i % HG_HEADS
        o, st_new = _hgrn_chunk(*load(c, h))
        o_s[h, chunk_rows(c), :] = o
        st_s[h] = st_new
        return carry

    bounded = _max_half_decay(b_s, tile) < HG_MAX_HALF_DECAY

    pl.when(bounded)(bounded_chunks)

    @pl.when(jnp.logical_not(bounded))
    def _():
        lax.fori_loop(0, (tile // HG_CHUNK) * HG_HEADS, general_body, 0)

    gn = gn_ref[...]
    gated = []
    for h in range(HG_HEADS):
        g = u[:, 2 * HG_KEY + HG_VAL + h * HG_DV: 2 * HG_KEY + HG_VAL + (h + 1) * HG_DV]
        gated.append(_rms_scale(o_s[h], gn) * _silu(g))
    y = x + _mm(jnp.concatenate(gated, axis=1), wout_ref[...])
    out_ref[0] = y
    if with_kv:
        _kv_project(y, *kv_in, *kv_out)

    @pl.when(t == nt - 1)
    def _():
        for h in range(HG_HEADS):
            sfin_ref[0, h] = st_s[h].T


def _hgrn_prompt_layer(h, norm_g, w_in, lb_logits, layer, gnorm, w_out, kv=None):
    B, T, D = h.shape
    tile = min(PROMPT_TILE, T)
    assert T % tile == 0 and tile % HG_CHUNK == 0 and HG_DK == HG_DV == LANES
    n_a = lb_logits.shape[0]
    const = lambda b, t: (0, 0)
    row_tile = lambda width: pl.BlockSpec((1, tile, width), lambda b, t: (b, t, 0))
    head_scratch = pltpu.VMEM((HG_HEADS, tile, HG_DK), F32)
    operands = [h, norm_g.reshape(1, D), w_in, lb_logits, gnorm.reshape(1, HG_DV), w_out]
    in_specs = [row_tile(D), pl.BlockSpec((1, D), const), _layer_block(w_in, layer, True),
                pl.BlockSpec((n_a, HG_KEY), const), pl.BlockSpec((1, HG_DV), const),
                _layer_block(w_out, layer, True)]
    out_specs = [row_tile(D), pl.BlockSpec((1, HG_HEADS, HG_DK, HG_DV), lambda b, t: (b, 0, 0, 0))]
    out_shape = [jax.ShapeDtypeStruct((B, T, D), F32),
                 jax.ShapeDtypeStruct((B, HG_HEADS, HG_DK, HG_DV), F32)]
    if kv is not None:
        kv_norm, w_kv, tables = kv
        operands += [kv_norm.reshape(1, D), w_kv, *tables]
        in_specs += [pl.BlockSpec((1, D), const), pl.BlockSpec(w_kv.shape, const)]
        in_specs += [pl.BlockSpec((tile, LANES), lambda b, t: (t, 0))] * 3
        out_specs += [row_tile(ATT_KVD)] * 2
        out_shape += [jax.ShapeDtypeStruct((B, T, ATT_KVD), F32)] * 2
    return pl.pallas_call(
        functools.partial(_hgrn_prompt_kernel, layer, kv is not None),
        grid=(B, T // tile),
        in_specs=in_specs,
        out_specs=out_specs,
        out_shape=out_shape,
        scratch_shapes=[pltpu.VMEM((tile, w_in.shape[-1]), F32)] + [head_scratch] * 5
                       + [pltpu.VMEM((HG_HEADS, HG_DV, HG_DK), F32)],
        compiler_params=_cparams("arbitrary", "arbitrary"),
        name=f"hgrn_prompt_l{layer}",
    )(*operands)


def _rope_tables(pos):
    half = ROT_DIM // 2
    pos = np.asarray(pos, np.float64)
    inv_freq = 1.0 / (ROPE_THETA ** (np.arange(half, dtype=np.float64) * 2.0 / ROT_DIM))
    ang = (pos[:, None].astype(np.float32) * inv_freq[None, :].astype(np.float32)).astype(np.float64)
    cos, sin = np.cos(ang), np.sin(ang)
    n = pos.shape[0]
    c = np.ones((n, ATT_HD)); s_up = np.zeros((n, ATT_HD)); s_dn = np.zeros((n, ATT_HD))
    c[:, :half] = cos; c[:, half:ROT_DIM] = cos
    s_dn[:, :half] = -sin
    s_up[:, half:ROT_DIM] = sin
    rep = LANES // ATT_HD
    tab = [np.tile(a, (1, rep)).astype(np.float32) for a in (c, s_up, s_dn)]
    return tuple(jnp.asarray(a) for a in tab)


def _rope_cols(x, cos, s_up, s_dn):
    half = ROT_DIM // 2
    cols = []
    for c in range(x.shape[1] // LANES):
        xc = x[:, c * LANES:(c + 1) * LANES]
        cols.append(xc * cos + pltpu.roll(xc, half, axis=1) * s_up
                    + pltpu.roll(xc, LANES - half, axis=1) * s_dn)
    return cols[0] if len(cols) == 1 else jnp.concatenate(cols, axis=1)


def _kv_kernel(h_ref, *refs):
    _kv_project(h_ref[0], *refs)


def _kv_project(x, ng_ref, w_ref, cos_ref, sup_ref, sdn_ref, k_ref, v_ref):
    u = _mm(_rms_scale(x, ng_ref[...]), w_ref[...])
    k_ref[0] = _rope_cols(u[:, :ATT_KVD], cos_ref[...], sup_ref[...], sdn_ref[...])
    v_ref[0] = u[:, ATT_KVD:]


def _shared_kv(h, kv_norm, w_kv, tables):
    B, T, D = h.shape
    tile = min(PROMPT_TILE, T)
    const = lambda b, t: (0, 0)
    tab = pl.BlockSpec((tile, LANES), lambda b, t: (t, 0))
    kv_spec = pl.BlockSpec((1, tile, ATT_KVD), lambda b, t: (b, t, 0))
    return pl.pallas_call(
        _kv_kernel,
        grid=(B, T // tile),
        in_specs=[pl.BlockSpec((1, tile, D), lambda b, t: (b, t, 0)),
                  pl.BlockSpec((1, D), const), pl.BlockSpec(w_kv.shape, const), tab, tab, tab],
        out_specs=[kv_spec, kv_spec],
        out_shape=[jax.ShapeDtypeStruct((B, T, ATT_KVD), F32)] * 2,
        compiler_params=_cparams("arbitrary", "arbitrary"),
        name="shared_kv",
    )(h, kv_norm.reshape(1, D), w_kv, *tables)


PAIRS_PER_KV = ATT_GROUP // 2


def _swa_attention(blocks, sinks_ref, side_product):
    groups = [(blk, kvh) for blk in range(len(blocks)) for kvh in range(ATT_KVH)]
    n_keys = blocks[0][4].shape[0]
    ones_rows = jnp.ones((SUBLANES, n_keys), F32)

    def score_products(g):
        blk, kvh = groups[g]
        q_cols, k_lo, k_hi, _, bias_t = blocks[blk]
        qs = jnp.concatenate(q_cols[kvh * PAIRS_PER_KV:(kvh + 1) * PAIRS_PER_KV], axis=0)
        keys = jnp.concatenate([k_lo[kvh], k_hi[kvh]], axis=0)
        bias = jnp.concatenate([jnp.concatenate([bias_t] * PAIRS_PER_KV, axis=1)] * 2, axis=0)
        return _mm_nt(keys, qs) + bias

    def softmax_terms(g, s_t):
        kvh = groups[g][1]
        cols = range(kvh * PAIRS_PER_KV, (kvh + 1) * PAIRS_PER_KV)
        terms = []
        for par in range(2):
            s = s_t[par * n_keys:(par + 1) * n_keys]
            sink = jnp.concatenate(
                [jnp.full((1, WINDOW), sinks_ref[0, 2 * c + par] * LOG2E, F32) for c in cols], axis=1)
            m = jnp.maximum(jnp.max(s, axis=0, keepdims=True), sink)
            terms.append((jnp.exp2(s - m), jnp.exp2(sink - m)))
        return terms

    def value_products(g, terms):
        blk, kvh = groups[g]
        v_t = blocks[blk][3]
        lhs = jnp.concatenate([v_t[kvh * ATT_HD:(kvh + 1) * ATT_HD], ones_rows], axis=0)
        halves = []
        for p, sink_term in terms:
            r = _mm(lhs, p)
            halves.append(r[:ATT_HD] * (1.0 / (r[ATT_HD:ATT_HD + 1] + sink_term)))
        return halves

    scores = [score_products(g) for g in range(len(groups))]
    side = side_product()
    terms = [softmax_terms(g, s) for g, s in enumerate(scores)]
    outs = [value_products(g, t) for g, t in enumerate(terms)]
    res = []
    for b in range(len(blocks)):
        cols = []
        for kvh in range(ATT_KVH):
            even, odd = outs[b * ATT_KVH + kvh]
            for i in range(PAIRS_PER_KV):
                queries = slice(i * WINDOW, (i + 1) * WINDOW)
                cols.append(jnp.concatenate([even[:, queries], odd[:, queries]], axis=0).T)
        res.append(jnp.concatenate(cols, axis=1))
    return res, side


def _split_kv_heads(a, fill):
    lane = lax.broadcasted_iota(jnp.int32, a.shape, 1)
    low = lane < ATT_HD
    sw = pltpu.roll(a, ATT_HD, axis=1)
    lo = [jnp.where(low, a, fill), jnp.where(low, sw, fill)]
    hi = [jnp.where(low, fill, sw), jnp.where(low, fill, a)]
    return lo, hi


def _swa_prompt_kernel(final, h_ref, ng_ref, win_ref, sinks_ref, wout_ref, fin_ref,
                       kp_ref, kc_ref, vp_ref, vc_ref, cos_ref, sup_ref, sdn_ref, out_ref):
    t = pl.program_id(1)
    tile = h_ref.shape[1]
    nblk = tile // WINDOW
    x = h_ref[0]
    xn = _rms_scale(x, ng_ref[...]).astype(BF16)
    u = _mm(xn, win_ref[:, :ATT_QD])
    scale = ATT_HD ** -0.5 * LOG2E
    cos, sup, sdn = cos_ref[...], sup_ref[...], sdn_ref[...]
    k_all = jnp.concatenate([kp_ref[0], kc_ref[0]], axis=0)
    v_all = jnp.concatenate([vp_ref[0], vc_ref[0]], axis=0)
    j = lax.broadcasted_iota(jnp.int32, (2 * WINDOW, WINDOW), 0)
    tq = lax.broadcasted_iota(jnp.int32, (2 * WINDOW, WINDOW), 1)
    rel = tq + WINDOW - j
    band = (rel >= 0) & (rel <= WINDOW)
    blocks = []
    for blk in range(nblk):
        rows = slice(blk * WINDOW, (blk + 1) * WINDOW)
        keys = slice(blk * WINDOW, (blk + 2) * WINDOW)
        k_lo, k_hi = _split_kv_heads(k_all[keys], 0.0)
        if blk == 0:
            valid = band & (j >= WINDOW * (t == 0).astype(jnp.int32))
        else:
            valid = band
        q_cols = []
        for c in range(ATT_QD // LANES):
            qc = u[rows, c * LANES:(c + 1) * LANES]
            q_cols.append(_rope_cols(qc, cos[rows], sup[rows], sdn[rows]) * scale)
        blocks.append((q_cols, k_lo, k_hi, v_all[keys].T, jnp.where(valid, 0.0, MASK_VALUE)))
    o_blocks, gate = _swa_attention(blocks, sinks_ref, lambda: _mm(xn, win_ref[:, ATT_QD:]))
    o = jnp.concatenate(o_blocks, axis=0) if nblk > 1 else o_blocks[0]
    y = x + _mm(o * _silu(gate), wout_ref[...])
    if final:
        y = _rms_scale(y, fin_ref[...])
    out_ref[0] = y


def _swa_prompt_layer(h, k, v, tables, norm_g, w_in, layer, sinks, w_out, final_norm, final):
    B, T, D = h.shape
    tile = min(SWA_TILE, T)
    assert T % tile == 0 and tile % WINDOW == 0
    per = tile // WINDOW
    const = lambda b, t: (0, 0)
    tab = pl.BlockSpec((tile, LANES), lambda b, t: (t, 0))
    prev = pl.BlockSpec((1, WINDOW, ATT_KVD), lambda b, t: (b, jnp.maximum(t * per - 1, 0), 0))
    cur = pl.BlockSpec((1, tile, ATT_KVD), lambda b, t: (b, t, 0))
    return pl.pallas_call(
        functools.partial(_swa_prompt_kernel, final),
        grid=(B, T // tile),
        in_specs=[
            pl.BlockSpec((1, tile, D), lambda b, t: (b, t, 0)),
            pl.BlockSpec((1, D), const),
            _layer_block(w_in, layer, True),
            pl.BlockSpec(memory_space=pltpu.SMEM),
            _layer_block(w_out, layer, True),
            pl.BlockSpec((1, D), const),
            prev, cur, prev, cur, tab, tab, tab,
        ],
        out_specs=pl.BlockSpec((1, tile, D), lambda b, t: (b, t, 0)),
        out_shape=jax.ShapeDtypeStruct((B, T, D), F32),
        compiler_params=_cparams("arbitrary", "arbitrary"),
        name="swa_prompt_final" if final else "swa_prompt",
    )(h, norm_g.reshape(1, D), w_in, sinks.reshape(1, ATT_QH), w_out, final_norm.reshape(1, D),
      k, k, v, v, *tables)


def _col(row):
    n = row.shape[1]
    return jnp.broadcast_to(row, (n, n)).T


def _hgrn_decode_kernel(x_ref, s_ref, ng_ref, win_ref, lbl_ref, gn_ref, wout_ref,
                        snew_ref, hout_ref, h_s, q_s, f_s, v_s, g_s, o_s, oblk_s):
    l, i = pl.program_id(0), pl.program_id(1)
    n_layers, n_blocks = pl.num_programs(0), pl.num_programs(1)
    bb = s_ref.shape[1]

    @pl.when(i == 0)
    def _():
        @pl.when(l == 0)
        def _():
            h_s[...] = x_ref[...]

        u = _mm(_rms_scale(h_s[...], ng_ref[0]), win_ref[0])
        lbl = lbl_ref[...]
        lb = jnp.zeros_like(lbl[:1])
        for layer in range(lbl.shape[0]):
            lb = jnp.where(l == layer, _lower_bound(lbl, layer), lb)
        q_s[...] = _silu(u[:, :HG_KEY])
        f_s[...] = _forget_gates(u[:, HG_KEY:2 * HG_KEY], lb)[0]
        v_s[...] = u[:, 2 * HG_KEY:2 * HG_KEY + HG_VAL]
        g_s[...] = _silu(u[:, 2 * HG_KEY + HG_VAL:])

    rows = pl.ds(pl.multiple_of(i * bb, bb), bb)
    qb, fb, vb = q_s[rows, :], f_s[rows, :], v_s[rows, :]
    for j in range(bb):
        for h in range(HG_HEADS):
            sl = slice(h * HG_DK, (h + 1) * HG_DK)
            v_r = vb[j:j + 1, h * HG_DV:(h + 1) * HG_DV]
            s_new = _col(fb[j:j + 1, sl]) * (s_ref[0, j, h] - v_r) + v_r
            snew_ref[0, j, h] = s_new
            oblk_s[j:j + 1, h * HG_DV:(h + 1) * HG_DV] = _mm(qb[j:j + 1, sl], s_new)
    o_s[rows, :] = oblk_s[...]

    @pl.when(i == n_blocks - 1)
    def _():
        gn = gn_ref[0]
        gated = []
        for h in range(HG_HEADS):
            sl = slice(h * HG_DV, (h + 1) * HG_DV)
            gated.append(_rms_scale(o_s[:, sl], gn) * g_s[:, sl])
        h_new = h_s[...] + _mm(jnp.concatenate(gated, axis=1), wout_ref[0])
        h_s[...] = h_new

        @pl.when(l == n_layers - 1)
        def _():
            hout_ref[...] = h_new


def _whole(shape):
    return pl.BlockSpec(shape, lambda *_: (0,) * len(shape))


def _hgrn_decode(h, states, norms, w_in, lb_logits, gnorms, w_out):
    B, D = h.shape
    n_a = states.shape[0]
    bb = min(DEC_BLOCK, B)
    assert B % bb == 0
    per_layer = lambda *tail: pl.BlockSpec((1,) + tail, lambda l, i: (l,) + (0,) * len(tail),
                                           pipeline_mode=pl.Buffered(1))
    st_blk = pl.BlockSpec((1, bb, HG_HEADS, HG_DK, HG_DV), lambda l, i: (l, i, 0, 0, 0))
    wide = pltpu.VMEM((B, HG_KEY), F32)
    s_new, h_new = pl.pallas_call(
        _hgrn_decode_kernel,
        grid=(n_a, B // bb),
        in_specs=[_whole((B, D)), st_blk, per_layer(1, D), per_layer(*w_in.shape[1:]),
                  _whole(lb_logits.shape), per_layer(1, HG_DV), per_layer(*w_out.shape[1:])],
        out_specs=[st_blk, _whole((B, D))],
        out_shape=[jax.ShapeDtypeStruct(states.shape, F32), jax.ShapeDtypeStruct((B, D), F32)],
        scratch_shapes=[pltpu.VMEM((B, D), F32)] + [wide] * 5 + [pltpu.VMEM((bb, HG_VAL), F32)],
        compiler_params=_cparams("arbitrary", "arbitrary"),
        name="hgrn_decode",
    )(h, states, norms.reshape(n_a, 1, D), w_in, lb_logits, gnorms.reshape(n_a, 1, HG_DV), w_out)
    return h_new, s_new


def _bf16_round(x):
    return x.astype(BF16).astype(F32)


def _swa_decode_kernel(final, h_ref, ng_ref, win_ref, cos_ref, sup_ref, sdn_ref, kc_ref, vc_ref,
                       kn_ref, vn_ref, sinks_ref, wout_ref, fin_ref, out_ref, qp_s, g_s, os_s):
    i = pl.program_id(0)
    n_blocks = pl.num_programs(0)
    bb, W = kc_ref.shape[0], kc_ref.shape[1]
    B = h_ref.shape[0]
    lane = lax.broadcasted_iota(jnp.int32, (B, LANES), 1)

    @pl.when(i == 0)
    def _():
        u = _mm(_rms_scale(h_ref[...], ng_ref[...]), win_ref[...])
        q = _rope_cols(u[:, :ATT_QD], cos_ref[...], sup_ref[...], sdn_ref[...]) * (ATT_HD ** -0.5)
        g_s[...] = _silu(u[:, ATT_QD:])
        for head in range(ATT_QH):
            col = q[:, (head // 2) * LANES:(head // 2 + 1) * LANES]
            kvh = head // ATT_GROUP
            if head % 2 != kvh:
                col = pltpu.roll(col, ATT_HD, axis=1)
            qp_s[:, head, :] = jnp.where((lane >= ATT_HD) if kvh == 1 else (lane < ATT_HD), col, 0.0)

    rows = pl.ds(pl.multiple_of(i * bb, bb), bb)
    kn, vn = kn_ref[rows, :], vn_ref[rows, :]
    qp = [qp_s[i * bb + j] for j in range(bb)]
    s_c = jnp.concatenate([_mm_nt(qp[j], kc_ref[j]) for j in range(bb)], axis=0)
    qp_all = jnp.concatenate(qp, axis=0)
    expand = lambda a: jnp.concatenate(
        [jnp.broadcast_to(a[j:j + 1], (ATT_QH, ATT_KVD)) for j in range(bb)], axis=0)
    s_n = jnp.sum(_bf16_round(qp_all) * _bf16_round(expand(kn)), axis=-1, keepdims=True)
    sink = jnp.concatenate([sinks_ref[...]] * bb, axis=0)
    m = jnp.maximum(jnp.maximum(jnp.max(s_c, axis=-1, keepdims=True), s_n), sink)
    p_c = jnp.exp(s_c - m)
    p_n = jnp.exp(s_n - m)
    inv = 1.0 / (jnp.sum(p_c, axis=-1, keepdims=True) + p_n + jnp.exp(sink - m))
    p_c = p_c * inv
    o = jnp.concatenate([_mm(p_c[j * ATT_QH:(j + 1) * ATT_QH], vc_ref[j]) for j in range(bb)], axis=0)
    o = o + _bf16_round(p_n * inv) * _bf16_round(expand(vn))
    for j in range(bb):
        os_s[i * bb + j] = o[j * ATT_QH:(j + 1) * ATT_QH]

    @pl.when(i == n_blocks - 1)
    def _():
        cols = []
        for c in range(ATT_QD // LANES):
            halves = []
            for par in range(2):
                head = 2 * c + par
                t = os_s[:, head, :]
                if par != head // ATT_GROUP:
                    t = pltpu.roll(t, ATT_HD, axis=1)
                halves.append(t)
            cols.append(jnp.where(lane < ATT_HD, halves[0], halves[1]))
        y = h_ref[...] + _mm(jnp.concatenate(cols, axis=1) * g_s[...], wout_ref[...])
        if final:
            y = _rms_scale(y, fin_ref[...])
        out_ref[...] = y


def _swa_decode_layer(h, k_new, v_new, cache_k, cache_v, tables, norm_g, w_in, layer, sinks, w_out,
                      final_norm, final):
    B, D = h.shape
    W = cache_k.shape[1]
    bb = min(SWA_DEC_BLOCK, B)
    assert B % bb == 0
    cache_blk = pl.BlockSpec((bb, W, ATT_KVD), lambda i: (i, 0, 0))
    head_rows = pltpu.VMEM((B, ATT_QH, ATT_KVD), F32)
    return pl.pallas_call(
        functools.partial(_swa_decode_kernel, final),
        grid=(B // bb,),
        in_specs=[_whole((B, D)), _whole((1, D)), _layer_block(w_in, layer)] + [_whole((1, LANES))] * 3
                 + [cache_blk, cache_blk, _whole((B, ATT_KVD)), _whole((B, ATT_KVD)),
                    _whole((ATT_QH, 1)), _layer_block(w_out, layer), _whole((1, D))],
        out_specs=_whole((B, D)),
        out_shape=jax.ShapeDtypeStruct((B, D), F32),
        scratch_shapes=[head_rows, pltpu.VMEM((B, ATT_QD), F32), head_rows],
        compiler_params=_cparams("arbitrary"),
        name="swa_decode_final" if final else "swa_decode",
    )(h, norm_g.reshape(1, D), w_in, *tables, cache_k.reshape(B, W, ATT_KVD),
      cache_v.reshape(B, W, ATT_KVD), k_new, v_new, sinks.reshape(ATT_QH, 1), w_out,
      final_norm.reshape(1, D))


def kernel(x_prompt, x_sample, state_hgrn, cache_k_win, cache_v_win, a_norm, a_w_in, a_lb_logits,
           a_gnorm, a_w_out, kv_norm, w_kv, b_norm, b_w_in, b_sinks, b_w_out, final_norm):
    B, T, D = x_prompt.shape
    BD, TD, _ = x_sample.shape
    n_a = a_w_in.shape[0]
    n_b = b_w_in.shape[0]
    a_w_in_b, a_w_out_b = a_w_in.astype(BF16), a_w_out.astype(BF16)
    b_w_in_b, b_w_out_b = b_w_in.astype(BF16), b_w_out.astype(BF16)
    w_kv_b = w_kv.astype(BF16)

    tab_p = _rope_tables(np.arange(T))
    h = x_prompt
    st_p = []
    for l in range(n_a):
        kv = (kv_norm, w_kv_b, tab_p) if l == n_a - 1 else None
        h, s, *kv_p = _hgrn_prompt_layer(h, a_norm[l], a_w_in_b, a_lb_logits, l, a_gnorm[l],
                                         a_w_out_b, kv)
        st_p.append(s)
    k_p, v_p = kv_p
    for l in range(n_b):
        h = _swa_prompt_layer(h, k_p, v_p, tab_p, b_norm[l], b_w_in_b, l, b_sinks[l], b_w_out_b,
                              final_norm, l == n_b - 1)
    y_prompt = h
    w_keep = min(WINDOW, T)
    k_win = k_p[:, T - w_keep:].reshape(B, w_keep, ATT_KVH, ATT_HD)
    v_win = v_p[:, T - w_keep:].reshape(B, w_keep, ATT_KVH, ATT_HD)

    tab_s = _rope_tables(PAST_LEN + np.arange(TD))
    hs = x_sample.reshape(BD * TD, D)
    hs, st_s = _hgrn_decode(hs, state_hgrn, a_norm, a_w_in_b, a_lb_logits, a_gnorm, a_w_out_b)
    k_s, v_s = _shared_kv(hs.reshape(1, BD, D), kv_norm, w_kv_b,
                          tuple(jnp.broadcast_to(t, (BD, LANES)) for t in tab_s))
    k_s, v_s = k_s.reshape(BD, ATT_KVD), v_s.reshape(BD, ATT_KVD)
    for l in range(n_b):
        hs = _swa_decode_layer(hs, k_s, v_s, cache_k_win, cache_v_win, tab_s, b_norm[l], b_w_in_b, l,
                               b_sinks[l], b_w_out_b, final_norm, l == n_b - 1)
    y_sample = hs.reshape(BD, TD, D)

    return (y_prompt, y_sample, jnp.stack(st_p), st_s, k_win, v_win,
            k_s.reshape(BD, TD, ATT_KVH, ATT_HD), v_s.reshape(BD, TD, ATT_KVH, ATT_HD))
```

```python
import functools

import numpy as np
import jax
import jax.numpy as jnp
from jax import lax
from jax.experimental import pallas as pl
from jax.experimental.pallas import tpu as pltpu

F32 = jnp.float32
BF16 = jnp.bfloat16

D_MODEL = 1024
HG_HEADS = 8
HG_DK = 128
HG_DV = 128
HG_KEY = HG_HEADS * HG_DK
HG_VAL = HG_HEADS * HG_DV
HG_CHUNK = 64
HG_HALF = HG_CHUNK // 2
HG_SUB = 16
HG_MAX_HALF_DECAY = 60.0
ATT_HD = 64
ATT_QH = 16
ATT_KVH = 2
ATT_GROUP = ATT_QH // ATT_KVH
ATT_QD = ATT_QH * ATT_HD
ATT_KVD = ATT_KVH * ATT_HD
WINDOW = 128
ROPE_THETA = 500000.0
ROT_DIM = ATT_HD // 4
NORM_EPS = 1e-6
MASK_VALUE = -1e30
NEG_BIG = -1e30
LOG2E = 1.4426950408889634
PAST_LEN = 8192

SUBLANES = 8
LANES = 128
PROMPT_TILE = 512
SWA_TILE = 1024
DEC_BLOCK = 16
SWA_DEC_BLOCK = 16
V7X_VMEM_BYTES = 64 * 1024 * 1024
VMEM_LIMIT = V7X_VMEM_BYTES * 3 // 4


def _cparams(*sem):
    return pltpu.CompilerParams(dimension_semantics=sem, vmem_limit_bytes=VMEM_LIMIT)


def _layer_block(stacked, layer, single_buffer=False):
    tail = stacked.shape[1:]
    mode = dict(pipeline_mode=pl.Buffered(1)) if single_buffer else {}
    return pl.BlockSpec((None,) + tail, lambda *_: (layer,) + (0,) * len(tail), **mode)


def _mm(a, b):
    return jnp.dot(a.astype(BF16), b.astype(BF16), preferred_element_type=F32)


def _mm_nt(a, b):
    return lax.dot_general(a.astype(BF16), b.astype(BF16), (((1,), (1,)), ((), ())),
                           preferred_element_type=F32)


def _mm_tn(a, b):
    return lax.dot_general(a.astype(BF16), b.astype(BF16), (((0,), (0,)), ((), ())),
                           preferred_element_type=F32)


def _rms_scale(x, g):
    ms = jnp.mean(x * x, axis=-1, keepdims=True)
    return x * lax.rsqrt(ms + NORM_EPS) * g


def _silu(x):
    half = 0.5 * x
    return half + half * jnp.tanh(half)


def _lower_bound(lbl, layer):
    n = lbl.shape[0]
    rows = [lbl[i:i + 1, :] for i in range(n)]
    m = functools.reduce(jnp.maximum, rows)
    es = [jnp.exp(r - m) for r in rows]
    inv = 1.0 / functools.reduce(jnp.add, es)
    ps = [e * inv for e in es]
    return functools.reduce(jnp.add, ps[:layer + 1]) - ps[0]


def _forget_gates(f, lb):
    e = jnp.exp(-f)
    r = 1.0 / (1.0 + e)
    return lb + (1.0 - lb) * r, (1.0 - lb) * (e * r)


def _decay(x):
    return jnp.exp(x)


def _chunk_cumsum(x):
    sub = lax.broadcasted_iota(jnp.int32, (SUBLANES, x.shape[1]), 0)
    out = []
    for r0 in range(0, x.shape[0], SUBLANES):
        t = x[r0:r0 + SUBLANES]
        s = 1
        while s < SUBLANES:
            t = t + jnp.where(sub >= s, pltpu.roll(t, s, axis=0), 0.0)
            s *= 2
        if r0 % HG_CHUNK:
            t = t + jnp.broadcast_to(out[-1][SUBLANES - 1:], t.shape)
        out.append(t)
    return jnp.concatenate(out, axis=0)


def _diag_blocks(q, k, v, b):
    sub_iota = lax.broadcasted_iota(jnp.int32, (SUBLANES, LANES), 0)
    outs = []
    for blk in range(HG_CHUNK // HG_SUB):
        base = blk * HG_SUB
        groups = HG_SUB // SUBLANES
        qs = [q[base + g * SUBLANES: base + (g + 1) * SUBLANES] for g in range(groups)]
        bs = [b[base + g * SUBLANES: base + (g + 1) * SUBLANES] for g in range(groups)]
        accs = [jnp.zeros((SUBLANES, HG_DV), F32) for _ in range(groups)]
        for s in range(HG_SUB):
            row = base + s
            k_s = k[row:row + 1]
            b_s = b[row:row + 1]
            v_s = v[row:row + 1]
            for g in range(s // SUBLANES, groups):
                d = bs[g] - b_s
                if g == s // SUBLANES:
                    d = jnp.where(sub_iota >= (s % SUBLANES), d, NEG_BIG)
                p = qs[g] * (k_s * _decay(d))
                a = jnp.sum(p, axis=-1, keepdims=True)
                accs[g] = accs[g] + a * v_s
        outs.extend(accs)
    return jnp.concatenate(outs, axis=0)


def _hgrn_chunk(q, k, v, b, st):
    C, H = HG_CHUNK, HG_HALF
    o_inter = _mm_nt(q * _decay(b), st)
    r0 = b[H - 1:H]
    a0 = _mm_nt(q[H:] * _decay(b[H:] - r0), k[:H] * _decay(r0 - b[:H]))
    o_hi = _mm(a0, v[:H])
    Q = HG_SUB
    o_q = []
    for base in (0, H):
        r1 = b[base + Q - 1: base + Q]
        a1 = _mm_nt(q[base + Q: base + 2 * Q] * _decay(b[base + Q: base + 2 * Q] - r1),
                    k[base: base + Q] * _decay(r1 - b[base: base + Q]))
        o_q.append(_mm(a1, v[base: base + Q]))
    zero = jnp.zeros((Q, HG_DV), F32)
    o_off = jnp.concatenate([zero, o_q[0], o_hi[:Q], o_hi[Q:] + o_q[1]], axis=0)
    o = o_inter + o_off + _diag_blocks(q, k, v, b)
    b_end = b[C - 1:C]
    st_new = st * _decay(b_end) + _mm_tn(v, k * _decay(b_end - b))
    return o, st_new


def _hgrn_bounded_first(q, k, v, b, st):
    C, H = HG_CHUNK, HG_HALF
    r = b[H - 1:H]
    eq = _decay(b - r)
    qt, kt = q * eq, k * _decay(r - b)
    e_mid, e_hi = _decay(r), eq[C - 1:C]
    scores = _mm_nt(qt, kt)
    v_t = v.T
    st_new = st * (e_mid * e_hi) + _mm(v_t, kt * e_hi)
    return scores, qt * e_mid, v_t, st_new


def _hgrn_bounded_second(scores, q_dec, v_t, st):
    t_i = lax.broadcasted_iota(jnp.int32, scores.shape, 0)
    s_i = lax.broadcasted_iota(jnp.int32, scores.shape, 1)
    a = jnp.where(s_i <= t_i, scores, 0.0)
    return _mm_nt(jnp.concatenate([q_dec, a], axis=1), jnp.concatenate([st, v_t], axis=1))


def _max_half_decay(b_s, tile):
    n = tile // HG_CHUNK
    worst = None
    for h in range(HG_HEADS):
        mid = b_s[h, pl.ds(HG_HALF - 1, n, stride=HG_CHUNK), :]
        end = b_s[h, pl.ds(HG_CHUNK - 1, n, stride=HG_CHUNK), :]
        w = jnp.maximum(-mid, mid - end)
        worst = w if worst is None else jnp.maximum(worst, w)
    return jnp.max(worst)


def _hgrn_prompt_kernel(layer, with_kv, *refs):
    h_ref, ng_ref, win_ref, lbl_ref, gn_ref, wout_ref = refs[:6]
    refs = refs[6:]
    if with_kv:
        kv_in, refs = refs[:5], refs[5:]
    out_ref, sfin_ref = refs[:2]
    refs = refs[2:]
    if with_kv:
        kv_out, refs = refs[:2], refs[2:]
    u_s, q_s, k_s, v_s, b_s, o_s, st_s = refs
    t = pl.program_id(1)
    nt = pl.num_programs(1)
    tile = h_ref.shape[1]

    @pl.when(t == 0)
    def _():
        st_s[...] = jnp.zeros_like(st_s)

    x = h_ref[0]
    u_s[...] = _mm(_rms_scale(x, ng_ref[...]), win_ref[...])
    u = u_s
    lb = _lower_bound(lbl_ref[...], layer)
    for h in range(HG_HEADS):
        sl = slice(h * HG_DK, (h + 1) * HG_DK)
        q_s[h] = _silu(u[:, sl])
        fg, kin = _forget_gates(u[:, HG_KEY + h * HG_DK: HG_KEY + (h + 1) * HG_DK], lb[:, sl])
        k_s[h] = kin
        b_s[h] = _chunk_cumsum(jnp.log(fg))
        v_s[h] = u[:, 2 * HG_KEY + h * HG_DV: 2 * HG_KEY + (h + 1) * HG_DV]

    def chunk_rows(c):
        return pl.ds(pl.multiple_of(c * HG_CHUNK, HG_CHUNK), HG_CHUNK)

    def load(c, h):
        rows = chunk_rows(c)
        return q_s[h, rows, :], k_s[h, rows, :], v_s[h, rows, :], b_s[h, rows, :], st_s[h]

    def bounded_chunks():
        pending = []
        for c in range(tile // HG_CHUNK):
            rows = slice(c * HG_CHUNK, (c + 1) * HG_CHUNK)
            started = []
            for h in range(HG_HEADS):
                st = st_s[h]
                scores, q_dec, v_t, st_new = _hgrn_bounded_first(
                    q_s[h, rows, :], k_s[h, rows, :], v_s[h, rows, :], b_s[h, rows, :], st)
                st_s[h] = st_new
                started.append((h, rows, scores, q_dec, v_t, st))
            for h, prev_rows, *second in pending:
                o_s[h, prev_rows, :] = _hgrn_bounded_second(*second)
            pending = started
        for h, prev_rows, *second in pending:
            o_s[h, prev_rows, :] = _hgrn_bounded_second(*second)

    def general_body(i, carry):
        c, h = i // HG_HEADS, i % HG_HEADS
        o, st_new = _hgrn_chunk(*load(c, h))
        o_s[h, chunk_rows(c), :] = o
        st_s[h] = st_new
        return carry

    bounded = _max_half_decay(b_s, tile) < HG_MAX_HALF_DECAY

    pl.when(bounded)(bounded_chunks)

    @pl.when(jnp.logical_not(bounded))
    def _():
        lax.fori_loop(0, (tile // HG_CHUNK) * HG_HEADS, general_body, 0)

    gn = gn_ref[...]
    gated = []
    for h in range(HG_HEADS):
        g = u[:, 2 * HG_KEY + HG_VAL + h * HG_DV: 2 * HG_KEY + HG_VAL + (h + 1) * HG_DV]
        gated.append(_rms_scale(o_s[h], gn) * _silu(g))
    y = x + _mm(jnp.concatenate(gated, axis=1), wout_ref[...])
    out_ref[0] = y
    if with_kv:
        _kv_project(y, *kv_in, *kv_out)

    @pl.when(t == nt - 1)
    def _():
        for h in range(HG_HEADS):
            sfin_ref[0, h] = st_s[h].T


def _hgrn_prompt_layer(h, norm_g, w_in, lb_logits, layer, gnorm, w_out, kv=None):
    B, T, D = h.shape
    tile = min(PROMPT_TILE, T)
    assert T % tile == 0 and tile % HG_CHUNK == 0 and HG_DK == HG_DV == LANES
    n_a = lb_logits.shape[0]
    const = lambda b, t: (0, 0)
    row_tile = lambda width: pl.BlockSpec((1, tile, width), lambda b, t: (b, t, 0))
    head_scratch = pltpu.VMEM((HG_HEADS, tile, HG_DK), F32)
    operands = [h, norm_g.reshape(1, D), w_in, lb_logits, gnorm.reshape(1, HG_DV), w_out]
    in_specs = [row_tile(D), pl.BlockSpec((1, D), const), _layer_block(w_in, layer, True),
                pl.BlockSpec((n_a, HG_KEY), const), pl.BlockSpec((1, HG_DV), const),
                _layer_block(w_out, layer, True)]
    out_specs = [row_tile(D), pl.BlockSpec((1, HG_HEADS, HG_DK, HG_DV), lambda b, t: (b, 0, 0, 0))]
    out_shape = [jax.ShapeDtypeStruct((B, T, D), F32),
                 jax.ShapeDtypeStruct((B, HG_HEADS, HG_DK, HG_DV), F32)]
    if kv is not None:
        kv_norm, w_kv, tables = kv
        operands += [kv_norm.reshape(1, D), w_kv, *tables]
        in_specs += [pl.BlockSpec((1, D), const), pl.BlockSpec(w_kv.shape, const)]
        in_specs += [pl.BlockSpec((tile, LANES), lambda b, t: (t, 0))] * 3
        out_specs += [row_tile(ATT_KVD)] * 2
        out_shape += [jax.ShapeDtypeStruct((B, T, ATT_KVD), F32)] * 2
    return pl.pallas_call(
        functools.partial(_hgrn_prompt_kernel, layer, kv is not None),
        grid=(B, T // tile),
        in_specs=in_specs,
        out_specs=out_specs,
        out_shape=out_shape,
        scratch_shapes=[pltpu.VMEM((tile, w_in.shape[-1]), F32)] + [head_scratch] * 5
                       + [pltpu.VMEM((HG_HEADS, HG_DV, HG_DK), F32)],
        compiler_params=_cparams("arbitrary", "arbitrary"),
        name=f"hgrn_prompt_l{layer}",
    )(*operands)


def _rope_tables(pos):
    half = ROT_DIM // 2
    pos = np.asarray(pos, np.float64)
    inv_freq = 1.0 / (ROPE_THETA ** (np.arange(half, dtype=np.float64) * 2.0 / ROT_DIM))
    ang = (pos[:, None].astype(np.float32) * inv_freq[None, :].astype(np.float32)).astype(np.float64)
    cos, sin = np.cos(ang), np.sin(ang)
    n = pos.shape[0]
    c = np.ones((n, ATT_HD)); s_up = np.zeros((n, ATT_HD)); s_dn = np.zeros((n, ATT_HD))
    c[:, :half] = cos; c[:, half:ROT_DIM] = cos
    s_dn[:, :half] = -sin
    s_up[:, half:ROT_DIM] = sin
    rep = LANES // ATT_HD
    tab = [np.tile(a, (1, rep)).astype(np.float32) for a in (c, s_up, s_dn)]
    return tuple(jnp.asarray(a) for a in tab)


def _rope_cols(x, cos, s_up, s_dn):
    half = ROT_DIM // 2
    cols = []
    for c in range(x.shape[1] // LANES):
        xc = x[:, c * LANES:(c + 1) * LANES]
        cols.append(xc * cos + pltpu.roll(xc, half, axis=1) * s_up
                    + pltpu.roll(xc, LANES - half, axis=1) * s_dn)
    return cols[0] if len(cols) == 1 else jnp.concatenate(cols, axis=1)


def _kv_kernel(h_ref, *refs):
    _kv_project(h_ref[0], *refs)


def _kv_project(x, ng_ref, w_ref, cos_ref, sup_ref, sdn_ref, k_ref, v_ref):
    u = _mm(_rms_scale(x, ng_ref[...]), w_ref[...])
    k_ref[0] = _rope_cols(u[:, :ATT_KVD], cos_ref[...], sup_ref[...], sdn_ref[...])
    v_ref[0] = u[:, ATT_KVD:]


def _shared_kv(h, kv_norm, w_kv, tables):
    B, T, D = h.shape
    tile = min(PROMPT_TILE, T)
    const = lambda b, t: (0, 0)
    tab = pl.BlockSpec((tile, LANES), lambda b, t: (t, 0))
    kv_spec = pl.BlockSpec((1, tile, ATT_KVD), lambda b, t: (b, t, 0))
    return pl.pallas_call(
        _kv_kernel,
        grid=(B, T // tile),
        in_specs=[pl.BlockSpec((1, tile, D), lambda b, t: (b, t, 0)),
                  pl.BlockSpec((1, D), const), pl.BlockSpec(w_kv.shape, const), tab, tab, tab],
        out_specs=[kv_spec, kv_spec],
        out_shape=[jax.ShapeDtypeStruct((B, T, ATT_KVD), F32)] * 2,
        compiler_params=_cparams("arbitrary", "arbitrary"),
        name="shared_kv",
    )(h, kv_norm.reshape(1, D), w_kv, *tables)


PAIRS_PER_KV = ATT_GROUP // 2


def _swa_attention(blocks, sinks_ref, side_product):
    groups = [(blk, kvh) for blk in range(len(blocks)) for kvh in range(ATT_KVH)]
    n_keys = blocks[0][4].shape[0]
    ones_rows = jnp.ones((SUBLANES, n_keys), F32)

    def score_products(g):
        blk, kvh = groups[g]
        q_cols, k_lo, k_hi, _, bias_t = blocks[blk]
        qs = jnp.concatenate(q_cols[kvh * PAIRS_PER_KV:(kvh + 1) * PAIRS_PER_KV], axis=0)
        keys = jnp.concatenate([k_lo[kvh], k_hi[kvh]], axis=0)
        bias = jnp.concatenate([jnp.concatenate([bias_t] * PAIRS_PER_KV, axis=1)] * 2, axis=0)
        return _mm_nt(keys, qs) + bias

    def softmax_terms(g, s_t):
        kvh = groups[g][1]
        cols = range(kvh * PAIRS_PER_KV, (kvh + 1) * PAIRS_PER_KV)
        terms = []
        for par in range(2):
            s = s_t[par * n_keys:(par + 1) * n_keys]
            sink = jnp.concatenate(
                [jnp.full((1, WINDOW), sinks_ref[0, 2 * c + par] * LOG2E, F32) for c in cols], axis=1)
            m = jnp.maximum(jnp.max(s, axis=0, keepdims=True), sink)
            terms.append((jnp.exp2(s - m), jnp.exp2(sink - m)))
        return terms

    def value_products(g, terms):
        blk, kvh = groups[g]
        v_t = blocks[blk][3]
        lhs = jnp.concatenate([v_t[kvh * ATT_HD:(kvh + 1) * ATT_HD], ones_rows], axis=0)
        halves = []
        for p, sink_term in terms:
            r = _mm(lhs, p)
            halves.append(r[:ATT_HD] * (1.0 / (r[ATT_HD:ATT_HD + 1] + sink_term)))
        return halves

    scores = [score_products(g) for g in range(len(groups))]
    side = side_product()
    terms = [softmax_terms(g, s) for g, s in enumerate(scores)]
    outs = [value_products(g, t) for g, t in enumerate(terms)]
    res = []
    for b in range(len(blocks)):
        cols = []
        for kvh in range(ATT_KVH):
            even, odd = outs[b * ATT_KVH + kvh]
            for i in range(PAIRS_PER_KV):
                queries = slice(i * WINDOW, (i + 1) * WINDOW)
                cols.append(jnp.concatenate([even[:, queries], odd[:, queries]], axis=0).T)
        res.append(jnp.concatenate(cols, axis=1))
    return res, side


def _split_kv_heads(a, fill):
    lane = lax.broadcasted_iota(jnp.int32, a.shape, 1)
    low = lane < ATT_HD
    sw = pltpu.roll(a, ATT_HD, axis=1)
    lo = [jnp.where(low, a, fill), jnp.where(low, sw, fill)]
    hi = [jnp.where(low, fill, sw), jnp.where(low, fill, a)]
    return lo, hi


def _swa_prompt_kernel(final, h_ref, ng_ref, win_ref, sinks_ref, wout_ref, fin_ref,
                       kp_ref, kc_ref, vp_ref, vc_ref, cos_ref, sup_ref, sdn_ref, out_ref):
    t = pl.program_id(1)
    tile = h_ref.shape[1]
    nblk = tile // WINDOW
    x = h_ref[0]
    xn = _rms_scale(x, ng_ref[...]).astype(BF16)
    u = _mm(xn, win_ref[:, :ATT_QD])
    scale = ATT_HD ** -0.5 * LOG2E
    cos, sup, sdn = cos_ref[...], sup_ref[...], sdn_ref[...]
    k_all = jnp.concatenate([kp_ref[0], kc_ref[0]], axis=0)
    v_all = jnp.concatenate([vp_ref[0], vc_ref[0]], axis=0)
    j = lax.broadcasted_iota(jnp.int32, (2 * WINDOW, WINDOW), 0)
    tq = lax.broadcasted_iota(jnp.int32, (2 * WINDOW, WINDOW), 1)
    rel = tq + WINDOW - j
    band = (rel >= 0) & (rel <= WINDOW)
    blocks = []
    for blk in range(nblk):
        rows = slice(blk * WINDOW, (blk + 1) * WINDOW)
        keys = slice(blk * WINDOW, (blk + 2) * WINDOW)
        k_lo, k_hi = _split_kv_heads(k_all[keys], 0.0)
        if blk == 0:
            valid = band & (j >= WINDOW * (t == 0).astype(jnp.int32))
        else:
            valid = band
        q_cols = []
        for c in range(ATT_QD // LANES):
            qc = u[rows, c * LANES:(c + 1) * LANES]
            q_cols.append(_rope_cols(qc, cos[rows], sup[rows], sdn[rows]) * scale)
        blocks.append((q_cols, k_lo, k_hi, v_all[keys].T, jnp.where(valid, 0.0, MASK_VALUE)))
    half = max(nblk // 2, 1)
    o_blocks, gate = _swa_attention(blocks[:half], sinks_ref, lambda: _mm(xn, win_ref[:, ATT_QD:]))
    if nblk > half:
        o_blocks += _swa_attention(blocks[half:], sinks_ref, lambda: None)[0]
    o = jnp.concatenate(o_blocks, axis=0) if nblk > 1 else o_blocks[0]
    y = x + _mm(o * _silu(gate), wout_ref[...])
    if final:
        y = _rms_scale(y, fin_ref[...])
    out_ref[0] = y


def _swa_prompt_layer(h, k, v, tables, norm_g, w_in, layer, sinks, w_out, final_norm, final):
    B, T, D = h.shape
    tile = min(SWA_TILE, T)
    assert T % tile == 0 and tile % WINDOW == 0
    per = tile // WINDOW
    const = lambda b, t: (0, 0)
    tab = pl.BlockSpec((tile, LANES), lambda b, t: (t, 0))
    prev = pl.BlockSpec((1, WINDOW, ATT_KVD), lambda b, t: (b, jnp.maximum(t * per - 1, 0), 0))
    cur = pl.BlockSpec((1, tile, ATT_KVD), lambda b, t: (b, t, 0))
    return pl.pallas_call(
        functools.partial(_swa_prompt_kernel, final),
        grid=(B, T // tile),
        in_specs=[
            pl.BlockSpec((1, tile, D), lambda b, t: (b, t, 0)),
            pl.BlockSpec((1, D), const),
            _layer_block(w_in, layer, True),
            pl.BlockSpec(memory_space=pltpu.SMEM),
            _layer_block(w_out, layer, True),
            pl.BlockSpec((1, D), const),
            prev, cur, prev, cur, tab, tab, tab,
        ],
        out_specs=pl.BlockSpec((1, tile, D), lambda b, t: (b, t, 0)),
        out_shape=jax.ShapeDtypeStruct((B, T, D), F32),
        compiler_params=_cparams("arbitrary", "arbitrary"),
        name="swa_prompt_final" if final else "swa_prompt",
    )(h, norm_g.reshape(1, D), w_in, sinks.reshape(1, ATT_QH), w_out, final_norm.reshape(1, D),
      k, k, v, v, *tables)


def _col(row):
    n = row.shape[1]
    return jnp.broadcast_to(row, (n, n)).T


def _hgrn_decode_kernel(x_ref, s_ref, ng_ref, win_ref, lbl_ref, gn_ref, wout_ref,
                        snew_ref, hout_ref, h_s, q_s, f_s, v_s, g_s, o_s, oblk_s):
    l, i = pl.program_id(0), pl.program_id(1)
    n_layers, n_blocks = pl.num_programs(0), pl.num_programs(1)
    bb = s_ref.shape[1]

    @pl.when(i == 0)
    def _():
        @pl.when(l == 0)
        def _():
            h_s[...] = x_ref[...]

        u = _mm(_rms_scale(h_s[...], ng_ref[0]), win_ref[0])
        lbl = lbl_ref[...]
        lb = jnp.zeros_like(lbl[:1])
        for layer in range(lbl.shape[0]):
            lb = jnp.where(l == layer, _lower_bound(lbl, layer), lb)
        q_s[...] = _silu(u[:, :HG_KEY])
        f_s[...] = _forget_gates(u[:, HG_KEY:2 * HG_KEY], lb)[0]
        v_s[...] = u[:, 2 * HG_KEY:2 * HG_KEY + HG_VAL]
        g_s[...] = _silu(u[:, 2 * HG_KEY + HG_VAL:])

    rows = pl.ds(pl.multiple_of(i * bb, bb), bb)
    qb, fb, vb = q_s[rows, :], f_s[rows, :], v_s[rows, :]
    for j in range(bb):
        for h in range(HG_HEADS):
            sl = slice(h * HG_DK, (h + 1) * HG_DK)
            v_r = vb[j:j + 1, h * HG_DV:(h + 1) * HG_DV]
            s_new = _col(fb[j:j + 1, sl]) * (s_ref[0, j, h] - v_r) + v_r
            snew_ref[0, j, h] = s_new
            oblk_s[j:j + 1, h * HG_DV:(h + 1) * HG_DV] = _mm(qb[j:j + 1, sl], s_new)
    o_s[rows, :] = oblk_s[...]

    @pl.when(i == n_blocks - 1)
    def _():
        gn = gn_ref[0]
        gated = []
        for h in range(HG_HEADS):
            sl = slice(h * HG_DV, (h + 1) * HG_DV)
            gated.append(_rms_scale(o_s[:, sl], gn) * g_s[:, sl])
        h_new = h_s[...] + _mm(jnp.concatenate(gated, axis=1), wout_ref[0])
        h_s[...] = h_new

        @pl.when(l == n_layers - 1)
        def _():
            hout_ref[...] = h_new


def _whole(shape):
    return pl.BlockSpec(shape, lambda *_: (0,) * len(shape))


def _hgrn_decode(h, states, norms, w_in, lb_logits, gnorms, w_out):
    B, D = h.shape
    n_a = states.shape[0]
    bb = min(DEC_BLOCK, B)
    assert B % bb == 0
    per_layer = lambda *tail: pl.BlockSpec((1,) + tail, lambda l, i: (l,) + (0,) * len(tail),
                                           pipeline_mode=pl.Buffered(1))
    st_blk = pl.BlockSpec((1, bb, HG_HEADS, HG_DK, HG_DV), lambda l, i: (l, i, 0, 0, 0))
    wide = pltpu.VMEM((B, HG_KEY), F32)
    s_new, h_new = pl.pallas_call(
        _hgrn_decode_kernel,
        grid=(n_a, B // bb),
        in_specs=[_whole((B, D)), st_blk, per_layer(1, D), per_layer(*w_in.shape[1:]),
                  _whole(lb_logits.shape), per_layer(1, HG_DV), per_layer(*w_out.shape[1:])],
        out_specs=[st_blk, _whole((B, D))],
        out_shape=[jax.ShapeDtypeStruct(states.shape, F32), jax.ShapeDtypeStruct((B, D), F32)],
        scratch_shapes=[pltpu.VMEM((B, D), F32)] + [wide] * 5 + [pltpu.VMEM((bb, HG_VAL), F32)],
        compiler_params=_cparams("arbitrary", "arbitrary"),
        name="hgrn_decode",
    )(h, states, norms.reshape(n_a, 1, D), w_in, lb_logits, gnorms.reshape(n_a, 1, HG_DV), w_out)
    return h_new, s_new


def _bf16_round(x):
    return x.astype(BF16).astype(F32)


def _swa_decode_kernel(final, h_ref, ng_ref, win_ref, cos_ref, sup_ref, sdn_ref, kc_ref, vc_ref,
                       kn_ref, vn_ref, sinks_ref, wout_ref, fin_ref, out_ref, qp_s, g_s, os_s):
    i = pl.program_id(0)
    n_blocks = pl.num_programs(0)
    bb, W = kc_ref.shape[0], kc_ref.shape[1]
    B = h_ref.shape[0]
    lane = lax.broadcasted_iota(jnp.int32, (B, LANES), 1)

    @pl.when(i == 0)
    def _():
        u = _mm(_rms_scale(h_ref[...], ng_ref[...]), win_ref[...])
        q = _rope_cols(u[:, :ATT_QD], cos_ref[...], sup_ref[...], sdn_ref[...]) * (ATT_HD ** -0.5)
        g_s[...] = _silu(u[:, ATT_QD:])
        for head in range(ATT_QH):
            col = q[:, (head // 2) * LANES:(head // 2 + 1) * LANES]
            kvh = head // ATT_GROUP
            if head % 2 != kvh:
                col = pltpu.roll(col, ATT_HD, axis=1)
            qp_s[:, head, :] = jnp.where((lane >= ATT_HD) if kvh == 1 else (lane < ATT_HD), col, 0.0)

    rows = pl.ds(pl.multiple_of(i * bb, bb), bb)
    kn, vn = kn_ref[rows, :], vn_ref[rows, :]
    qp = [qp_s[i * bb + j] for j in range(bb)]
    s_c = jnp.concatenate([_mm_nt(qp[j], kc_ref[j]) for j in range(bb)], axis=0)
    qp_all = jnp.concatenate(qp, axis=0)
    expand = lambda a: jnp.concatenate(
        [jnp.broadcast_to(a[j:j + 1], (ATT_QH, ATT_KVD)) for j in range(bb)], axis=0)
    s_n = jnp.sum(_bf16_round(qp_all) * _bf16_round(expand(kn)), axis=-1, keepdims=True)
    sink = jnp.concatenate([sinks_ref[...]] * bb, axis=0)
    m = jnp.maximum(jnp.maximum(jnp.max(s_c, axis=-1, keepdims=True), s_n), sink)
    p_c = jnp.exp(s_c - m)
    p_n = jnp.exp(s_n - m)
    inv = 1.0 / (jnp.sum(p_c, axis=-1, keepdims=True) + p_n + jnp.exp(sink - m))
    p_c = p_c * inv
    o = jnp.concatenate([_mm(p_c[j * ATT_QH:(j + 1) * ATT_QH], vc_ref[j]) for j in range(bb)], axis=0)
    o = o + _bf16_round(p_n * inv) * _bf16_round(expand(vn))
    for j in range(bb):
        os_s[i * bb + j] = o[j * ATT_QH:(j + 1) * ATT_QH]

    @pl.when(i == n_blocks - 1)
    def _():
        cols = []
        for c in range(ATT_QD // LANES):
            halves = []
            for par in range(2):
                head = 2 * c + par
                t = os_s[:, head, :]
                if par != head // ATT_GROUP:
                    t = pltpu.roll(t, ATT_HD, axis=1)
                halves.append(t)
            cols.append(jnp.where(lane < ATT_HD, halves[0], halves[1]))
        y = h_ref[...] + _mm(jnp.concatenate(cols, axis=1) * g_s[...], wout_ref[...])
        if final:
            y = _rms_scale(y, fin_ref[...])
        out_ref[...] = y


def _swa_decode_layer(h, k_new, v_new, cache_k, cache_v, tables, norm_g, w_in, layer, sinks, w_out,
                      final_norm, final):
    B, D = h.shape
    W = cache_k.shape[1]
    bb = min(SWA_DEC_BLOCK, B)
    assert B % bb == 0
    cache_blk = pl.BlockSpec((bb, W, ATT_KVD), lambda i: (i, 0, 0))
    head_rows = pltpu.VMEM((B, ATT_QH, ATT_KVD), F32)
    return pl.pallas_call(
        functools.partial(_swa_decode_kernel, final),
        grid=(B // bb,),
        in_specs=[_whole((B, D)), _whole((1, D)), _layer_block(w_in, layer)] + [_whole((1, LANES))] * 3
                 + [cache_blk, cache_blk, _whole((B, ATT_KVD)), _whole((B, ATT_KVD)),
                    _whole((ATT_QH, 1)), _layer_block(w_out, layer), _whole((1, D))],
        out_specs=_whole((B, D)),
        out_shape=jax.ShapeDtypeStruct((B, D), F32),
        scratch_shapes=[head_rows, pltpu.VMEM((B, ATT_QD), F32), head_rows],
        compiler_params=_cparams("arbitrary"),
        name="swa_decode_final" if final else "swa_decode",
    )(h, norm_g.reshape(1, D), w_in, *tables, cache_k.reshape(B, W, ATT_KVD),
      cache_v.reshape(B, W, ATT_KVD), k_new, v_new, sinks.reshape(ATT_QH, 1), w_out,
      final_norm.reshape(1, D))


def kernel(x_prompt, x_sample, state_hgrn, cache_k_win, cache_v_win, a_norm, a_w_in, a_lb_logits,
           a_gnorm, a_w_out, kv_norm, w_kv, b_norm, b_w_in, b_sinks, b_w_out, final_norm):
    B, T, D = x_prompt.shape
    BD, TD, _ = x_sample.shape
    n_a = a_w_in.shape[0]
    n_b = b_w_in.shape[0]
    a_w_in_b, a_w_out_b = a_w_in.astype(BF16), a_w_out.astype(BF16)
    b_w_in_b, b_w_out_b = b_w_in.astype(BF16), b_w_out.astype(BF16)
    w_kv_b = w_kv.astype(BF16)

    tab_p = _rope_tables(np.arange(T))
    h = x_prompt
    st_p = []
    for l in range(n_a):
        kv = (kv_norm, w_kv_b, tab_p) if l == n_a - 1 else None
        h, s, *kv_p = _hgrn_prompt_layer(h, a_norm[l], a_w_in_b, a_lb_logits, l, a_gnorm[l],
                                         a_w_out_b, kv)
        st_p.append(s)
    k_p, v_p = kv_p
    for l in range(n_b):
        h = _swa_prompt_layer(h, k_p, v_p, tab_p, b_norm[l], b_w_in_b, l, b_sinks[l], b_w_out_b,
                              final_norm, l == n_b - 1)
    y_prompt = h
    w_keep = min(WINDOW, T)
    k_win = k_p[:, T - w_keep:].reshape(B, w_keep, ATT_KVH, ATT_HD)
    v_win = v_p[:, T - w_keep:].reshape(B, w_keep, ATT_KVH, ATT_HD)

    tab_s = _rope_tables(PAST_LEN + np.arange(TD))
    hs = x_sample.reshape(BD * TD, D)
    hs, st_s = _hgrn_decode(hs, state_hgrn, a_norm, a_w_in_b, a_lb_logits, a_gnorm, a_w_out_b)
    k_s, v_s = _shared_kv(hs.reshape(1, BD, D), kv_norm, w_kv_b,
                          tuple(jnp.broadcast_to(t, (BD, LANES)) for t in tab_s))
    k_s, v_s = k_s.reshape(BD, ATT_KVD), v_s.reshape(BD, ATT_KVD)
    for l in range(n_b):
        hs = _swa_decode_layer(hs, k_s, v_s, cache_k_win, cache_v_win, tab_s, b_norm[l], b_w_in_b, l,
                               b_sinks[l], b_w_out_b, final_norm, l == n_b - 1)
    y_sample = hs.reshape(BD, TD, D)

    return (y_prompt, y_sample, jnp.stack(st_p), st_s, k_win, v_win,
            k_s.reshape(BD, TD, ATT_KVH, ATT_HD), v_s.reshape(BD, TD, ATT_KVH, ATT_HD))
```

```python
import functools

import numpy as np
import jax
import jax.numpy as jnp
from jax import lax
from jax.experimental import pallas as pl
from jax.experimental.pallas import tpu as pltpu

F32 = jnp.float32
BF16 = jnp.bfloat16

D_MODEL = 1024
HG_HEADS = 8
HG_DK = 128
HG_DV = 128
HG_KEY = HG_HEADS * HG_DK
HG_VAL = HG_HEADS * HG_DV
HG_CHUNK = 64
HG_HALF = HG_CHUNK // 2
HG_SUB = 16
HG_MAX_HALF_DECAY = 60.0
ATT_HD = 64
ATT_QH = 16
ATT_KVH = 2
ATT_GROUP = ATT_QH // ATT_KVH
ATT_QD = ATT_QH * ATT_HD
ATT_KVD = ATT_KVH * ATT_HD
WINDOW = 128
ROPE_THETA = 500000.0
ROT_DIM = ATT_HD // 4
NORM_EPS = 1e-6
MASK_VALUE = -1e30
NEG_BIG = -1e30
LOG2E = 1.4426950408889634
PAST_LEN = 8192

SUBLANES = 8
LANES = 128
PROMPT_TILE = 512
SWA_TILE = 1024
SWA_PHASE_BLOCKS = 2
DEC_BLOCK = 16
SWA_DEC_BLOCK = 16
V7X_VMEM_BYTES = 64 * 1024 * 1024
VMEM_LIMIT = V7X_VMEM_BYTES * 3 // 4


def _cparams(*sem):
    return pltpu.CompilerParams(dimension_semantics=sem, vmem_limit_bytes=VMEM_LIMIT)


def _layer_block(stacked, layer, single_buffer=False):
    tail = stacked.shape[1:]
    mode = dict(pipeline_mode=pl.Buffered(1)) if single_buffer else {}
    return pl.BlockSpec((None,) + tail, lambda *_: (layer,) + (0,) * len(tail), **mode)


def _mm(a, b):
    return jnp.dot(a.astype(BF16), b.astype(BF16), preferred_element_type=F32)


def _mm_nt(a, b):
    return lax.dot_general(a.astype(BF16), b.astype(BF16), (((1,), (1,)), ((), ())),
                           preferred_element_type=F32)


def _mm_tn(a, b):
    return lax.dot_general(a.astype(BF16), b.astype(BF16), (((0,), (0,)), ((), ())),
                           preferred_element_type=F32)


def _rms_scale(x, g):
    ms = jnp.mean(x * x, axis=-1, keepdims=True)
    return x * lax.rsqrt(ms + NORM_EPS) * g


def _silu(x):
    half = 0.5 * x
    return half + half * jnp.tanh(half)


def _lower_bound(lbl, layer):
    n = lbl.shape[0]
    rows = [lbl[i:i + 1, :] for i in range(n)]
    m = functools.reduce(jnp.maximum, rows)
    es = [jnp.exp(r - m) for r in rows]
    inv = 1.0 / functools.reduce(jnp.add, es)
    ps = [e * inv for e in es]
    return functools.reduce(jnp.add, ps[:layer + 1]) - ps[0]


def _forget_gates(f, lb):
    e = jnp.exp(-f)
    r = 1.0 / (1.0 + e)
    return lb + (1.0 - lb) * r, (1.0 - lb) * (e * r)


def _decay(x):
    return jnp.exp(x)


def _chunk_cumsum(x):
    sub = lax.broadcasted_iota(jnp.int32, (SUBLANES, x.shape[1]), 0)
    out = []
    for r0 in range(0, x.shape[0], SUBLANES):
        t = x[r0:r0 + SUBLANES]
        s = 1
        while s < SUBLANES:
            t = t + jnp.where(sub >= s, pltpu.roll(t, s, axis=0), 0.0)
            s *= 2
        if r0 % HG_CHUNK:
            t = t + jnp.broadcast_to(out[-1][SUBLANES - 1:], t.shape)
        out.append(t)
    return jnp.concatenate(out, axis=0)


def _diag_blocks(q, k, v, b):
    sub_iota = lax.broadcasted_iota(jnp.int32, (SUBLANES, LANES), 0)
    outs = []
    for blk in range(HG_CHUNK // HG_SUB):
        base = blk * HG_SUB
        groups = HG_SUB // SUBLANES
        qs = [q[base + g * SUBLANES: base + (g + 1) * SUBLANES] for g in range(groups)]
        bs = [b[base + g * SUBLANES: base + (g + 1) * SUBLANES] for g in range(groups)]
        accs = [jnp.zeros((SUBLANES, HG_DV), F32) for _ in range(groups)]
        for s in range(HG_SUB):
            row = base + s
            k_s = k[row:row + 1]
            b_s = b[row:row + 1]
            v_s = v[row:row + 1]
            for g in range(s // SUBLANES, groups):
                d = bs[g] - b_s
                if g == s // SUBLANES:
                    d = jnp.where(sub_iota >= (s % SUBLANES), d, NEG_BIG)
                p = qs[g] * (k_s * _decay(d))
                a = jnp.sum(p, axis=-1, keepdims=True)
                accs[g] = accs[g] + a * v_s
        outs.extend(accs)
    return jnp.concatenate(outs, axis=0)


def _hgrn_chunk(q, k, v, b, st):
    C, H = HG_CHUNK, HG_HALF
    o_inter = _mm_nt(q * _decay(b), st)
    r0 = b[H - 1:H]
    a0 = _mm_nt(q[H:] * _decay(b[H:] - r0), k[:H] * _decay(r0 - b[:H]))
    o_hi = _mm(a0, v[:H])
    Q = HG_SUB
    o_q = []
    for base in (0, H):
        r1 = b[base + Q - 1: base + Q]
        a1 = _mm_nt(q[base + Q: base + 2 * Q] * _decay(b[base + Q: base + 2 * Q] - r1),
                    k[base: base + Q] * _decay(r1 - b[base: base + Q]))
        o_q.append(_mm(a1, v[base: base + Q]))
    zero = jnp.zeros((Q, HG_DV), F32)
    o_off = jnp.concatenate([zero, o_q[0], o_hi[:Q], o_hi[Q:] + o_q[1]], axis=0)
    o = o_inter + o_off + _diag_blocks(q, k, v, b)
    b_end = b[C - 1:C]
    st_new = st * _decay(b_end) + _mm_tn(v, k * _decay(b_end - b))
    return o, st_new


def _hgrn_bounded_first(q, k, v, b, st):
    C, H = HG_CHUNK, HG_HALF
    r = b[H - 1:H]
    eq = _decay(b - r)
    qt, kt = q * eq, k * _decay(r - b)
    e_mid, e_hi = _decay(r), eq[C - 1:C]
    scores = _mm_nt(qt, kt)
    v_t = v.T
    st_new = st * (e_mid * e_hi) + _mm(v_t, kt * e_hi)
    return scores, qt * e_mid, v_t, st_new


def _hgrn_bounded_second(scores, q_dec, v_t, st):
    t_i = lax.broadcasted_iota(jnp.int32, scores.shape, 0)
    s_i = lax.broadcasted_iota(jnp.int32, scores.shape, 1)
    a = jnp.where(s_i <= t_i, scores, 0.0)
    return _mm_nt(jnp.concatenate([q_dec, a], axis=1), jnp.concatenate([st, v_t], axis=1))


def _max_half_decay(b_s, tile):
    n = tile // HG_CHUNK
    worst = None
    for h in range(HG_HEADS):
        mid = b_s[h, pl.ds(HG_HALF - 1, n, stride=HG_CHUNK), :]
        end = b_s[h, pl.ds(HG_CHUNK - 1, n, stride=HG_CHUNK), :]
        w = jnp.maximum(-mid, mid - end)
        worst = w if worst is None else jnp.maximum(worst, w)
    return jnp.max(worst)


def _hgrn_prompt_kernel(layer, with_kv, *refs):
    h_ref, ng_ref, win_ref, lbl_ref, gn_ref, wout_ref = refs[:6]
    refs = refs[6:]
    if with_kv:
        kv_in, refs = refs[:5], refs[5:]
    out_ref, sfin_ref = refs[:2]
    refs = refs[2:]
    if with_kv:
        kv_out, refs = refs[:2], refs[2:]
    u_s, q_s, k_s, v_s, b_s, o_s, st_s = refs
    t = pl.program_id(1)
    nt = pl.num_programs(1)
    tile = h_ref.shape[1]

    @pl.when(t == 0)
    def _():
        st_s[...] = jnp.zeros_like(st_s)

    x = h_ref[0]
    u_s[...] = _mm(_rms_scale(x, ng_ref[...]), win_ref[...])
    u = u_s
    lb = _lower_bound(lbl_ref[...], layer)
    for h in range(HG_HEADS):
        sl = slice(h * HG_DK, (h + 1) * HG_DK)
        q_s[h] = _silu(u[:, sl])
        fg, kin = _forget_gates(u[:, HG_KEY + h * HG_DK: HG_KEY + (h + 1) * HG_DK], lb[:, sl])
        k_s[h] = kin
        b_s[h] = _chunk_cumsum(jnp.log(fg))
        v_s[h] = u[:, 2 * HG_KEY + h * HG_DV: 2 * HG_KEY + (h + 1) * HG_DV]

    def chunk_rows(c):
        return pl.ds(pl.multiple_of(c * HG_CHUNK, HG_CHUNK), HG_CHUNK)

    def load(c, h):
        rows = chunk_rows(c)
        return q_s[h, rows, :], k_s[h, rows, :], v_s[h, rows, :], b_s[h, rows, :], st_s[h]

    def bounded_chunks():
        pending = []
        for c in range(tile // HG_CHUNK):
            rows = slice(c * HG_CHUNK, (c + 1) * HG_CHUNK)
            started = []
            for h in range(HG_HEADS):
                st = st_s[h]
                scores, q_dec, v_t, st_new = _hgrn_bounded_first(
                    q_s[h, rows, :], k_s[h, rows, :], v_s[h, rows, :], b_s[h, rows, :], st)
                st_s[h] = st_new
                started.append((h, rows, scores, q_dec, v_t, st))
            for h, prev_rows, *second in pending:
                o_s[h, prev_rows, :] = _hgrn_bounded_second(*second)
            pending = started
        for h, prev_rows, *second in pending:
            o_s[h, prev_rows, :] = _hgrn_bounded_second(*second)

    def general_body(i, carry):
        c, h = i // HG_HEADS, i % HG_HEADS
        o, st_new = _hgrn_chunk(*load(c, h))
        o_s[h, chunk_rows(c), :] = o
        st_s[h] = st_new
        return carry

    bounded = _max_half_decay(b_s, tile) < HG_MAX_HALF_DECAY

    pl.when(bounded)(bounded_chunks)

    @pl.when(jnp.logical_not(bounded))
    def _():
        lax.fori_loop(0, (tile // HG_CHUNK) * HG_HEADS, general_body, 0)

    gn = gn_ref[...]
    gated = []
    for h in range(HG_HEADS):
        g = u[:, 2 * HG_KEY + HG_VAL + h * HG_DV: 2 * HG_KEY + HG_VAL + (h + 1) * HG_DV]
        gated.append(_rms_scale(o_s[h], gn) * _silu(g))
    y = x + _mm(jnp.concatenate(gated, axis=1), wout_ref[...])
    out_ref[0] = y
    if with_kv:
        _kv_project(y, *kv_in, *kv_out)

    @pl.when(t == nt - 1)
    def _():
        for h in range(HG_HEADS):
            sfin_ref[0, h] = st_s[h].T


def _hgrn_prompt_layer(h, norm_g, w_in, lb_logits, layer, gnorm, w_out, kv=None):
    B, T, D = h.shape
    tile = min(PROMPT_TILE, T)
    assert T % tile == 0 and tile % HG_CHUNK == 0 and HG_DK == HG_DV == LANES
    n_a = lb_logits.shape[0]
    const = lambda b, t: (0, 0)
    row_tile = lambda width: pl.BlockSpec((1, tile, width), lambda b, t: (b, t, 0))
    head_scratch = pltpu.VMEM((HG_HEADS, tile, HG_DK), F32)
    operands = [h, norm_g.reshape(1, D), w_in, lb_logits, gnorm.reshape(1, HG_DV), w_out]
    in_specs = [row_tile(D), pl.BlockSpec((1, D), const), _layer_block(w_in, layer, True),
                pl.BlockSpec((n_a, HG_KEY), const), pl.BlockSpec((1, HG_DV), const),
                _layer_block(w_out, layer, True)]
    out_specs = [row_tile(D), pl.BlockSpec((1, HG_HEADS, HG_DK, HG_DV), lambda b, t: (b, 0, 0, 0))]
    out_shape = [jax.ShapeDtypeStruct((B, T, D), F32),
                 jax.ShapeDtypeStruct((B, HG_HEADS, HG_DK, HG_DV), F32)]
    if kv is not None:
        kv_norm, w_kv, tables = kv
        operands += [kv_norm.reshape(1, D), w_kv, *tables]
        in_specs += [pl.BlockSpec((1, D), const), pl.BlockSpec(w_kv.shape, const)]
        in_specs += [pl.BlockSpec((tile, LANES), lambda b, t: (t, 0))] * 3
        out_specs += [row_tile(ATT_KVD)] * 2
        out_shape += [jax.ShapeDtypeStruct((B, T, ATT_KVD), F32)] * 2
    return pl.pallas_call(
        functools.partial(_hgrn_prompt_kernel, layer, kv is not None),
        grid=(B, T // tile),
        in_specs=in_specs,
        out_specs=out_specs,
        out_shape=out_shape,
        scratch_shapes=[pltpu.VMEM((tile, w_in.shape[-1]), F32)] + [head_scratch] * 5
                       + [pltpu.VMEM((HG_HEADS, HG_DV, HG_DK), F32)],
        compiler_params=_cparams("arbitrary", "arbitrary"),
        name=f"hgrn_prompt_l{layer}",
    )(*operands)


def _rope_tables(pos):
    half = ROT_DIM // 2
    pos = np.asarray(pos, np.float64)
    inv_freq = 1.0 / (ROPE_THETA ** (np.arange(half, dtype=np.float64) * 2.0 / ROT_DIM))
    ang = (pos[:, None].astype(np.float32) * inv_freq[None, :].astype(np.float32)).astype(np.float64)
    cos, sin = np.cos(ang), np.sin(ang)
    n = pos.shape[0]
    c = np.ones((n, ATT_HD)); s_up = np.zeros((n, ATT_HD)); s_dn = np.zeros((n, ATT_HD))
    c[:, :half] = cos; c[:, half:ROT_DIM] = cos
    s_dn[:, :half] = -sin
    s_up[:, half:ROT_DIM] = sin
    rep = LANES // ATT_HD
    tab = [np.tile(a, (1, rep)).astype(np.float32) for a in (c, s_up, s_dn)]
    return tuple(jnp.asarray(a) for a in tab)


def _rope_cols(x, cos, s_up, s_dn):
    half = ROT_DIM // 2
    cols = []
    for c in range(x.shape[1] // LANES):
        xc = x[:, c * LANES:(c + 1) * LANES]
        cols.append(xc * cos + pltpu.roll(xc, half, axis=1) * s_up
                    + pltpu.roll(xc, LANES - half, axis=1) * s_dn)
    return cols[0] if len(cols) == 1 else jnp.concatenate(cols, axis=1)


def _kv_kernel(h_ref, *refs):
    _kv_project(h_ref[0], *refs)


def _kv_project(x, ng_ref, w_ref, cos_ref, sup_ref, sdn_ref, k_ref, v_ref):
    u = _mm(_rms_scale(x, ng_ref[...]), w_ref[...])
    k_ref[0] = _rope_cols(u[:, :ATT_KVD], cos_ref[...], sup_ref[...], sdn_ref[...])
    v_ref[0] = u[:, ATT_KVD:]


def _shared_kv(h, kv_norm, w_kv, tables):
    B, T, D = h.shape
    tile = min(PROMPT_TILE, T)
    const = lambda b, t: (0, 0)
    tab = pl.BlockSpec((tile, LANES), lambda b, t: (t, 0))
    kv_spec = pl.BlockSpec((1, tile, ATT_KVD), lambda b, t: (b, t, 0))
    return pl.pallas_call(
        _kv_kernel,
        grid=(B, T // tile),
        in_specs=[pl.BlockSpec((1, tile, D), lambda b, t: (b, t, 0)),
                  pl.BlockSpec((1, D), const), pl.BlockSpec(w_kv.shape, const), tab, tab, tab],
        out_specs=[kv_spec, kv_spec],
        out_shape=[jax.ShapeDtypeStruct((B, T, ATT_KVD), F32)] * 2,
        compiler_params=_cparams("arbitrary", "arbitrary"),
        name="shared_kv",
    )(h, kv_norm.reshape(1, D), w_kv, *tables)


PAIRS_PER_KV = ATT_GROUP // 2


def _swa_attention(blocks, sinks_ref, side_product):
    groups = [(blk, kvh) for blk in range(len(blocks)) for kvh in range(ATT_KVH)]
    n_keys = blocks[0][4].shape[0]
    ones_rows = jnp.ones((SUBLANES, n_keys), F32)

    def score_products(g):
        blk, kvh = groups[g]
        q_cols, k_lo, k_hi, _, bias_t = blocks[blk]
        qs = jnp.concatenate(q_cols[kvh * PAIRS_PER_KV:(kvh + 1) * PAIRS_PER_KV], axis=0)
        keys = jnp.concatenate([k_lo[kvh], k_hi[kvh]], axis=0)
        bias = jnp.concatenate([jnp.concatenate([bias_t] * PAIRS_PER_KV, axis=1)] * 2, axis=0)
        return _mm_nt(keys, qs) + bias

    def softmax_terms(g, s_t):
        kvh = groups[g][1]
        cols = range(kvh * PAIRS_PER_KV, (kvh + 1) * PAIRS_PER_KV)
        terms = []
        for par in range(2):
            s = s_t[par * n_keys:(par + 1) * n_keys]
            sink = jnp.concatenate(
                [jnp.full((1, WINDOW), sinks_ref[0, 2 * c + par] * LOG2E, F32) for c in cols], axis=1)
            m = jnp.maximum(jnp.max(s, axis=0, keepdims=True), sink)
            terms.append((jnp.exp2(s - m), jnp.exp2(sink - m)))
        return terms

    def value_products(g, terms):
        blk, kvh = groups[g]
        v_t = blocks[blk][3]
        lhs = jnp.concatenate([v_t[kvh * ATT_HD:(kvh + 1) * ATT_HD], ones_rows], axis=0)
        halves = []
        for p, sink_term in terms:
            r = _mm(lhs, p)
            halves.append(r[:ATT_HD] * (1.0 / (r[ATT_HD:ATT_HD + 1] + sink_term)))
        return halves

    scores = [score_products(g) for g in range(len(groups))]
    side = side_product()
    terms = [softmax_terms(g, s) for g, s in enumerate(scores)]
    outs = [value_products(g, t) for g, t in enumerate(terms)]
    res = []
    for b in range(len(blocks)):
        cols = []
        for kvh in range(ATT_KVH):
            even, odd = outs[b * ATT_KVH + kvh]
            for i in range(PAIRS_PER_KV):
                queries = slice(i * WINDOW, (i + 1) * WINDOW)
                cols.append(jnp.concatenate([even[:, queries], odd[:, queries]], axis=0).T)
        res.append(jnp.concatenate(cols, axis=1))
    return res, side


def _split_kv_heads(a, fill):
    lane = lax.broadcasted_iota(jnp.int32, a.shape, 1)
    low = lane < ATT_HD
    sw = pltpu.roll(a, ATT_HD, axis=1)
    lo = [jnp.where(low, a, fill), jnp.where(low, sw, fill)]
    hi = [jnp.where(low, fill, sw), jnp.where(low, fill, a)]
    return lo, hi


def _swa_prompt_kernel(final, h_ref, ng_ref, win_ref, sinks_ref, wout_ref, fin_ref,
                       kp_ref, kc_ref, vp_ref, vc_ref, cos_ref, sup_ref, sdn_ref, out_ref):
    t = pl.program_id(1)
    tile = h_ref.shape[1]
    nblk = tile // WINDOW
    x = h_ref[0]
    xn = _rms_scale(x, ng_ref[...]).astype(BF16)
    u = _mm(xn, win_ref[:, :ATT_QD])
    scale = ATT_HD ** -0.5 * LOG2E
    cos, sup, sdn = cos_ref[...], sup_ref[...], sdn_ref[...]
    k_all = jnp.concatenate([kp_ref[0], kc_ref[0]], axis=0)
    v_all = jnp.concatenate([vp_ref[0], vc_ref[0]], axis=0)
    j = lax.broadcasted_iota(jnp.int32, (2 * WINDOW, WINDOW), 0)
    tq = lax.broadcasted_iota(jnp.int32, (2 * WINDOW, WINDOW), 1)
    rel = tq + WINDOW - j
    band = (rel >= 0) & (rel <= WINDOW)
    blocks = []
    for blk in range(nblk):
        rows = slice(blk * WINDOW, (blk + 1) * WINDOW)
        keys = slice(blk * WINDOW, (blk + 2) * WINDOW)
        k_lo, k_hi = _split_kv_heads(k_all[keys], 0.0)
        if blk == 0:
            valid = band & (j >= WINDOW * (t == 0).astype(jnp.int32))
        else:
            valid = band
        q_cols = []
        for c in range(ATT_QD // LANES):
            qc = u[rows, c * LANES:(c + 1) * LANES]
            q_cols.append(_rope_cols(qc, cos[rows], sup[rows], sdn[rows]) * scale)
        blocks.append((q_cols, k_lo, k_hi, v_all[keys].T, jnp.where(valid, 0.0, MASK_VALUE)))
    step = min(SWA_PHASE_BLOCKS, nblk)
    o_blocks, gate = _swa_attention(blocks[:step], sinks_ref, lambda: _mm(xn, win_ref[:, ATT_QD:]))
    for b0 in range(step, nblk, step):
        o_blocks += _swa_attention(blocks[b0:b0 + step], sinks_ref, lambda: None)[0]
    o = jnp.concatenate(o_blocks, axis=0) if nblk > 1 else o_blocks[0]
    y = x + _mm(o * _silu(gate), wout_ref[...])
    if final:
        y = _rms_scale(y, fin_ref[...])
    out_ref[0] = y


def _swa_prompt_layer(h, k, v, tables, norm_g, w_in, layer, sinks, w_out, final_norm, final):
    B, T, D = h.shape
    tile = min(SWA_TILE, T)
    assert T % tile == 0 and tile % WINDOW == 0
    per = tile // WINDOW
    const = lambda b, t: (0, 0)
    tab = pl.BlockSpec((tile, LANES), lambda b, t: (t, 0))
    prev = pl.BlockSpec((1, WINDOW, ATT_KVD), lambda b, t: (b, jnp.maximum(t * per - 1, 0), 0))
    cur = pl.BlockSpec((1, tile, ATT_KVD), lambda b, t: (b, t, 0))
    return pl.pallas_call(
        functools.partial(_swa_prompt_kernel, final),
        grid=(B, T // tile),
        in_specs=[
            pl.BlockSpec((1, tile, D), lambda b, t: (b, t, 0)),
            pl.BlockSpec((1, D), const),
            _layer_block(w_in, layer, True),
            pl.BlockSpec(memory_space=pltpu.SMEM),
            _layer_block(w_out, layer, True),
            pl.BlockSpec((1, D), const),
            prev, cur, prev, cur, tab, tab, tab,
        ],
        out_specs=pl.BlockSpec((1, tile, D), lambda b, t: (b, t, 0)),
        out_shape=jax.ShapeDtypeStruct((B, T, D), F32),
        compiler_params=_cparams("arbitrary", "arbitrary"),
        name="swa_prompt_final" if final else "swa_prompt",
    )(h, norm_g.reshape(1, D), w_in, sinks.reshape(1, ATT_QH), w_out, final_norm.reshape(1, D),
      k, k, v, v, *tables)


def _col(row):
    n = row.shape[1]
    return jnp.broadcast_to(row, (n, n)).T


def _hgrn_decode_kernel(x_ref, s_ref, ng_ref, win_ref, lbl_ref, gn_ref, wout_ref,
                        snew_ref, hout_ref, h_s, q_s, f_s, v_s, g_s, o_s, oblk_s):
    l, i = pl.program_id(0), pl.program_id(1)
    n_layers, n_blocks = pl.num_programs(0), pl.num_programs(1)
    bb = s_ref.shape[1]

    @pl.when(i == 0)
    def _():
        @pl.when(l == 0)
        def _():
            h_s[...] = x_ref[...]

        u = _mm(_rms_scale(h_s[...], ng_ref[0]), win_ref[0])
        lbl = lbl_ref[...]
        lb = jnp.zeros_like(lbl[:1])
        for layer in range(lbl.shape[0]):
            lb = jnp.where(l == layer, _lower_bound(lbl, layer), lb)
        q_s[...] = _silu(u[:, :HG_KEY])
        f_s[...] = _forget_gates(u[:, HG_KEY:2 * HG_KEY], lb)[0]
        v_s[...] = u[:, 2 * HG_KEY:2 * HG_KEY + HG_VAL]
        g_s[...] = _silu(u[:, 2 * HG_KEY + HG_VAL:])

    rows = pl.ds(pl.multiple_of(i * bb, bb), bb)
    qb, fb, vb = q_s[rows, :], f_s[rows, :], v_s[rows, :]
    for j in range(bb):
        for h in range(HG_HEADS):
            sl = slice(h * HG_DK, (h + 1) * HG_DK)
            v_r = vb[j:j + 1, h * HG_DV:(h + 1) * HG_DV]
            s_new = _col(fb[j:j + 1, sl]) * (s_ref[0, j, h] - v_r) + v_r
            snew_ref[0, j, h] = s_new
            oblk_s[j:j + 1, h * HG_DV:(h + 1) * HG_DV] = _mm(qb[j:j + 1, sl], s_new)
    o_s[rows, :] = oblk_s[...]

    @pl.when(i == n_blocks - 1)
    def _():
        gn = gn_ref[0]
        gated = []
        for h in range(HG_HEADS):
            sl = slice(h * HG_DV, (h + 1) * HG_DV)
            gated.append(_rms_scale(o_s[:, sl], gn) * g_s[:, sl])
        h_new = h_s[...] + _mm(jnp.concatenate(gated, axis=1), wout_ref[0])
        h_s[...] = h_new

        @pl.when(l == n_layers - 1)
        def _():
            hout_ref[...] = h_new


def _whole(shape):
    return pl.BlockSpec(shape, lambda *_: (0,) * len(shape))


def _hgrn_decode(h, states, norms, w_in, lb_logits, gnorms, w_out):
    B, D = h.shape
    n_a = states.shape[0]
    bb = min(DEC_BLOCK, B)
    assert B % bb == 0
    per_layer = lambda *tail: pl.BlockSpec((1,) + tail, lambda l, i: (l,) + (0,) * len(tail),
                                           pipeline_mode=pl.Buffered(1))
    st_blk = pl.BlockSpec((1, bb, HG_HEADS, HG_DK, HG_DV), lambda l, i: (l, i, 0, 0, 0))
    wide = pltpu.VMEM((B, HG_KEY), F32)
    s_new, h_new = pl.pallas_call(
        _hgrn_decode_kernel,
        grid=(n_a, B // bb),
        in_specs=[_whole((B, D)), st_blk, per_layer(1, D), per_layer(*w_in.shape[1:]),
                  _whole(lb_logits.shape), per_layer(1, HG_DV), per_layer(*w_out.shape[1:])],
        out_specs=[st_blk, _whole((B, D))],
        out_shape=[jax.ShapeDtypeStruct(states.shape, F32), jax.ShapeDtypeStruct((B, D), F32)],
        scratch_shapes=[pltpu.VMEM((B, D), F32)] + [wide] * 5 + [pltpu.VMEM((bb, HG_VAL), F32)],
        compiler_params=_cparams("arbitrary", "arbitrary"),
        name="hgrn_decode",
    )(h, states, norms.reshape(n_a, 1, D), w_in, lb_logits, gnorms.reshape(n_a, 1, HG_DV), w_out)
    return h_new, s_new


def _bf16_round(x):
    return x.astype(BF16).astype(F32)


def _swa_decode_kernel(final, h_ref, ng_ref, win_ref, cos_ref, sup_ref, sdn_ref, kc_ref, vc_ref,
                       kn_ref, vn_ref, sinks_ref, wout_ref, fin_ref, out_ref, qp_s, g_s, os_s):
    i = pl.program_id(0)
    n_blocks = pl.num_programs(0)
    bb, W = kc_ref.shape[0], kc_ref.shape[1]
    B = h_ref.shape[0]
    lane = lax.broadcasted_iota(jnp.int32, (B, LANES), 1)

    @pl.when(i == 0)
    def _():
        u = _mm(_rms_scale(h_ref[...], ng_ref[...]), win_ref[...])
        q = _rope_cols(u[:, :ATT_QD], cos_ref[...], sup_ref[...], sdn_ref[...]) * (ATT_HD ** -0.5)
        g_s[...] = _silu(u[:, ATT_QD:])
        for head in range(ATT_QH):
            col = q[:, (head // 2) * LANES:(head // 2 + 1) * LANES]
            kvh = head // ATT_GROUP
            if head % 2 != kvh:
                col = pltpu.roll(col, ATT_HD, axis=1)
            qp_s[:, head, :] = jnp.where((lane >= ATT_HD) if kvh == 1 else (lane < ATT_HD), col, 0.0)

    rows = pl.ds(pl.multiple_of(i * bb, bb), bb)
    kn, vn = kn_ref[rows, :], vn_ref[rows, :]
    qp = [qp_s[i * bb + j] for j in range(bb)]
    s_c = jnp.concatenate([_mm_nt(qp[j], kc_ref[j]) for j in range(bb)], axis=0)
    qp_all = jnp.concatenate(qp, axis=0)
    expand = lambda a: jnp.concatenate(
        [jnp.broadcast_to(a[j:j + 1], (ATT_QH, ATT_KVD)) for j in range(bb)], axis=0)
    s_n = jnp.sum(_bf16_round(qp_all) * _bf16_round(expand(kn)), axis=-1, keepdims=True)
    sink = jnp.concatenate([sinks_ref[...]] * bb, axis=0)
    m = jnp.maximum(jnp.maximum(jnp.max(s_c, axis=-1, keepdims=True), s_n), sink)
    p_c = jnp.exp(s_c - m)
    p_n = jnp.exp(s_n - m)
    inv = 1.0 / (jnp.sum(p_c, axis=-1, keepdims=True) + p_n + jnp.exp(sink - m))
    p_c = p_c * inv
    o = jnp.concatenate([_mm(p_c[j * ATT_QH:(j + 1) * ATT_QH], vc_ref[j]) for j in range(bb)], axis=0)
    o = o + _bf16_round(p_n * inv) * _bf16_round(expand(vn))
    for j in range(bb):
        os_s[i * bb + j] = o[j * ATT_QH:(j + 1) * ATT_QH]

    @pl.when(i == n_blocks - 1)
    def _():
        cols = []
        for c in range(ATT_QD // LANES):
            halves = []
            for par in range(2):
                head = 2 * c + par
                t = os_s[:, head, :]
                if par != head // ATT_GROUP:
                    t = pltpu.roll(t, ATT_HD, axis=1)
                halves.append(t)
            cols.append(jnp.where(lane < ATT_HD, halves[0], halves[1]))
        y = h_ref[...] + _mm(jnp.concatenate(cols, axis=1) * g_s[...], wout_ref[...])
        if final:
            y = _rms_scale(y, fin_ref[...])
        out_ref[...] = y


def _swa_decode_layer(h, k_new, v_new, cache_k, cache_v, tables, norm_g, w_in, layer, sinks, w_out,
                      final_norm, final):
    B, D = h.shape
    W = cache_k.shape[1]
    bb = min(SWA_DEC_BLOCK, B)
    assert B % bb == 0
    cache_blk = pl.BlockSpec((bb, W, ATT_KVD), lambda i: (i, 0, 0))
    head_rows = pltpu.VMEM((B, ATT_QH, ATT_KVD), F32)
    return pl.pallas_call(
        functools.partial(_swa_decode_kernel, final),
        grid=(B // bb,),
        in_specs=[_whole((B, D)), _whole((1, D)), _layer_block(w_in, layer)] + [_whole((1, LANES))] * 3
                 + [cache_blk, cache_blk, _whole((B, ATT_KVD)), _whole((B, ATT_KVD)),
                    _whole((ATT_QH, 1)), _layer_block(w_out, layer), _whole((1, D))],
        out_specs=_whole((B, D)),
        out_shape=jax.ShapeDtypeStruct((B, D), F32),
        scratch_shapes=[head_rows, pltpu.VMEM((B, ATT_QD), F32), head_rows],
        compiler_params=_cparams("arbitrary"),
        name="swa_decode_final" if final else "swa_decode",
    )(h, norm_g.reshape(1, D), w_in, *tables, cache_k.reshape(B, W, ATT_KVD),
      cache_v.reshape(B, W, ATT_KVD), k_new, v_new, sinks.reshape(ATT_QH, 1), w_out,
      final_norm.reshape(1, D))


def kernel(x_prompt, x_sample, state_hgrn, cache_k_win, cache_v_win, a_norm, a_w_in, a_lb_logits,
           a_gnorm, a_w_out, kv_norm, w_kv, b_norm, b_w_in, b_sinks, b_w_out, final_norm):
    B, T, D = x_prompt.shape
    BD, TD, _ = x_sample.shape
    n_a = a_w_in.shape[0]
    n_b = b_w_in.shape[0]
    a_w_in_b, a_w_out_b = a_w_in.astype(BF16), a_w_out.astype(BF16)
    b_w_in_b, b_w_out_b = b_w_in.astype(BF16), b_w_out.astype(BF16)
    w_kv_b = w_kv.astype(BF16)

    tab_p = _rope_tables(np.arange(T))
    h = x_prompt
    st_p = []
    for l in range(n_a):
        kv = (kv_norm, w_kv_b, tab_p) if l == n_a - 1 else None
        h, s, *kv_p = _hgrn_prompt_layer(h, a_norm[l], a_w_in_b, a_lb_logits, l, a_gnorm[l],
                                         a_w_out_b, kv)
        st_p.append(s)
    k_p, v_p = kv_p
    for l in range(n_b):
        h = _swa_prompt_layer(h, k_p, v_p, tab_p, b_norm[l], b_w_in_b, l, b_sinks[l], b_w_out_b,
                              final_norm, l == n_b - 1)
    y_prompt = h
    w_keep = min(WINDOW, T)
    k_win = k_p[:, T - w_keep:].reshape(B, w_keep, ATT_KVH, ATT_HD)
    v_win = v_p[:, T - w_keep:].reshape(B, w_keep, ATT_KVH, ATT_HD)

    tab_s = _rope_tables(PAST_LEN + np.arange(TD))
    hs = x_sample.reshape(BD * TD, D)
    hs, st_s = _hgrn_decode(hs, state_hgrn, a_norm, a_w_in_b, a_lb_logits, a_gnorm, a_w_out_b)
    k_s, v_s = _shared_kv(hs.reshape(1, BD, D), kv_norm, w_kv_b,
                          tuple(jnp.broadcast_to(t, (BD, LANES)) for t in tab_s))
    k_s, v_s = k_s.reshape(BD, ATT_KVD), v_s.reshape(BD, ATT_KVD)
    for l in range(n_b):
        hs = _swa_decode_layer(hs, k_s, v_s, cache_k_win, cache_v_win, tab_s, b_norm[l], b_w_in_b, l,
                               b_sinks[l], b_w_out_b, final_norm, l == n_b - 1)
    y_sample = hs.reshape(BD, TD, D)

    return (y_prompt, y_sample, jnp.stack(st_p), st_s, k_win, v_win,
            k_s.reshape(BD, TD, ATT_KVH, ATT_HD), v_s.reshape(BD, TD, ATT_KVH, ATT_HD))
```

```python
import functools

import numpy as np
import jax
import jax.numpy as jnp
from jax import lax
from jax.experimental import pallas as pl
from jax.experimental.pallas import tpu as pltpu

F32 = jnp.float32
BF16 = jnp.bfloat16

D_MODEL = 1024
HG_HEADS = 8
HG_DK = 128
HG_DV = 128
HG_KEY = HG_HEADS * HG_DK
HG_VAL = HG_HEADS * HG_DV
HG_CHUNK = 64
HG_HALF = HG_CHUNK // 2
HG_SUB = 16
HG_MAX_HALF_DECAY = 60.0
ATT_HD = 64
ATT_QH = 16
ATT_KVH = 2
ATT_GROUP = ATT_QH // ATT_KVH
ATT_QD = ATT_QH * ATT_HD
ATT_KVD = ATT_KVH * ATT_HD
WINDOW = 128
ROPE_THETA = 500000.0
ROT_DIM = ATT_HD // 4
NORM_EPS = 1e-6
MASK_VALUE = -1e30
NEG_BIG = -1e30
LOG2E = 1.4426950408889634
PAST_LEN = 8192

SUBLANES = 8
LANES = 128
PROMPT_TILE = 512
SWA_TILE = 1024
DEC_BLOCK = 16
SWA_DEC_BLOCK = 16
V7X_VMEM_BYTES = 64 * 1024 * 1024
VMEM_LIMIT = V7X_VMEM_BYTES * 3 // 4


def _cparams(*sem):
    return pltpu.CompilerParams(dimension_semantics=sem, vmem_limit_bytes=VMEM_LIMIT)


def _layer_block(stacked, layer, single_buffer=False):
    tail = stacked.shape[1:]
    mode = dict(pipeline_mode=pl.Buffered(1)) if single_buffer else {}
    return pl.BlockSpec((None,) + tail, lambda *_: (layer,) + (0,) * len(tail), **mode)


def _mm(a, b):
    return jnp.dot(a.astype(BF16), b.astype(BF16), preferred_element_type=F32)


def _mm_nt(a, b):
    return lax.dot_general(a.astype(BF16), b.astype(BF16), (((1,), (1,)), ((), ())),
                           preferred_element_type=F32)


def _mm_tn(a, b):
    return lax.dot_general(a.astype(BF16), b.astype(BF16), (((0,), (0,)), ((), ())),
                           preferred_element_type=F32)


def _rms_scale(x, g):
    ms = jnp.mean(x * x, axis=-1, keepdims=True)
    return x * lax.rsqrt(ms + NORM_EPS) * g


def _silu(x):
    half = 0.5 * x
    return half + half * jnp.tanh(half)


def _lower_bound(lbl, layer):
    n = lbl.shape[0]
    rows = [lbl[i:i + 1, :] for i in range(n)]
    m = functools.reduce(jnp.maximum, rows)
    es = [jnp.exp(r - m) for r in rows]
    inv = 1.0 / functools.reduce(jnp.add, es)
    ps = [e * inv for e in es]
    return functools.reduce(jnp.add, ps[:layer + 1]) - ps[0]


def _forget_gates(f, lb):
    e = jnp.exp(-f)
    r = 1.0 / (1.0 + e)
    return lb + (1.0 - lb) * r, (1.0 - lb) * (e * r)


def _decay(x):
    return jnp.exp(x)


def _chunk_cumsum(x):
    sub = lax.broadcasted_iota(jnp.int32, (SUBLANES, x.shape[1]), 0)
    out = []
    for r0 in range(0, x.shape[0], SUBLANES):
        t = x[r0:r0 + SUBLANES]
        s = 1
        while s < SUBLANES:
            t = t + jnp.where(sub >= s, pltpu.roll(t, s, axis=0), 0.0)
            s *= 2
        if r0 % HG_CHUNK:
            t = t + jnp.broadcast_to(out[-1][SUBLANES - 1:], t.shape)
        out.append(t)
    return jnp.concatenate(out, axis=0)


def _diag_blocks(q, k, v, b):
    sub_iota = lax.broadcasted_iota(jnp.int32, (SUBLANES, LANES), 0)
    outs = []
    for blk in range(HG_CHUNK // HG_SUB):
        base = blk * HG_SUB
        groups = HG_SUB // SUBLANES
        qs = [q[base + g * SUBLANES: base + (g + 1) * SUBLANES] for g in range(groups)]
        bs = [b[base + g * SUBLANES: base + (g + 1) * SUBLANES] for g in range(groups)]
        accs = [jnp.zeros((SUBLANES, HG_DV), F32) for _ in range(groups)]
        for s in range(HG_SUB):
            row = base + s
            k_s = k[row:row + 1]
            b_s = b[row:row + 1]
            v_s = v[row:row + 1]
            for g in range(s // SUBLANES, groups):
                d = bs[g] - b_s
                if g == s // SUBLANES:
                    d = jnp.where(sub_iota >= (s % SUBLANES), d, NEG_BIG)
                p = qs[g] * (k_s * _decay(d))
                a = jnp.sum(p, axis=-1, keepdims=True)
                accs[g] = accs[g] + a * v_s
        outs.extend(accs)
    return jnp.concatenate(outs, axis=0)


def _hgrn_chunk(q, k, v, b, st):
    C, H = HG_CHUNK, HG_HALF
    o_inter = _mm_nt(q * _decay(b), st)
    r0 = b[H - 1:H]
    a0 = _mm_nt(q[H:] * _decay(b[H:] - r0), k[:H] * _decay(r0 - b[:H]))
    o_hi = _mm(a0, v[:H])
    Q = HG_SUB
    o_q = []
    for base in (0, H):
        r1 = b[base + Q - 1: base + Q]
        a1 = _mm_nt(q[base + Q: base + 2 * Q] * _decay(b[base + Q: base + 2 * Q] - r1),
                    k[base: base + Q] * _decay(r1 - b[base: base + Q]))
        o_q.append(_mm(a1, v[base: base + Q]))
    zero = jnp.zeros((Q, HG_DV), F32)
    o_off = jnp.concatenate([zero, o_q[0], o_hi[:Q], o_hi[Q:] + o_q[1]], axis=0)
    o = o_inter + o_off + _diag_blocks(q, k, v, b)
    b_end = b[C - 1:C]
    st_new = st * _decay(b_end) + _mm_tn(v, k * _decay(b_end - b))
    return o, st_new


def _hgrn_bounded_first(q, k, v, b, st):
    C, H = HG_CHUNK, HG_HALF
    r = b[H - 1:H]
    eq = _decay(b - r)
    qt, kt = q * eq, k * _decay(r - b)
    e_mid, e_hi = _decay(r), eq[C - 1:C]
    scores = _mm_nt(qt, kt)
    v_t = v.T
    st_new = st * (e_mid * e_hi) + _mm(v_t, kt * e_hi)
    return scores, qt * e_mid, v_t, st_new


def _hgrn_bounded_second(scores, q_dec, v_t, st):
    t_i = lax.broadcasted_iota(jnp.int32, scores.shape, 0)
    s_i = lax.broadcasted_iota(jnp.int32, scores.shape, 1)
    a = jnp.where(s_i <= t_i, scores, 0.0)
    return _mm_nt(jnp.concatenate([q_dec, a], axis=1), jnp.concatenate([st, v_t], axis=1))


def _max_half_decay(b_s, tile):
    n = tile // HG_CHUNK
    worst = None
    for h in range(HG_HEADS):
        mid = b_s[h, pl.ds(HG_HALF - 1, n, stride=HG_CHUNK), :]
        end = b_s[h, pl.ds(HG_CHUNK - 1, n, stride=HG_CHUNK), :]
        w = jnp.maximum(-mid, mid - end)
        worst = w if worst is None else jnp.maximum(worst, w)
    return jnp.max(worst)


def _hgrn_prompt_kernel(layer, with_kv, *refs):
    h_ref, ng_ref, win_ref, lbl_ref, gn_ref, wout_ref = refs[:6]
    refs = refs[6:]
    if with_kv:
        kv_in, refs = refs[:5], refs[5:]
    out_ref, sfin_ref = refs[:2]
    refs = refs[2:]
    if with_kv:
        kv_out, refs = refs[:2], refs[2:]
    u_s, q_s, k_s, v_s, b_s, o_s, st_s = refs
    t = pl.program_id(1)
    nt = pl.num_programs(1)
    tile = h_ref.shape[1]

    @pl.when(t == 0)
    def _():
        st_s[...] = jnp.zeros_like(st_s)

    x = h_ref[0]
    u_s[...] = _mm(_rms_scale(x, ng_ref[...]), win_ref[...])
    u = u_s
    lb = _lower_bound(lbl_ref[...], layer)
    for h in range(HG_HEADS):
        sl = slice(h * HG_DK, (h + 1) * HG_DK)
        q_s[h] = _silu(u[:, sl])
        fg, kin = _forget_gates(u[:, HG_KEY + h * HG_DK: HG_KEY + (h + 1) * HG_DK], lb[:, sl])
        k_s[h] = kin
        b_s[h] = _chunk_cumsum(jnp.log(fg))
        v_s[h] = u[:, 2 * HG_KEY + h * HG_DV: 2 * HG_KEY + (h + 1) * HG_DV]

    def chunk_rows(c):
        return pl.ds(pl.multiple_of(c * HG_CHUNK, HG_CHUNK), HG_CHUNK)

    def load(c, h):
        rows = chunk_rows(c)
        return q_s[h, rows, :], k_s[h, rows, :], v_s[h, rows, :], b_s[h, rows, :], st_s[h]

    def bounded_chunks():
        pending = []
        for c in range(tile // HG_CHUNK):
            rows = slice(c * HG_CHUNK, (c + 1) * HG_CHUNK)
            started = []
            for h in range(HG_HEADS):
                st = st_s[h]
                scores, q_dec, v_t, st_new = _hgrn_bounded_first(
                    q_s[h, rows, :], k_s[h, rows, :], v_s[h, rows, :], b_s[h, rows, :], st)
                st_s[h] = st_new
                started.append((h, rows, scores, q_dec, v_t, st))
            for h, prev_rows, *second in pending:
                o_s[h, prev_rows, :] = _hgrn_bounded_second(*second)
            pending = started
        for h, prev_rows, *second in pending:
            o_s[h, prev_rows, :] = _hgrn_bounded_second(*second)

    def general_body(i, carry):
        c, h = i // HG_HEADS, i % HG_HEADS
        o, st_new = _hgrn_chunk(*load(c, h))
        o_s[h, chunk_rows(c), :] = o
        st_s[h] = st_new
        return carry

    bounded = _max_half_decay(b_s, tile) < HG_MAX_HALF_DECAY

    pl.when(bounded)(bounded_chunks)

    @pl.when(jnp.logical_not(bounded))
    def _():
        lax.fori_loop(0, (tile // HG_CHUNK) * HG_HEADS, general_body, 0)

    gn = gn_ref[...]
    gated = []
    for h in range(HG_HEADS):
        g = u[:, 2 * HG_KEY + HG_VAL + h * HG_DV: 2 * HG_KEY + HG_VAL + (h + 1) * HG_DV]
        gated.append(_rms_scale(o_s[h], gn) * _silu(g))
    y = x + _mm(jnp.concatenate(gated, axis=1), wout_ref[...])
    out_ref[0] = y
    if with_kv:
        _kv_project(y, *kv_in, *kv_out)

    @pl.when(t == nt - 1)
    def _():
        for h in range(HG_HEADS):
            sfin_ref[0, h] = st_s[h].T


def _hgrn_prompt_layer(h, norm_g, w_in, lb_logits, layer, gnorm, w_out, kv=None):
    B, T, D = h.shape
    tile = min(PROMPT_TILE, T)
    assert T % tile == 0 and tile % HG_CHUNK == 0 and HG_DK == HG_DV == LANES
    n_a = lb_logits.shape[0]
    const = lambda b, t: (0, 0)
    row_tile = lambda width: pl.BlockSpec((1, tile, width), lambda b, t: (b, t, 0))
    head_scratch = pltpu.VMEM((HG_HEADS, tile, HG_DK), F32)
    operands = [h, norm_g.reshape(1, D), w_in, lb_logits, gnorm.reshape(1, HG_DV), w_out]
    in_specs = [row_tile(D), pl.BlockSpec((1, D), const), _layer_block(w_in, layer, True),
                pl.BlockSpec((n_a, HG_KEY), const), pl.BlockSpec((1, HG_DV), const),
                _layer_block(w_out, layer, True)]
    out_specs = [row_tile(D), pl.BlockSpec((1, HG_HEADS, HG_DK, HG_DV), lambda b, t: (b, 0, 0, 0))]
    out_shape = [jax.ShapeDtypeStruct((B, T, D), F32),
                 jax.ShapeDtypeStruct((B, HG_HEADS, HG_DK, HG_DV), F32)]
    if kv is not None:
        kv_norm, w_kv, tables = kv
        operands += [kv_norm.reshape(1, D), w_kv, *tables]
        in_specs += [pl.BlockSpec((1, D), const), pl.BlockSpec(w_kv.shape, const)]
        in_specs += [pl.BlockSpec((tile, LANES), lambda b, t: (t, 0))] * 3
        out_specs += [row_tile(ATT_KVD)] * 2
        out_shape += [jax.ShapeDtypeStruct((B, T, ATT_KVD), F32)] * 2
    return pl.pallas_call(
        functools.partial(_hgrn_prompt_kernel, layer, kv is not None),
        grid=(B, T // tile),
        in_specs=in_specs,
        out_specs=out_specs,
        out_shape=out_shape,
        scratch_shapes=[pltpu.VMEM((tile, w_in.shape[-1]), F32)] + [head_scratch] * 5
                       + [pltpu.VMEM((HG_HEADS, HG_DV, HG_DK), F32)],
        compiler_params=_cparams("arbitrary", "arbitrary"),
        name=f"hgrn_prompt_l{layer}",
    )(*operands)


def _rope_tables(pos):
    half = ROT_DIM // 2
    pos = np.asarray(pos, np.float64)
    inv_freq = 1.0 / (ROPE_THETA ** (np.arange(half, dtype=np.float64) * 2.0 / ROT_DIM))
    ang = (pos[:, None].astype(np.float32) * inv_freq[None, :].astype(np.float32)).astype(np.float64)
    cos, sin = np.cos(ang), np.sin(ang)
    n = pos.shape[0]
    c = np.ones((n, ATT_HD)); s_up = np.zeros((n, ATT_HD)); s_dn = np.zeros((n, ATT_HD))
    c[:, :half] = cos; c[:, half:ROT_DIM] = cos
    s_dn[:, :half] = -sin
    s_up[:, half:ROT_DIM] = sin
    rep = LANES // ATT_HD
    tab = [np.tile(a, (1, rep)).astype(np.float32) for a in (c, s_up, s_dn)]
    return tuple(jnp.asarray(a) for a in tab)


def _rope_cols(x, cos, s_up, s_dn):
    half = ROT_DIM // 2
    cols = []
    for c in range(x.shape[1] // LANES):
        xc = x[:, c * LANES:(c + 1) * LANES]
        cols.append(xc * cos + pltpu.roll(xc, half, axis=1) * s_up
                    + pltpu.roll(xc, LANES - half, axis=1) * s_dn)
    return cols[0] if len(cols) == 1 else jnp.concatenate(cols, axis=1)


def _kv_kernel(h_ref, *refs):
    _kv_project(h_ref[0], *refs)


def _kv_project(x, ng_ref, w_ref, cos_ref, sup_ref, sdn_ref, k_ref, v_ref):
    u = _mm(_rms_scale(x, ng_ref[...]), w_ref[...])
    k_ref[0] = _rope_cols(u[:, :ATT_KVD], cos_ref[...], sup_ref[...], sdn_ref[...])
    v_ref[0] = u[:, ATT_KVD:]


def _shared_kv(h, kv_norm, w_kv, tables):
    B, T, D = h.shape
    tile = min(PROMPT_TILE, T)
    const = lambda b, t: (0, 0)
    tab = pl.BlockSpec((tile, LANES), lambda b, t: (t, 0))
    kv_spec = pl.BlockSpec((1, tile, ATT_KVD), lambda b, t: (b, t, 0))
    return pl.pallas_call(
        _kv_kernel,
        grid=(B, T // tile),
        in_specs=[pl.BlockSpec((1, tile, D), lambda b, t: (b, t, 0)),
                  pl.BlockSpec((1, D), const), pl.BlockSpec(w_kv.shape, const), tab, tab, tab],
        out_specs=[kv_spec, kv_spec],
        out_shape=[jax.ShapeDtypeStruct((B, T, ATT_KVD), F32)] * 2,
        compiler_params=_cparams("arbitrary", "arbitrary"),
        name="shared_kv",
    )(h, kv_norm.reshape(1, D), w_kv, *tables)


PAIRS_PER_KV = ATT_GROUP // 2


def _swa_attention(blocks, sinks_ref, side_product):
    groups = [(blk, kvh) for blk in range(len(blocks)) for kvh in range(ATT_KVH)]
    n_keys = blocks[0][4].shape[0]
    ones_rows = jnp.ones((SUBLANES, n_keys), F32)

    def score_products(g):
        blk, kvh = groups[g]
        q_cols, k_lo, k_hi, _, bias_t = blocks[blk]
        qs = jnp.concatenate(q_cols[kvh * PAIRS_PER_KV:(kvh + 1) * PAIRS_PER_KV], axis=0)
        keys = jnp.concatenate([k_lo[kvh], k_hi[kvh]], axis=0)
        bias = jnp.concatenate([jnp.concatenate([bias_t] * PAIRS_PER_KV, axis=1)] * 2, axis=0)
        return _mm_nt(keys, qs) + bias

    def softmax_terms(g, s_t):
        kvh = groups[g][1]
        cols = range(kvh * PAIRS_PER_KV, (kvh + 1) * PAIRS_PER_KV)
        terms = []
        for par in range(2):
            s = s_t[par * n_keys:(par + 1) * n_keys]
            sink = jnp.concatenate(
                [jnp.full((1, WINDOW), sinks_ref[0, 2 * c + par] * LOG2E, F32) for c in cols], axis=1)
            m = jnp.maximum(jnp.max(s, axis=0, keepdims=True), sink)
            terms.append((jnp.exp2(s - m), jnp.exp2(sink - m)))
        return terms

    def value_products(g, terms):
        blk, kvh = groups[g]
        v_t = blocks[blk][3]
        lhs = jnp.concatenate([v_t[kvh * ATT_HD:(kvh + 1) * ATT_HD], ones_rows], axis=0)
        halves = []
        for p, sink_term in terms:
            r = _mm(lhs, p)
            halves.append(r[:ATT_HD] * (1.0 / (r[ATT_HD:ATT_HD + 1] + sink_term)))
        return halves

    scores = [score_products(g) for g in range(len(groups))]
    side = side_product()
    terms = [softmax_terms(g, s) for g, s in enumerate(scores)]
    outs = [value_products(g, t) for g, t in enumerate(terms)]
    res = []
    for b in range(len(blocks)):
        cols = []
        for kvh in range(ATT_KVH):
            even, odd = outs[b * ATT_KVH + kvh]
            for i in range(PAIRS_PER_KV):
                queries = slice(i * WINDOW, (i + 1) * WINDOW)
                cols.append(jnp.concatenate([even[:, queries], odd[:, queries]], axis=0).T)
        res.append(jnp.concatenate(cols, axis=1))
    return res, side


def _split_kv_heads(a, fill):
    lane = lax.broadcasted_iota(jnp.int32, a.shape, 1)
    low = lane < ATT_HD
    sw = pltpu.roll(a, ATT_HD, axis=1)
    lo = [jnp.where(low, a, fill), jnp.where(low, sw, fill)]
    hi = [jnp.where(low, fill, sw), jnp.where(low, fill, a)]
    return lo, hi


def _swa_prompt_kernel(final, h_ref, ng_ref, win_ref, sinks_ref, wout_ref, fin_ref,
                       kp_ref, kc_ref, vp_ref, vc_ref, cos_ref, sup_ref, sdn_ref, out_ref):
    t = pl.program_id(1)
    tile = h_ref.shape[1]
    nblk = tile // WINDOW
    x = h_ref[0]
    xn = _rms_scale(x, ng_ref[...]).astype(BF16)
    u = _mm(xn, win_ref[:, :ATT_QD])
    scale = ATT_HD ** -0.5 * LOG2E
    cos, sup, sdn = cos_ref[...], sup_ref[...], sdn_ref[...]
    k_all = jnp.concatenate([kp_ref[0], kc_ref[0]], axis=0)
    v_all = jnp.concatenate([vp_ref[0], vc_ref[0]], axis=0)
    j = lax.broadcasted_iota(jnp.int32, (2 * WINDOW, WINDOW), 0)
    tq = lax.broadcasted_iota(jnp.int32, (2 * WINDOW, WINDOW), 1)
    rel = tq + WINDOW - j
    band = (rel >= 0) & (rel <= WINDOW)
    blocks = []
    for blk in range(nblk):
        rows = slice(blk * WINDOW, (blk + 1) * WINDOW)
        keys = slice(blk * WINDOW, (blk + 2) * WINDOW)
        k_lo, k_hi = _split_kv_heads(k_all[keys], 0.0)
        if blk == 0:
            valid = band & (j >= WINDOW * (t == 0).astype(jnp.int32))
        else:
            valid = band
        q_cols = []
        for c in range(ATT_QD // LANES):
            qc = u[rows, c * LANES:(c + 1) * LANES]
            q_cols.append(_rope_cols(qc, cos[rows], sup[rows], sdn[rows]) * scale)
        blocks.append((q_cols, k_lo, k_hi, v_all[keys].T, jnp.where(valid, 0.0, MASK_VALUE)))
    half = max(nblk // 2, 1)
    o_blocks, gate = _swa_attention(blocks[:half], sinks_ref, lambda: _mm(xn, win_ref[:, ATT_QD:]))
    if nblk > half:
        o_blocks += _swa_attention(blocks[half:], sinks_ref, lambda: None)[0]
    o = jnp.concatenate(o_blocks, axis=0) if nblk > 1 else o_blocks[0]
    y = x + _mm(o * _silu(gate), wout_ref[...])
    if final:
        y = _rms_scale(y, fin_ref[...])
    out_ref[0] = y


def _swa_prompt_layer(h, k, v, tables, norm_g, w_in, layer, sinks, w_out, final_norm, final):
    B, T, D = h.shape
    tile = min(SWA_TILE, T)
    assert T % tile == 0 and tile % WINDOW == 0
    per = tile // WINDOW
    const = lambda b, t: (0, 0)
    tab = pl.BlockSpec((tile, LANES), lambda b, t: (t, 0))
    prev = pl.BlockSpec((1, WINDOW, ATT_KVD), lambda b, t: (b, jnp.maximum(t * per - 1, 0), 0))
    cur = pl.BlockSpec((1, tile, ATT_KVD), lambda b, t: (b, t, 0))
    return pl.pallas_call(
        functools.partial(_swa_prompt_kernel, final),
        grid=(B, T // tile),
        in_specs=[
            pl.BlockSpec((1, tile, D), lambda b, t: (b, t, 0)),
            pl.BlockSpec((1, D), const),
            _layer_block(w_in, layer, True),
            pl.BlockSpec(memory_space=pltpu.SMEM),
            _layer_block(w_out, layer, True),
            pl.BlockSpec((1, D), const),
            prev, cur, prev, cur, tab, tab, tab,
        ],
        out_specs=pl.BlockSpec((1, tile, D), lambda b, t: (b, t, 0)),
        out_shape=jax.ShapeDtypeStruct((B, T, D), F32),
        compiler_params=_cparams("arbitrary", "arbitrary"),
        name="swa_prompt_final" if final else "swa_prompt",
    )(h, norm_g.reshape(1, D), w_in, sinks.reshape(1, ATT_QH), w_out, final_norm.reshape(1, D),
      k, k, v, v, *tables)


def _col(row):
    n = row.shape[1]
    return jnp.broadcast_to(row, (n, n)).T


def _hgrn_decode_kernel(x_ref, s_ref, ng_ref, win_ref, lbl_ref, gn_ref, wout_ref,
                        kvn_ref, wkv_ref, cos_ref, sup_ref, sdn_ref,
                        snew_ref, hout_ref, k_ref, v_ref, h_s, q_s, f_s, v_s, g_s, o_s, oblk_s):
    l, i = pl.program_id(0), pl.program_id(1)
    n_layers, n_blocks = pl.num_programs(0), pl.num_programs(1)
    bb = s_ref.shape[1]

    @pl.when(i == 0)
    def _():
        @pl.when(l == 0)
        def _():
            h_s[...] = x_ref[...]

        u = _mm(_rms_scale(h_s[...], ng_ref[0]), win_ref[0])
        lbl = lbl_ref[...]
        lb = jnp.zeros_like(lbl[:1])
        for layer in range(lbl.shape[0]):
            lb = jnp.where(l == layer, _lower_bound(lbl, layer), lb)
        q_s[...] = _silu(u[:, :HG_KEY])
        f_s[...] = _forget_gates(u[:, HG_KEY:2 * HG_KEY], lb)[0]
        v_s[...] = u[:, 2 * HG_KEY:2 * HG_KEY + HG_VAL]
        g_s[...] = _silu(u[:, 2 * HG_KEY + HG_VAL:])

    rows = pl.ds(pl.multiple_of(i * bb, bb), bb)
    qb, fb, vb = q_s[rows, :], f_s[rows, :], v_s[rows, :]
    for j in range(bb):
        for h in range(HG_HEADS):
            sl = slice(h * HG_DK, (h + 1) * HG_DK)
            v_r = vb[j:j + 1, h * HG_DV:(h + 1) * HG_DV]
            s_new = _col(fb[j:j + 1, sl]) * (s_ref[0, j, h] - v_r) + v_r
            snew_ref[0, j, h] = s_new
            oblk_s[j:j + 1, h * HG_DV:(h + 1) * HG_DV] = _mm(qb[j:j + 1, sl], s_new)
    o_s[rows, :] = oblk_s[...]

    @pl.when(i == n_blocks - 1)
    def _():
        gn = gn_ref[0]
        gated = []
        for h in range(HG_HEADS):
            sl = slice(h * HG_DV, (h + 1) * HG_DV)
            gated.append(_rms_scale(o_s[:, sl], gn) * g_s[:, sl])
        h_new = h_s[...] + _mm(jnp.concatenate(gated, axis=1), wout_ref[0])
        h_s[...] = h_new

        @pl.when(l == n_layers - 1)
        def _():
            hout_ref[...] = h_new
            _kv_project(h_new, kvn_ref, wkv_ref, cos_ref, sup_ref, sdn_ref, k_ref, v_ref)


def _whole(shape):
    return pl.BlockSpec(shape, lambda *_: (0,) * len(shape))


def _hgrn_decode(h, states, norms, w_in, lb_logits, gnorms, w_out, kv_norm, w_kv, tables):
    B, D = h.shape
    n_a = states.shape[0]
    bb = min(DEC_BLOCK, B)
    assert B % bb == 0
    per_layer = lambda *tail: pl.BlockSpec((1,) + tail, lambda l, i: (l,) + (0,) * len(tail),
                                           pipeline_mode=pl.Buffered(1))
    st_blk = pl.BlockSpec((1, bb, HG_HEADS, HG_DK, HG_DV), lambda l, i: (l, i, 0, 0, 0))
    wide = pltpu.VMEM((B, HG_KEY), F32)
    kv_rows = jax.ShapeDtypeStruct((1, B, ATT_KVD), F32)
    s_new, h_new, k_new, v_new = pl.pallas_call(
        _hgrn_decode_kernel,
        grid=(n_a, B // bb),
        in_specs=[_whole((B, D)), st_blk, per_layer(1, D), per_layer(*w_in.shape[1:]),
                  _whole(lb_logits.shape), per_layer(1, HG_DV), per_layer(*w_out.shape[1:]),
                  _whole((1, D)), _whole(w_kv.shape)] + [_whole((B, LANES))] * 3,
        out_specs=[st_blk, _whole((B, D)), _whole((1, B, ATT_KVD)), _whole((1, B, ATT_KVD))],
        out_shape=[jax.ShapeDtypeStruct(states.shape, F32), jax.ShapeDtypeStruct((B, D), F32),
                   kv_rows, kv_rows],
        scratch_shapes=[pltpu.VMEM((B, D), F32)] + [wide] * 5 + [pltpu.VMEM((bb, HG_VAL), F32)],
        compiler_params=_cparams("arbitrary", "arbitrary"),
        name="hgrn_decode",
    )(h, states, norms.reshape(n_a, 1, D), w_in, lb_logits, gnorms.reshape(n_a, 1, HG_DV), w_out,
      kv_norm.reshape(1, D), w_kv, *(jnp.broadcast_to(t, (B, LANES)) for t in tables))
    return h_new, s_new, k_new.reshape(B, ATT_KVD), v_new.reshape(B, ATT_KVD)


def _bf16_round(x):
    return x.astype(BF16).astype(F32)


def _swa_decode_kernel(final, h_ref, ng_ref, win_ref, cos_ref, sup_ref, sdn_ref, kc_ref, vc_ref,
                       kn_ref, vn_ref, sinks_ref, wout_ref, fin_ref, out_ref, qp_s, g_s, os_s):
    i = pl.program_id(0)
    n_blocks = pl.num_programs(0)
    bb, W = kc_ref.shape[0], kc_ref.shape[1]
    B = h_ref.shape[0]
    lane = lax.broadcasted_iota(jnp.int32, (B, LANES), 1)

    @pl.when(i == 0)
    def _():
        u = _mm(_rms_scale(h_ref[...], ng_ref[...]), win_ref[...])
        q = _rope_cols(u[:, :ATT_QD], cos_ref[...], sup_ref[...], sdn_ref[...]) * (ATT_HD ** -0.5)
        g_s[...] = _silu(u[:, ATT_QD:])
        for head in range(ATT_QH):
            col = q[:, (head // 2) * LANES:(head // 2 + 1) * LANES]
            kvh = head // ATT_GROUP
            if head % 2 != kvh:
                col = pltpu.roll(col, ATT_HD, axis=1)
            qp_s[:, head, :] = jnp.where((lane >= ATT_HD) if kvh == 1 else (lane < ATT_HD), col, 0.0)

    rows = pl.ds(pl.multiple_of(i * bb, bb), bb)
    kn, vn = kn_ref[rows, :], vn_ref[rows, :]
    qp = [qp_s[i * bb + j] for j in range(bb)]
    s_c = jnp.concatenate([_mm_nt(qp[j], kc_ref[j]) for j in range(bb)], axis=0)
    qp_all = jnp.concatenate(qp, axis=0)
    expand = lambda a: jnp.concatenate(
        [jnp.broadcast_to(a[j:j + 1], (ATT_QH, ATT_KVD)) for j in range(bb)], axis=0)
    s_n = jnp.sum(_bf16_round(qp_all) * _bf16_round(expand(kn)), axis=-1, keepdims=True)
    sink = jnp.concatenate([sinks_ref[...]] * bb, axis=0)
    m = jnp.maximum(jnp.maximum(jnp.max(s_c, axis=-1, keepdims=True), s_n), sink)
    p_c = jnp.exp(s_c - m)
    p_n = jnp.exp(s_n - m)
    inv = 1.0 / (jnp.sum(p_c, axis=-1, keepdims=True) + p_n + jnp.exp(sink - m))
    p_c = p_c * inv
    o = jnp.concatenate([_mm(p_c[j * ATT_QH:(j + 1) * ATT_QH], vc_ref[j]) for j in range(bb)], axis=0)
    o = o + _bf16_round(p_n * inv) * _bf16_round(expand(vn))
    for j in range(bb):
        os_s[i * bb + j] = o[j * ATT_QH:(j + 1) * ATT_QH]

    @pl.when(i == n_blocks - 1)
    def _():
        cols = []
        for c in range(ATT_QD // LANES):
            halves = []
            for par in range(2):
                head = 2 * c + par
                t = os_s[:, head, :]
                if par != head // ATT_GROUP:
                    t = pltpu.roll(t, ATT_HD, axis=1)
                halves.append(t)
            cols.append(jnp.where(lane < ATT_HD, halves[0], halves[1]))
        y = h_ref[...] + _mm(jnp.concatenate(cols, axis=1) * g_s[...], wout_ref[...])
        if final:
            y = _rms_scale(y, fin_ref[...])
        out_ref[...] = y


def _swa_decode_layer(h, k_new, v_new, cache_k, cache_v, tables, norm_g, w_in, layer, sinks, w_out,
                      final_norm, final):
    B, D = h.shape
    W = cache_k.shape[1]
    bb = min(SWA_DEC_BLOCK, B)
    assert B % bb == 0
    cache_blk = pl.BlockSpec((bb, W, ATT_KVD), lambda i: (i, 0, 0))
    head_rows = pltpu.VMEM((B, ATT_QH, ATT_KVD), F32)
    return pl.pallas_call(
        functools.partial(_swa_decode_kernel, final),
        grid=(B // bb,),
        in_specs=[_whole((B, D)), _whole((1, D)), _layer_block(w_in, layer)] + [_whole((1, LANES))] * 3
                 + [cache_blk, cache_blk, _whole((B, ATT_KVD)), _whole((B, ATT_KVD)),
                    _whole((ATT_QH, 1)), _layer_block(w_out, layer), _whole((1, D))],
        out_specs=_whole((B, D)),
        out_shape=jax.ShapeDtypeStruct((B, D), F32),
        scratch_shapes=[head_rows, pltpu.VMEM((B, ATT_QD), F32), head_rows],
        compiler_params=_cparams("arbitrary"),
        name="swa_decode_final" if final else "swa_decode",
    )(h, norm_g.reshape(1, D), w_in, *tables, cache_k.reshape(B, W, ATT_KVD),
      cache_v.reshape(B, W, ATT_KVD), k_new, v_new, sinks.reshape(ATT_QH, 1), w_out,
      final_norm.reshape(1, D))


def kernel(x_prompt, x_sample, state_hgrn, cache_k_win, cache_v_win, a_norm, a_w_in, a_lb_logits,
           a_gnorm, a_w_out, kv_norm, w_kv, b_norm, b_w_in, b_sinks, b_w_out, final_norm):
    B, T, D = x_prompt.shape
    BD, TD, _ = x_sample.shape
    n_a = a_w_in.shape[0]
    n_b = b_w_in.shape[0]
    a_w_in_b, a_w_out_b = a_w_in.astype(BF16), a_w_out.astype(BF16)
    b_w_in_b, b_w_out_b = b_w_in.astype(BF16), b_w_out.astype(BF16)
    w_kv_b = w_kv.astype(BF16)

    tab_p = _rope_tables(np.arange(T))
    h = x_prompt
    st_p = []
    for l in range(n_a):
        kv = (kv_norm, w_kv_b, tab_p) if l == n_a - 1 else None
        h, s, *kv_p = _hgrn_prompt_layer(h, a_norm[l], a_w_in_b, a_lb_logits, l, a_gnorm[l],
                                         a_w_out_b, kv)
        st_p.append(s)
    k_p, v_p = kv_p
    for l in range(n_b):
        h = _swa_prompt_layer(h, k_p, v_p, tab_p, b_norm[l], b_w_in_b, l, b_sinks[l], b_w_out_b,
                              final_norm, l == n_b - 1)
    y_prompt = h
    w_keep = min(WINDOW, T)
    k_win = k_p[:, T - w_keep:].reshape(B, w_keep, ATT_KVH, ATT_HD)
    v_win = v_p[:, T - w_keep:].reshape(B, w_keep, ATT_KVH, ATT_HD)

    tab_s = _rope_tables(PAST_LEN + np.arange(TD))
    hs = x_sample.reshape(BD * TD, D)
    hs, st_s, k_s, v_s = _hgrn_decode(hs, state_hgrn, a_norm, a_w_in_b, a_lb_logits, a_gnorm, a_w_out_b,
                                      kv_norm, w_kv_b, tab_s)
    for l in range(n_b):
        hs = _swa_decode_layer(hs, k_s, v_s, cache_k_win, cache_v_win, tab_s, b_norm[l], b_w_in_b, l,
                               b_sinks[l], b_w_out_b, final_norm, l == n_b - 1)
    y_sample = hs.reshape(BD, TD, D)

    return (y_prompt, y_sample, jnp.stack(st_p), st_s, k_win, v_win,
            k_s.reshape(BD, TD, ATT_KVH, ATT_HD), v_s.reshape(BD, TD, ATT_KVH, ATT_HD))
```
